```python
import math
import jax, jax.numpy as jnp
from jax import lax
import numpy as np

D_MODEL = 2048
BATCH = 4
SEQ = 2048
DEPTH = 2
DEC_BATCH = 128
DEC_SEQ = 8
PAST_LEN = 16384
PAGE_SIZE = 128

M_HEADS = 8
M_DQK = D_MODEL // (2 * M_HEADS)
M_DV = D_MODEL // M_HEADS
M_CHUNK = 64
GATE_CAP = 15.0
H_DK = 128
H_HEADS = D_MODEL // H_DK
H_DV = D_MODEL // H_HEADS
H_FDIM = H_HEADS * H_DK
H_CHUNK = 32
FF_DIM = -(-8 * D_MODEL // (3 * 256)) * 256
N_MLSTM = (DEPTH + 1) // 2
N_HGRN = DEPTH // 2
M_IN = 2 * M_HEADS * M_DQK + 2 * M_HEADS * M_DV + 2 * M_HEADS
H_IN = 2 * H_FDIM + 2 * H_HEADS * H_DV
EPS = 1e-6

kernel_name = 'hybrid_mlstm_hgrn2_step'


def rms_norm(x, g):
    x32 = x.astype(jnp.float32)
    return x32 * lax.rsqrt(jnp.mean(x32 * x32, axis=-1, keepdims=True) + EPS) * g.astype(jnp.float32)


def to_chunks(a, L):
    B, T = a.shape[:2]
    return a.reshape(B, T // L, L, *a.shape[2:]).swapaxes(0, 1)


def from_chunks(a):
    NC, B, L = a.shape[:3]
    return a.swapaxes(0, 1).reshape(B, NC * L, *a.shape[3:])


def soft_cap(z):
    return GATE_CAP * jnp.tanh(z / GATE_CAP)


def mlstm_mixer(h, C0, n0, m0, w_in, b_gates, head_g, w_out):
    B, T, _ = h.shape
    proj = h @ w_in.astype(jnp.float32)
    s1 = M_HEADS * M_DQK
    s2 = M_HEADS * M_DV
    q, k, v, o, ig, fg = jnp.split(proj, [s1, 2 * s1, 2 * s1 + s2, 2 * s1 + 2 * s2, 2 * s1 + 2 * s2 + M_HEADS], axis=-1)
    q = q.reshape(B, T, M_HEADS, M_DQK)
    k = k.reshape(B, T, M_HEADS, M_DQK) * (M_DQK ** -0.5)
    v = v.reshape(B, T, M_HEADS, M_DV)
    b32 = b_gates.astype(jnp.float32)
    ig = soft_cap(ig + b32[:M_HEADS])
    lf = jax.nn.log_sigmoid(soft_cap(fg + b32[M_HEADS:]))
    L = math.gcd(T, M_CHUNK)
    causal = jnp.tril(jnp.ones((L, L), dtype=bool))

    def body(carry, xs):
        C, n, m = carry
        qc, kc, vc, igc, lfc = xs
        b = jnp.cumsum(lfc, axis=1)
        dlog = b[:, :, None, :] - b[:, None, :, :] + igc[:, None, :, :]
        dlog = jnp.where(causal[None, :, :, None], dlog, -jnp.inf)
        inter = b + m[:, None, :]
        m_t = jnp.maximum(inter, jnp.max(dlog, axis=2))
        w_intra = jnp.exp(dlog - m_t[:, :, None, :])
        w_inter = jnp.exp(inter - m_t)
        s = jnp.einsum('bthc,bjhc->btjh', qc, kc) * w_intra
        num = jnp.einsum('btjh,bjhv->bthv', s, vc) + w_inter[..., None] * jnp.einsum('bthc,bhcv->bthv', qc, C)
        den = jnp.sum(s, axis=2) + w_inter * jnp.einsum('bthc,bhc->bth', qc, n)
        hc = num / jnp.maximum(jnp.abs(den), jnp.exp(-m_t))[..., None]
        b_last = b[:, -1]
        g = b_last[:, None, :] - b + igc
        m_new = jnp.maximum(b_last + m, jnp.max(g, axis=1))
        w = jnp.exp(g - m_new[:, None, :])
        decay = jnp.exp(b_last + m - m_new)
        C_new = decay[..., None, None] * C + jnp.einsum('bjh,bjhc,bjhv->bhcv', w, kc, vc)
        n_new = decay[..., None] * n + jnp.einsum('bjh,bjhc->bhc', w, kc)
        return (C_new, n_new, m_new), hc

    carry0 = (C0.astype(jnp.float32), n0.astype(jnp.float32), m0.astype(jnp.float32))
    (C, n, m), hs = lax.scan(body, carry0, (to_chunks(q, L), to_chunks(k, L), to_chunks(v, L), to_chunks(ig, L), to_chunks(lf, L)))
    hs = rms_norm(from_chunks(hs), head_g.reshape(M_HEADS, M_DV))
    out = (hs.reshape(B, T, M_HEADS * M_DV) * jax.nn.sigmoid(o)) @ w_out.astype(jnp.float32)
    return out, C, n, m


def hgrn2_mixer(h, S0, lb, w_in, gn_g, w_out):
    B, T, _ = h.shape
    proj = h @ w_in.astype(jnp.float32)
    q, fpre, iv, g = jnp.split(proj, [H_FDIM, 2 * H_FDIM, 2 * H_FDIM + H_HEADS * H_DV], axis=-1)
    f = lb + (1.0 - lb) * jax.nn.sigmoid(fpre)
    lf = jnp.log(f).reshape(B, T, H_HEADS, H_DK)
    k = (1.0 - f).reshape(B, T, H_HEADS, H_DK)
    q = q.reshape(B, T, H_HEADS, H_DK) * (H_DK ** -0.5)
    iv = iv.reshape(B, T, H_HEADS, H_DV)
    L = math.gcd(T, H_CHUNK)
    causal = jnp.tril(jnp.ones((L, L), dtype=bool))

    def body(S, xs):
        qc, kc, lfc, ic = xs
        A = jnp.cumsum(lfc, axis=1)
        diff = jnp.where(causal[None, :, :, None, None], A[:, :, None] - A[:, None, :], -jnp.inf)
        wd = jnp.exp(diff)
        s = jnp.einsum('bthc,bjhc,btjhc->btjh', qc, kc, wd)
        oc = jnp.einsum('btjh,bjhv->bthv', s, ic) + jnp.einsum('bthc,bhcv->bthv', qc * jnp.exp(A), S)
        A_last = A[:, -1]
        S_new = jnp.exp(A_last)[..., None] * S + jnp.einsum('bjhc,bjhv->bhcv', kc * jnp.exp(A_last[:, None] - A), ic)
        return S_new, oc

    S, os_ = lax.scan(body, S0.astype(jnp.float32), (to_chunks(q, L), to_chunks(k, L), to_chunks(lf, L), to_chunks(iv, L)))
    o = rms_norm(from_chunks(os_), gn_g.reshape(H_HEADS, H_DV)).reshape(B, T, H_HEADS * H_DV)
    out = (o * jax.nn.silu(g)) @ w_out.astype(jnp.float32)
    return out, S


def swiglu_ffn(h, w_up, w_down):
    a, u = jnp.split(h @ w_up.astype(jnp.float32), 2, axis=-1)
    return (jax.nn.silu(a) * u) @ w_down.astype(jnp.float32)


def trunk(x, mC, mn, mm, hS, norm_mix_g, norm_ffn_g, norm_final_g, mlstm_w_in, mlstm_b_gates,
          mlstm_head_norm_g, mlstm_w_out, hgrn_w_in, hgrn_lower_bounds, hgrn_g_norm_g, hgrn_w_out,
          ffn_w_up, ffn_w_down):
    lb_all = jnp.cumsum(jax.nn.softmax(hgrn_lower_bounds.astype(jnp.float32), axis=0), axis=0)
    lb_all = lb_all - lb_all[0]
    Cs, ns, ms, Ss = [], [], [], []
    for i in range(DEPTH):
        h = rms_norm(x, norm_mix_g[i])
        j = i // 2
        if i % 2 == 0:
            out, C, n, m = mlstm_mixer(h, mC[j], mn[j], mm[j], mlstm_w_in[j], mlstm_b_gates[j],
                                       mlstm_head_norm_g[j], mlstm_w_out[j])
            Cs.append(C); ns.append(n); ms.append(m)
        else:
            out, S = hgrn2_mixer(h, hS[j], lb_all[i], hgrn_w_in[j], hgrn_g_norm_g[j], hgrn_w_out[j])
            Ss.append(S)
        x = x + out.astype(x.dtype)
        x = x + swiglu_ffn(rms_norm(x, norm_ffn_g[i]), ffn_w_up[i], ffn_w_down[i]).astype(x.dtype)
    y = rms_norm(x, norm_final_g).astype(x.dtype)
    return y, jnp.stack(Cs), jnp.stack(ns), jnp.stack(ms), jnp.stack(Ss)


def setup_inputs(seed: int = 0) -> dict:
    key = jax.random.key(seed)
    ks = jax.random.split(key, 24)
    nrm = jax.random.normal
    f32 = jnp.float32
    return {
        'x_prompt': nrm(ks[0], (BATCH, SEQ, D_MODEL), f32),
        'x_sample': nrm(ks[1], (DEC_BATCH, DEC_SEQ, D_MODEL), f32),
        'state_mlstm_C': 0.5 * nrm(ks[2], (N_MLSTM, DEC_BATCH, M_HEADS, M_DQK, M_DV), f32),
        'state_mlstm_n': jnp.abs(nrm(ks[3], (N_MLSTM, DEC_BATCH, M_HEADS, M_DQK), f32)),
        'state_mlstm_m': nrm(ks[4], (N_MLSTM, DEC_BATCH, M_HEADS), f32),
        'state_hgrn_S': 0.5 * nrm(ks[5], (N_HGRN, DEC_BATCH, H_HEADS, H_DK, H_DV), f32),
        'norm_mix_g': 1.0 + 0.02 * nrm(ks[6], (DEPTH, D_MODEL), f32),
        'norm_ffn_g': 1.0 + 0.02 * nrm(ks[7], (DEPTH, D_MODEL), f32),
        'norm_final_g': 1.0 + 0.02 * nrm(ks[8], (D_MODEL,), f32),
        'mlstm_w_in': nrm(ks[9], (N_MLSTM, D_MODEL, M_IN), f32) * D_MODEL ** -0.5,
        'mlstm_b_gates': jnp.concatenate([0.1 * nrm(ks[10], (N_MLSTM, M_HEADS), f32),
                                          3.0 + 3.0 * jax.random.uniform(ks[11], (N_MLSTM, M_HEADS), f32)], axis=-1),
        'mlstm_head_norm_g': 1.0 + 0.02 * nrm(ks[12], (N_MLSTM, M_HEADS * M_DV), f32),
        'mlstm_w_out': nrm(ks[13], (N_MLSTM, M_HEADS * M_DV, D_MODEL), f32) * (M_HEADS * M_DV) ** -0.5,
        'hgrn_w_in': nrm(ks[14], (N_HGRN, D_MODEL, H_IN), f32) * D_MODEL ** -0.5,
        'hgrn_lower_bounds': 0.1 * nrm(ks[15], (DEPTH, H_FDIM), f32),
        'hgrn_g_norm_g': 1.0 + 0.02 * nrm(ks[16], (N_HGRN, H_HEADS * H_DV), f32),
        'hgrn_w_out': nrm(ks[17], (N_HGRN, H_HEADS * H_DV, D_MODEL), f32) * (H_HEADS * H_DV) ** -0.5,
        'ffn_w_up': nrm(ks[18], (DEPTH, D_MODEL, 2 * FF_DIM), f32) * D_MODEL ** -0.5,
        'ffn_w_down': nrm(ks[19], (DEPTH, FF_DIM, D_MODEL), f32) * FF_DIM ** -0.5,
    }


def reference(x_prompt, x_sample, state_mlstm_C, state_mlstm_n, state_mlstm_m, state_hgrn_S,
              norm_mix_g, norm_ffn_g, norm_final_g, mlstm_w_in, mlstm_b_gates, mlstm_head_norm_g,
              mlstm_w_out, hgrn_w_in, hgrn_lower_bounds, hgrn_g_norm_g, hgrn_w_out, ffn_w_up, ffn_w_down):
    B = x_prompt.shape[0]
    zC = jnp.zeros((N_MLSTM, B, M_HEADS, M_DQK, M_DV), jnp.float32)
    zn = jnp.zeros((N_MLSTM, B, M_HEADS, M_DQK), jnp.float32)
    zm = jnp.zeros((N_MLSTM, B, M_HEADS), jnp.float32)
    zS = jnp.zeros((N_HGRN, B, H_HEADS, H_DK, H_DV), jnp.float32)
    y_prompt, C_p, n_p, m_p, S_p = trunk(
        x_prompt, zC, zn, zm, zS, norm_mix_g, norm_ffn_g, norm_final_g, mlstm_w_in, mlstm_b_gates,
        mlstm_head_norm_g, mlstm_w_out, hgrn_w_in, hgrn_lower_bounds, hgrn_g_norm_g, hgrn_w_out,
        ffn_w_up, ffn_w_down)
    y_sample, C_s, n_s, m_s, S_s = trunk(
        x_sample, state_mlstm_C, state_mlstm_n, state_mlstm_m, state_hgrn_S, norm_mix_g, norm_ffn_g,
        norm_final_g, mlstm_w_in, mlstm_b_gates, mlstm_head_norm_g, mlstm_w_out, hgrn_w_in,
        hgrn_lower_bounds, hgrn_g_norm_g, hgrn_w_out, ffn_w_up, ffn_w_down)
    return (y_prompt, y_sample, C_p, n_p, m_p, S_p, C_s, n_s, m_s, S_s)
```

```python
import functools

import jax
import jax.numpy as jnp
from jax import lax
from jax.experimental import pallas as pl
from jax.experimental.pallas import tpu as pltpu

F32 = jnp.float32
BF16 = jnp.bfloat16

EPS = 1e-6
GATE_CAP = 15.0
M_HEADS = 8
M_CHUNK = 64
H_DK = 128
H_CHUNK = 32
H_SUB = 16

VMEM_LIMIT_BYTES = 52 * 1024 * 1024


def _params(*sem):
    return pltpu.CompilerParams(dimension_semantics=sem, vmem_limit_bytes=VMEM_LIMIT_BYTES)


def _rms_normed_bf16(x, g):
    ms = jnp.mean(x * x, axis=-1, keepdims=True)
    return (x * lax.rsqrt(ms + EPS) * g).astype(BF16)


def _norm_matmul_kernel(x_ref, g_ref, w_ref, o_ref, h_scr):
    @pl.when(pl.program_id(1) == 0)
    def _():
        h_scr[...] = _rms_normed_bf16(x_ref[...], g_ref[...])

    o_ref[...] = jnp.dot(h_scr[...], w_ref[...].astype(BF16),
                         preferred_element_type=F32).astype(o_ref.dtype)


def norm_matmul(x, g, w, n_cols, *, tm, tn, out_dtype):
    m, d = x.shape
    return pl.pallas_call(
        _norm_matmul_kernel,
        grid=(m // tm, n_cols // tn),
        in_specs=[
            pl.BlockSpec((tm, d), lambda i, j: (i, 0)),
            pl.BlockSpec((1, d), lambda i, j: (0, 0)),
            pl.BlockSpec((d, tn), lambda i, j: (0, j)),
        ],
        out_specs=pl.BlockSpec((tm, tn), lambda i, j: (i, j)),
        out_shape=jax.ShapeDtypeStruct((m, n_cols), out_dtype),
        scratch_shapes=[pltpu.VMEM((tm, d), BF16)],
        compiler_params=_params("arbitrary", "arbitrary"),
        name="norm_matmul",
    )(x, g.reshape(1, d), w)


def _norm_swiglu_kernel(x_ref, g_ref, wa_ref, wu_ref, o_ref, h_scr):
    @pl.when(pl.program_id(1) == 0)
    def _():
        h_scr[...] = _rms_normed_bf16(x_ref[...], g_ref[...])

    h = h_scr[...]
    a = jnp.dot(h, wa_ref[...].astype(BF16), preferred_element_type=F32)
    u = jnp.dot(h, wu_ref[...].astype(BF16), preferred_element_type=F32)
    o_ref[...] = (a * jax.nn.sigmoid(a) * u).astype(o_ref.dtype)


def norm_swiglu(x, g, w_up, *, tm, tn):
    m, d = x.shape
    ff = w_up.shape[1] // 2
    nj = ff // tn
    return pl.pallas_call(
        _norm_swiglu_kernel,
        grid=(m // tm, nj),
        in_specs=[
            pl.BlockSpec((tm, d), lambda i, j: (i, 0)),
            pl.BlockSpec((1, d), lambda i, j: (0, 0)),
            pl.BlockSpec((d, tn), lambda i, j: (0, j)),
            pl.BlockSpec((d, tn), lambda i, j: (0, j + nj)),
        ],
        out_specs=pl.BlockSpec((tm, tn), lambda i, j: (i, j)),
        out_shape=jax.ShapeDtypeStruct((m, ff), BF16),
        scratch_shapes=[pltpu.VMEM((tm, d), BF16)],
        compiler_params=_params("arbitrary", "arbitrary"),
        name="norm_swiglu",
    )(x, g.reshape(1, d), w_up, w_up)


def _matmul_residual_kernel(y_ref, w_ref, r_ref, o_ref, w_scr):
    @pl.when(pl.program_id(1) == 0)
    def _():
        w_scr[...] = w_ref[...].astype(BF16)

    o_ref[...] = r_ref[...] + jnp.dot(y_ref[...], w_scr[...], preferred_element_type=F32)


def matmul_residual(y, w, res, *, tm, tn):
    m, k = y.shape
    n = w.shape[1]
    return pl.pallas_call(
        _matmul_residual_kernel,
        grid=(n // tn, m // tm),
        in_specs=[
            pl.BlockSpec((tm, k), lambda j, i: (i, 0)),
            pl.BlockSpec((k, tn), lambda j, i: (0, j)),
            pl.BlockSpec((tm, tn), lambda j, i: (i, j)),
        ],
        out_specs=pl.BlockSpec((tm, tn), lambda j, i: (i, j)),
        out_shape=jax.ShapeDtypeStruct((m, n), F32),
        scratch_shapes=[pltpu.VMEM((k, tn), BF16)],
        compiler_params=_params("arbitrary", "arbitrary"),
        name="matmul_residual",
    )(y, w, res)


def _final_norm_kernel(x_ref, g_ref, o_ref):
    x = x_ref[...]
    ms = jnp.mean(x * x, axis=-1, keepdims=True)
    o_ref[...] = x * lax.rsqrt(ms + EPS) * g_ref[...]


def final_norm(x, g, *, tm):
    m, d = x.shape
    return pl.pallas_call(
        _final_norm_kernel,
        grid=(m // tm,),
        in_specs=[pl.BlockSpec((tm, d), lambda i: (i, 0)), pl.BlockSpec((1, d), lambda i: (0, 0))],
        out_specs=pl.BlockSpec((tm, d), lambda i: (i, 0)),
        out_shape=jax.ShapeDtypeStruct((m, d), F32),
        compiler_params=_params("arbitrary"),
        name="final_norm",
    )(x, g.reshape(1, d))


def _segment_masks(nseg, L):
    R = nseg * L
    r = lax.broadcasted_iota(jnp.int32, (R, R), 0)
    c = lax.broadcasted_iota(jnp.int32, (R, R), 1)
    same = (r // L) == (c // L)
    return r, c, same


def _dot_nt(a, b):
    return lax.dot_general(a, b, (((1,), (1,)), ((), ())), preferred_element_type=F32)


def _dot_tn(a, b):
    return lax.dot_general(a, b, (((0,), (0,)), ((), ())), preferred_element_type=F32)


def _log_sigmoid(x):
    return jnp.minimum(x, 0.0) - jnp.log1p(jnp.exp(-jnp.abs(x)))


def _mlstm_kernel(nseg, L, dqk, dv,
                  p_ref, gp_ref, bias_ref, hg_ref, c0_ref, n0_ref, m0_ref,
                  y_ref, c_ref, n_ref, m_ref):
    H = M_HEADS
    R = nseg * L
    s1 = H * dqk
    scale = dqk ** -0.5

    @pl.when(pl.program_id(1) == 0)
    def _():
        c_ref[...] = c0_ref[...]
        n_ref[...] = n0_ref[...]
        m_ref[...] = m0_ref[...]

    r, c, same = _segment_masks(nseg, L)
    causal = same & (c <= r)
    causal_t = same & (r <= c)
    eye = r == c
    row_seg = lax.broadcasted_iota(jnp.int32, (R, 1), 0) // L

    gates = gp_ref[...] + bias_ref[...]
    capped = GATE_CAP * jnp.tanh(gates / GATE_CAP)
    log_f = _log_sigmoid(capped)

    m_prev = m_ref[0, 0]
    m_new_all = m_prev
    seg_i = lax.broadcasted_iota(jnp.int32, (nseg, H), 0)
    head_i = lax.broadcasted_iota(jnp.int32, (nseg, H), 1)

    for h in range(H):
        q = p_ref[:, h * dqk:(h + 1) * dqk]
        k = p_ref[:, s1 + h * dqk:s1 + (h + 1) * dqk]
        v = p_ref[:, 2 * s1 + h * dv:2 * s1 + (h + 1) * dv]
        o = p_ref[:, 2 * s1 + H * dv + h * dv:2 * s1 + H * dv + (h + 1) * dv]
        qf = q.astype(F32)
        kf = k.astype(F32)
        vf = v.astype(F32)

        ig_col = capped[:, h:h + 1]
        lf_col = log_f[:, H + h:H + h + 1]
        lf_row = jnp.sum(jnp.where(eye, lf_col, 0.0), axis=0, keepdims=True)
        ig_row = jnp.sum(jnp.where(eye, ig_col, 0.0), axis=0, keepdims=True)
        b_col = jnp.sum(jnp.where(causal, lf_row, 0.0), axis=1, keepdims=True)
        b_row = jnp.sum(jnp.where(causal_t, lf_col, 0.0), axis=0, keepdims=True)
        b_last_col = jnp.sum(jnp.where(same, lf_row, 0.0), axis=1, keepdims=True)

        m_col = jnp.zeros((R, 1), F32)
        n_rows = jnp.zeros((R, dqk), F32)
        for s in range(nseg):
            m_s = m_prev[s:s + 1, h:h + 1]
            m_col = jnp.where(row_seg == s, m_s, m_col)
            n_rows = jnp.where(row_seg == s, n_ref[0, s, h:h + 1, :], n_rows)

        dlog = jnp.where(causal, b_col - b_row + ig_row, -jnp.inf)
        inter = b_col + m_col
        m_t = jnp.maximum(inter, jnp.max(dlog, axis=1, keepdims=True))
        w_intra = jnp.exp(dlog - m_t)
        w_inter = jnp.exp(inter - m_t)
        sc = _dot_nt(q, k) * scale * w_intra

        qc = jnp.zeros((R, dv), F32)
        for s in range(nseg):
            qc_s = jnp.dot(q, c_ref[0, s, h].astype(BF16), preferred_element_type=F32)
            qc = qc_s if nseg == 1 else jnp.where(row_seg == s, qc_s, qc)
        num = jnp.dot(sc.astype(BF16), v, preferred_element_type=F32) + w_inter * qc
        qn = jnp.sum(qf * n_rows, axis=1, keepdims=True)
        den = jnp.sum(sc, axis=1, keepdims=True) + w_inter * qn
        hc = num / jnp.maximum(jnp.abs(den), jnp.exp(-m_t))

        hn = hc * lax.rsqrt(jnp.mean(hc * hc, axis=1, keepdims=True) + EPS) * hg_ref[:, h * dv:(h + 1) * dv]
        y_ref[:, h * dv:(h + 1) * dv] = (hn * jax.nn.sigmoid(o.astype(F32))).astype(y_ref.dtype)

        g = b_last_col - b_col + ig_col
        m_new_col = jnp.zeros((R, 1), F32)
        decays = []
        for s in range(nseg):
            in_s = row_seg == s
            m_s = m_prev[s:s + 1, h:h + 1]
            b_last_s = jnp.sum(jnp.where(in_s, lf_col, 0.0), axis=0, keepdims=True)
            g_max_s = jnp.max(jnp.where(in_s, g, -jnp.inf), axis=0, keepdims=True)
            m_new_s = jnp.maximum(b_last_s + m_s, g_max_s)
            decays.append(jnp.exp(b_last_s + m_s - m_new_s))
            m_new_col = jnp.where(in_s, m_new_s, m_new_col)
            m_new_all = jnp.where((seg_i == s) & (head_i == h), m_new_s, m_new_all)
        w_k = jnp.exp(g - m_new_col) * scale
        wv = w_k * vf
        wk = w_k * kf
        for s in range(nseg):
            in_s = row_seg == s
            wv_s = wv if nseg == 1 else jnp.where(in_s, wv, 0.0)
            wk_s = wk if nseg == 1 else jnp.where(in_s, wk, 0.0)
            c_ref[0, s, h] = decays[s] * c_ref[0, s, h] + _dot_tn(k, wv_s.astype(BF16))
            n_ref[0, s, h:h + 1, :] = decays[s] * n_ref[0, s, h:h + 1, :] + jnp.sum(wk_s, axis=0, keepdims=True)

    m_ref[0, 0] = m_new_all


def mlstm_mixer(proj, gate_pre, bias, head_g, c0, n0, m0, *, row0, batch, seq, nseg, chunk):
    H = M_HEADS
    dqk, dv = c0.shape[-2], c0.shape[-1]
    R = nseg * chunk
    nchunks = seq // chunk
    assert nseg == 1 or nchunks == 1
    nblocks = batch // nseg
    blk0 = row0 // R
    rows = lambda i, c: (blk0 + i * nchunks + c, 0)
    width = proj.shape[1]
    c0 = c0.reshape(nblocks, nseg, H, dqk, dv)
    n0 = n0.reshape(nblocks, nseg, H, dqk)
    m0 = m0.reshape(nblocks, 1, nseg, H)
    y, c_out, n_out, m_out = pl.pallas_call(
        functools.partial(_mlstm_kernel, nseg, chunk, dqk, dv),
        grid=(nblocks, nchunks),
        in_specs=[
            pl.BlockSpec((R, width), rows),
            pl.BlockSpec((R, 2 * H), rows),
            pl.BlockSpec((1, 2 * H), lambda i, c: (0, 0)),
            pl.BlockSpec((1, H * dv), lambda i, c: (0, 0)),
            pl.BlockSpec((1, nseg, H, dqk, dv), lambda i, c: (i, 0, 0, 0, 0)),
            pl.BlockSpec((1, nseg, H, dqk), lambda i, c: (i, 0, 0, 0)),
            pl.BlockSpec((1, 1, nseg, H), lambda i, c: (i, 0, 0, 0)),
        ],
        out_specs=[
            pl.BlockSpec((R, H * dv), lambda i, c: (i * nchunks + c, 0)),
            pl.BlockSpec((1, nseg, H, dqk, dv), lambda i, c: (i, 0, 0, 0, 0)),
            pl.BlockSpec((1, nseg, H, dqk), lambda i, c: (i, 0, 0, 0)),
            pl.BlockSpec((1, 1, nseg, H), lambda i, c: (i, 0, 0, 0)),
        ],
        out_shape=[
            jax.ShapeDtypeStruct((batch * seq, H * dv), BF16),
            jax.ShapeDtypeStruct(c0.shape, F32),
            jax.ShapeDtypeStruct(n0.shape, F32),
            jax.ShapeDtypeStruct(m0.shape, F32),
        ],
        compiler_params=_params("arbitrary", "arbitrary"),
        name="mlstm_mixer",
    )(proj, gate_pre, bias.reshape(1, 2 * H), head_g.reshape(1, H * dv), c0, n0, m0)
    return (y, c_out.reshape(batch, H, dqk, dv), n_out.reshape(batch, H, dqk), m_out.reshape(batch, H))


def _hgrn_kernel(nseg, L, heads, dk, dv,
                 p_ref, lb_ref, gn_ref, s0_ref, y_ref, s_ref):
    R = nseg * L
    fdim = heads * dk
    scale = dk ** -0.5
    sub = min(L, H_SUB)
    assert L % sub == 0 and L // sub in (1, 2)

    @pl.when(pl.program_id(1) == 0)
    def _():
        s_ref[...] = s0_ref[...]

    r, c, same = _segment_masks(nseg, L)
    causal = same & (c <= r)
    same_sub = (r // sub) == (c // sub)
    diag_mask = causal & same_sub
    tril = causal.astype(F32)

    row = lax.broadcasted_iota(jnp.int32, (R, 1), 0)
    row_seg = row // L
    pos = row % L
    mid_sel = (same_sub & ((c % sub) == (sub // 2))).astype(F32)
    last_sel = (same & ((c % L) == (L - 1))).astype(F32)
    if L // sub == 2:
        bnd_sel = (same & ((c % L) == (sub - 1))).astype(F32)
        upper = pos >= sub
    hp = lax.Precision.HIGHEST

    for h in range(heads):
        q = p_ref[:, h * dk:(h + 1) * dk].astype(F32) * scale
        fpre = p_ref[:, fdim + h * dk:fdim + (h + 1) * dk].astype(F32)
        iv = p_ref[:, 2 * fdim + h * dv:2 * fdim + (h + 1) * dv]
        gate = p_ref[:, 2 * fdim + heads * dv + h * dv:2 * fdim + heads * dv + (h + 1) * dv].astype(F32)
        lb = lb_ref[:, h * dk:(h + 1) * dk]
        f = lb + (1.0 - lb) * jax.nn.sigmoid(fpre)
        log_f = jnp.log(f)
        k = 1.0 - f

        a = jnp.dot(tril, log_f, preferred_element_type=F32, precision=hp)
        a_mid = jnp.dot(mid_sel, a, preferred_element_type=F32, precision=hp)
        a_last = jnp.dot(last_sel, a, preferred_element_type=F32, precision=hp)

        qd = (q * jnp.exp(a - a_mid)).astype(BF16)
        kd = (k * jnp.exp(a_mid - a)).astype(BF16)
        sc = jnp.where(diag_mask, _dot_nt(qd, kd), 0.0)
        if L // sub == 2:
            a_bnd = jnp.dot(bnd_sel, a, preferred_element_type=F32, precision=hp)
            qo = jnp.where(upper, q * jnp.exp(jnp.minimum(a - a_bnd, 0.0)), 0.0).astype(BF16)
            ko = jnp.where(upper, 0.0, k * jnp.exp(jnp.minimum(a_bnd - a, 0.0))).astype(BF16)
            sc = sc + jnp.where(same, _dot_nt(qo, ko), 0.0)

        qe = (q * jnp.exp(a)).astype(BF16)
        inter = jnp.zeros((R, dv), F32)
        for s in range(nseg):
            inter_s = jnp.dot(qe, s_ref[0, s, h].astype(BF16), preferred_element_type=F32)
            inter = inter_s if nseg == 1 else jnp.where(row_seg == s, inter_s, inter)
        oc = jnp.dot(sc.astype(BF16), iv, preferred_element_type=F32) + inter

        on = oc * lax.rsqrt(jnp.mean(oc * oc, axis=1, keepdims=True) + EPS) * gn_ref[:, h * dv:(h + 1) * dv]
        y_ref[:, h * dv:(h + 1) * dv] = (on * (gate * jax.nn.sigmoid(gate))).astype(y_ref.dtype)

        ke = k * jnp.exp(a_last - a)
        for s in range(nseg):
            in_s = row_seg == s
            ke_s = ke if nseg == 1 else jnp.where(in_s, ke, 0.0)
            a_last_s = a[s * L + L - 1:s * L + L, :]
            upd = _dot_tn(ke_s.astype(BF16), iv)
            decay_col = jnp.sum(jnp.where(lax.broadcasted_iota(jnp.int32, (dk, dk), 0)
                                          == lax.broadcasted_iota(jnp.int32, (dk, dk), 1),
                                          jnp.exp(a_last_s), 0.0), axis=1, keepdims=True)
            s_ref[0, s, h] = decay_col * s_ref[0, s, h] + upd


def hgrn_mixer(proj, lb, gn_g, s0, *, row0, batch, seq, nseg, chunk):
    heads, dk, dv = s0.shape[-3], s0.shape[-2], s0.shape[-1]
    R = nseg * chunk
    nchunks = seq // chunk
    assert nseg == 1 or nchunks == 1
    nblocks = batch // nseg
    blk0 = row0 // R
    rows = lambda i, c: (blk0 + i * nchunks + c, 0)
    s0 = s0.reshape(nblocks, nseg, heads, dk, dv)
    y, s_out = pl.pallas_call(
        functools.partial(_hgrn_kernel, nseg, chunk, heads, dk, dv),
        grid=(nblocks, nchunks),
        in_specs=[
            pl.BlockSpec((R, proj.shape[1]), rows),
            pl.BlockSpec((1, heads * dk), lambda i, c: (0, 0)),
            pl.BlockSpec((1, heads * dv), lambda i, c: (0, 0)),
            pl.BlockSpec((1, nseg, heads, dk, dv), lambda i, c: (i, 0, 0, 0, 0)),
        ],
        out_specs=[
            pl.BlockSpec((R, heads * dv), lambda i, c: (i * nchunks + c, 0)),
            pl.BlockSpec((1, nseg, heads, dk, dv), lambda i, c: (i, 0, 0, 0, 0)),
        ],
        out_shape=[
            jax.ShapeDtypeStruct((batch * seq, heads * dv), BF16),
            jax.ShapeDtypeStruct(s0.shape, F32),
        ],
        compiler_params=_params("arbitrary", "arbitrary"),
        name="hgrn_mixer",
    )(proj, lb.reshape(1, heads * dk), gn_g.reshape(1, heads * dv), s0)
    return y, s_out.reshape(batch, heads, dk, dv)


def kernel(x_prompt, x_sample, state_mlstm_C, state_mlstm_n, state_mlstm_m, state_hgrn_S,
           norm_mix_g, norm_ffn_g, norm_final_g, mlstm_w_in, mlstm_b_gates, mlstm_head_norm_g,
           mlstm_w_out, hgrn_w_in, hgrn_lower_bounds, hgrn_g_norm_g, hgrn_w_out, ffn_w_up, ffn_w_down):
    bp, tp, d = x_prompt.shape
    bs, ts, _ = x_sample.shape
    np_, ns_ = bp * tp, bs * ts
    depth = norm_mix_g.shape[0]
    H = M_HEADS
    dqk, dv = state_mlstm_C.shape[-2], state_mlstm_C.shape[-1]
    hh, hdk, hdv = state_hgrn_S.shape[-3], state_hgrn_S.shape[-2], state_hgrn_S.shape[-1]

    x = jnp.concatenate([x_prompt.reshape(np_, d), x_sample.reshape(ns_, d)], axis=0)

    lb_all = jnp.cumsum(jax.nn.softmax(hgrn_lower_bounds.astype(F32), axis=0), axis=0)
    lb_all = lb_all - lb_all[0]

    tm = 1024
    Cp, Np, Mp, Sp, Cs, Ns, Ms, Ss = [], [], [], [], [], [], [], []
    for i in range(depth):
        j = i // 2
        if i % 2 == 0:
            w_in = mlstm_w_in[j]
            n_main = 2 * H * dqk + 2 * H * dv
            proj = norm_matmul(x, norm_mix_g[i], w_in, n_main, tm=tm, tn=512, out_dtype=BF16)
            w_g = jnp.pad(w_in[:, n_main:], ((0, 0), (0, 128 - 2 * H)))
            gate_pre = norm_matmul(x, norm_mix_g[i], w_g, 128, tm=tm, tn=128, out_dtype=F32)[:, :2 * H]
            args = (proj, gate_pre, mlstm_b_gates[j], mlstm_head_norm_g[j])
            yp, c_p, n_p, m_p = mlstm_mixer(
                *args, jnp.zeros((bp, H, dqk, dv), F32), jnp.zeros((bp, H, dqk), F32), jnp.zeros((bp, H), F32),
                row0=0, batch=bp, seq=tp, nseg=1, chunk=min(tp, M_CHUNK))
            ysm, c_s, n_s, m_s = mlstm_mixer(
                *args, state_mlstm_C[j], state_mlstm_n[j], state_mlstm_m[j],
                row0=np_, batch=bs, seq=ts, nseg=4, chunk=ts)
            Cp.append(c_p); Np.append(n_p); Mp.append(m_p)
            Cs.append(c_s); Ns.append(n_s); Ms.append(m_s)
            w_out = mlstm_w_out[j]
        else:
            proj = norm_matmul(x, norm_mix_g[i], hgrn_w_in[j], hgrn_w_in.shape[-1], tm=tm, tn=512, out_dtype=BF16)
            args = (proj, lb_all[i], hgrn_g_norm_g[j])
            yp, s_p = hgrn_mixer(*args, jnp.zeros((bp, hh, hdk, hdv), F32),
                                 row0=0, batch=bp, seq=tp, nseg=1, chunk=min(tp, H_CHUNK))
            ysm, s_s = hgrn_mixer(*args, state_hgrn_S[j], row0=np_, batch=bs, seq=ts, nseg=4, chunk=ts)
            Sp.append(s_p); Ss.append(s_s)
            w_out = hgrn_w_out[j]
        y = jnp.concatenate([yp, ysm], axis=0)
        x = matmul_residual(y, w_out, x, tm=512, tn=512)
        act = norm_swiglu(x, norm_ffn_g[i], ffn_w_up[i], tm=tm, tn=512)
        x = matmul_residual(act, ffn_w_down[i], x, tm=512, tn=512)

    out = final_norm(x, norm_final_g, tm=512)
    y_prompt = out[:np_].reshape(bp, tp, d)
    y_sample = out[np_:].reshape(bs, ts, d)
    return (y_prompt, y_sample, jnp.stack(Cp), jnp.stack(Np), jnp.stack(Mp), jnp.stack(Sp),
            jnp.stack(Cs), jnp.stack(Ns), jnp.stack(Ms), jnp.stack(Ss))
```

```python
import functools

import jax
import jax.numpy as jnp
from jax import lax
from jax.experimental import pallas as pl
from jax.experimental.pallas import tpu as pltpu

F32 = jnp.float32
BF16 = jnp.bfloat16

EPS = 1e-6
GATE_CAP = 15.0
M_HEADS = 8
M_CHUNK = 64
H_DK = 128
H_CHUNK = 32
H_SUB = 16

VMEM_LIMIT_BYTES = 52 * 1024 * 1024


def _params(*sem):
    return pltpu.CompilerParams(dimension_semantics=sem, vmem_limit_bytes=VMEM_LIMIT_BYTES)


def _row_tiles(xs, tm):
    for x in xs:
        assert x.shape[0] % tm == 0, (x.shape, tm)
    return tuple(x.shape[0] // tm for x in xs)


def _clamped_row_tile(first, count, row_of, col_of, *ids):
    return (jnp.clip(row_of(*ids) - first, 0, count - 1), col_of(*ids))


def _row_specs(tiles, block, row_of, col_of):
    specs, first = [], 0
    for count in tiles:
        specs.append(pl.BlockSpec(block, functools.partial(_clamped_row_tile, first, count, row_of, col_of)))
        first += count
    return specs


def _on_owner(refs, tiles, i, fn):
    if len(refs) == 1:
        fn(refs[0])
        return
    first = 0
    for ref, count in zip(refs, tiles):
        pl.when((i >= first) & (i < first + count))(functools.partial(fn, ref))
        first += count


def _rms_normed(x, g):
    ms = jnp.mean(x * x, axis=-1, keepdims=True)
    return x * lax.rsqrt(ms + EPS) * g


def _norm_matmul_kernel(tiles, with_extra, *refs):
    nx = len(tiles)
    x_refs, g_ref, w_ref = refs[:nx], refs[nx], refs[nx + 1]
    if with_extra:
        we_ref, o_ref, oe_ref, h_scr = refs[nx + 2:]
    else:
        o_ref, h_scr = refs[nx + 2:]

    @pl.when(pl.program_id(1) == 0)
    def _():
        def build(x_ref):
            h_scr[...] = _rms_normed(x_ref[...], g_ref[...]).astype(BF16)
            if with_extra:
                oe_ref[...] = jnp.dot(h_scr[...], we_ref[...].astype(BF16), preferred_element_type=F32)
        _on_owner(x_refs, tiles, pl.program_id(0), build)

    o_ref[...] = jnp.dot(h_scr[...], w_ref[...].astype(BF16),
                         preferred_element_type=F32).astype(o_ref.dtype)


def norm_matmul(xs, g, w, layer, n_cols, *, tm, tn, out_dtype, w_extra=None):
    d = xs[0].shape[1]
    tiles = _row_tiles(xs, tm)
    m = tm * sum(tiles)
    row_of, col0 = (lambda i, j: i), (lambda i, j: 0)
    in_specs = _row_specs(tiles, (tm, d), row_of, col0) + [
        pl.BlockSpec((1, d), lambda i, j: (0, 0)),
        pl.BlockSpec((None, d, tn), lambda i, j: (layer, 0, j)),
    ]
    out_specs = [pl.BlockSpec((tm, tn), lambda i, j: (i, j))]
    out_shape = [jax.ShapeDtypeStruct((m, n_cols), out_dtype)]
    args = list(xs) + [g.reshape(1, d), w]
    if w_extra is not None:
        ne = w_extra.shape[1]
        in_specs.append(pl.BlockSpec((d, ne), lambda i, j: (0, 0)))
        out_specs.append(pl.BlockSpec((tm, ne), lambda i, j: (i, 0)))
        out_shape.append(jax.ShapeDtypeStruct((m, ne), F32))
        args.append(w_extra)
    outs = pl.pallas_call(
        functools.partial(_norm_matmul_kernel, tiles, w_extra is not None),
        grid=(m // tm, n_cols // tn),
        in_specs=in_specs,
        out_specs=out_specs,
        out_shape=out_shape,
        scratch_shapes=[pltpu.VMEM((tm, d), BF16)],
        compiler_params=_params("arbitrary", "arbitrary"),
        name="norm_matmul",
    )(*args)
    return outs if w_extra is not None else outs[0]


def _norm_swiglu_kernel(x_ref, g_ref, wa_ref, wu_ref, o_ref, h_scr):
    @pl.when(pl.program_id(1) == 0)
    def _():
        h_scr[...] = _rms_normed(x_ref[...], g_ref[...]).astype(BF16)

    h = h_scr[...]
    a = jnp.dot(h, wa_ref[...].astype(BF16), preferred_element_type=F32)
    u = jnp.dot(h, wu_ref[...].astype(BF16), preferred_element_type=F32)
    o_ref[...] = (a * jax.nn.sigmoid(a) * u).astype(o_ref.dtype)


def norm_swiglu(x, g, w_up, layer, *, tm, tn):
    m, d = x.shape
    ff = w_up.shape[-1] // 2
    nj = ff // tn
    return pl.pallas_call(
        _norm_swiglu_kernel,
        grid=(m // tm, nj),
        in_specs=[
            pl.BlockSpec((tm, d), lambda i, j: (i, 0)),
            pl.BlockSpec((1, d), lambda i, j: (0, 0)),
            pl.BlockSpec((None, d, tn), lambda i, j: (layer, 0, j)),
            pl.BlockSpec((None, d, tn), lambda i, j: (layer, 0, j + nj)),
        ],
        out_specs=pl.BlockSpec((tm, tn), lambda i, j: (i, j)),
        out_shape=jax.ShapeDtypeStruct((m, ff), BF16),
        scratch_shapes=[pltpu.VMEM((tm, d), BF16)],
        compiler_params=_params("arbitrary", "arbitrary"),
        name="norm_swiglu",
    )(x, g.reshape(1, d), w_up, w_up)


def _matmul_residual_kernel(y_tiles, r_tiles, *refs):
    ny, nr = len(y_tiles), len(r_tiles)
    y_refs, w_ref, r_refs = refs[:ny], refs[ny], refs[ny + 1:ny + 1 + nr]
    o_ref, w_scr = refs[ny + 1 + nr:]
    i = pl.program_id(1)

    @pl.when(i == 0)
    def _():
        w_scr[...] = w_ref[...].astype(BF16)

    def product(y_ref):
        o_ref[...] = jnp.dot(y_ref[...], w_scr[...], preferred_element_type=F32)

    def add_residual(r_ref):
        o_ref[...] = o_ref[...] + r_ref[...]

    _on_owner(y_refs, y_tiles, i, product)
    _on_owner(r_refs, r_tiles, i, add_residual)


def matmul_residual(ys, w, layer, rs, *, tm, tn):
    k, n = w.shape[-2], w.shape[-1]
    y_tiles, r_tiles = _row_tiles(ys, tm), _row_tiles(rs, tm)
    assert sum(y_tiles) == sum(r_tiles)
    m = tm * sum(y_tiles)
    row_of = lambda j, i: i
    in_specs = (_row_specs(y_tiles, (tm, k), row_of, lambda j, i: 0)
                + [pl.BlockSpec((None, k, tn), lambda j, i: (layer, 0, j))]
                + _row_specs(r_tiles, (tm, tn), row_of, lambda j, i: j))
    return pl.pallas_call(
        functools.partial(_matmul_residual_kernel, y_tiles, r_tiles),
        grid=(n // tn, m // tm),
        in_specs=in_specs,
        out_specs=pl.BlockSpec((tm, tn), lambda j, i: (i, j)),
        out_shape=jax.ShapeDtypeStruct((m, n), F32),
        scratch_shapes=[pltpu.VMEM((k, tn), BF16)],
        compiler_params=_params("arbitrary", "arbitrary"),
        name="matmul_residual",
    )(*ys, w, *rs)


def _final_norm_kernel(x_ref, g_ref, o_ref):
    o_ref[...] = _rms_normed(x_ref[...], g_ref[...])


def final_norm(x, g, *, row0, nrows, tm):
    d = x.shape[1]
    assert row0 % tm == 0 and nrows % tm == 0
    return pl.pallas_call(
        _final_norm_kernel,
        grid=(nrows // tm,),
        in_specs=[pl.BlockSpec((tm, d), lambda i: (row0 // tm + i, 0)), pl.BlockSpec((1, d), lambda i: (0, 0))],
        out_specs=pl.BlockSpec((tm, d), lambda i: (i, 0)),
        out_shape=jax.ShapeDtypeStruct((nrows, d), F32),
        compiler_params=_params("arbitrary"),
        name="final_norm",
    )(x, g.reshape(1, d))


def _segment_masks(nseg, L):
    R = nseg * L
    r = lax.broadcasted_iota(jnp.int32, (R, R), 0)
    c = lax.broadcasted_iota(jnp.int32, (R, R), 1)
    same = (r // L) == (c // L)
    return r, c, same


def _dot_nt(a, b):
    return lax.dot_general(a, b, (((1,), (1,)), ((), ())), preferred_element_type=F32)


def _dot_tn(a, b):
    return lax.dot_general(a, b, (((0,), (0,)), ((), ())), preferred_element_type=F32)


def _log_sigmoid(x):
    return jnp.minimum(x, 0.0) - jnp.log1p(jnp.exp(-jnp.abs(x)))


def _mlstm_kernel(nseg, L, dqk, dv,
                  p_ref, gp_ref, bias_ref, hg_ref, c0_ref, n0_ref, m0_ref,
                  y_ref, c_ref, n_ref, m_ref):
    H = M_HEADS
    R = nseg * L
    s1 = H * dqk
    scale = dqk ** -0.5

    @pl.when(pl.program_id(1) == 0)
    def _():
        c_ref[...] = c0_ref[...]
        n_ref[...] = n0_ref[...]
        m_ref[...] = m0_ref[...]

    r, c, same = _segment_masks(nseg, L)
    causal = same & (c <= r)
    causal_t = same & (r <= c)
    eye = r == c
    row_seg = lax.broadcasted_iota(jnp.int32, (R, 1), 0) // L

    gates = gp_ref[:, :2 * H] + bias_ref[...]
    capped = GATE_CAP * jnp.tanh(gates / GATE_CAP)
    log_f = _log_sigmoid(capped)

    m_prev = m_ref[0, 0]
    m_new_all = m_prev
    seg_i = lax.broadcasted_iota(jnp.int32, (nseg, H), 0)
    head_i = lax.broadcasted_iota(jnp.int32, (nseg, H), 1)

    for h in range(H):
        q = p_ref[:, h * dqk:(h + 1) * dqk]
        k = p_ref[:, s1 + h * dqk:s1 + (h + 1) * dqk]
        v = p_ref[:, 2 * s1 + h * dv:2 * s1 + (h + 1) * dv]
        o = p_ref[:, 2 * s1 + H * dv + h * dv:2 * s1 + H * dv + (h + 1) * dv]
        qf = q.astype(F32)
        kf = k.astype(F32)
        vf = v.astype(F32)

        ig_col = capped[:, h:h + 1]
        lf_col = log_f[:, H + h:H + h + 1]
        lf_row = jnp.sum(jnp.where(eye, lf_col, 0.0), axis=0, keepdims=True)
        ig_row = jnp.sum(jnp.where(eye, ig_col, 0.0), axis=0, keepdims=True)
        b_col = jnp.sum(jnp.where(causal, lf_row, 0.0), axis=1, keepdims=True)
        b_row = jnp.sum(jnp.where(causal_t, lf_col, 0.0), axis=0, keepdims=True)
        b_last_col = jnp.sum(jnp.where(same, lf_row, 0.0), axis=1, keepdims=True)

        m_col = jnp.zeros((R, 1), F32)
        n_rows = jnp.zeros((R, dqk), F32)
        for s in range(nseg):
            m_s = m_prev[s:s + 1, h:h + 1]
            m_col = jnp.where(row_seg == s, m_s, m_col)
            n_rows = jnp.where(row_seg == s, n_ref[0, s, h:h + 1, :], n_rows)

        dlog = jnp.where(causal, b_col - b_row + ig_row, -jnp.inf)
        inter = b_col + m_col
        m_t = jnp.maximum(inter, jnp.max(dlog, axis=1, keepdims=True))
        w_intra = jnp.exp(dlog - m_t)
        w_inter = jnp.exp(inter - m_t)
        sc = _dot_nt(q, k) * scale * w_intra

        qc = jnp.zeros((R, dv), F32)
        for s in range(nseg):
            qc_s = jnp.dot(q, c_ref[0, s, h].astype(BF16), preferred_element_type=F32)
            qc = qc_s if nseg == 1 else jnp.where(row_seg == s, qc_s, qc)
        num = jnp.dot(sc.astype(BF16), v, preferred_element_type=F32) + w_inter * qc
        qn = jnp.sum(qf * n_rows, axis=1, keepdims=True)
        den = jnp.sum(sc, axis=1, keepdims=True) + w_inter * qn
        hc = num / jnp.maximum(jnp.abs(den), jnp.exp(-m_t))

        hn = hc * lax.rsqrt(jnp.mean(hc * hc, axis=1, keepdims=True) + EPS) * hg_ref[:, h * dv:(h + 1) * dv]
        y_ref[:, h * dv:(h + 1) * dv] = (hn * jax.nn.sigmoid(o.astype(F32))).astype(y_ref.dtype)

        g = b_last_col - b_col + ig_col
        m_new_col = jnp.zeros((R, 1), F32)
        decays = []
        for s in range(nseg):
            in_s = row_seg == s
            m_s = m_prev[s:s + 1, h:h + 1]
            b_last_s = jnp.sum(jnp.where(in_s, lf_col, 0.0), axis=0, keepdims=True)
            g_max_s = jnp.max(jnp.where(in_s, g, -jnp.inf), axis=0, keepdims=True)
            m_new_s = jnp.maximum(b_last_s + m_s, g_max_s)
            decays.append(jnp.exp(b_last_s + m_s - m_new_s))
            m_new_col = jnp.where(in_s, m_new_s, m_new_col)
            m_new_all = jnp.where((seg_i == s) & (head_i == h), m_new_s, m_new_all)
        w_k = jnp.exp(g - m_new_col) * scale
        wv = w_k * vf
        wk = w_k * kf
        for s in range(nseg):
            in_s = row_seg == s
            wv_s = wv if nseg == 1 else jnp.where(in_s, wv, 0.0)
            wk_s = wk if nseg == 1 else jnp.where(in_s, wk, 0.0)
            c_ref[0, s, h] = decays[s] * c_ref[0, s, h] + _dot_tn(k, wv_s.astype(BF16))
            n_ref[0, s, h:h + 1, :] = decays[s] * n_ref[0, s, h:h + 1, :] + jnp.sum(wk_s, axis=0, keepdims=True)

    m_ref[0, 0] = m_new_all


def mlstm_mixer(proj, gate_pre, bias, head_g, c0, n0, m0, *, row0, batch, seq, nseg, chunk):
    H = M_HEADS
    dqk, dv = c0.shape[-2], c0.shape[-1]
    R = nseg * chunk
    nchunks = seq // chunk
    assert nseg == 1 or nchunks == 1
    nblocks = batch // nseg
    blk0 = row0 // R
    rows = lambda i, c: (blk0 + i * nchunks + c, 0)
    width = proj.shape[1]
    c0 = c0.reshape(nblocks, nseg, H, dqk, dv)
    n0 = n0.reshape(nblocks, nseg, H, dqk)
    m0 = m0.reshape(nblocks, 1, nseg, H)
    y, c_out, n_out, m_out = pl.pallas_call(
        functools.partial(_mlstm_kernel, nseg, chunk, dqk, dv),
        grid=(nblocks, nchunks),
        in_specs=[
            pl.BlockSpec((R, width), rows),
            pl.BlockSpec((R, gate_pre.shape[1]), rows),
            pl.BlockSpec((1, 2 * H), lambda i, c: (0, 0)),
            pl.BlockSpec((1, H * dv), lambda i, c: (0, 0)),
            pl.BlockSpec((1, nseg, H, dqk, dv), lambda i, c: (i, 0, 0, 0, 0)),
            pl.BlockSpec((1, nseg, H, dqk), lambda i, c: (i, 0, 0, 0)),
            pl.BlockSpec((1, 1, nseg, H), lambda i, c: (i, 0, 0, 0)),
        ],
        out_specs=[
            pl.BlockSpec((R, H * dv), lambda i, c: (i * nchunks + c, 0)),
            pl.BlockSpec((1, nseg, H, dqk, dv), lambda i, c: (i, 0, 0, 0, 0)),
            pl.BlockSpec((1, nseg, H, dqk), lambda i, c: (i, 0, 0, 0)),
            pl.BlockSpec((1, 1, nseg, H), lambda i, c: (i, 0, 0, 0)),
        ],
        out_shape=[
            jax.ShapeDtypeStruct((batch * seq, H * dv), BF16),
            jax.ShapeDtypeStruct(c0.shape, F32),
            jax.ShapeDtypeStruct(n0.shape, F32),
            jax.ShapeDtypeStruct(m0.shape, F32),
        ],
        compiler_params=_params("arbitrary", "arbitrary"),
        name="mlstm_mixer",
    )(proj, gate_pre, bias.reshape(1, 2 * H), head_g.reshape(1, H * dv), c0, n0, m0)
    return (y, c_out.reshape(batch, H, dqk, dv), n_out.reshape(batch, H, dqk), m_out.reshape(batch, H))


def _hgrn_kernel(nseg, L, heads, dk, dv,
                 p_ref, lb_ref, gn_ref, s0_ref, y_ref, s_ref):
    R = nseg * L
    fdim = heads * dk
    scale = dk ** -0.5
    sub = min(L, H_SUB)
    assert L % sub == 0 and L // sub in (1, 2)

    @pl.when(pl.program_id(1) == 0)
    def _():
        s_ref[...] = s0_ref[...]

    r, c, same = _segment_masks(nseg, L)
    causal = same & (c <= r)
    same_sub = (r // sub) == (c // sub)
    diag_mask = causal & same_sub
    tril = causal.astype(F32)

    row = lax.broadcasted_iota(jnp.int32, (R, 1), 0)
    row_seg = row // L
    pos = row % L
    mid_sel = (same_sub & ((c % sub) == (sub // 2))).astype(F32)
    last_sel = (same & ((c % L) == (L - 1))).astype(F32)
    if L // sub == 2:
        bnd_sel = (same & ((c % L) == (sub - 1))).astype(F32)
        upper = pos >= sub
    hp = lax.Precision.HIGHEST

    for h in range(heads):
        q = p_ref[:, h * dk:(h + 1) * dk].astype(F32) * scale
        fpre = p_ref[:, fdim + h * dk:fdim + (h + 1) * dk].astype(F32)
        iv = p_ref[:, 2 * fdim + h * dv:2 * fdim + (h + 1) * dv]
        gate = p_ref[:, 2 * fdim + heads * dv + h * dv:2 * fdim + heads * dv + (h + 1) * dv].astype(F32)
        lb = lb_ref[:, h * dk:(h + 1) * dk]
        f = lb + (1.0 - lb) * jax.nn.sigmoid(fpre)
        log_f = jnp.log(f)
        k = 1.0 - f

        a = jnp.dot(tril, log_f, preferred_element_type=F32, precision=hp)
        a_mid = jnp.dot(mid_sel, a, preferred_element_type=F32, precision=hp)
        a_last = jnp.dot(last_sel, a, preferred_element_type=F32, precision=hp)

        qd = (q * jnp.exp(a - a_mid)).astype(BF16)
        kd = (k * jnp.exp(a_mid - a)).astype(BF16)
        sc = jnp.where(diag_mask, _dot_nt(qd, kd), 0.0)
        if L // sub == 2:
            a_bnd = jnp.dot(bnd_sel, a, preferred_element_type=F32, precision=hp)
            qo = jnp.where(upper, q * jnp.exp(jnp.minimum(a - a_bnd, 0.0)), 0.0).astype(BF16)
            ko = jnp.where(upper, 0.0, k * jnp.exp(jnp.minimum(a_bnd - a, 0.0))).astype(BF16)
            sc = sc + jnp.where(same, _dot_nt(qo, ko), 0.0)

        qe = (q * jnp.exp(a)).astype(BF16)
        inter = jnp.zeros((R, dv), F32)
        for s in range(nseg):
            inter_s = jnp.dot(qe, s_ref[0, s, h].astype(BF16), preferred_element_type=F32)
            inter = inter_s if nseg == 1 else jnp.where(row_seg == s, inter_s, inter)
        oc = jnp.dot(sc.astype(BF16), iv, preferred_element_type=F32) + inter

        on = oc * lax.rsqrt(jnp.mean(oc * oc, axis=1, keepdims=True) + EPS) * gn_ref[:, h * dv:(h + 1) * dv]
        y_ref[:, h * dv:(h + 1) * dv] = (on * (gate * jax.nn.sigmoid(gate))).astype(y_ref.dtype)

        ke = k * jnp.exp(a_last - a)
        for s in range(nseg):
            in_s = row_seg == s
            ke_s = ke if nseg == 1 else jnp.where(in_s, ke, 0.0)
            a_last_s = a[s * L + L - 1:s * L + L, :]
            upd = _dot_tn(ke_s.astype(BF16), iv)
            decay_col = jnp.sum(jnp.where(lax.broadcasted_iota(jnp.int32, (dk, dk), 0)
                                          == lax.broadcasted_iota(jnp.int32, (dk, dk), 1),
                                          jnp.exp(a_last_s), 0.0), axis=1, keepdims=True)
            s_ref[0, s, h] = decay_col * s_ref[0, s, h] + upd


def hgrn_mixer(proj, lb, gn_g, s0, *, row0, batch, seq, nseg, chunk):
    heads, dk, dv = s0.shape[-3], s0.shape[-2], s0.shape[-1]
    R = nseg * chunk
    nchunks = seq // chunk
    assert nseg == 1 or nchunks == 1
    nblocks = batch // nseg
    blk0 = row0 // R
    rows = lambda i, c: (blk0 + i * nchunks + c, 0)
    s0 = s0.reshape(nblocks, nseg, heads, dk, dv)
    y, s_out = pl.pallas_call(
        functools.partial(_hgrn_kernel, nseg, chunk, heads, dk, dv),
        grid=(nblocks, nchunks),
        in_specs=[
            pl.BlockSpec((R, proj.shape[1]), rows),
            pl.BlockSpec((1, heads * dk), lambda i, c: (0, 0)),
            pl.BlockSpec((1, heads * dv), lambda i, c: (0, 0)),
            pl.BlockSpec((1, nseg, heads, dk, dv), lambda i, c: (i, 0, 0, 0, 0)),
        ],
        out_specs=[
            pl.BlockSpec((R, heads * dv), lambda i, c: (i * nchunks + c, 0)),
            pl.BlockSpec((1, nseg, heads, dk, dv), lambda i, c: (i, 0, 0, 0, 0)),
        ],
        out_shape=[
            jax.ShapeDtypeStruct((batch * seq, heads * dv), BF16),
            jax.ShapeDtypeStruct(s0.shape, F32),
        ],
        compiler_params=_params("arbitrary", "arbitrary"),
        name="hgrn_mixer",
    )(proj, lb.reshape(1, heads * dk), gn_g.reshape(1, heads * dv), s0)
    return y, s_out.reshape(batch, heads, dk, dv)


def _hgrn_long_kernel(R, heads, dk, dv, levels,
                      p_ref, lb_ref, gn_ref, s0_ref, y_ref, s_ref,
                      st_scr, dec_scr, qd_scr, kd_scr, qe_scr, ke_scr, ql_scr, kl_scr):
    fdim = heads * dk
    scale = dk ** -0.5
    W = 2 * dk
    last_step = pl.num_programs(1) - 1

    @pl.when(pl.program_id(1) == 0)
    def _():
        for h in range(heads):
            st_scr[h] = s0_ref[0, h].T

    r = lax.broadcasted_iota(jnp.int32, (R, R), 0)
    c = lax.broadcasted_iota(jnp.int32, (R, R), 1)
    pos = lax.broadcasted_iota(jnp.int32, (R, 1), 0)
    tril = (c <= r).astype(BF16)
    diag_mask = ((r // H_SUB) == (c // H_SUB)) & (c <= r)
    level_masks = [((r // G) == (c // G)) & ((r % G) >= G // 2) & ((c % G) < G // 2) for G in levels]

    def minus_ref_row(a, G, row):
        a3 = a.reshape(R // G, G, a.shape[-1])
        return (a3 - a3[:, row:row + 1, :]).reshape(a.shape)

    for g in range(fdim // W):
        sl = slice(g * W, (g + 1) * W)
        q = p_ref[:, sl].astype(F32) * scale
        fpre = p_ref[:, fdim + g * W:fdim + (g + 1) * W].astype(F32)
        lb = lb_ref[:, sl]
        f = lb + (1.0 - lb) * jax.nn.sigmoid(fpre)
        log_f = jnp.log(f)
        k = 1.0 - f
        hi = log_f.astype(BF16)
        r1 = log_f - hi.astype(F32)
        mid = r1.astype(BF16)
        lo = (r1 - mid.astype(F32)).astype(BF16)
        a = (jnp.dot(tril, hi, preferred_element_type=F32) + jnp.dot(tril, mid, preferred_element_type=F32)
             + jnp.dot(tril, lo, preferred_element_type=F32))

        d = minus_ref_row(a, H_SUB, H_SUB // 2)
        qd_scr[:, sl] = (q * jnp.exp(d)).astype(BF16)
        kd_scr[:, sl] = (k * jnp.exp(-d)).astype(BF16)
        for li, G in enumerate(levels):
            upper = (pos % G) >= G // 2
            dl = minus_ref_row(a, G, G // 2 - 1)
            e = jnp.exp(jnp.where(upper, dl, -dl))
            ql_scr[li, :, sl] = jnp.where(upper, q * e, 0.0).astype(BF16)
            kl_scr[li, :, sl] = jnp.where(upper, 0.0, k * e).astype(BF16)
        a_last = a[R - 1:R, :]
        qe_scr[:, sl] = (q * jnp.exp(a)).astype(BF16)
        ke_scr[:, sl] = (k * jnp.exp(a_last - a)).astype(BF16)
        dec_scr[:, sl] = jnp.exp(a_last)

    for h in range(heads):
        sl = slice(h * dk, (h + 1) * dk)
        sc = jnp.where(diag_mask, _dot_nt(qd_scr[:, sl], kd_scr[:, sl]), 0.0)
        for li in range(len(levels)):
            sc = jnp.where(level_masks[li], _dot_nt(ql_scr[li, :, sl], kl_scr[li, :, sl]), sc)
        iv = p_ref[:, 2 * fdim + h * dv:2 * fdim + (h + 1) * dv]
        gate = p_ref[:, 2 * fdim + heads * dv + h * dv:2 * fdim + heads * dv + (h + 1) * dv].astype(F32)
        st = st_scr[h]
        oc = jnp.dot(sc.astype(BF16), iv, preferred_element_type=F32) + _dot_nt(qe_scr[:, sl], st.astype(BF16))
        st_scr[h] = st * dec_scr[:, sl] + _dot_tn(iv, ke_scr[:, sl])
        on = oc * lax.rsqrt(jnp.mean(oc * oc, axis=1, keepdims=True) + EPS) * gn_ref[:, h * dv:(h + 1) * dv]
        y_ref[:, h * dv:(h + 1) * dv] = (on * (gate * jax.nn.sigmoid(gate))).astype(y_ref.dtype)

    @pl.when(pl.program_id(1) == last_step)
    def _():
        for h in range(heads):
            s_ref[0, h] = st_scr[h].T


def hgrn_long_mixer(proj, lb, gn_g, s0, *, row0, batch, seq, rows):
    heads, dk, dv = s0.shape[-3], s0.shape[-2], s0.shape[-1]
    R = rows
    assert seq % R == 0 and R % (2 * H_SUB) == 0 and row0 % R == 0
    levels = []
    G = 2 * H_SUB
    while G <= R:
        levels.append(G)
        G *= 2
    assert levels[-1] == R
    nsteps = seq // R
    blk0 = row0 // R
    fdim = heads * dk
    wide = lambda: pltpu.VMEM((R, fdim), BF16)
    return pl.pallas_call(
        functools.partial(_hgrn_long_kernel, R, heads, dk, dv, tuple(levels)),
        grid=(batch, nsteps),
        in_specs=[
            pl.BlockSpec((R, proj.shape[1]), lambda i, c: (blk0 + i * nsteps + c, 0)),
            pl.BlockSpec((1, fdim), lambda i, c: (0, 0)),
            pl.BlockSpec((1, heads * dv), lambda i, c: (0, 0)),
            pl.BlockSpec((1, heads, dk, dv), lambda i, c: (i, 0, 0, 0)),
        ],
        out_specs=[
            pl.BlockSpec((R, heads * dv), lambda i, c: (i * nsteps + c, 0)),
            pl.BlockSpec((1, heads, dk, dv), lambda i, c: (i, 0, 0, 0)),
        ],
        out_shape=[
            jax.ShapeDtypeStruct((batch * seq, heads * dv), BF16),
            jax.ShapeDtypeStruct(s0.shape, F32),
        ],
        scratch_shapes=[
            pltpu.VMEM((heads, dv, dk), F32),
            pltpu.VMEM((1, fdim), F32),
            wide(), wide(), wide(), wide(),
            pltpu.VMEM((len(levels), R, fdim), BF16),
            pltpu.VMEM((len(levels), R, fdim), BF16),
        ],
        compiler_params=_params("arbitrary", "arbitrary"),
        name="hgrn_long_mixer",
    )(proj, lb.reshape(1, fdim), gn_g.reshape(1, heads * dv), s0)


def kernel(x_prompt, x_sample, state_mlstm_C, state_mlstm_n, state_mlstm_m, state_hgrn_S,
           norm_mix_g, norm_ffn_g, norm_final_g, mlstm_w_in, mlstm_b_gates, mlstm_head_norm_g,
           mlstm_w_out, hgrn_w_in, hgrn_lower_bounds, hgrn_g_norm_g, hgrn_w_out, ffn_w_up, ffn_w_down):
    bp, tp, d = x_prompt.shape
    bs, ts, _ = x_sample.shape
    np_, ns_ = bp * tp, bs * ts
    depth = norm_mix_g.shape[0]
    H = M_HEADS
    dqk, dv = state_mlstm_C.shape[-2], state_mlstm_C.shape[-1]
    hh, hdk, hdv = state_hgrn_S.shape[-3], state_hgrn_S.shape[-2], state_hgrn_S.shape[-1]

    xs = (x_prompt.reshape(np_, d), x_sample.reshape(ns_, d))

    lb_all = jnp.cumsum(jax.nn.softmax(hgrn_lower_bounds.astype(F32), axis=0), axis=0)
    lb_all = lb_all - lb_all[0]

    tm = 1024
    Cp, Np, Mp, Sp, Cs, Ns, Ms, Ss = [], [], [], [], [], [], [], []
    for i in range(depth):
        j = i // 2
        if i % 2 == 0:
            n_main = 2 * H * dqk + 2 * H * dv
            w_gates = jnp.pad(mlstm_w_in[j][:, n_main:], ((0, 0), (0, 128 - 2 * H)))
            proj, gate_pre = norm_matmul(xs, norm_mix_g[i], mlstm_w_in, j, n_main, tm=tm, tn=512,
                                         out_dtype=BF16, w_extra=w_gates)
            args = (proj, gate_pre, mlstm_b_gates[j], mlstm_head_norm_g[j])
            yp, c_p, n_p, m_p = mlstm_mixer(
                *args, jnp.zeros((bp, H, dqk, dv), F32), jnp.zeros((bp, H, dqk), F32), jnp.zeros((bp, H), F32),
                row0=0, batch=bp, seq=tp, nseg=1, chunk=min(tp, M_CHUNK))
            ysm, c_s, n_s, m_s = mlstm_mixer(
                *args, state_mlstm_C[j], state_mlstm_n[j], state_mlstm_m[j],
                row0=np_, batch=bs, seq=ts, nseg=4, chunk=ts)
            Cp.append(c_p); Np.append(n_p); Mp.append(m_p)
            Cs.append(c_s); Ns.append(n_s); Ms.append(m_s)
            w_out = mlstm_w_out
        else:
            proj = norm_matmul(xs, norm_mix_g[i], hgrn_w_in, j, hgrn_w_in.shape[-1], tm=tm, tn=512, out_dtype=BF16)
            args = (proj, lb_all[i], hgrn_g_norm_g[j])
            yp, s_p = hgrn_long_mixer(*args, jnp.zeros((bp, hh, hdk, hdv), F32),
                                      row0=0, batch=bp, seq=tp, rows=128)
            ysm, s_s = hgrn_mixer(*args, state_hgrn_S[j], row0=np_, batch=bs, seq=ts, nseg=4, chunk=ts)
            Sp.append(s_p); Ss.append(s_s)
            w_out = hgrn_w_out
        x = matmul_residual((yp, ysm), w_out, j, xs, tm=tm, tn=512)
        act = norm_swiglu(x, norm_ffn_g[i], ffn_w_up, i, tm=tm, tn=512)
        x = matmul_residual((act,), ffn_w_down, i, (x,), tm=512, tn=512)
        xs = (x,)

    y_prompt = final_norm(x, norm_final_g, row0=0, nrows=np_, tm=512).reshape(bp, tp, d)
    y_sample = final_norm(x, norm_final_g, row0=np_, nrows=ns_, tm=512).reshape(bs, ts, d)
    cat = lambda parts: jnp.stack(parts) if len(parts) > 1 else parts[0][None]
    return (y_prompt, y_sample, cat(Cp), cat(Np), cat(Mp), cat(Sp), cat(Cs), cat(Ns), cat(Ms), cat(Ss))
```

```python
import functools

import jax
import jax.numpy as jnp
from jax import lax
from jax.experimental import pallas as pl
from jax.experimental.pallas import tpu as pltpu

F32 = jnp.float32
BF16 = jnp.bfloat16

EPS = 1e-6
GATE_CAP = 15.0
M_HEADS = 8
M_CHUNK = 64
H_DK = 128
H_CHUNK = 32
H_SUB = 16

VMEM_LIMIT_BYTES = 52 * 1024 * 1024


def _params(*sem):
    return pltpu.CompilerParams(dimension_semantics=sem, vmem_limit_bytes=VMEM_LIMIT_BYTES)


def _row_tiles(xs, tm):
    for x in xs:
        assert x.shape[0] % tm == 0, (x.shape, tm)
    return tuple(x.shape[0] // tm for x in xs)


def _clamped_row_tile(first, count, row_of, col_of, *ids):
    return (jnp.clip(row_of(*ids) - first, 0, count - 1), col_of(*ids))


def _row_specs(tiles, block, row_of, col_of):
    specs, first = [], 0
    for count in tiles:
        specs.append(pl.BlockSpec(block, functools.partial(_clamped_row_tile, first, count, row_of, col_of)))
        first += count
    return specs


def _on_owner(refs, tiles, i, fn):
    if len(refs) == 1:
        fn(refs[0])
        return
    first = 0
    for ref, count in zip(refs, tiles):
        pl.when((i >= first) & (i < first + count))(functools.partial(fn, ref))
        first += count


def _rms_normed(x, g):
    ms = jnp.mean(x * x, axis=-1, keepdims=True)
    return x * lax.rsqrt(ms + EPS) * g


def _norm_matmul_kernel(tiles, with_extra, w_rows_are_outputs, *refs):
    nx = len(tiles)
    x_refs, g_ref, w_ref = refs[:nx], refs[nx], refs[nx + 1]
    if with_extra:
        we_ref, o_ref, oe_ref, h_scr = refs[nx + 2:]
    else:
        o_ref, h_scr = refs[nx + 2:]
    matmul = _dot_nt if w_rows_are_outputs else functools.partial(jnp.dot, preferred_element_type=F32)

    @pl.when(pl.program_id(1) == 0)
    def _():
        def build(x_ref):
            h_scr[...] = _rms_normed(x_ref[...], g_ref[...]).astype(BF16)
            if with_extra:
                oe_ref[...] = matmul(h_scr[...], we_ref[...].astype(BF16))
        _on_owner(x_refs, tiles, pl.program_id(0), build)

    o_ref[...] = matmul(h_scr[...], w_ref[...].astype(BF16)).astype(o_ref.dtype)


def norm_matmul(xs, g, w, layer, n_cols, *, tm, tn, out_dtype, w_extra=None, w_rows_are_outputs=False):
    d = xs[0].shape[1]
    tiles = _row_tiles(xs, tm)
    m = tm * sum(tiles)
    row_of, col0 = (lambda i, j: i), (lambda i, j: 0)
    w_spec = (pl.BlockSpec((None, tn, d), lambda i, j: (layer, j, 0)) if w_rows_are_outputs
              else pl.BlockSpec((None, d, tn), lambda i, j: (layer, 0, j)))
    in_specs = _row_specs(tiles, (tm, d), row_of, col0) + [pl.BlockSpec((1, d), lambda i, j: (0, 0)), w_spec]
    out_specs = [pl.BlockSpec((tm, tn), lambda i, j: (i, j))]
    out_shape = [jax.ShapeDtypeStruct((m, n_cols), out_dtype)]
    args = list(xs) + [g.reshape(1, d), w]
    if w_extra is not None:
        ne = w_extra.shape[0 if w_rows_are_outputs else 1]
        in_specs.append(pl.BlockSpec(w_extra.shape, lambda i, j: (0, 0)))
        out_specs.append(pl.BlockSpec((tm, ne), lambda i, j: (i, 0)))
        out_shape.append(jax.ShapeDtypeStruct((m, ne), F32))
        args.append(w_extra)
    outs = pl.pallas_call(
        functools.partial(_norm_matmul_kernel, tiles, w_extra is not None, w_rows_are_outputs),
        grid=(m // tm, n_cols // tn),
        in_specs=in_specs,
        out_specs=out_specs,
        out_shape=out_shape,
        scratch_shapes=[pltpu.VMEM((tm, d), BF16)],
        compiler_params=_params("arbitrary", "arbitrary"),
        name="norm_matmul",
    )(*args)
    return outs if w_extra is not None else outs[0]


def _norm_swiglu_kernel(x_ref, g_ref, wa_ref, wu_ref, o_ref, h_scr):
    @pl.when(pl.program_id(1) == 0)
    def _():
        h_scr[...] = _rms_normed(x_ref[...], g_ref[...]).astype(BF16)

    h = h_scr[...]
    a = jnp.dot(h, wa_ref[...].astype(BF16), preferred_element_type=F32)
    u = jnp.dot(h, wu_ref[...].astype(BF16), preferred_element_type=F32)
    o_ref[...] = (a * jax.nn.sigmoid(a) * u).astype(o_ref.dtype)


def norm_swiglu(x, g, w_up, layer, *, tm, tn):
    m, d = x.shape
    ff = w_up.shape[-1] // 2
    nj = ff // tn
    return pl.pallas_call(
        _norm_swiglu_kernel,
        grid=(m // tm, nj),
        in_specs=[
            pl.BlockSpec((tm, d), lambda i, j: (i, 0)),
            pl.BlockSpec((1, d), lambda i, j: (0, 0)),
            pl.BlockSpec((None, d, tn), lambda i, j: (layer, 0, j)),
            pl.BlockSpec((None, d, tn), lambda i, j: (layer, 0, j + nj)),
        ],
        out_specs=pl.BlockSpec((tm, tn), lambda i, j: (i, j)),
        out_shape=jax.ShapeDtypeStruct((m, ff), BF16),
        scratch_shapes=[pltpu.VMEM((tm, d), BF16)],
        compiler_params=_params("arbitrary", "arbitrary"),
        name="norm_swiglu",
    )(x, g.reshape(1, d), w_up, w_up)


def _matmul_residual_kernel(y_tiles, r_tiles, *refs):
    ny, nr = len(y_tiles), len(r_tiles)
    y_refs, w_ref, r_refs = refs[:ny], refs[ny], refs[ny + 1:ny + 1 + nr]
    o_ref, w_scr = refs[ny + 1 + nr:]
    i = pl.program_id(1)

    @pl.when(i == 0)
    def _():
        w_scr[...] = w_ref[...].astype(BF16)

    def product(y_ref):
        o_ref[...] = jnp.dot(y_ref[...], w_scr[...], preferred_element_type=F32)

    def add_residual(r_ref):
        o_ref[...] = o_ref[...] + r_ref[...]

    _on_owner(y_refs, y_tiles, i, product)
    _on_owner(r_refs, r_tiles, i, add_residual)


def matmul_residual(ys, w, layer, rs, *, tm, tn):
    k, n = w.shape[-2], w.shape[-1]
    y_tiles, r_tiles = _row_tiles(ys, tm), _row_tiles(rs, tm)
    assert sum(y_tiles) == sum(r_tiles)
    m = tm * sum(y_tiles)
    row_of = lambda j, i: i
    in_specs = (_row_specs(y_tiles, (tm, k), row_of, lambda j, i: 0)
                + [pl.BlockSpec((None, k, tn), lambda j, i: (layer, 0, j))]
                + _row_specs(r_tiles, (tm, tn), row_of, lambda j, i: j))
    return pl.pallas_call(
        functools.partial(_matmul_residual_kernel, y_tiles, r_tiles),
        grid=(n // tn, m // tm),
        in_specs=in_specs,
        out_specs=pl.BlockSpec((tm, tn), lambda j, i: (i, j)),
        out_shape=jax.ShapeDtypeStruct((m, n), F32),
        scratch_shapes=[pltpu.VMEM((k, tn), BF16)],
        compiler_params=_params("arbitrary", "arbitrary"),
        name="matmul_residual",
    )(*ys, w, *rs)


def _final_norm_kernel(x_ref, g_ref, o_ref):
    o_ref[...] = _rms_normed(x_ref[...], g_ref[...])


def final_norm(x, g, *, row0, nrows, tm):
    d = x.shape[1]
    assert row0 % tm == 0 and nrows % tm == 0
    return pl.pallas_call(
        _final_norm_kernel,
        grid=(nrows // tm,),
        in_specs=[pl.BlockSpec((tm, d), lambda i: (row0 // tm + i, 0)), pl.BlockSpec((1, d), lambda i: (0, 0))],
        out_specs=pl.BlockSpec((tm, d), lambda i: (i, 0)),
        out_shape=jax.ShapeDtypeStruct((nrows, d), F32),
        compiler_params=_params("arbitrary"),
        name="final_norm",
    )(x, g.reshape(1, d))


def _segment_masks(nseg, L):
    R = nseg * L
    r = lax.broadcasted_iota(jnp.int32, (R, R), 0)
    c = lax.broadcasted_iota(jnp.int32, (R, R), 1)
    same = (r // L) == (c // L)
    return r, c, same


def _dot_nt(a, b):
    return lax.dot_general(a, b, (((1,), (1,)), ((), ())), preferred_element_type=F32)


def _dot_tn(a, b):
    return lax.dot_general(a, b, (((0,), (0,)), ((), ())), preferred_element_type=F32)


def _log_sigmoid(x):
    return jnp.minimum(x, 0.0) - jnp.log1p(jnp.exp(-jnp.abs(x)))


def _cumsum_rows_exact(tril, x):
    hi = x.astype(BF16)
    r1 = x - hi.astype(F32)
    mid = r1.astype(BF16)
    lo = (r1 - mid.astype(F32)).astype(BF16)
    return (jnp.dot(tril, hi, preferred_element_type=F32) + jnp.dot(tril, mid, preferred_element_type=F32)
            + jnp.dot(tril, lo, preferred_element_type=F32))


def _mlstm_kernel(nseg, L, dqk, dv,
                  p_ref, gp_ref, bias_ref, hg_ref, c0_ref, n0_ref, m0_ref,
                  y_ref, c_ref, n_ref, m_ref):
    H = M_HEADS
    R = nseg * L
    s1 = H * dqk
    scale = dqk ** -0.5

    @pl.when(pl.program_id(1) == 0)
    def _():
        c_ref[...] = c0_ref[...]
        n_ref[...] = n0_ref[...]
        m_ref[...] = m0_ref[...]

    r, c, same = _segment_masks(nseg, L)
    causal = same & (c <= r)
    causal_t = same & (r <= c)
    eye = r == c
    row_seg = lax.broadcasted_iota(jnp.int32, (R, 1), 0) // L

    gates = gp_ref[:, :2 * H] + bias_ref[...]
    capped = GATE_CAP * jnp.tanh(gates / GATE_CAP)
    log_f = _log_sigmoid(capped)

    m_prev = m_ref[0, 0]
    m_new_all = m_prev
    seg_i = lax.broadcasted_iota(jnp.int32, (nseg, H), 0)
    head_i = lax.broadcasted_iota(jnp.int32, (nseg, H), 1)

    for h in range(H):
        q = p_ref[:, h * dqk:(h + 1) * dqk]
        k = p_ref[:, s1 + h * dqk:s1 + (h + 1) * dqk]
        v = p_ref[:, 2 * s1 + h * dv:2 * s1 + (h + 1) * dv]
        o = p_ref[:, 2 * s1 + H * dv + h * dv:2 * s1 + H * dv + (h + 1) * dv]
        qf = q.astype(F32)
        kf = k.astype(F32)
        vf = v.astype(F32)

        ig_col = capped[:, h:h + 1]
        lf_col = log_f[:, H + h:H + h + 1]
        lf_row = jnp.sum(jnp.where(eye, lf_col, 0.0), axis=0, keepdims=True)
        ig_row = jnp.sum(jnp.where(eye, ig_col, 0.0), axis=0, keepdims=True)
        b_col = jnp.sum(jnp.where(causal, lf_row, 0.0), axis=1, keepdims=True)
        b_row = jnp.sum(jnp.where(causal_t, lf_col, 0.0), axis=0, keepdims=True)
        b_last_col = jnp.sum(jnp.where(same, lf_row, 0.0), axis=1, keepdims=True)

        m_col = jnp.zeros((R, 1), F32)
        n_rows = jnp.zeros((R, dqk), F32)
        for s in range(nseg):
            m_s = m_prev[s:s + 1, h:h + 1]
            m_col = jnp.where(row_seg == s, m_s, m_col)
            n_rows = jnp.where(row_seg == s, n_ref[0, s, h:h + 1, :], n_rows)

        dlog = jnp.where(causal, b_col - b_row + ig_row, -jnp.inf)
        inter = b_col + m_col
        m_t = jnp.maximum(inter, jnp.max(dlog, axis=1, keepdims=True))
        w_intra = jnp.exp(dlog - m_t)
        w_inter = jnp.exp(inter - m_t)
        sc = _dot_nt(q, k) * scale * w_intra

        qc = jnp.zeros((R, dv), F32)
        for s in range(nseg):
            qc_s = jnp.dot(q, c_ref[0, s, h].astype(BF16), preferred_element_type=F32)
            qc = qc_s if nseg == 1 else jnp.where(row_seg == s, qc_s, qc)
        num = jnp.dot(sc.astype(BF16), v, preferred_element_type=F32) + w_inter * qc
        qn = jnp.sum(qf * n_rows, axis=1, keepdims=True)
        den = jnp.sum(sc, axis=1, keepdims=True) + w_inter * qn
        hc = num / jnp.maximum(jnp.abs(den), jnp.exp(-m_t))

        hn = hc * lax.rsqrt(jnp.mean(hc * hc, axis=1, keepdims=True) + EPS) * hg_ref[:, h * dv:(h + 1) * dv]
        y_ref[:, h * dv:(h + 1) * dv] = (hn * jax.nn.sigmoid(o.astype(F32))).astype(y_ref.dtype)

        g = b_last_col - b_col + ig_col
        m_new_col = jnp.zeros((R, 1), F32)
        decays = []
        for s in range(nseg):
            in_s = row_seg == s
            m_s = m_prev[s:s + 1, h:h + 1]
            b_last_s = jnp.sum(jnp.where(in_s, lf_col, 0.0), axis=0, keepdims=True)
            g_max_s = jnp.max(jnp.where(in_s, g, -jnp.inf), axis=0, keepdims=True)
            m_new_s = jnp.maximum(b_last_s + m_s, g_max_s)
            decays.append(jnp.exp(b_last_s + m_s - m_new_s))
            m_new_col = jnp.where(in_s, m_new_s, m_new_col)
            m_new_all = jnp.where((seg_i == s) & (head_i == h), m_new_s, m_new_all)
        w_k = jnp.exp(g - m_new_col) * scale
        wv = w_k * vf
        wk = w_k * kf
        for s in range(nseg):
            in_s = row_seg == s
            wv_s = wv if nseg == 1 else jnp.where(in_s, wv, 0.0)
            wk_s = wk if nseg == 1 else jnp.where(in_s, wk, 0.0)
            c_ref[0, s, h] = decays[s] * c_ref[0, s, h] + _dot_tn(k, wv_s.astype(BF16))
            n_ref[0, s, h:h + 1, :] = decays[s] * n_ref[0, s, h:h + 1, :] + jnp.sum(wk_s, axis=0, keepdims=True)

    m_ref[0, 0] = m_new_all


def mlstm_mixer(proj, gate_pre, bias, head_g, c0, n0, m0, *, row0, batch, seq, nseg, chunk):
    H = M_HEADS
    dqk, dv = c0.shape[-2], c0.shape[-1]
    R = nseg * chunk
    nchunks = seq // chunk
    assert nseg == 1 or nchunks == 1
    nblocks = batch // nseg
    blk0 = row0 // R
    rows = lambda i, c: (blk0 + i * nchunks + c, 0)
    width = proj.shape[1]
    c0 = c0.reshape(nblocks, nseg, H, dqk, dv)
    n0 = n0.reshape(nblocks, nseg, H, dqk)
    m0 = m0.reshape(nblocks, 1, nseg, H)
    y, c_out, n_out, m_out = pl.pallas_call(
        functools.partial(_mlstm_kernel, nseg, chunk, dqk, dv),
        grid=(nblocks, nchunks),
        in_specs=[
            pl.BlockSpec((R, width), rows),
            pl.BlockSpec((R, gate_pre.shape[1]), rows),
            pl.BlockSpec((1, 2 * H), lambda i, c: (0, 0)),
            pl.BlockSpec((1, H * dv), lambda i, c: (0, 0)),
            pl.BlockSpec((1, nseg, H, dqk, dv), lambda i, c: (i, 0, 0, 0, 0)),
            pl.BlockSpec((1, nseg, H, dqk), lambda i, c: (i, 0, 0, 0)),
            pl.BlockSpec((1, 1, nseg, H), lambda i, c: (i, 0, 0, 0)),
        ],
        out_specs=[
            pl.BlockSpec((R, H * dv), lambda i, c: (i * nchunks + c, 0)),
            pl.BlockSpec((1, nseg, H, dqk, dv), lambda i, c: (i, 0, 0, 0, 0)),
            pl.BlockSpec((1, nseg, H, dqk), lambda i, c: (i, 0, 0, 0)),
            pl.BlockSpec((1, 1, nseg, H), lambda i, c: (i, 0, 0, 0)),
        ],
        out_shape=[
            jax.ShapeDtypeStruct((batch * seq, H * dv), BF16),
            jax.ShapeDtypeStruct(c0.shape, F32),
            jax.ShapeDtypeStruct(n0.shape, F32),
            jax.ShapeDtypeStruct(m0.shape, F32),
        ],
        compiler_params=_params("arbitrary", "arbitrary"),
        name="mlstm_mixer",
    )(proj, gate_pre, bias.reshape(1, 2 * H), head_g.reshape(1, H * dv), c0, n0, m0)
    return (y, c_out.reshape(batch, H, dqk, dv), n_out.reshape(batch, H, dqk), m_out.reshape(batch, H))


def _mlstm_long_kernel(R, dqk, dv,
                       p_ref, gp_ref, bias_ref, hg_ref, c0_ref, n0_ref, m0_ref,
                       y_ref, c_ref, n_ref, m_ref):
    H = M_HEADS
    s1 = H * dqk
    scale = dqk ** -0.5

    @pl.when(pl.program_id(1) == 0)
    def _():
        c_ref[...] = c0_ref[...]
        n_ref[...] = n0_ref[...]
        m_ref[...] = m0_ref[...]

    r = lax.broadcasted_iota(jnp.int32, (R, R), 0)
    c = lax.broadcasted_iota(jnp.int32, (R, R), 1)
    causal = c <= r
    tril = causal.astype(BF16)

    capped = GATE_CAP * jnp.tanh((gp_ref[...] + bias_ref[...]) / GATE_CAP)
    cum = _cumsum_rows_exact(tril, _log_sigmoid(capped))
    ig_on_f = pltpu.roll(capped, H, axis=1)
    m_prev = m_ref[0]
    b_last = cum[R - 1:R, :]
    g_all = b_last - cum + ig_on_f
    m_new = jnp.maximum(b_last + m_prev, jnp.max(g_all, axis=0, keepdims=True))
    decay_all = jnp.exp(b_last + m_prev - m_new)
    wk_all = jnp.exp(g_all - m_new) * scale
    inter_all = cum + m_prev
    rows_t = (ig_on_f - cum).T

    for h in range(H):
        lane = H + h
        q = p_ref[:, h * dqk:(h + 1) * dqk]
        k = p_ref[:, s1 + h * dqk:s1 + (h + 1) * dqk]
        v = p_ref[:, 2 * s1 + h * dv:2 * s1 + (h + 1) * dv]
        o = p_ref[:, 2 * s1 + H * dv + h * dv:2 * s1 + H * dv + (h + 1) * dv]

        dlog = jnp.where(causal, cum[:, lane:lane + 1] + rows_t[lane:lane + 1, :], -jnp.inf)
        inter = inter_all[:, lane:lane + 1]
        m_t = jnp.maximum(inter, jnp.max(dlog, axis=1, keepdims=True))
        w_intra = jnp.exp(dlog - m_t)
        w_inter = jnp.exp(inter - m_t)
        sc = _dot_nt(q, k) * scale * w_intra

        c_prev = c_ref[0, h]
        n_prev = n_ref[0, h:h + 1, :]
        num = (jnp.dot(sc.astype(BF16), v, preferred_element_type=F32)
               + w_inter * jnp.dot(q, c_prev.astype(BF16), preferred_element_type=F32))
        qn = jnp.sum(q.astype(F32) * n_prev, axis=1, keepdims=True)
        den = jnp.sum(sc, axis=1, keepdims=True) + w_inter * qn
        hc = num / jnp.maximum(jnp.abs(den), jnp.exp(-m_t))
        hn = hc * lax.rsqrt(jnp.mean(hc * hc, axis=1, keepdims=True) + EPS) * hg_ref[:, h * dv:(h + 1) * dv]
        y_ref[:, h * dv:(h + 1) * dv] = (hn * jax.nn.sigmoid(o.astype(F32))).astype(y_ref.dtype)

        w_k = wk_all[:, lane:lane + 1]
        decay = decay_all[:, lane:lane + 1]
        c_ref[0, h] = decay * c_prev + _dot_tn(k, (w_k * v.astype(F32)).astype(BF16))
        n_ref[0, h:h + 1, :] = decay * n_prev + jnp.sum(w_k * k.astype(F32), axis=0, keepdims=True)

    m_ref[0] = m_new


def mlstm_long_mixer(proj, gate_pre, bias, head_g, c0, n0, m0, *, row0, batch, seq, rows):
    H = M_HEADS
    dqk, dv = c0.shape[-2], c0.shape[-1]
    R = rows
    lanes = gate_pre.shape[1]
    assert seq % R == 0 and row0 % R == 0 and lanes >= 2 * H
    nsteps = seq // R
    blk0 = row0 // R
    rows_of = lambda i, c: (blk0 + i * nsteps + c, 0)
    bias_l = jnp.pad(bias.reshape(1, 2 * H), ((0, 0), (0, lanes - 2 * H)))
    m0_l = jnp.pad(m0, ((0, 0), (H, lanes - 2 * H))).reshape(batch, 1, lanes)
    y, c_out, n_out, m_out = pl.pallas_call(
        functools.partial(_mlstm_long_kernel, R, dqk, dv),
        grid=(batch, nsteps),
        in_specs=[
            pl.BlockSpec((R, proj.shape[1]), rows_of),
            pl.BlockSpec((R, lanes), rows_of),
            pl.BlockSpec((1, lanes), lambda i, c: (0, 0)),
            pl.BlockSpec((1, H * dv), lambda i, c: (0, 0)),
            pl.BlockSpec((1, H, dqk, dv), lambda i, c: (i, 0, 0, 0)),
            pl.BlockSpec((1, H, dqk), lambda i, c: (i, 0, 0)),
            pl.BlockSpec((1, 1, lanes), lambda i, c: (i, 0, 0)),
        ],
        out_specs=[
            pl.BlockSpec((R, H * dv), lambda i, c: (i * nsteps + c, 0)),
            pl.BlockSpec((1, H, dqk, dv), lambda i, c: (i, 0, 0, 0)),
            pl.BlockSpec((1, H, dqk), lambda i, c: (i, 0, 0)),
            pl.BlockSpec((1, 1, lanes), lambda i, c: (i, 0, 0)),
        ],
        out_shape=[
            jax.ShapeDtypeStruct((batch * seq, H * dv), BF16),
            jax.ShapeDtypeStruct(c0.shape, F32),
            jax.ShapeDtypeStruct(n0.shape, F32),
            jax.ShapeDtypeStruct(m0_l.shape, F32),
        ],
        compiler_params=_params("arbitrary", "arbitrary"),
        name="mlstm_long_mixer",
    )(proj, gate_pre, bias_l, head_g.reshape(1, H * dv), c0, n0, m0_l)
    return y, c_out, n_out, m_out[:, 0, H:2 * H]


def _hgrn_kernel(nseg, L, heads, dk, dv,
                 p_ref, lb_ref, gn_ref, s0_ref, y_ref, s_ref):
    R = nseg * L
    fdim = heads * dk
    scale = dk ** -0.5
    sub = min(L, H_SUB)
    assert L % sub == 0 and L // sub in (1, 2)

    @pl.when(pl.program_id(1) == 0)
    def _():
        s_ref[...] = s0_ref[...]

    r, c, same = _segment_masks(nseg, L)
    causal = same & (c <= r)
    same_sub = (r // sub) == (c // sub)
    diag_mask = causal & same_sub
    tril = causal.astype(F32)

    row = lax.broadcasted_iota(jnp.int32, (R, 1), 0)
    row_seg = row // L
    pos = row % L
    mid_sel = (same_sub & ((c % sub) == (sub // 2))).astype(F32)
    last_sel = (same & ((c % L) == (L - 1))).astype(F32)
    if L // sub == 2:
        bnd_sel = (same & ((c % L) == (sub - 1))).astype(F32)
        upper = pos >= sub
    hp = lax.Precision.HIGHEST

    for h in range(heads):
        q = p_ref[:, h * dk:(h + 1) * dk].astype(F32) * scale
        fpre = p_ref[:, fdim + h * dk:fdim + (h + 1) * dk].astype(F32)
        iv = p_ref[:, 2 * fdim + h * dv:2 * fdim + (h + 1) * dv]
        gate = p_ref[:, 2 * fdim + heads * dv + h * dv:2 * fdim + heads * dv + (h + 1) * dv].astype(F32)
        lb = lb_ref[:, h * dk:(h + 1) * dk]
        f = lb + (1.0 - lb) * jax.nn.sigmoid(fpre)
        log_f = jnp.log(f)
        k = 1.0 - f

        a = jnp.dot(tril, log_f, preferred_element_type=F32, precision=hp)
        a_mid = jnp.dot(mid_sel, a, preferred_element_type=F32, precision=hp)
        a_last = jnp.dot(last_sel, a, preferred_element_type=F32, precision=hp)

        qd = (q * jnp.exp(a - a_mid)).astype(BF16)
        kd = (k * jnp.exp(a_mid - a)).astype(BF16)
        sc = jnp.where(diag_mask, _dot_nt(qd, kd), 0.0)
        if L // sub == 2:
            a_bnd = jnp.dot(bnd_sel, a, preferred_element_type=F32, precision=hp)
            qo = jnp.where(upper, q * jnp.exp(jnp.minimum(a - a_bnd, 0.0)), 0.0).astype(BF16)
            ko = jnp.where(upper, 0.0, k * jnp.exp(jnp.minimum(a_bnd - a, 0.0))).astype(BF16)
            sc = sc + jnp.where(same, _dot_nt(qo, ko), 0.0)

        qe = (q * jnp.exp(a)).astype(BF16)
        inter = jnp.zeros((R, dv), F32)
        for s in range(nseg):
            inter_s = jnp.dot(qe, s_ref[0, s, h].astype(BF16), preferred_element_type=F32)
            inter = inter_s if nseg == 1 else jnp.where(row_seg == s, inter_s, inter)
        oc = jnp.dot(sc.astype(BF16), iv, preferred_element_type=F32) + inter

        on = oc * lax.rsqrt(jnp.mean(oc * oc, axis=1, keepdims=True) + EPS) * gn_ref[:, h * dv:(h + 1) * dv]
        y_ref[:, h * dv:(h + 1) * dv] = (on * (gate * jax.nn.sigmoid(gate))).astype(y_ref.dtype)

        ke = k * jnp.exp(a_last - a)
        for s in range(nseg):
            in_s = row_seg == s
            ke_s = ke if nseg == 1 else jnp.where(in_s, ke, 0.0)
            a_last_s = a[s * L + L - 1:s * L + L, :]
            upd = _dot_tn(ke_s.astype(BF16), iv)
            decay_col = jnp.sum(jnp.where(lax.broadcasted_iota(jnp.int32, (dk, dk), 0)
                                          == lax.broadcasted_iota(jnp.int32, (dk, dk), 1),
                                          jnp.exp(a_last_s), 0.0), axis=1, keepdims=True)
            s_ref[0, s, h] = decay_col * s_ref[0, s, h] + upd


def hgrn_mixer(proj, lb, gn_g, s0, *, row0, batch, seq, nseg, chunk):
    heads, dk, dv = s0.shape[-3], s0.shape[-2], s0.shape[-1]
    R = nseg * chunk
    nchunks = seq // chunk
    assert nseg == 1 or nchunks == 1
    nblocks = batch // nseg
    blk0 = row0 // R
    rows = lambda i, c: (blk0 + i * nchunks + c, 0)
    s0 = s0.reshape(nblocks, nseg, heads, dk, dv)
    y, s_out = pl.pallas_call(
        functools.partial(_hgrn_kernel, nseg, chunk, heads, dk, dv),
        grid=(nblocks, nchunks),
        in_specs=[
            pl.BlockSpec((R, proj.shape[1]), rows),
            pl.BlockSpec((1, heads * dk), lambda i, c: (0, 0)),
            pl.BlockSpec((1, heads * dv), lambda i, c: (0, 0)),
            pl.BlockSpec((1, nseg, heads, dk, dv), lambda i, c: (i, 0, 0, 0, 0)),
        ],
        out_specs=[
            pl.BlockSpec((R, heads * dv), lambda i, c: (i * nchunks + c, 0)),
            pl.BlockSpec((1, nseg, heads, dk, dv), lambda i, c: (i, 0, 0, 0, 0)),
        ],
        out_shape=[
            jax.ShapeDtypeStruct((batch * seq, heads * dv), BF16),
            jax.ShapeDtypeStruct(s0.shape, F32),
        ],
        compiler_params=_params("arbitrary", "arbitrary"),
        name="hgrn_mixer",
    )(proj, lb.reshape(1, heads * dk), gn_g.reshape(1, heads * dv), s0)
    return y, s_out.reshape(batch, heads, dk, dv)


def _hgrn_short_kernel(nb, T, heads, dk, dv,
                       p_ref, lb_ref, gn_ref, s0_ref, y_ref, s_ref,
                       qd_scr, kd_scr, qe_scr, ke_scr, dec_scr):
    R = nb * T
    fdim = heads * dk
    scale = dk ** -0.5
    W = 2 * dk
    PAIR = 2 * T
    assert T % 8 == 0 and T <= H_SUB and nb % 2 == 0

    r = lax.broadcasted_iota(jnp.int32, (R, R), 0)
    c = lax.broadcasted_iota(jnp.int32, (R, R), 1)
    causal = ((r // T) == (c // T)) & (c <= r)
    tril = causal.astype(BF16)
    seq_cols = (lax.broadcasted_iota(jnp.int32, (nb, 1, R), 2) // T
                == lax.broadcasted_iota(jnp.int32, (nb, 1, R), 0)).astype(BF16)
    first_of_pair = lax.broadcasted_iota(jnp.int32, (PAIR, 1), 0) < T

    def minus_ref_row(a, row):
        a3 = a.reshape(nb, T, a.shape[-1])
        return (a3 - a3[:, row:row + 1, :]).reshape(a.shape)

    for g in range(fdim // W):
        sl = slice(g * W, (g + 1) * W)
        q = p_ref[:, sl].astype(F32) * scale
        fpre = p_ref[:, fdim + g * W:fdim + (g + 1) * W].astype(F32)
        lb = lb_ref[:, sl]
        f = lb + (1.0 - lb) * jax.nn.sigmoid(fpre)
        k = 1.0 - f
        a = _cumsum_rows_exact(tril, jnp.log(f))
        d = minus_ref_row(a, T // 2)
        to_last = -minus_ref_row(a, T - 1)
        qd_scr[:, sl] = (q * jnp.exp(d)).astype(BF16)
        kd_scr[:, sl] = (k * jnp.exp(-d)).astype(BF16)
        qe_scr[:, sl] = (q * jnp.exp(a)).astype(BF16)
        ke_scr[:, sl] = k * jnp.exp(to_last)
        dec_scr[:, sl] = jnp.exp(a + to_last)

    for h in range(heads):
        sl = slice(h * dk, (h + 1) * dk)
        iv = p_ref[:, 2 * fdim + h * dv:2 * fdim + (h + 1) * dv]
        sc = jnp.where(causal, _dot_nt(qd_scr[:, sl], kd_scr[:, sl]), 0.0)
        oc = jnp.dot(sc.astype(BF16), iv, preferred_element_type=F32)
        ke_t = ke_scr[:, sl].T.astype(BF16)
        dec_t = dec_scr[:, sl].T
        stacked = (ke_t[None] * seq_cols).reshape(nb * dk, R)
        upd = jnp.dot(stacked, iv, preferred_element_type=F32)
        gn = gn_ref[:, h * dv:(h + 1) * dv]
        for pr in range(nb // 2):
            rows = slice(pr * PAIR, (pr + 1) * PAIR)
            lhs = qe_scr[rows, sl]
            inter = []
            for b in (2 * pr, 2 * pr + 1):
                s_prev = s0_ref[0, b, h]
                inter.append(jnp.dot(lhs, s_prev.astype(BF16), preferred_element_type=F32))
                s_ref[0, b, h] = dec_t[:, b * T:b * T + 1] * s_prev + upd[b * dk:(b + 1) * dk, :]
            o2 = oc[rows, :] + jnp.where(first_of_pair, inter[0], inter[1])
            gate = p_ref[rows, 2 * fdim + heads * dv + h * dv:2 * fdim + heads * dv + (h + 1) * dv].astype(F32)
            on = o2 * lax.rsqrt(jnp.mean(o2 * o2, axis=1, keepdims=True) + EPS) * gn
            y_ref[rows, h * dv:(h + 1) * dv] = (on * (gate * jax.nn.sigmoid(gate))).astype(y_ref.dtype)


def hgrn_short_mixer(proj, lb, gn_g, s0, *, row0, batch, seq, nb):
    heads, dk, dv = s0.shape[-3], s0.shape[-2], s0.shape[-1]
    R = nb * seq
    assert batch % nb == 0 and row0 % R == 0
    nblocks = batch // nb
    blk0 = row0 // R
    fdim = heads * dk
    s0 = s0.reshape(nblocks, nb, heads, dk, dv)
    y, s_out = pl.pallas_call(
        functools.partial(_hgrn_short_kernel, nb, seq, heads, dk, dv),
        grid=(nblocks,),
        in_specs=[
            pl.BlockSpec((R, proj.shape[1]), lambda i: (blk0 + i, 0)),
            pl.BlockSpec((1, fdim), lambda i: (0, 0)),
            pl.BlockSpec((1, heads * dv), lambda i: (0, 0)),
            pl.BlockSpec((1, nb, heads, dk, dv), lambda i: (i, 0, 0, 0, 0)),
        ],
        out_specs=[
            pl.BlockSpec((R, heads * dv), lambda i: (i, 0)),
            pl.BlockSpec((1, nb, heads, dk, dv), lambda i: (i, 0, 0, 0, 0)),
        ],
        out_shape=[
            jax.ShapeDtypeStruct((batch * seq, heads * dv), BF16),
            jax.ShapeDtypeStruct(s0.shape, F32),
        ],
        scratch_shapes=[
            pltpu.VMEM((R, fdim), BF16), pltpu.VMEM((R, fdim), BF16), pltpu.VMEM((R, fdim), BF16),
            pltpu.VMEM((R, fdim), F32), pltpu.VMEM((R, fdim), F32),
        ],
        compiler_params=_params("arbitrary"),
        name="hgrn_short_mixer",
    )(proj, lb.reshape(1, fdim), gn_g.reshape(1, heads * dv), s0)
    return y, s_out.reshape(batch, heads, dk, dv)


def _hgrn_long_kernel(R, heads, dk, dv, levels,
                      p_ref, lb_ref, gn_ref, s0_ref, y_ref, s_ref,
                      st_scr, dec_scr, qd_scr, kd_scr, qe_scr, ke_scr, ql_scr, kl_scr):
    fdim = heads * dk
    scale = dk ** -0.5
    W = 2 * dk
    last_step = pl.num_programs(1) - 1

    @pl.when(pl.program_id(1) == 0)
    def _():
        for h in range(heads):
            st_scr[h] = s0_ref[0, h].T

    r = lax.broadcasted_iota(jnp.int32, (R, R), 0)
    c = lax.broadcasted_iota(jnp.int32, (R, R), 1)
    pos = lax.broadcasted_iota(jnp.int32, (R, 1), 0)
    tril = (c <= r).astype(BF16)
    diag_mask = ((r // H_SUB) == (c // H_SUB)) & (c <= r)
    level_masks = [((r // G) == (c // G)) & ((r % G) >= G // 2) & ((c % G) < G // 2) for G in levels]

    def minus_ref_row(a, G, row):
        a3 = a.reshape(R // G, G, a.shape[-1])
        return (a3 - a3[:, row:row + 1, :]).reshape(a.shape)

    for g in range(fdim // W):
        sl = slice(g * W, (g + 1) * W)
        q = p_ref[:, sl].astype(F32) * scale
        fpre = p_ref[:, fdim + g * W:fdim + (g + 1) * W].astype(F32)
        lb = lb_ref[:, sl]
        f = lb + (1.0 - lb) * jax.nn.sigmoid(fpre)
        k = 1.0 - f
        a = _cumsum_rows_exact(tril, jnp.log(f))

        d = minus_ref_row(a, H_SUB, H_SUB // 2)
        qd_scr[:, sl] = (q * jnp.exp(d)).astype(BF16)
        kd_scr[:, sl] = (k * jnp.exp(-d)).astype(BF16)
        for li, G in enumerate(levels):
            upper = (pos % G) >= G // 2
            dl = minus_ref_row(a, G, G // 2 - 1)
            e = jnp.exp(jnp.where(upper, dl, -dl))
            ql_scr[li, :, sl] = jnp.where(upper, q * e, 0.0).astype(BF16)
            kl_scr[li, :, sl] = jnp.where(upper, 0.0, k * e).astype(BF16)
        a_last = a[R - 1:R, :]
        qe_scr[:, sl] = (q * jnp.exp(a)).astype(BF16)
        ke_scr[:, sl] = (k * jnp.exp(a_last - a)).astype(BF16)
        dec_scr[:, sl] = jnp.exp(a_last)

    for h in range(heads):
        sl = slice(h * dk, (h + 1) * dk)
        sc = jnp.where(diag_mask, _dot_nt(qd_scr[:, sl], kd_scr[:, sl]), 0.0)
        for li in range(len(levels)):
            sc = jnp.where(level_masks[li], _dot_nt(ql_scr[li, :, sl], kl_scr[li, :, sl]), sc)
        iv = p_ref[:, 2 * fdim + h * dv:2 * fdim + (h + 1) * dv]
        gate = p_ref[:, 2 * fdim + heads * dv + h * dv:2 * fdim + heads * dv + (h + 1) * dv].astype(F32)
        st = st_scr[h]
        oc = jnp.dot(sc.astype(BF16), iv, preferred_element_type=F32) + _dot_nt(qe_scr[:, sl], st.astype(BF16))
        st_scr[h] = st * dec_scr[:, sl] + _dot_tn(iv, ke_scr[:, sl])
        on = oc * lax.rsqrt(jnp.mean(oc * oc, axis=1, keepdims=True) + EPS) * gn_ref[:, h * dv:(h + 1) * dv]
        y_ref[:, h * dv:(h + 1) * dv] = (on * (gate * jax.nn.sigmoid(gate))).astype(y_ref.dtype)

    @pl.when(pl.program_id(1) == last_step)
    def _():
        for h in range(heads):
            s_ref[0, h] = st_scr[h].T


def hgrn_long_mixer(proj, lb, gn_g, s0, *, row0, batch, seq, rows):
    heads, dk, dv = s0.shape[-3], s0.shape[-2], s0.shape[-1]
    R = rows
    assert seq % R == 0 and R % (2 * H_SUB) == 0 and row0 % R == 0
    levels = []
    G = 2 * H_SUB
    while G <= R:
        levels.append(G)
        G *= 2
    assert levels[-1] == R
    nsteps = seq // R
    blk0 = row0 // R
    fdim = heads * dk
    wide = lambda: pltpu.VMEM((R, fdim), BF16)
    return pl.pallas_call(
        functools.partial(_hgrn_long_kernel, R, heads, dk, dv, tuple(levels)),
        grid=(batch, nsteps),
        in_specs=[
            pl.BlockSpec((R, proj.shape[1]), lambda i, c: (blk0 + i * nsteps + c, 0)),
            pl.BlockSpec((1, fdim), lambda i, c: (0, 0)),
            pl.BlockSpec((1, heads * dv), lambda i, c: (0, 0)),
            pl.BlockSpec((1, heads, dk, dv), lambda i, c: (i, 0, 0, 0)),
        ],
        out_specs=[
            pl.BlockSpec((R, heads * dv), lambda i, c: (i * nsteps + c, 0)),
            pl.BlockSpec((1, heads, dk, dv), lambda i, c: (i, 0, 0, 0)),
        ],
        out_shape=[
            jax.ShapeDtypeStruct((batch * seq, heads * dv), BF16),
            jax.ShapeDtypeStruct(s0.shape, F32),
        ],
        scratch_shapes=[
            pltpu.VMEM((heads, dv, dk), F32),
            pltpu.VMEM((1, fdim), F32),
            wide(), wide(), wide(), wide(),
            pltpu.VMEM((len(levels), R, fdim), BF16),
            pltpu.VMEM((len(levels), R, fdim), BF16),
        ],
        compiler_params=_params("arbitrary", "arbitrary"),
        name="hgrn_long_mixer",
    )(proj, lb.reshape(1, fdim), gn_g.reshape(1, heads * dv), s0)


def kernel(x_prompt, x_sample, state_mlstm_C, state_mlstm_n, state_mlstm_m, state_hgrn_S,
           norm_mix_g, norm_ffn_g, norm_final_g, mlstm_w_in, mlstm_b_gates, mlstm_head_norm_g,
           mlstm_w_out, hgrn_w_in, hgrn_lower_bounds, hgrn_g_norm_g, hgrn_w_out, ffn_w_up, ffn_w_down):
    bp, tp, d = x_prompt.shape
    bs, ts, _ = x_sample.shape
    np_, ns_ = bp * tp, bs * ts
    depth = norm_mix_g.shape[0]
    H = M_HEADS
    dqk, dv = state_mlstm_C.shape[-2], state_mlstm_C.shape[-1]
    hh, hdk, hdv = state_hgrn_S.shape[-3], state_hgrn_S.shape[-2], state_hgrn_S.shape[-1]

    xs = (x_prompt.reshape(np_, d), x_sample.reshape(ns_, d))

    lb_all = jnp.cumsum(jax.nn.softmax(hgrn_lower_bounds.astype(F32), axis=0), axis=0)
    lb_all = lb_all - lb_all[0]

    tm = 1024
    Cp, Np, Mp, Sp, Cs, Ns, Ms, Ss = [], [], [], [], [], [], [], []
    for i in range(depth):
        j = i // 2
        if i % 2 == 0:
            n_main = 2 * H * dqk + 2 * H * dv
            w_in_t = jnp.swapaxes(mlstm_w_in, 1, 2)
            w_gates = jnp.pad(w_in_t[j, n_main:], ((0, 128 - 2 * H), (0, 0)))
            proj, gate_pre = norm_matmul(xs, norm_mix_g[i], w_in_t, j, n_main, tm=tm, tn=512,
                                         out_dtype=BF16, w_extra=w_gates, w_rows_are_outputs=True)
            args = (proj, gate_pre, mlstm_b_gates[j], mlstm_head_norm_g[j])
            yp, c_p, n_p, m_p = mlstm_long_mixer(
                *args, jnp.zeros((bp, H, dqk, dv), F32), jnp.zeros((bp, H, dqk), F32), jnp.zeros((bp, H), F32),
                row0=0, batch=bp, seq=tp, rows=256)
            ysm, c_s, n_s, m_s = mlstm_mixer(
                *args, state_mlstm_C[j], state_mlstm_n[j], state_mlstm_m[j],
                row0=np_, batch=bs, seq=ts, nseg=4, chunk=ts)
            Cp.append(c_p); Np.append(n_p); Mp.append(m_p)
            Cs.append(c_s); Ns.append(n_s); Ms.append(m_s)
            w_out = mlstm_w_out
        else:
            proj = norm_matmul(xs, norm_mix_g[i], hgrn_w_in, j, hgrn_w_in.shape[-1], tm=tm, tn=512, out_dtype=BF16)
            args = (proj, lb_all[i], hgrn_g_norm_g[j])
            yp, s_p = hgrn_long_mixer(*args, jnp.zeros((bp, hh, hdk, hdv), F32),
                                      row0=0, batch=bp, seq=tp, rows=128)
            ysm, s_s = hgrn_short_mixer(*args, state_hgrn_S[j], row0=np_, batch=bs, seq=ts, nb=8)
            Sp.append(s_p); Ss.append(s_s)
            w_out = hgrn_w_out
        x = matmul_residual((yp, ysm), w_out, j, xs, tm=tm, tn=512)
        act = norm_swiglu(x, norm_ffn_g[i], ffn_w_up, i, tm=tm, tn=512)
        x = matmul_residual((act,), ffn_w_down, i, (x,), tm=512, tn=512)
        xs = (x,)

    y_prompt = final_norm(x, norm_final_g, row0=0, nrows=np_, tm=512).reshape(bp, tp, d)
    y_sample = final_norm(x, norm_final_g, row0=np_, nrows=ns_, tm=512).reshape(bs, ts, d)
    cat = lambda parts: jnp.stack(parts) if len(parts) > 1 else parts[0][None]
    return (y_prompt, y_sample, cat(Cp), cat(Np), cat(Mp), cat(Sp), cat(Cs), cat(Ns), cat(Ms), cat(Ss))
```

```python
import functools

import jax
import jax.numpy as jnp
from jax import lax
from jax.experimental import pallas as pl
from jax.experimental.pallas import tpu as pltpu

F32 = jnp.float32
BF16 = jnp.bfloat16

EPS = 1e-6
GATE_CAP = 15.0
M_HEADS = 8
M_CHUNK = 64
H_DK = 128
H_CHUNK = 32
H_SUB = 16

VMEM_LIMIT_BYTES = 52 * 1024 * 1024


def _params(*sem):
    return pltpu.CompilerParams(dimension_semantics=sem, vmem_limit_bytes=VMEM_LIMIT_BYTES)


def _row_tiles(xs, tm):
    for x in xs:
        assert x.shape[0] % tm == 0, (x.shape, tm)
    return tuple(x.shape[0] // tm for x in xs)


def _clamped_row_tile(first, count, row_of, col_of, *ids):
    return (jnp.clip(row_of(*ids) - first, 0, count - 1), col_of(*ids))


def _row_specs(tiles, block, row_of, col_of):
    specs, first = [], 0
    for count in tiles:
        specs.append(pl.BlockSpec(block, functools.partial(_clamped_row_tile, first, count, row_of, col_of)))
        first += count
    return specs


def _on_owner(refs, tiles, i, fn):
    if len(refs) == 1:
        fn(refs[0])
        return
    first = 0
    for ref, count in zip(refs, tiles):
        pl.when((i >= first) & (i < first + count))(functools.partial(fn, ref))
        first += count


def _rms_normed(x, g):
    ms = jnp.mean(x * x, axis=-1, keepdims=True)
    return x * lax.rsqrt(ms + EPS) * g


def _norm_matmul_kernel(tiles, with_extra, w_rows_are_outputs, *refs):
    nx = len(tiles)
    x_refs, g_ref, w_ref = refs[:nx], refs[nx], refs[nx + 1]
    if with_extra:
        we_ref, o_ref, oe_ref, h_scr = refs[nx + 2:]
    else:
        o_ref, h_scr = refs[nx + 2:]
    matmul = _dot_nt if w_rows_are_outputs else functools.partial(jnp.dot, preferred_element_type=F32)

    @pl.when(pl.program_id(1) == 0)
    def _():
        def build(x_ref):
            h_scr[...] = _rms_normed(x_ref[...], g_ref[...]).astype(BF16)
            if with_extra:
                oe_ref[...] = matmul(h_scr[...], we_ref[...].astype(BF16))
        _on_owner(x_refs, tiles, pl.program_id(0), build)

    o_ref[...] = matmul(h_scr[...], w_ref[...].astype(BF16)).astype(o_ref.dtype)


def norm_matmul(xs, g, w, layer, n_cols, *, tm, tn, out_dtype, w_extra=None, w_rows_are_outputs=False):
    d = xs[0].shape[1]
    tiles = _row_tiles(xs, tm)
    m = tm * sum(tiles)
    row_of, col0 = (lambda i, j: i), (lambda i, j: 0)
    w_spec = (pl.BlockSpec((None, tn, d), lambda i, j: (layer, j, 0)) if w_rows_are_outputs
              else pl.BlockSpec((None, d, tn), lambda i, j: (layer, 0, j)))
    in_specs = _row_specs(tiles, (tm, d), row_of, col0) + [pl.BlockSpec((1, d), lambda i, j: (0, 0)), w_spec]
    out_specs = [pl.BlockSpec((tm, tn), lambda i, j: (i, j))]
    out_shape = [jax.ShapeDtypeStruct((m, n_cols), out_dtype)]
    args = list(xs) + [g.reshape(1, d), w]
    if w_extra is not None:
        ne = w_extra.shape[0 if w_rows_are_outputs else 1]
        in_specs.append(pl.BlockSpec(w_extra.shape, lambda i, j: (0, 0)))
        out_specs.append(pl.BlockSpec((tm, ne), lambda i, j: (i, 0)))
        out_shape.append(jax.ShapeDtypeStruct((m, ne), F32))
        args.append(w_extra)
    outs = pl.pallas_call(
        functools.partial(_norm_matmul_kernel, tiles, w_extra is not None, w_rows_are_outputs),
        grid=(m // tm, n_cols // tn),
        in_specs=in_specs,
        out_specs=out_specs,
        out_shape=out_shape,
        scratch_shapes=[pltpu.VMEM((tm, d), BF16)],
        compiler_params=_params("arbitrary", "arbitrary"),
        name="norm_matmul",
    )(*args)
    return outs if w_extra is not None else outs[0]


def _norm_swiglu_kernel(x_ref, g_ref, wa_ref, wu_ref, o_ref, h_scr):
    @pl.when(pl.program_id(1) == 0)
    def _():
        h_scr[...] = _rms_normed(x_ref[...], g_ref[...]).astype(BF16)

    h = h_scr[...]
    a = jnp.dot(h, wa_ref[...].astype(BF16), preferred_element_type=F32)
    u = jnp.dot(h, wu_ref[...].astype(BF16), preferred_element_type=F32)
    o_ref[...] = (a * jax.nn.sigmoid(a) * u).astype(o_ref.dtype)


def norm_swiglu(x, g, w_up, layer, *, tm, tn):
    m, d = x.shape
    ff = w_up.shape[-1] // 2
    nj = ff // tn
    return pl.pallas_call(
        _norm_swiglu_kernel,
        grid=(m // tm, nj),
        in_specs=[
            pl.BlockSpec((tm, d), lambda i, j: (i, 0)),
            pl.BlockSpec((1, d), lambda i, j: (0, 0)),
            pl.BlockSpec((None, d, tn), lambda i, j: (layer, 0, j)),
            pl.BlockSpec((None, d, tn), lambda i, j: (layer, 0, j + nj)),
        ],
        out_specs=pl.BlockSpec((tm, tn), lambda i, j: (i, j)),
        out_shape=jax.ShapeDtypeStruct((m, ff), BF16),
        scratch_shapes=[pltpu.VMEM((tm, d), BF16)],
        compiler_params=_params("arbitrary", "arbitrary"),
        name="norm_swiglu",
    )(x, g.reshape(1, d), w_up, w_up)


def _matmul_residual_kernel(y_tiles, r_tiles, *refs):
    ny, nr = len(y_tiles), len(r_tiles)
    y_refs, w_ref, r_refs = refs[:ny], refs[ny], refs[ny + 1:ny + 1 + nr]
    o_ref, w_scr = refs[ny + 1 + nr:]
    i = pl.program_id(1)

    @pl.when(i == 0)
    def _():
        w_scr[...] = w_ref[...].astype(BF16)

    def product(y_ref):
        o_ref[...] = jnp.dot(y_ref[...], w_scr[...], preferred_element_type=F32)

    def add_residual(r_ref):
        o_ref[...] = o_ref[...] + r_ref[...]

    _on_owner(y_refs, y_tiles, i, product)
    _on_owner(r_refs, r_tiles, i, add_residual)


def matmul_residual(ys, w, layer, rs, *, tm, tn):
    k, n = w.shape[-2], w.shape[-1]
    y_tiles, r_tiles = _row_tiles(ys, tm), _row_tiles(rs, tm)
    assert sum(y_tiles) == sum(r_tiles)
    m = tm * sum(y_tiles)
    row_of = lambda j, i: i
    w_mode = dict(pipeline_mode=pl.Buffered(1)) if tn == n else {}
    in_specs = (_row_specs(y_tiles, (tm, k), row_of, lambda j, i: 0)
                + [pl.BlockSpec((None, k, tn), lambda j, i: (layer, 0, j), **w_mode)]
                + _row_specs(r_tiles, (tm, tn), row_of, lambda j, i: j))
    return pl.pallas_call(
        functools.partial(_matmul_residual_kernel, y_tiles, r_tiles),
        grid=(n // tn, m // tm),
        in_specs=in_specs,
        out_specs=pl.BlockSpec((tm, tn), lambda j, i: (i, j)),
        out_shape=jax.ShapeDtypeStruct((m, n), F32),
        scratch_shapes=[pltpu.VMEM((k, tn), BF16)],
        compiler_params=_params("arbitrary", "arbitrary"),
        name="matmul_residual",
    )(*ys, w, *rs)


def _final_norm_kernel(x_ref, g_ref, o_ref):
    o_ref[...] = _rms_normed(x_ref[...], g_ref[...])


def final_norm(x, g, *, row0, nrows, tm):
    d = x.shape[1]
    assert row0 % tm == 0 and nrows % tm == 0
    return pl.pallas_call(
        _final_norm_kernel,
        grid=(nrows // tm,),
        in_specs=[pl.BlockSpec((tm, d), lambda i: (row0 // tm + i, 0)), pl.BlockSpec((1, d), lambda i: (0, 0))],
        out_specs=pl.BlockSpec((tm, d), lambda i: (i, 0)),
        out_shape=jax.ShapeDtypeStruct((nrows, d), F32),
        compiler_params=_params("arbitrary"),
        name="final_norm",
    )(x, g.reshape(1, d))


def _segment_masks(nseg, L):
    R = nseg * L
    r = lax.broadcasted_iota(jnp.int32, (R, R), 0)
    c = lax.broadcasted_iota(jnp.int32, (R, R), 1)
    same = (r // L) == (c // L)
    return r, c, same


def _dot_nt(a, b):
    return lax.dot_general(a, b, (((1,), (1,)), ((), ())), preferred_element_type=F32)


def _dot_tn(a, b):
    return lax.dot_general(a, b, (((0,), (0,)), ((), ())), preferred_element_type=F32)


def _log_sigmoid(x):
    return jnp.minimum(x, 0.0) - jnp.log1p(jnp.exp(-jnp.abs(x)))


def _cumsum_rows_exact(tril, x):
    hi = x.astype(BF16)
    r1 = x - hi.astype(F32)
    mid = r1.astype(BF16)
    lo = (r1 - mid.astype(F32)).astype(BF16)
    return (jnp.dot(tril, hi, preferred_element_type=F32) + jnp.dot(tril, mid, preferred_element_type=F32)
            + jnp.dot(tril, lo, preferred_element_type=F32))


def _mlstm_kernel(nseg, L, dqk, dv,
                  p_ref, gp_ref, bias_ref, hg_ref, c0_ref, n0_ref, m0_ref,
                  y_ref, c_ref, n_ref, m_ref):
    H = M_HEADS
    R = nseg * L
    s1 = H * dqk
    scale = dqk ** -0.5

    @pl.when(pl.program_id(1) == 0)
    def _():
        c_ref[...] = c0_ref[...]
        n_ref[...] = n0_ref[...]
        m_ref[...] = m0_ref[...]

    r, c, same = _segment_masks(nseg, L)
    causal = same & (c <= r)
    causal_t = same & (r <= c)
    eye = r == c
    row_seg = lax.broadcasted_iota(jnp.int32, (R, 1), 0) // L

    gates = gp_ref[:, :2 * H] + bias_ref[...]
    capped = GATE_CAP * jnp.tanh(gates / GATE_CAP)
    log_f = _log_sigmoid(capped)

    m_prev = m_ref[0, 0]
    m_new_all = m_prev
    seg_i = lax.broadcasted_iota(jnp.int32, (nseg, H), 0)
    head_i = lax.broadcasted_iota(jnp.int32, (nseg, H), 1)

    for h in range(H):
        q = p_ref[:, h * dqk:(h + 1) * dqk]
        k = p_ref[:, s1 + h * dqk:s1 + (h + 1) * dqk]
        v = p_ref[:, 2 * s1 + h * dv:2 * s1 + (h + 1) * dv]
        o = p_ref[:, 2 * s1 + H * dv + h * dv:2 * s1 + H * dv + (h + 1) * dv]
        qf = q.astype(F32)
        kf = k.astype(F32)
        vf = v.astype(F32)

        ig_col = capped[:, h:h + 1]
        lf_col = log_f[:, H + h:H + h + 1]
        lf_row = jnp.sum(jnp.where(eye, lf_col, 0.0), axis=0, keepdims=True)
        ig_row = jnp.sum(jnp.where(eye, ig_col, 0.0), axis=0, keepdims=True)
        b_col = jnp.sum(jnp.where(causal, lf_row, 0.0), axis=1, keepdims=True)
        b_row = jnp.sum(jnp.where(causal_t, lf_col, 0.0), axis=0, keepdims=True)
        b_last_col = jnp.sum(jnp.where(same, lf_row, 0.0), axis=1, keepdims=True)

        m_col = jnp.zeros((R, 1), F32)
        n_rows = jnp.zeros((R, dqk), F32)
        for s in range(nseg):
            m_s = m_prev[s:s + 1, h:h + 1]
            m_col = jnp.where(row_seg == s, m_s, m_col)
            n_rows = jnp.where(row_seg == s, n_ref[0, s, h:h + 1, :], n_rows)

        dlog = jnp.where(causal, b_col - b_row + ig_row, -jnp.inf)
        inter = b_col + m_col
        m_t = jnp.maximum(inter, jnp.max(dlog, axis=1, keepdims=True))
        w_intra = jnp.exp(dlog - m_t)
        w_inter = jnp.exp(inter - m_t)
        sc = _dot_nt(q, k) * scale * w_intra

        qc = jnp.zeros((R, dv), F32)
        for s in range(nseg):
            qc_s = jnp.dot(q, c_ref[0, s, h].astype(BF16), preferred_element_type=F32)
            qc = qc_s if nseg == 1 else jnp.where(row_seg == s, qc_s, qc)
        num = jnp.dot(sc.astype(BF16), v, preferred_element_type=F32) + w_inter * qc
        qn = jnp.sum(qf * n_rows, axis=1, keepdims=True)
        den = jnp.sum(sc, axis=1, keepdims=True) + w_inter * qn
        hc = num / jnp.maximum(jnp.abs(den), jnp.exp(-m_t))

        hn = hc * lax.rsqrt(jnp.mean(hc * hc, axis=1, keepdims=True) + EPS) * hg_ref[:, h * dv:(h + 1) * dv]
        y_ref[:, h * dv:(h + 1) * dv] = (hn * jax.nn.sigmoid(o.astype(F32))).astype(y_ref.dtype)

        g = b_last_col - b_col + ig_col
        m_new_col = jnp.zeros((R, 1), F32)
        decays = []
        for s in range(nseg):
            in_s = row_seg == s
            m_s = m_prev[s:s + 1, h:h + 1]
            b_last_s = jnp.sum(jnp.where(in_s, lf_col, 0.0), axis=0, keepdims=True)
            g_max_s = jnp.max(jnp.where(in_s, g, -jnp.inf), axis=0, keepdims=True)
            m_new_s = jnp.maximum(b_last_s + m_s, g_max_s)
            decays.append(jnp.exp(b_last_s + m_s - m_new_s))
            m_new_col = jnp.where(in_s, m_new_s, m_new_col)
            m_new_all = jnp.where((seg_i == s) & (head_i == h), m_new_s, m_new_all)
        w_k = jnp.exp(g - m_new_col) * scale
        wv = w_k * vf
        wk = w_k * kf
        for s in range(nseg):
            in_s = row_seg == s
            wv_s = wv if nseg == 1 else jnp.where(in_s, wv, 0.0)
            wk_s = wk if nseg == 1 else jnp.where(in_s, wk, 0.0)
            c_ref[0, s, h] = decays[s] * c_ref[0, s, h] + _dot_tn(k, wv_s.astype(BF16))
            n_ref[0, s, h:h + 1, :] = decays[s] * n_ref[0, s, h:h + 1, :] + jnp.sum(wk_s, axis=0, keepdims=True)

    m_ref[0, 0] = m_new_all


def mlstm_mixer(proj, gate_pre, bias, head_g, c0, n0, m0, *, row0, batch, seq, nseg, chunk):
    H = M_HEADS
    dqk, dv = c0.shape[-2], c0.shape[-1]
    R = nseg * chunk
    nchunks = seq // chunk
    assert nseg == 1 or nchunks == 1
    nblocks = batch // nseg
    blk0 = row0 // R
    rows = lambda i, c: (blk0 + i * nchunks + c, 0)
    width = proj.shape[1]
    c0 = c0.reshape(nblocks, nseg, H, dqk, dv)
    n0 = n0.reshape(nblocks, nseg, H, dqk)
    m0 = m0.reshape(nblocks, 1, nseg, H)
    y, c_out, n_out, m_out = pl.pallas_call(
        functools.partial(_mlstm_kernel, nseg, chunk, dqk, dv),
        grid=(nblocks, nchunks),
        in_specs=[
            pl.BlockSpec((R, width), rows),
            pl.BlockSpec((R, gate_pre.shape[1]), rows),
            pl.BlockSpec((1, 2 * H), lambda i, c: (0, 0)),
            pl.BlockSpec((1, H * dv), lambda i, c: (0, 0)),
            pl.BlockSpec((1, nseg, H, dqk, dv), lambda i, c: (i, 0, 0, 0, 0)),
            pl.BlockSpec((1, nseg, H, dqk), lambda i, c: (i, 0, 0, 0)),
            pl.BlockSpec((1, 1, nseg, H), lambda i, c: (i, 0, 0, 0)),
        ],
        out_specs=[
            pl.BlockSpec((R, H * dv), lambda i, c: (i * nchunks + c, 0)),
            pl.BlockSpec((1, nseg, H, dqk, dv), lambda i, c: (i, 0, 0, 0, 0)),
            pl.BlockSpec((1, nseg, H, dqk), lambda i, c: (i, 0, 0, 0)),
            pl.BlockSpec((1, 1, nseg, H), lambda i, c: (i, 0, 0, 0)),
        ],
        out_shape=[
            jax.ShapeDtypeStruct((batch * seq, H * dv), BF16),
            jax.ShapeDtypeStruct(c0.shape, F32),
            jax.ShapeDtypeStruct(n0.shape, F32),
            jax.ShapeDtypeStruct(m0.shape, F32),
        ],
        compiler_params=_params("arbitrary", "arbitrary"),
        name="mlstm_mixer",
    )(proj, gate_pre, bias.reshape(1, 2 * H), head_g.reshape(1, H * dv), c0, n0, m0)
    return (y, c_out.reshape(batch, H, dqk, dv), n_out.reshape(batch, H, dqk), m_out.reshape(batch, H))


def _mlstm_long_kernel(R, dqk, dv,
                       p_ref, gp_ref, bias_ref, hg_ref, c0_ref, n0_ref, m0_ref,
                       y_ref, c_ref, n_ref, m_ref):
    H = M_HEADS
    s1 = H * dqk
    scale = dqk ** -0.5

    @pl.when(pl.program_id(1) == 0)
    def _():
        c_ref[...] = c0_ref[...]
        n_ref[...] = n0_ref[...]
        m_ref[...] = m0_ref[...]

    r = lax.broadcasted_iota(jnp.int32, (R, R), 0)
    c = lax.broadcasted_iota(jnp.int32, (R, R), 1)
    causal = c <= r
    tril = causal.astype(BF16)

    capped = GATE_CAP * jnp.tanh((gp_ref[...] + bias_ref[...]) / GATE_CAP)
    cum = _cumsum_rows_exact(tril, _log_sigmoid(capped))
    ig_on_f = pltpu.roll(capped, H, axis=1)
    m_prev = m_ref[0]
    b_last = cum[R - 1:R, :]
    g_all = b_last - cum + ig_on_f
    m_new = jnp.maximum(b_last + m_prev, jnp.max(g_all, axis=0, keepdims=True))
    decay_all = jnp.exp(b_last + m_prev - m_new)
    wk_all = jnp.exp(g_all - m_new) * scale
    inter_all = cum + m_prev
    rows_t = (ig_on_f - cum).T

    for h in range(H):
        lane = H + h
        q = p_ref[:, h * dqk:(h + 1) * dqk]
        k = p_ref[:, s1 + h * dqk:s1 + (h + 1) * dqk]
        v = p_ref[:, 2 * s1 + h * dv:2 * s1 + (h + 1) * dv]
        o = p_ref[:, 2 * s1 + H * dv + h * dv:2 * s1 + H * dv + (h + 1) * dv]

        dlog = jnp.where(causal, cum[:, lane:lane + 1] + rows_t[lane:lane + 1, :], -jnp.inf)
        inter = inter_all[:, lane:lane + 1]
        m_t = jnp.maximum(inter, jnp.max(dlog, axis=1, keepdims=True))
        w_intra = jnp.exp(dlog - m_t)
        w_inter = jnp.exp(inter - m_t)
        sc = _dot_nt(q, k) * scale * w_intra

        c_prev = c_ref[0, h]
        n_prev = n_ref[0, h:h + 1, :]
        num = (jnp.dot(sc.astype(BF16), v, preferred_element_type=F32)
               + w_inter * jnp.dot(q, c_prev.astype(BF16), preferred_element_type=F32))
        qn = jnp.sum(q.astype(F32) * n_prev, axis=1, keepdims=True)
        den = jnp.sum(sc, axis=1, keepdims=True) + w_inter * qn
        hc = num / jnp.maximum(jnp.abs(den), jnp.exp(-m_t))
        hn = hc * lax.rsqrt(jnp.mean(hc * hc, axis=1, keepdims=True) + EPS) * hg_ref[:, h * dv:(h + 1) * dv]
        y_ref[:, h * dv:(h + 1) * dv] = (hn * jax.nn.sigmoid(o.astype(F32))).astype(y_ref.dtype)

        w_k = wk_all[:, lane:lane + 1]
        decay = decay_all[:, lane:lane + 1]
        c_ref[0, h] = decay * c_prev + _dot_tn(k, (w_k * v.astype(F32)).astype(BF16))
        n_ref[0, h:h + 1, :] = decay * n_prev + jnp.sum(w_k * k.astype(F32), axis=0, keepdims=True)

    m_ref[0] = m_new


def mlstm_long_mixer(proj, gate_pre, bias, head_g, c0, n0, m0, *, row0, batch, seq, rows):
    H = M_HEADS
    dqk, dv = c0.shape[-2], c0.shape[-1]
    R = rows
    lanes = gate_pre.shape[1]
    assert seq % R == 0 and row0 % R == 0 and lanes >= 2 * H
    nsteps = seq // R
    blk0 = row0 // R
    rows_of = lambda i, c: (blk0 + i * nsteps + c, 0)
    bias_l = jnp.pad(bias.reshape(1, 2 * H), ((0, 0), (0, lanes - 2 * H)))
    m0_l = jnp.pad(m0, ((0, 0), (H, lanes - 2 * H))).reshape(batch, 1, lanes)
    y, c_out, n_out, m_out = pl.pallas_call(
        functools.partial(_mlstm_long_kernel, R, dqk, dv),
        grid=(batch, nsteps),
        in_specs=[
            pl.BlockSpec((R, proj.shape[1]), rows_of),
            pl.BlockSpec((R, lanes), rows_of),
            pl.BlockSpec((1, lanes), lambda i, c: (0, 0)),
            pl.BlockSpec((1, H * dv), lambda i, c: (0, 0)),
            pl.BlockSpec((1, H, dqk, dv), lambda i, c: (i, 0, 0, 0)),
            pl.BlockSpec((1, H, dqk), lambda i, c: (i, 0, 0)),
            pl.BlockSpec((1, 1, lanes), lambda i, c: (i, 0, 0)),
        ],
        out_specs=[
            pl.BlockSpec((R, H * dv), lambda i, c: (i * nsteps + c, 0)),
            pl.BlockSpec((1, H, dqk, dv), lambda i, c: (i, 0, 0, 0)),
            pl.BlockSpec((1, H, dqk), lambda i, c: (i, 0, 0)),
            pl.BlockSpec((1, 1, lanes), lambda i, c: (i, 0, 0)),
        ],
        out_shape=[
            jax.ShapeDtypeStruct((batch * seq, H * dv), BF16),
            jax.ShapeDtypeStruct(c0.shape, F32),
            jax.ShapeDtypeStruct(n0.shape, F32),
            jax.ShapeDtypeStruct(m0_l.shape, F32),
        ],
        compiler_params=_params("arbitrary", "arbitrary"),
        name="mlstm_long_mixer",
    )(proj, gate_pre, bias_l, head_g.reshape(1, H * dv), c0, n0, m0_l)
    return y, c_out, n_out, m_out[:, 0, H:2 * H]


def _mlstm_short_kernel(nb, T, dqk, dv,
                        p_ref, gp_ref, bias_ref, hg_ref, c0_ref, n0_ref, m0_ref,
                        y_ref, c_ref, n_ref, m_ref):
    H = M_HEADS
    R = nb * T
    PAIR = 2 * T
    s1 = H * dqk
    scale = dqk ** -0.5
    assert T % 8 == 0 and nb % 2 == 0

    r = lax.broadcasted_iota(jnp.int32, (R, R), 0)
    c = lax.broadcasted_iota(jnp.int32, (R, R), 1)
    causal = ((r // T) == (c // T)) & (c <= r)
    tril = causal.astype(BF16)
    seq_cols = (lax.broadcasted_iota(jnp.int32, (nb, 1, R), 2) // T
                == lax.broadcasted_iota(jnp.int32, (nb, 1, R), 0)).astype(BF16)
    first_of_pair = lax.broadcasted_iota(jnp.int32, (PAIR, 1), 0) < T

    def per_seq(x):
        return x.reshape(nb, T, x.shape[-1])

    def seq_rows(x3):
        return jnp.broadcast_to(x3, (nb, T, x3.shape[-1])).reshape(R, x3.shape[-1])

    capped = GATE_CAP * jnp.tanh((gp_ref[...] + bias_ref[...]) / GATE_CAP)
    cum = _cumsum_rows_exact(tril, _log_sigmoid(capped))
    ig_on_f = pltpu.roll(capped, H, axis=1)
    m_prev = m0_ref[...]
    b_last = seq_rows(per_seq(cum)[:, T - 1:T, :])
    g_all = b_last - cum + ig_on_f
    m_new = jnp.maximum(b_last + m_prev, seq_rows(jnp.max(per_seq(g_all), axis=1, keepdims=True)))
    decay_all = jnp.exp(b_last + m_prev - m_new)
    wk_all = jnp.exp(g_all - m_new) * scale
    inter_all = cum + m_prev
    rows_t = (ig_on_f - cum).T

    for h in range(H):
        lane = H + h
        q = p_ref[:, h * dqk:(h + 1) * dqk]
        k = p_ref[:, s1 + h * dqk:s1 + (h + 1) * dqk]
        v = p_ref[:, 2 * s1 + h * dv:2 * s1 + (h + 1) * dv]
        o = p_ref[:, 2 * s1 + H * dv + h * dv:2 * s1 + H * dv + (h + 1) * dv]
        kf = k.astype(F32)

        dlog = jnp.where(causal, cum[:, lane:lane + 1] + rows_t[lane:lane + 1, :], -jnp.inf)
        inter = inter_all[:, lane:lane + 1]
        m_t = jnp.maximum(inter, jnp.max(dlog, axis=1, keepdims=True))
        w_intra = jnp.exp(dlog - m_t)
        w_inter = jnp.exp(inter - m_t)
        sc = _dot_nt(q, k) * scale * w_intra
        num_intra = jnp.dot(sc.astype(BF16), v, preferred_element_type=F32)
        n_prev = n0_ref[h]
        qn = jnp.sum(q.astype(F32) * n_prev, axis=1, keepdims=True)
        den = jnp.sum(sc, axis=1, keepdims=True) + w_inter * qn
        inv = 1.0 / jnp.maximum(jnp.abs(den), jnp.exp(-m_t))
        hg = hg_ref[:, h * dv:(h + 1) * dv]

        w_k = wk_all[:, lane:lane + 1]
        k_t = kf.T.astype(BF16)
        stacked = (k_t[None] * seq_cols).reshape(nb * dqk, R)
        upd = jnp.dot(stacked, (w_k * v.astype(F32)).astype(BF16), preferred_element_type=F32)
        n_ref[h] = (decay_all[:, lane:lane + 1] * n_prev
                    + seq_rows(jnp.sum(per_seq(w_k * kf), axis=1, keepdims=True)))

        for pr in range(nb // 2):
            rows = slice(pr * PAIR, (pr + 1) * PAIR)
            lhs = p_ref[rows, h * dqk:(h + 1) * dqk]
            qc = []
            for b in (2 * pr, 2 * pr + 1):
                c_prev = c0_ref[0, b, h]
                qc.append(jnp.dot(lhs, c_prev.astype(BF16), preferred_element_type=F32))
                c_ref[0, b, h] = decay_all[b * T:b * T + 1, lane:lane + 1] * c_prev + upd[b * dqk:(b + 1) * dqk, :]
            num = num_intra[rows, :] + w_inter[rows, :] * jnp.where(first_of_pair, qc[0], qc[1])
            hc = num * inv[rows, :]
            hn = hc * lax.rsqrt(jnp.mean(hc * hc, axis=1, keepdims=True) + EPS) * hg
            y_ref[rows, h * dv:(h + 1) * dv] = (hn * jax.nn.sigmoid(o[rows, :].astype(F32))).astype(y_ref.dtype)

    m_ref[...] = m_new


def mlstm_short_mixer(proj, gate_pre, bias, head_g, c0, n0, m0, *, row0, batch, seq, nb):
    H = M_HEADS
    dqk, dv = c0.shape[-2], c0.shape[-1]
    R = nb * seq
    lanes = gate_pre.shape[1]
    assert batch % nb == 0 and row0 % R == 0 and lanes >= 2 * H
    nblocks = batch // nb
    blk0 = row0 // R
    bias_l = jnp.pad(bias.reshape(1, 2 * H), ((0, 0), (0, lanes - 2 * H)))
    m0_l = jnp.repeat(jnp.pad(m0, ((0, 0), (H, lanes - 2 * H))), seq, axis=0)
    n0_t = jnp.repeat(jnp.swapaxes(n0, 0, 1), seq, axis=1)
    y, c_out, n_out, m_out = pl.pallas_call(
        functools.partial(_mlstm_short_kernel, nb, seq, dqk, dv),
        grid=(nblocks,),
        in_specs=[
            pl.BlockSpec((R, proj.shape[1]), lambda i: (blk0 + i, 0)),
            pl.BlockSpec((R, lanes), lambda i: (blk0 + i, 0)),
            pl.BlockSpec((1, lanes), lambda i: (0, 0)),
            pl.BlockSpec((1, H * dv), lambda i: (0, 0)),
            pl.BlockSpec((1, nb, H, dqk, dv), lambda i: (i, 0, 0, 0, 0)),
            pl.BlockSpec((H, R, dqk), lambda i: (0, i, 0)),
            pl.BlockSpec((R, lanes), lambda i: (i, 0)),
        ],
        out_specs=[
            pl.BlockSpec((R, H * dv), lambda i: (i, 0)),
            pl.BlockSpec((1, nb, H, dqk, dv), lambda i: (i, 0, 0, 0, 0)),
            pl.BlockSpec((H, R, dqk), lambda i: (0, i, 0)),
            pl.BlockSpec((R, lanes), lambda i: (i, 0)),
        ],
        out_shape=[
            jax.ShapeDtypeStruct((batch * seq, H * dv), BF16),
            jax.ShapeDtypeStruct((nblocks, nb, H, dqk, dv), F32),
            jax.ShapeDtypeStruct(n0_t.shape, F32),
            jax.ShapeDtypeStruct(m0_l.shape, F32),
        ],
        compiler_params=_params("arbitrary"),
        name="mlstm_short_mixer",
    )(proj, gate_pre, bias_l, head_g.reshape(1, H * dv), c0.reshape(nblocks, nb, H, dqk, dv), n0_t, m0_l)
    return (y, c_out.reshape(batch, H, dqk, dv), jnp.swapaxes(n_out[:, ::seq, :], 0, 1), m_out[::seq, H:2 * H])


def _hgrn_kernel(nseg, L, heads, dk, dv,
                 p_ref, lb_ref, gn_ref, s0_ref, y_ref, s_ref):
    R = nseg * L
    fdim = heads * dk
    scale = dk ** -0.5
    sub = min(L, H_SUB)
    assert L % sub == 0 and L // sub in (1, 2)

    @pl.when(pl.program_id(1) == 0)
    def _():
        s_ref[...] = s0_ref[...]

    r, c, same = _segment_masks(nseg, L)
    causal = same & (c <= r)
    same_sub = (r // sub) == (c // sub)
    diag_mask = causal & same_sub
    tril = causal.astype(F32)

    row = lax.broadcasted_iota(jnp.int32, (R, 1), 0)
    row_seg = row // L
    pos = row % L
    mid_sel = (same_sub & ((c % sub) == (sub // 2))).astype(F32)
    last_sel = (same & ((c % L) == (L - 1))).astype(F32)
    if L // sub == 2:
        bnd_sel = (same & ((c % L) == (sub - 1))).astype(F32)
        upper = pos >= sub
    hp = lax.Precision.HIGHEST

    for h in range(heads):
        q = p_ref[:, h * dk:(h + 1) * dk].astype(F32) * scale
        fpre = p_ref[:, fdim + h * dk:fdim + (h + 1) * dk].astype(F32)
        iv = p_ref[:, 2 * fdim + h * dv:2 * fdim + (h + 1) * dv]
        gate = p_ref[:, 2 * fdim + heads * dv + h * dv:2 * fdim + heads * dv + (h + 1) * dv].astype(F32)
        lb = lb_ref[:, h * dk:(h + 1) * dk]
        f = lb + (1.0 - lb) * jax.nn.sigmoid(fpre)
        log_f = jnp.log(f)
        k = 1.0 - f

        a = jnp.dot(tril, log_f, preferred_element_type=F32, precision=hp)
        a_mid = jnp.dot(mid_sel, a, preferred_element_type=F32, precision=hp)
        a_last = jnp.dot(last_sel, a, preferred_element_type=F32, precision=hp)

        qd = (q * jnp.exp(a - a_mid)).astype(BF16)
        kd = (k * jnp.exp(a_mid - a)).astype(BF16)
        sc = jnp.where(diag_mask, _dot_nt(qd, kd), 0.0)
        if L // sub == 2:
            a_bnd = jnp.dot(bnd_sel, a, preferred_element_type=F32, precision=hp)
            qo = jnp.where(upper, q * jnp.exp(jnp.minimum(a - a_bnd, 0.0)), 0.0).astype(BF16)
            ko = jnp.where(upper, 0.0, k * jnp.exp(jnp.minimum(a_bnd - a, 0.0))).astype(BF16)
            sc = sc + jnp.where(same, _dot_nt(qo, ko), 0.0)

        qe = (q * jnp.exp(a)).astype(BF16)
        inter = jnp.zeros((R, dv), F32)
        for s in range(nseg):
            inter_s = jnp.dot(qe, s_ref[0, s, h].astype(BF16), preferred_element_type=F32)
            inter = inter_s if nseg == 1 else jnp.where(row_seg == s, inter_s, inter)
        oc = jnp.dot(sc.astype(BF16), iv, preferred_element_type=F32) + inter

        on = oc * lax.rsqrt(jnp.mean(oc * oc, axis=1, keepdims=True) + EPS) * gn_ref[:, h * dv:(h + 1) * dv]
        y_ref[:, h * dv:(h + 1) * dv] = (on * (gate * jax.nn.sigmoid(gate))).astype(y_ref.dtype)

        ke = k * jnp.exp(a_last - a)
        for s in range(nseg):
            in_s = row_seg == s
            ke_s = ke if nseg == 1 else jnp.where(in_s, ke, 0.0)
            a_last_s = a[s * L + L - 1:s * L + L, :]
            upd = _dot_tn(ke_s.astype(BF16), iv)
            decay_col = jnp.sum(jnp.where(lax.broadcasted_iota(jnp.int32, (dk, dk), 0)
                                          == lax.broadcasted_iota(jnp.int32, (dk, dk), 1),
                                          jnp.exp(a_last_s), 0.0), axis=1, keepdims=True)
            s_ref[0, s, h] = decay_col * s_ref[0, s, h] + upd


def hgrn_mixer(proj, lb, gn_g, s0, *, row0, batch, seq, nseg, chunk):
    heads, dk, dv = s0.shape[-3], s0.shape[-2], s0.shape[-1]
    R = nseg * chunk
    nchunks = seq // chunk
    assert nseg == 1 or nchunks == 1
    nblocks = batch // nseg
    blk0 = row0 // R
    rows = lambda i, c: (blk0 + i * nchunks + c, 0)
    s0 = s0.reshape(nblocks, nseg, heads, dk, dv)
    y, s_out = pl.pallas_call(
        functools.partial(_hgrn_kernel, nseg, chunk, heads, dk, dv),
        grid=(nblocks, nchunks),
        in_specs=[
            pl.BlockSpec((R, proj.shape[1]), rows),
            pl.BlockSpec((1, heads * dk), lambda i, c: (0, 0)),
            pl.BlockSpec((1, heads * dv), lambda i, c: (0, 0)),
            pl.BlockSpec((1, nseg, heads, dk, dv), lambda i, c: (i, 0, 0, 0, 0)),
        ],
        out_specs=[
            pl.BlockSpec((R, heads * dv), lambda i, c: (i * nchunks + c, 0)),
            pl.BlockSpec((1, nseg, heads, dk, dv), lambda i, c: (i, 0, 0, 0, 0)),
        ],
        out_shape=[
            jax.ShapeDtypeStruct((batch * seq, heads * dv), BF16),
            jax.ShapeDtypeStruct(s0.shape, F32),
        ],
        compiler_params=_params("arbitrary", "arbitrary"),
        name="hgrn_mixer",
    )(proj, lb.reshape(1, heads * dk), gn_g.reshape(1, heads * dv), s0)
    return y, s_out.reshape(batch, heads, dk, dv)


def _hgrn_short_kernel(nb, T, heads, dk, dv,
                       p_ref, lb_ref, gn_ref, s0_ref, y_ref, s_ref,
                       qd_scr, kd_scr, qe_scr, ke_scr, dec_scr):
    R = nb * T
    fdim = heads * dk
    scale = dk ** -0.5
    W = 2 * dk
    PAIR = 2 * T
    assert T % 8 == 0 and T <= H_SUB and nb % 2 == 0

    r = lax.broadcasted_iota(jnp.int32, (R, R), 0)
    c = lax.broadcasted_iota(jnp.int32, (R, R), 1)
    causal = ((r // T) == (c // T)) & (c <= r)
    tril = causal.astype(BF16)
    seq_cols = (lax.broadcasted_iota(jnp.int32, (nb, 1, R), 2) // T
                == lax.broadcasted_iota(jnp.int32, (nb, 1, R), 0)).astype(BF16)
    first_of_pair = lax.broadcasted_iota(jnp.int32, (PAIR, 1), 0) < T

    def minus_ref_row(a, row):
        a3 = a.reshape(nb, T, a.shape[-1])
        return (a3 - a3[:, row:row + 1, :]).reshape(a.shape)

    for g in range(fdim // W):
        sl = slice(g * W, (g + 1) * W)
        q = p_ref[:, sl].astype(F32) * scale
        fpre = p_ref[:, fdim + g * W:fdim + (g + 1) * W].astype(F32)
        lb = lb_ref[:, sl]
        f = lb + (1.0 - lb) * jax.nn.sigmoid(fpre)
        k = 1.0 - f
        a = _cumsum_rows_exact(tril, jnp.log(f))
        d = minus_ref_row(a, T // 2)
        to_last = -minus_ref_row(a, T - 1)
        qd_scr[:, sl] = (q * jnp.exp(d)).astype(BF16)
        kd_scr[:, sl] = (k * jnp.exp(-d)).astype(BF16)
        qe_scr[:, sl] = (q * jnp.exp(a)).astype(BF16)
        ke_scr[:, sl] = k * jnp.exp(to_last)
        dec_scr[:, sl] = jnp.exp(a + to_last)

    for h in range(heads):
        sl = slice(h * dk, (h + 1) * dk)
        iv = p_ref[:, 2 * fdim + h * dv:2 * fdim + (h + 1) * dv]
        sc = jnp.where(causal, _dot_nt(qd_scr[:, sl], kd_scr[:, sl]), 0.0)
        oc = jnp.dot(sc.astype(BF16), iv, preferred_element_type=F32)
        ke_t = ke_scr[:, sl].T.astype(BF16)
        dec_t = dec_scr[:, sl].T
        stacked = (ke_t[None] * seq_cols).reshape(nb * dk, R)
        upd = jnp.dot(stacked, iv, preferred_element_type=F32)
        gn = gn_ref[:, h * dv:(h + 1) * dv]
        for pr in range(nb // 2):
            rows = slice(pr * PAIR, (pr + 1) * PAIR)
            lhs = qe_scr[rows, sl]
            inter = []
            for b in (2 * pr, 2 * pr + 1):
                s_prev = s0_ref[0, b, h]
                inter.append(jnp.dot(lhs, s_prev.astype(BF16), preferred_element_type=F32))
                s_ref[0, b, h] = dec_t[:, b * T:b * T + 1] * s_prev + upd[b * dk:(b + 1) * dk, :]
            o2 = oc[rows, :] + jnp.where(first_of_pair, inter[0], inter[1])
            gate = p_ref[rows, 2 * fdim + heads * dv + h * dv:2 * fdim + heads * dv + (h + 1) * dv].astype(F32)
            on = o2 * lax.rsqrt(jnp.mean(o2 * o2, axis=1, keepdims=True) + EPS) * gn
            y_ref[rows, h * dv:(h + 1) * dv] = (on * (gate * jax.nn.sigmoid(gate))).astype(y_ref.dtype)


def hgrn_short_mixer(proj, lb, gn_g, s0, *, row0, batch, seq, nb):
    heads, dk, dv = s0.shape[-3], s0.shape[-2], s0.shape[-1]
    R = nb * seq
    assert batch % nb == 0 and row0 % R == 0
    nblocks = batch // nb
    blk0 = row0 // R
    fdim = heads * dk
    s0 = s0.reshape(nblocks, nb, heads, dk, dv)
    y, s_out = pl.pallas_call(
        functools.partial(_hgrn_short_kernel, nb, seq, heads, dk, dv),
        grid=(nblocks,),
        in_specs=[
            pl.BlockSpec((R, proj.shape[1]), lambda i: (blk0 + i, 0)),
            pl.BlockSpec((1, fdim), lambda i: (0, 0)),
            pl.BlockSpec((1, heads * dv), lambda i: (0, 0)),
            pl.BlockSpec((1, nb, heads, dk, dv), lambda i: (i, 0, 0, 0, 0)),
        ],
        out_specs=[
            pl.BlockSpec((R, heads * dv), lambda i: (i, 0)),
            pl.BlockSpec((1, nb, heads, dk, dv), lambda i: (i, 0, 0, 0, 0)),
        ],
        out_shape=[
            jax.ShapeDtypeStruct((batch * seq, heads * dv), BF16),
            jax.ShapeDtypeStruct(s0.shape, F32),
        ],
        scratch_shapes=[
            pltpu.VMEM((R, fdim), BF16), pltpu.VMEM((R, fdim), BF16), pltpu.VMEM((R, fdim), BF16),
            pltpu.VMEM((R, fdim), F32), pltpu.VMEM((R, fdim), F32),
        ],
        compiler_params=_params("arbitrary"),
        name="hgrn_short_mixer",
    )(proj, lb.reshape(1, fdim), gn_g.reshape(1, heads * dv), s0)
    return y, s_out.reshape(batch, heads, dk, dv)


def _hgrn_long_kernel(R, heads, dk, dv, levels,
                      p_ref, lb_ref, gn_ref, s0_ref, y_ref, s_ref,
                      st_scr, dec_scr, qd_scr, kd_scr, qe_scr, ke_scr, ql_scr, kl_scr):
    fdim = heads * dk
    scale = dk ** -0.5
    W = 2 * dk
    last_step = pl.num_programs(1) - 1

    @pl.when(pl.program_id(1) == 0)
    def _():
        for h in range(heads):
            st_scr[h] = s0_ref[0, h].T

    r = lax.broadcasted_iota(jnp.int32, (R, R), 0)
    c = lax.broadcasted_iota(jnp.int32, (R, R), 1)
    pos = lax.broadcasted_iota(jnp.int32, (R, 1), 0)
    tril = (c <= r).astype(BF16)
    diag_mask = ((r // H_SUB) == (c // H_SUB)) & (c <= r)
    level_masks = [((r // G) == (c // G)) & ((r % G) >= G // 2) & ((c % G) < G // 2) for G in levels]

    def minus_ref_row(a, G, row):
        a3 = a.reshape(R // G, G, a.shape[-1])
        return (a3 - a3[:, row:row + 1, :]).reshape(a.shape)

    for g in range(fdim // W):
        sl = slice(g * W, (g + 1) * W)
        q = p_ref[:, sl].astype(F32) * scale
        fpre = p_ref[:, fdim + g * W:fdim + (g + 1) * W].astype(F32)
        lb = lb_ref[:, sl]
        f = lb + (1.0 - lb) * jax.nn.sigmoid(fpre)
        k = 1.0 - f
        a = _cumsum_rows_exact(tril, jnp.log(f))

        d = minus_ref_row(a, H_SUB, H_SUB // 2)
        qd_scr[:, sl] = (q * jnp.exp(d)).astype(BF16)
        kd_scr[:, sl] = (k * jnp.exp(-d)).astype(BF16)
        for li, G in enumerate(levels):
            upper = (pos % G) >= G // 2
            dl = minus_ref_row(a, G, G // 2 - 1)
            e = jnp.exp(jnp.where(upper, dl, -dl))
            ql_scr[li, :, sl] = (q * e).astype(BF16)
            kl_scr[li, :, sl] = (k * e).astype(BF16)
        a_last = a[R - 1:R, :]
        qe_scr[:, sl] = (q * jnp.exp(a)).astype(BF16)
        ke_scr[:, sl] = (k * jnp.exp(a_last - a)).astype(BF16)
        dec_scr[:, sl] = jnp.exp(a_last)

    for h in range(heads):
        sl = slice(h * dk, (h + 1) * dk)
        sc = jnp.where(diag_mask, _dot_nt(qd_scr[:, sl], kd_scr[:, sl]), 0.0)
        for li in range(len(levels)):
            sc = jnp.where(level_masks[li], _dot_nt(ql_scr[li, :, sl], kl_scr[li, :, sl]), sc)
        iv = p_ref[:, 2 * fdim + h * dv:2 * fdim + (h + 1) * dv]
        gate = p_ref[:, 2 * fdim + heads * dv + h * dv:2 * fdim + heads * dv + (h + 1) * dv].astype(F32)
        st = st_scr[h]
        oc = jnp.dot(sc.astype(BF16), iv, preferred_element_type=F32) + _dot_nt(qe_scr[:, sl], st.astype(BF16))
        st_scr[h] = st * dec_scr[:, sl] + _dot_tn(iv, ke_scr[:, sl])
        on = oc * lax.rsqrt(jnp.mean(oc * oc, axis=1, keepdims=True) + EPS) * gn_ref[:, h * dv:(h + 1) * dv]
        y_ref[:, h * dv:(h + 1) * dv] = (on * (gate * jax.nn.sigmoid(gate))).astype(y_ref.dtype)

    @pl.when(pl.program_id(1) == last_step)
    def _():
        for h in range(heads):
            s_ref[0, h] = st_scr[h].T


def hgrn_long_mixer(proj, lb, gn_g, s0, *, row0, batch, seq, rows):
    heads, dk, dv = s0.shape[-3], s0.shape[-2], s0.shape[-1]
    R = rows
    assert seq % R == 0 and R % (2 * H_SUB) == 0 and row0 % R == 0
    levels = []
    G = 2 * H_SUB
    while G <= R:
        levels.append(G)
        G *= 2
    assert levels[-1] == R
    nsteps = seq // R
    blk0 = row0 // R
    fdim = heads * dk
    wide = lambda: pltpu.VMEM((R, fdim), BF16)
    return pl.pallas_call(
        functools.partial(_hgrn_long_kernel, R, heads, dk, dv, tuple(levels)),
        grid=(batch, nsteps),
        in_specs=[
            pl.BlockSpec((R, proj.shape[1]), lambda i, c: (blk0 + i * nsteps + c, 0)),
            pl.BlockSpec((1, fdim), lambda i, c: (0, 0)),
            pl.BlockSpec((1, heads * dv), lambda i, c: (0, 0)),
            pl.BlockSpec((1, heads, dk, dv), lambda i, c: (i, 0, 0, 0)),
        ],
        out_specs=[
            pl.BlockSpec((R, heads * dv), lambda i, c: (i * nsteps + c, 0)),
            pl.BlockSpec((1, heads, dk, dv), lambda i, c: (i, 0, 0, 0)),
        ],
        out_shape=[
            jax.ShapeDtypeStruct((batch * seq, heads * dv), BF16),
            jax.ShapeDtypeStruct(s0.shape, F32),
        ],
        scratch_shapes=[
            pltpu.VMEM((heads, dv, dk), F32),
            pltpu.VMEM((1, fdim), F32),
            wide(), wide(), wide(), wide(),
            pltpu.VMEM((len(levels), R, fdim), BF16),
            pltpu.VMEM((len(levels), R, fdim), BF16),
        ],
        compiler_params=_params("arbitrary", "arbitrary"),
        name="hgrn_long_mixer",
    )(proj, lb.reshape(1, fdim), gn_g.reshape(1, heads * dv), s0)


def kernel(x_prompt, x_sample, state_mlstm_C, state_mlstm_n, state_mlstm_m, state_hgrn_S,
           norm_mix_g, norm_ffn_g, norm_final_g, mlstm_w_in, mlstm_b_gates, mlstm_head_norm_g,
           mlstm_w_out, hgrn_w_in, hgrn_lower_bounds, hgrn_g_norm_g, hgrn_w_out, ffn_w_up, ffn_w_down):
    bp, tp, d = x_prompt.shape
    bs, ts, _ = x_sample.shape
    np_, ns_ = bp * tp, bs * ts
    depth = norm_mix_g.shape[0]
    H = M_HEADS
    dqk, dv = state_mlstm_C.shape[-2], state_mlstm_C.shape[-1]
    hh, hdk, hdv = state_hgrn_S.shape[-3], state_hgrn_S.shape[-2], state_hgrn_S.shape[-1]

    xs = (x_prompt.reshape(np_, d), x_sample.reshape(ns_, d))

    lb_all = jnp.cumsum(jax.nn.softmax(hgrn_lower_bounds.astype(F32), axis=0), axis=0)
    lb_all = lb_all - lb_all[0]

    tm = 1024
    Cp, Np, Mp, Sp, Cs, Ns, Ms, Ss = [], [], [], [], [], [], [], []
    for i in range(depth):
        j = i // 2
        if i % 2 == 0:
            n_main = 2 * H * dqk + 2 * H * dv
            w_in_t = jnp.swapaxes(mlstm_w_in, 1, 2)
            w_gates = jnp.pad(w_in_t[j, n_main:], ((0, 128 - 2 * H), (0, 0)))
            proj, gate_pre = norm_matmul(xs, norm_mix_g[i], w_in_t, j, n_main, tm=tm, tn=512,
                                         out_dtype=BF16, w_extra=w_gates, w_rows_are_outputs=True)
            args = (proj, gate_pre, mlstm_b_gates[j], mlstm_head_norm_g[j])
            yp, c_p, n_p, m_p = mlstm_long_mixer(
                *args, jnp.zeros((bp, H, dqk, dv), F32), jnp.zeros((bp, H, dqk), F32), jnp.zeros((bp, H), F32),
                row0=0, batch=bp, seq=tp, rows=256)
            ysm, c_s, n_s, m_s = mlstm_short_mixer(
                *args, state_mlstm_C[j], state_mlstm_n[j], state_mlstm_m[j],
                row0=np_, batch=bs, seq=ts, nb=8)
            Cp.append(c_p); Np.append(n_p); Mp.append(m_p)
            Cs.append(c_s); Ns.append(n_s); Ms.append(m_s)
            w_out = mlstm_w_out
        else:
            proj = norm_matmul(xs, norm_mix_g[i], hgrn_w_in, j, hgrn_w_in.shape[-1], tm=tm, tn=1024, out_dtype=BF16)
            args = (proj, lb_all[i], hgrn_g_norm_g[j])
            yp, s_p = hgrn_long_mixer(*args, jnp.zeros((bp, hh, hdk, hdv), F32),
                                      row0=0, batch=bp, seq=tp, rows=128)
            ysm, s_s = hgrn_short_mixer(*args, state_hgrn_S[j], row0=np_, batch=bs, seq=ts, nb=8)
            Sp.append(s_p); Ss.append(s_s)
            w_out = hgrn_w_out
        x = matmul_residual((yp, ysm), w_out, j, xs, tm=256, tn=d)
        act = norm_swiglu(x, norm_ffn_g[i], ffn_w_up, i, tm=tm, tn=512)
        x = matmul_residual((act,), ffn_w_down, i, (x,), tm=512, tn=512)
        xs = (x,)

    y_prompt = final_norm(x, norm_final_g, row0=0, nrows=np_, tm=512).reshape(bp, tp, d)
    y_sample = final_norm(x, norm_final_g, row0=np_, nrows=ns_, tm=512).reshape(bs, ts, d)
    cat = lambda parts: jnp.stack(parts) if len(parts) > 1 else parts[0][None]
    return (y_prompt, y_sample, cat(Cp), cat(Np), cat(Mp), cat(Sp), cat(Cs), cat(Ns), cat(Ms), cat(Ss))
```

```python
import functools

import jax
import jax.numpy as jnp
from jax import lax
from jax.experimental import pallas as pl
from jax.experimental.pallas import tpu as pltpu

F32 = jnp.float32
BF16 = jnp.bfloat16

EPS = 1e-6
GATE_CAP = 15.0
M_HEADS = 8
M_CHUNK = 64
M_ROW_BLOCK = 256
H_DK = 128
H_CHUNK = 32
H_SUB = 16

VMEM_LIMIT_BYTES = 56 * 1024 * 1024


def _params(*sem):
    return pltpu.CompilerParams(dimension_semantics=sem, vmem_limit_bytes=VMEM_LIMIT_BYTES)


def _row_tiles(xs, tm):
    for x in xs:
        assert x.shape[0] % tm == 0, (x.shape, tm)
    return tuple(x.shape[0] // tm for x in xs)


def _clamped_row_tile(first, count, row_of, col_of, *ids):
    return (jnp.clip(row_of(*ids) - first, 0, count - 1), col_of(*ids))


def _row_specs(tiles, block, row_of, col_of, single_tile_unbuffered=False):
    specs, first = [], 0
    for count in tiles:
        mode = dict(pipeline_mode=pl.Buffered(1)) if single_tile_unbuffered and count == 1 else {}
        specs.append(pl.BlockSpec(block, functools.partial(_clamped_row_tile, first, count, row_of, col_of), **mode))
        first += count
    return specs


def _on_owner(refs, tiles, i, fn):
    if len(refs) == 1:
        fn(refs[0])
        return
    first = 0
    for ref, count in zip(refs, tiles):
        pl.when((i >= first) & (i < first + count))(functools.partial(fn, ref))
        first += count


def _rms_normed(x, g):
    ms = jnp.mean(x * x, axis=-1, keepdims=True)
    return x * lax.rsqrt(ms + EPS) * g


def _norm_matmul_kernel(tiles, with_extra, w_rows_are_outputs, *refs):
    nx = len(tiles)
    x_refs, g_ref, w_ref = refs[:nx], refs[nx], refs[nx + 1]
    if with_extra:
        we_ref, o_ref, oe_ref, h_scr = refs[nx + 2:]
    else:
        o_ref, h_scr = refs[nx + 2:]
    matmul = _dot_nt if w_rows_are_outputs else functools.partial(jnp.dot, preferred_element_type=F32)

    @pl.when(pl.program_id(1) == 0)
    def _():
        def build(x_ref):
            h_scr[...] = _rms_normed(x_ref[...], g_ref[...]).astype(BF16)
            if with_extra:
                oe_ref[...] = matmul(h_scr[...], we_ref[...].astype(BF16))
        _on_owner(x_refs, tiles, pl.program_id(0), build)

    o_ref[...] = matmul(h_scr[...], w_ref[...].astype(BF16)).astype(o_ref.dtype)


def norm_matmul(xs, g, w, layer, n_cols, *, tm, tn, out_dtype, w_extra=None, w_rows_are_outputs=False):
    d = xs[0].shape[1]
    tiles = _row_tiles(xs, tm)
    m = tm * sum(tiles)
    row_of, col0 = (lambda i, j: i), (lambda i, j: 0)
    w_spec = (pl.BlockSpec((None, tn, d), lambda i, j: (layer, j, 0)) if w_rows_are_outputs
              else pl.BlockSpec((None, d, tn), lambda i, j: (layer, 0, j)))
    in_specs = (_row_specs(tiles, (tm, d), row_of, col0, single_tile_unbuffered=len(tiles) > 1)
                + [pl.BlockSpec((1, d), lambda i, j: (0, 0)), w_spec])
    out_specs = [pl.BlockSpec((tm, tn), lambda i, j: (i, j))]
    out_shape = [jax.ShapeDtypeStruct((m, n_cols), out_dtype)]
    args = list(xs) + [g.reshape(1, d), w]
    if w_extra is not None:
        ne = w_extra.shape[0 if w_rows_are_outputs else 1]
        in_specs.append(pl.BlockSpec(w_extra.shape, lambda i, j: (0, 0)))
        out_specs.append(pl.BlockSpec((tm, ne), lambda i, j: (i, 0)))
        out_shape.append(jax.ShapeDtypeStruct((m, ne), F32))
        args.append(w_extra)
    outs = pl.pallas_call(
        functools.partial(_norm_matmul_kernel, tiles, w_extra is not None, w_rows_are_outputs),
        grid=(m // tm, n_cols // tn),
        in_specs=in_specs,
        out_specs=out_specs,
        out_shape=out_shape,
        scratch_shapes=[pltpu.VMEM((tm, d), BF16)],
        compiler_params=_params("arbitrary", "arbitrary"),
        name="norm_matmul",
    )(*args)
    return outs if w_extra is not None else outs[0]


def _norm_swiglu_kernel(x_ref, g_ref, wa_ref, wu_ref, o_ref, h_scr):
    @pl.when(pl.program_id(1) == 0)
    def _():
        h_scr[...] = _rms_normed(x_ref[...], g_ref[...]).astype(BF16)

    h = h_scr[...]
    a = jnp.dot(h, wa_ref[...].astype(BF16), preferred_element_type=F32)
    u = jnp.dot(h, wu_ref[...].astype(BF16), preferred_element_type=F32)
    o_ref[...] = (a * jax.nn.sigmoid(a) * u).astype(o_ref.dtype)


def norm_swiglu(x, g, w_up, layer, *, tm, tn):
    m, d = x.shape
    ff = w_up.shape[-1] // 2
    nj = ff // tn
    return pl.pallas_call(
        _norm_swiglu_kernel,
        grid=(m // tm, nj),
        in_specs=[
            pl.BlockSpec((tm, d), lambda i, j: (i, 0)),
            pl.BlockSpec((1, d), lambda i, j: (0, 0)),
            pl.BlockSpec((None, d, tn), lambda i, j: (layer, 0, j)),
            pl.BlockSpec((None, d, tn), lambda i, j: (layer, 0, j + nj)),
        ],
        out_specs=pl.BlockSpec((tm, tn), lambda i, j: (i, j)),
        out_shape=jax.ShapeDtypeStruct((m, ff), BF16),
        scratch_shapes=[pltpu.VMEM((tm, d), BF16)],
        compiler_params=_params("arbitrary", "arbitrary"),
        name="norm_swiglu",
    )(x, g.reshape(1, d), w_up, w_up)


def _matmul_residual_kernel(y_tiles, r_tiles, *refs):
    ny, nr = len(y_tiles), len(r_tiles)
    y_refs, w_ref, r_refs = refs[:ny], refs[ny], refs[ny + 1:ny + 1 + nr]
    o_ref, w_scr = refs[ny + 1 + nr:]
    i = pl.program_id(1)

    @pl.when(i == 0)
    def _():
        w_scr[...] = w_ref[...].astype(BF16)

    def product(y_ref):
        o_ref[...] = jnp.dot(y_ref[...], w_scr[...], preferred_element_type=F32)

    def add_residual(r_ref):
        o_ref[...] = o_ref[...] + r_ref[...]

    _on_owner(y_refs, y_tiles, i, product)
    _on_owner(r_refs, r_tiles, i, add_residual)


def matmul_residual(ys, w, layer, rs, *, tm, tn):
    k, n = w.shape[-2], w.shape[-1]
    y_tiles, r_tiles = _row_tiles(ys, tm), _row_tiles(rs, tm)
    assert sum(y_tiles) == sum(r_tiles)
    m = tm * sum(y_tiles)
    row_of = lambda j, i: i
    w_mode = dict(pipeline_mode=pl.Buffered(1)) if tn == n else {}
    in_specs = (_row_specs(y_tiles, (tm, k), row_of, lambda j, i: 0)
                + [pl.BlockSpec((None, k, tn), lambda j, i: (layer, 0, j), **w_mode)]
                + _row_specs(r_tiles, (tm, tn), row_of, lambda j, i: j))
    return pl.pallas_call(
        functools.partial(_matmul_residual_kernel, y_tiles, r_tiles),
        grid=(n // tn, m // tm),
        in_specs=in_specs,
        out_specs=pl.BlockSpec((tm, tn), lambda j, i: (i, j)),
        out_shape=jax.ShapeDtypeStruct((m, n), F32),
        scratch_shapes=[pltpu.VMEM((k, tn), BF16)],
        compiler_params=_params("arbitrary", "arbitrary"),
        name="matmul_residual",
    )(*ys, w, *rs)


def _final_norm_kernel(x_ref, g_ref, o_ref):
    o_ref[...] = _rms_normed(x_ref[...], g_ref[...])


def final_norm(x, g, *, row0, nrows, tm):
    d = x.shape[1]
    assert row0 % tm == 0 and nrows % tm == 0
    return pl.pallas_call(
        _final_norm_kernel,
        grid=(nrows // tm,),
        in_specs=[pl.BlockSpec((tm, d), lambda i: (row0 // tm + i, 0)), pl.BlockSpec((1, d), lambda i: (0, 0))],
        out_specs=pl.BlockSpec((tm, d), lambda i: (i, 0)),
        out_shape=jax.ShapeDtypeStruct((nrows, d), F32),
        compiler_params=_params("arbitrary"),
        name="final_norm",
    )(x, g.reshape(1, d))


def _segment_masks(nseg, L):
    R = nseg * L
    r = lax.broadcasted_iota(jnp.int32, (R, R), 0)
    c = lax.broadcasted_iota(jnp.int32, (R, R), 1)
    same = (r // L) == (c // L)
    return r, c, same


def _dot_nt(a, b):
    return lax.dot_general(a, b, (((1,), (1,)), ((), ())), preferred_element_type=F32)


def _dot_tn(a, b):
    return lax.dot_general(a, b, (((0,), (0,)), ((), ())), preferred_element_type=F32)


def _log_sigmoid(x):
    return jnp.minimum(x, 0.0) - jnp.log1p(jnp.exp(-jnp.abs(x)))


def _cumsum_rows_exact(tril, x):
    hi = x.astype(BF16)
    r1 = x - hi.astype(F32)
    mid = r1.astype(BF16)
    lo = (r1 - mid.astype(F32)).astype(BF16)
    return (jnp.dot(tril, hi, preferred_element_type=F32) + jnp.dot(tril, mid, preferred_element_type=F32)
            + jnp.dot(tril, lo, preferred_element_type=F32))


def _mlstm_kernel(nseg, L, dqk, dv,
                  p_ref, gp_ref, bias_ref, hg_ref, c0_ref, n0_ref, m0_ref,
                  y_ref, c_ref, n_ref, m_ref):
    H = M_HEADS
    R = nseg * L
    s1 = H * dqk
    scale = dqk ** -0.5

    @pl.when(pl.program_id(1) == 0)
    def _():
        c_ref[...] = c0_ref[...]
        n_ref[...] = n0_ref[...]
        m_ref[...] = m0_ref[...]

    r, c, same = _segment_masks(nseg, L)
    causal = same & (c <= r)
    causal_t = same & (r <= c)
    eye = r == c
    row_seg = lax.broadcasted_iota(jnp.int32, (R, 1), 0) // L

    gates = gp_ref[:, :2 * H] + bias_ref[...]
    capped = GATE_CAP * jnp.tanh(gates / GATE_CAP)
    log_f = _log_sigmoid(capped)

    m_prev = m_ref[0, 0]
    m_new_all = m_prev
    seg_i = lax.broadcasted_iota(jnp.int32, (nseg, H), 0)
    head_i = lax.broadcasted_iota(jnp.int32, (nseg, H), 1)

    for h in range(H):
        q = p_ref[:, h * dqk:(h + 1) * dqk]
        k = p_ref[:, s1 + h * dqk:s1 + (h + 1) * dqk]
        v = p_ref[:, 2 * s1 + h * dv:2 * s1 + (h + 1) * dv]
        o = p_ref[:, 2 * s1 + H * dv + h * dv:2 * s1 + H * dv + (h + 1) * dv]
        qf = q.astype(F32)
        kf = k.astype(F32)
        vf = v.astype(F32)

        ig_col = capped[:, h:h + 1]
        lf_col = log_f[:, H + h:H + h + 1]
        lf_row = jnp.sum(jnp.where(eye, lf_col, 0.0), axis=0, keepdims=True)
        ig_row = jnp.sum(jnp.where(eye, ig_col, 0.0), axis=0, keepdims=True)
        b_col = jnp.sum(jnp.where(causal, lf_row, 0.0), axis=1, keepdims=True)
        b_row = jnp.sum(jnp.where(causal_t, lf_col, 0.0), axis=0, keepdims=True)
        b_last_col = jnp.sum(jnp.where(same, lf_row, 0.0), axis=1, keepdims=True)

        m_col = jnp.zeros((R, 1), F32)
        n_rows = jnp.zeros((R, dqk), F32)
        for s in range(nseg):
            m_s = m_prev[s:s + 1, h:h + 1]
            m_col = jnp.where(row_seg == s, m_s, m_col)
            n_rows = jnp.where(row_seg == s, n_ref[0, s, h:h + 1, :], n_rows)

        dlog = jnp.where(causal, b_col - b_row + ig_row, -jnp.inf)
        inter = b_col + m_col
        m_t = jnp.maximum(inter, jnp.max(dlog, axis=1, keepdims=True))
        w_intra = jnp.exp(dlog - m_t)
        w_inter = jnp.exp(inter - m_t)
        sc = _dot_nt(q, k) * scale * w_intra

        qc = jnp.zeros((R, dv), F32)
        for s in range(nseg):
            qc_s = jnp.dot(q, c_ref[0, s, h].astype(BF16), preferred_element_type=F32)
            qc = qc_s if nseg == 1 else jnp.where(row_seg == s, qc_s, qc)
        num = jnp.dot(sc.astype(BF16), v, preferred_element_type=F32) + w_inter * qc
        qn = jnp.sum(qf * n_rows, axis=1, keepdims=True)
        den = jnp.sum(sc, axis=1, keepdims=True) + w_inter * qn
        hc = num / jnp.maximum(jnp.abs(den), jnp.exp(-m_t))

        hn = hc * lax.rsqrt(jnp.mean(hc * hc, axis=1, keepdims=True) + EPS) * hg_ref[:, h * dv:(h + 1) * dv]
        y_ref[:, h * dv:(h + 1) * dv] = (hn * jax.nn.sigmoid(o.astype(F32))).astype(y_ref.dtype)

        g = b_last_col - b_col + ig_col
        m_new_col = jnp.zeros((R, 1), F32)
        decays = []
        for s in range(nseg):
            in_s = row_seg == s
            m_s = m_prev[s:s + 1, h:h + 1]
            b_last_s = jnp.sum(jnp.where(in_s, lf_col, 0.0), axis=0, keepdims=True)
            g_max_s = jnp.max(jnp.where(in_s, g, -jnp.inf), axis=0, keepdims=True)
            m_new_s = jnp.maximum(b_last_s + m_s, g_max_s)
            decays.append(jnp.exp(b_last_s + m_s - m_new_s))
            m_new_col = jnp.where(in_s, m_new_s, m_new_col)
            m_new_all = jnp.where((seg_i == s) & (head_i == h), m_new_s, m_new_all)
        w_k = jnp.exp(g - m_new_col) * scale
        wv = w_k * vf
        wk = w_k * kf
        for s in range(nseg):
            in_s = row_seg == s
            wv_s = wv if nseg == 1 else jnp.where(in_s, wv, 0.0)
            wk_s = wk if nseg == 1 else jnp.where(in_s, wk, 0.0)
            c_ref[0, s, h] = decays[s] * c_ref[0, s, h] + _dot_tn(k, wv_s.astype(BF16))
            n_ref[0, s, h:h + 1, :] = decays[s] * n_ref[0, s, h:h + 1, :] + jnp.sum(wk_s, axis=0, keepdims=True)

    m_ref[0, 0] = m_new_all


def mlstm_mixer(proj, gate_pre, bias, head_g, c0, n0, m0, *, row0, batch, seq, nseg, chunk):
    H = M_HEADS
    dqk, dv = c0.shape[-2], c0.shape[-1]
    R = nseg * chunk
    nchunks = seq // chunk
    assert nseg == 1 or nchunks == 1
    nblocks = batch // nseg
    blk0 = row0 // R
    rows = lambda i, c: (blk0 + i * nchunks + c, 0)
    width = proj.shape[1]
    c0 = c0.reshape(nblocks, nseg, H, dqk, dv)
    n0 = n0.reshape(nblocks, nseg, H, dqk)
    m0 = m0.reshape(nblocks, 1, nseg, H)
    y, c_out, n_out, m_out = pl.pallas_call(
        functools.partial(_mlstm_kernel, nseg, chunk, dqk, dv),
        grid=(nblocks, nchunks),
        in_specs=[
            pl.BlockSpec((R, width), rows),
            pl.BlockSpec((R, gate_pre.shape[1]), rows),
            pl.BlockSpec((1, 2 * H), lambda i, c: (0, 0)),
            pl.BlockSpec((1, H * dv), lambda i, c: (0, 0)),
            pl.BlockSpec((1, nseg, H, dqk, dv), lambda i, c: (i, 0, 0, 0, 0)),
            pl.BlockSpec((1, nseg, H, dqk), lambda i, c: (i, 0, 0, 0)),
            pl.BlockSpec((1, 1, nseg, H), lambda i, c: (i, 0, 0, 0)),
        ],
        out_specs=[
            pl.BlockSpec((R, H * dv), lambda i, c: (i * nchunks + c, 0)),
            pl.BlockSpec((1, nseg, H, dqk, dv), lambda i, c: (i, 0, 0, 0, 0)),
            pl.BlockSpec((1, nseg, H, dqk), lambda i, c: (i, 0, 0, 0)),
            pl.BlockSpec((1, 1, nseg, H), lambda i, c: (i, 0, 0, 0)),
        ],
        out_shape=[
            jax.ShapeDtypeStruct((batch * seq, H * dv), BF16),
            jax.ShapeDtypeStruct(c0.shape, F32),
            jax.ShapeDtypeStruct(n0.shape, F32),
            jax.ShapeDtypeStruct(m0.shape, F32),
        ],
        compiler_params=_params("arbitrary", "arbitrary"),
        name="mlstm_mixer",
    )(proj, gate_pre, bias.reshape(1, 2 * H), head_g.reshape(1, H * dv), c0, n0, m0)
    return (y, c_out.reshape(batch, H, dqk, dv), n_out.reshape(batch, H, dqk), m_out.reshape(batch, H))


def _mlstm_long_kernel(R, dqk, dv,
                       p_ref, gp_ref, bias_ref, hg_ref, c0_ref, n0_ref, m0_ref,
                       y_ref, c_ref, n_ref, m_ref):
    H = M_HEADS
    s1 = H * dqk
    scale = dqk ** -0.5

    @pl.when(pl.program_id(1) == 0)
    def _():
        c_ref[...] = c0_ref[...]
        n_ref[...] = n0_ref[...]
        m_ref[...] = m0_ref[...]

    r = lax.broadcasted_iota(jnp.int32, (R, R), 0)
    c = lax.broadcasted_iota(jnp.int32, (R, R), 1)
    causal = c <= r
    tril = causal.astype(BF16)

    capped = GATE_CAP * jnp.tanh((gp_ref[...] + bias_ref[...]) / GATE_CAP)
    cum = _cumsum_rows_exact(tril, _log_sigmoid(capped))
    ig_on_f = pltpu.roll(capped, H, axis=1)
    m_prev = m_ref[0]
    b_last = cum[R - 1:R, :]
    g_all = b_last - cum + ig_on_f
    m_new = jnp.maximum(b_last + m_prev, jnp.max(g_all, axis=0, keepdims=True))
    decay_all = jnp.exp(b_last + m_prev - m_new)
    wk_all = jnp.exp(g_all - m_new) * scale
    inter_all = cum + m_prev
    rows_t = (ig_on_f - cum).T

    RB = min(R, M_ROW_BLOCK)
    block_masks = [lax.broadcasted_iota(jnp.int32, (RB, (i + 1) * RB), 1)
                   <= lax.broadcasted_iota(jnp.int32, (RB, (i + 1) * RB), 0) + i * RB for i in range(R // RB)]

    for h in range(H):
        lane = H + h
        q_cols = slice(h * dqk, (h + 1) * dqk)
        k_cols = slice(s1 + h * dqk, s1 + (h + 1) * dqk)
        v_cols = slice(2 * s1 + h * dv, 2 * s1 + (h + 1) * dv)
        o_cols = slice(2 * s1 + H * dv + h * dv, 2 * s1 + H * dv + (h + 1) * dv)
        c_prev = c_ref[0, h]
        c_bf = c_prev.astype(BF16)
        n_prev = n_ref[0, h:h + 1, :]
        hg = hg_ref[:, h * dv:(h + 1) * dv]

        for i in range(R // RB):
            rows = slice(i * RB, (i + 1) * RB)
            ncols = (i + 1) * RB
            q = p_ref[rows, q_cols]
            gate_diff = cum[rows, lane:lane + 1] + rows_t[lane:lane + 1, :ncols]
            dlog = jnp.where(block_masks[i], gate_diff, -jnp.inf)
            inter = inter_all[rows, lane:lane + 1]
            m_t = jnp.maximum(inter, jnp.max(dlog, axis=1, keepdims=True))
            w_intra = jnp.exp(dlog - m_t)
            w_inter = jnp.exp(inter - m_t)
            sc = _dot_nt(q, p_ref[:ncols, k_cols]) * scale * w_intra
            num = (jnp.dot(sc.astype(BF16), p_ref[:ncols, v_cols], preferred_element_type=F32)
                   + w_inter * jnp.dot(q, c_bf, preferred_element_type=F32))
            qn = jnp.sum(q.astype(F32) * n_prev, axis=1, keepdims=True)
            den = jnp.sum(sc, axis=1, keepdims=True) + w_inter * qn
            inv = 1.0 / jnp.maximum(jnp.abs(den), jnp.exp(-m_t))
            norm = inv * lax.rsqrt(inv * inv * jnp.mean(num * num, axis=1, keepdims=True) + EPS)
            o = p_ref[rows, o_cols].astype(F32)
            y_ref[rows, h * dv:(h + 1) * dv] = (num * norm * hg * jax.nn.sigmoid(o)).astype(y_ref.dtype)

        k = p_ref[:, k_cols]
        w_k = wk_all[:, lane:lane + 1]
        decay = decay_all[:, lane:lane + 1]
        wv = (w_k * p_ref[:, v_cols].astype(F32)).astype(BF16)
        c_ref[0, h] = decay * c_prev + _dot_tn(k, wv)
        n_ref[0, h:h + 1, :] = decay * n_prev + jnp.sum(w_k * k.astype(F32), axis=0, keepdims=True)

    m_ref[0] = m_new


def mlstm_long_mixer(proj, gate_pre, bias, head_g, c0, n0, m0, *, row0, batch, seq, rows):
    H = M_HEADS
    dqk, dv = c0.shape[-2], c0.shape[-1]
    R = rows
    lanes = gate_pre.shape[1]
    assert seq % R == 0 and row0 % R == 0 and lanes >= 2 * H
    nsteps = seq // R
    blk0 = row0 // R
    rows_of = lambda i, c: (blk0 + i * nsteps + c, 0)
    bias_l = jnp.pad(bias.reshape(1, 2 * H), ((0, 0), (0, lanes - 2 * H)))
    m0_l = jnp.pad(m0, ((0, 0), (H, lanes - 2 * H))).reshape(batch, 1, lanes)
    y, c_out, n_out, m_out = pl.pallas_call(
        functools.partial(_mlstm_long_kernel, R, dqk, dv),
        grid=(batch, nsteps),
        in_specs=[
            pl.BlockSpec((R, proj.shape[1]), rows_of),
            pl.BlockSpec((R, lanes), rows_of),
            pl.BlockSpec((1, lanes), lambda i, c: (0, 0)),
            pl.BlockSpec((1, H * dv), lambda i, c: (0, 0)),
            pl.BlockSpec((1, H, dqk, dv), lambda i, c: (i, 0, 0, 0)),
            pl.BlockSpec((1, H, dqk), lambda i, c: (i, 0, 0)),
            pl.BlockSpec((1, 1, lanes), lambda i, c: (i, 0, 0)),
        ],
        out_specs=[
            pl.BlockSpec((R, H * dv), lambda i, c: (i * nsteps + c, 0)),
            pl.BlockSpec((1, H, dqk, dv), lambda i, c: (i, 0, 0, 0)),
            pl.BlockSpec((1, H, dqk), lambda i, c: (i, 0, 0)),
            pl.BlockSpec((1, 1, lanes), lambda i, c: (i, 0, 0)),
        ],
        out_shape=[
            jax.ShapeDtypeStruct((batch * seq, H * dv), BF16),
            jax.ShapeDtypeStruct(c0.shape, F32),
            jax.ShapeDtypeStruct(n0.shape, F32),
            jax.ShapeDtypeStruct(m0_l.shape, F32),
        ],
        compiler_params=_params("arbitrary", "arbitrary"),
        name="mlstm_long_mixer",
    )(proj, gate_pre, bias_l, head_g.reshape(1, H * dv), c0, n0, m0_l)
    return y, c_out, n_out, m_out[:, 0, H:2 * H]


def _mlstm_short_kernel(nb, T, dqk, dv,
                        p_ref, gp_ref, bias_ref, hg_ref, c0_ref, n0_ref, m0_ref,
                        y_ref, c_ref, n_ref, m_ref):
    H = M_HEADS
    R = nb * T
    PAIR = 2 * T
    s1 = H * dqk
    scale = dqk ** -0.5
    assert T % 8 == 0 and nb % 2 == 0

    r = lax.broadcasted_iota(jnp.int32, (R, R), 0)
    c = lax.broadcasted_iota(jnp.int32, (R, R), 1)
    causal = ((r // T) == (c // T)) & (c <= r)
    tril = causal.astype(BF16)
    seq_cols = (lax.broadcasted_iota(jnp.int32, (nb, 1, R), 2) // T
                == lax.broadcasted_iota(jnp.int32, (nb, 1, R), 0)).astype(BF16)
    first_of_pair = lax.broadcasted_iota(jnp.int32, (PAIR, 1), 0) < T

    def per_seq(x):
        return x.reshape(nb, T, x.shape[-1])

    def seq_rows(x3):
        return jnp.broadcast_to(x3, (nb, T, x3.shape[-1])).reshape(R, x3.shape[-1])

    capped = GATE_CAP * jnp.tanh((gp_ref[...] + bias_ref[...]) / GATE_CAP)
    cum = _cumsum_rows_exact(tril, _log_sigmoid(capped))
    ig_on_f = pltpu.roll(capped, H, axis=1)
    m_prev = m0_ref[...]
    b_last = seq_rows(per_seq(cum)[:, T - 1:T, :])
    g_all = b_last - cum + ig_on_f
    m_new = jnp.maximum(b_last + m_prev, seq_rows(jnp.max(per_seq(g_all), axis=1, keepdims=True)))
    decay_all = jnp.exp(b_last + m_prev - m_new)
    wk_all = jnp.exp(g_all - m_new) * scale
    inter_all = cum + m_prev
    rows_t = (ig_on_f - cum).T

    for h in range(H):
        lane = H + h
        q = p_ref[:, h * dqk:(h + 1) * dqk]
        k = p_ref[:, s1 + h * dqk:s1 + (h + 1) * dqk]
        v = p_ref[:, 2 * s1 + h * dv:2 * s1 + (h + 1) * dv]
        o = p_ref[:, 2 * s1 + H * dv + h * dv:2 * s1 + H * dv + (h + 1) * dv]
        kf = k.astype(F32)

        dlog = jnp.where(causal, cum[:, lane:lane + 1] + rows_t[lane:lane + 1, :], -jnp.inf)
        inter = inter_all[:, lane:lane + 1]
        m_t = jnp.maximum(inter, jnp.max(dlog, axis=1, keepdims=True))
        w_intra = jnp.exp(dlog - m_t)
        w_inter = jnp.exp(inter - m_t)
        sc = _dot_nt(q, k) * scale * w_intra
        num_intra = jnp.dot(sc.astype(BF16), v, preferred_element_type=F32)
        n_prev = n0_ref[h]
        qn = jnp.sum(q.astype(F32) * n_prev, axis=1, keepdims=True)
        den = jnp.sum(sc, axis=1, keepdims=True) + w_inter * qn
        inv = 1.0 / jnp.maximum(jnp.abs(den), jnp.exp(-m_t))
        hg = hg_ref[:, h * dv:(h + 1) * dv]

        w_k = wk_all[:, lane:lane + 1]
        k_t = kf.T.astype(BF16)
        stacked = (k_t[None] * seq_cols).reshape(nb * dqk, R)
        upd = jnp.dot(stacked, (w_k * v.astype(F32)).astype(BF16), preferred_element_type=F32)
        n_ref[h] = (decay_all[:, lane:lane + 1] * n_prev
                    + seq_rows(jnp.sum(per_seq(w_k * kf), axis=1, keepdims=True)))

        for pr in range(nb // 2):
            rows = slice(pr * PAIR, (pr + 1) * PAIR)
            lhs = p_ref[rows, h * dqk:(h + 1) * dqk]
            qc = []
            for b in (2 * pr, 2 * pr + 1):
                c_prev = c0_ref[0, b, h]
                qc.append(jnp.dot(lhs, c_prev.astype(BF16), preferred_element_type=F32))
                c_ref[0, b, h] = decay_all[b * T:b * T + 1, lane:lane + 1] * c_prev + upd[b * dqk:(b + 1) * dqk, :]
            num = num_intra[rows, :] + w_inter[rows, :] * jnp.where(first_of_pair, qc[0], qc[1])
            hc = num * inv[rows, :]
            hn = hc * lax.rsqrt(jnp.mean(hc * hc, axis=1, keepdims=True) + EPS) * hg
            y_ref[rows, h * dv:(h + 1) * dv] = (hn * jax.nn.sigmoid(o[rows, :].astype(F32))).astype(y_ref.dtype)

    m_ref[...] = m_new


def mlstm_short_mixer(proj, gate_pre, bias, head_g, c0, n0, m0, *, row0, batch, seq, nb):
    H = M_HEADS
    dqk, dv = c0.shape[-2], c0.shape[-1]
    R = nb * seq
    lanes = gate_pre.shape[1]
    assert batch % nb == 0 and row0 % R == 0 and lanes >= 2 * H
    nblocks = batch // nb
    blk0 = row0 // R
    bias_l = jnp.pad(bias.reshape(1, 2 * H), ((0, 0), (0, lanes - 2 * H)))
    m0_l = jnp.repeat(jnp.pad(m0, ((0, 0), (H, lanes - 2 * H))), seq, axis=0)
    n0_t = jnp.repeat(jnp.swapaxes(n0, 0, 1), seq, axis=1)
    y, c_out, n_out, m_out = pl.pallas_call(
        functools.partial(_mlstm_short_kernel, nb, seq, dqk, dv),
        grid=(nblocks,),
        in_specs=[
            pl.BlockSpec((R, proj.shape[1]), lambda i: (blk0 + i, 0)),
            pl.BlockSpec((R, lanes), lambda i: (blk0 + i, 0)),
            pl.BlockSpec((1, lanes), lambda i: (0, 0)),
            pl.BlockSpec((1, H * dv), lambda i: (0, 0)),
            pl.BlockSpec((1, nb, H, dqk, dv), lambda i: (i, 0, 0, 0, 0)),
            pl.BlockSpec((H, R, dqk), lambda i: (0, i, 0)),
            pl.BlockSpec((R, lanes), lambda i: (i, 0)),
        ],
        out_specs=[
            pl.BlockSpec((R, H * dv), lambda i: (i, 0)),
            pl.BlockSpec((1, nb, H, dqk, dv), lambda i: (i, 0, 0, 0, 0)),
            pl.BlockSpec((H, R, dqk), lambda i: (0, i, 0)),
            pl.BlockSpec((R, lanes), lambda i: (i, 0)),
        ],
        out_shape=[
            jax.ShapeDtypeStruct((batch * seq, H * dv), BF16),
            jax.ShapeDtypeStruct((nblocks, nb, H, dqk, dv), F32),
            jax.ShapeDtypeStruct(n0_t.shape, F32),
            jax.ShapeDtypeStruct(m0_l.shape, F32),
        ],
        compiler_params=_params("arbitrary"),
        name="mlstm_short_mixer",
    )(proj, gate_pre, bias_l, head_g.reshape(1, H * dv), c0.reshape(nblocks, nb, H, dqk, dv), n0_t, m0_l)
    return (y, c_out.reshape(batch, H, dqk, dv), jnp.swapaxes(n_out[:, ::seq, :], 0, 1), m_out[::seq, H:2 * H])


def _hgrn_kernel(nseg, L, heads, dk, dv,
                 p_ref, lb_ref, gn_ref, s0_ref, y_ref, s_ref):
    R = nseg * L
    fdim = heads * dk
    scale = dk ** -0.5
    sub = min(L, H_SUB)
    assert L % sub == 0 and L // sub in (1, 2)

    @pl.when(pl.program_id(1) == 0)
    def _():
        s_ref[...] = s0_ref[...]

    r, c, same = _segment_masks(nseg, L)
    causal = same & (c <= r)
    same_sub = (r // sub) == (c // sub)
    diag_mask = causal & same_sub
    tril = causal.astype(F32)

    row = lax.broadcasted_iota(jnp.int32, (R, 1), 0)
    row_seg = row // L
    pos = row % L
    mid_sel = (same_sub & ((c % sub) == (sub // 2))).astype(F32)
    last_sel = (same & ((c % L) == (L - 1))).astype(F32)
    if L // sub == 2:
        bnd_sel = (same & ((c % L) == (sub - 1))).astype(F32)
        upper = pos >= sub
    hp = lax.Precision.HIGHEST

    for h in range(heads):
        q = p_ref[:, h * dk:(h + 1) * dk].astype(F32) * scale
        fpre = p_ref[:, fdim + h * dk:fdim + (h + 1) * dk].astype(F32)
        iv = p_ref[:, 2 * fdim + h * dv:2 * fdim + (h + 1) * dv]
        gate = p_ref[:, 2 * fdim + heads * dv + h * dv:2 * fdim + heads * dv + (h + 1) * dv].astype(F32)
        lb = lb_ref[:, h * dk:(h + 1) * dk]
        f = lb + (1.0 - lb) * jax.nn.sigmoid(fpre)
        log_f = jnp.log(f)
        k = 1.0 - f

        a = jnp.dot(tril, log_f, preferred_element_type=F32, precision=hp)
        a_mid = jnp.dot(mid_sel, a, preferred_element_type=F32, precision=hp)
        a_last = jnp.dot(last_sel, a, preferred_element_type=F32, precision=hp)

        qd = (q * jnp.exp(a - a_mid)).astype(BF16)
        kd = (k * jnp.exp(a_mid - a)).astype(BF16)
        sc = jnp.where(diag_mask, _dot_nt(qd, kd), 0.0)
        if L // sub == 2:
            a_bnd = jnp.dot(bnd_sel, a, preferred_element_type=F32, precision=hp)
            qo = jnp.where(upper, q * jnp.exp(jnp.minimum(a - a_bnd, 0.0)), 0.0).astype(BF16)
            ko = jnp.where(upper, 0.0, k * jnp.exp(jnp.minimum(a_bnd - a, 0.0))).astype(BF16)
            sc = sc + jnp.where(same, _dot_nt(qo, ko), 0.0)

        qe = (q * jnp.exp(a)).astype(BF16)
        inter = jnp.zeros((R, dv), F32)
        for s in range(nseg):
            inter_s = jnp.dot(qe, s_ref[0, s, h].astype(BF16), preferred_element_type=F32)
            inter = inter_s if nseg == 1 else jnp.where(row_seg == s, inter_s, inter)
        oc = jnp.dot(sc.astype(BF16), iv, preferred_element_type=F32) + inter

        on = oc * lax.rsqrt(jnp.mean(oc * oc, axis=1, keepdims=True) + EPS) * gn_ref[:, h * dv:(h + 1) * dv]
        y_ref[:, h * dv:(h + 1) * dv] = (on * (gate * jax.nn.sigmoid(gate))).astype(y_ref.dtype)

        ke = k * jnp.exp(a_last - a)
        for s in range(nseg):
            in_s = row_seg == s
            ke_s = ke if nseg == 1 else jnp.where(in_s, ke, 0.0)
            a_last_s = a[s * L + L - 1:s * L + L, :]
            upd = _dot_tn(ke_s.astype(BF16), iv)
            decay_col = jnp.sum(jnp.where(lax.broadcasted_iota(jnp.int32, (dk, dk), 0)
                                          == lax.broadcasted_iota(jnp.int32, (dk, dk), 1),
                                          jnp.exp(a_last_s), 0.0), axis=1, keepdims=True)
            s_ref[0, s, h] = decay_col * s_ref[0, s, h] + upd


def hgrn_mixer(proj, lb, gn_g, s0, *, row0, batch, seq, nseg, chunk):
    heads, dk, dv = s0.shape[-3], s0.shape[-2], s0.shape[-1]
    R = nseg * chunk
    nchunks = seq // chunk
    assert nseg == 1 or nchunks == 1
    nblocks = batch // nseg
    blk0 = row0 // R
    rows = lambda i, c: (blk0 + i * nchunks + c, 0)
    s0 = s0.reshape(nblocks, nseg, heads, dk, dv)
    y, s_out = pl.pallas_call(
        functools.partial(_hgrn_kernel, nseg, chunk, heads, dk, dv),
        grid=(nblocks, nchunks),
        in_specs=[
            pl.BlockSpec((R, proj.shape[1]), rows),
            pl.BlockSpec((1, heads * dk), lambda i, c: (0, 0)),
            pl.BlockSpec((1, heads * dv), lambda i, c: (0, 0)),
            pl.BlockSpec((1, nseg, heads, dk, dv), lambda i, c: (i, 0, 0, 0, 0)),
        ],
        out_specs=[
            pl.BlockSpec((R, heads * dv), lambda i, c: (i * nchunks + c, 0)),
            pl.BlockSpec((1, nseg, heads, dk, dv), lambda i, c: (i, 0, 0, 0, 0)),
        ],
        out_shape=[
            jax.ShapeDtypeStruct((batch * seq, heads * dv), BF16),
            jax.ShapeDtypeStruct(s0.shape, F32),
        ],
        compiler_params=_params("arbitrary", "arbitrary"),
        name="hgrn_mixer",
    )(proj, lb.reshape(1, heads * dk), gn_g.reshape(1, heads * dv), s0)
    return y, s_out.reshape(batch, heads, dk, dv)


def _hgrn_short_kernel(nb, T, heads, dk, dv,
                       p_ref, lb_ref, gn_ref, s0_ref, y_ref, s_ref,
                       qd_scr, kd_scr, qe_scr, ke_scr, dec_scr):
    R = nb * T
    fdim = heads * dk
    scale = dk ** -0.5
    W = 2 * dk
    PAIR = 2 * T
    assert T % 8 == 0 and T <= H_SUB and nb % 2 == 0

    r = lax.broadcasted_iota(jnp.int32, (R, R), 0)
    c = lax.broadcasted_iota(jnp.int32, (R, R), 1)
    causal = ((r // T) == (c // T)) & (c <= r)
    tril = causal.astype(BF16)
    seq_cols = (lax.broadcasted_iota(jnp.int32, (nb, 1, R), 2) // T
                == lax.broadcasted_iota(jnp.int32, (nb, 1, R), 0)).astype(BF16)
    first_of_pair = lax.broadcasted_iota(jnp.int32, (PAIR, 1), 0) < T

    def minus_ref_row(a, row):
        a3 = a.reshape(nb, T, a.shape[-1])
        return (a3 - a3[:, row:row + 1, :]).reshape(a.shape)

    for g in range(fdim // W):
        sl = slice(g * W, (g + 1) * W)
        q = p_ref[:, sl].astype(F32) * scale
        fpre = p_ref[:, fdim + g * W:fdim + (g + 1) * W].astype(F32)
        lb = lb_ref[:, sl]
        f = lb + (1.0 - lb) * jax.nn.sigmoid(fpre)
        k = 1.0 - f
        a = _cumsum_rows_exact(tril, jnp.log(f))
        d = minus_ref_row(a, T // 2)
        to_last = -minus_ref_row(a, T - 1)
        qd_scr[:, sl] = (q * jnp.exp(d)).astype(BF16)
        kd_scr[:, sl] = (k * jnp.exp(-d)).astype(BF16)
        qe_scr[:, sl] = (q * jnp.exp(a)).astype(BF16)
        ke_scr[:, sl] = k * jnp.exp(to_last)
        dec_scr[:, sl] = jnp.exp(a + to_last)

    for h in range(heads):
        sl = slice(h * dk, (h + 1) * dk)
        iv = p_ref[:, 2 * fdim + h * dv:2 * fdim + (h + 1) * dv]
        sc = jnp.where(causal, _dot_nt(qd_scr[:, sl], kd_scr[:, sl]), 0.0)
        oc = jnp.dot(sc.astype(BF16), iv, preferred_element_type=F32)
        ke_t = ke_scr[:, sl].T.astype(BF16)
        dec_t = dec_scr[:, sl].T
        stacked = (ke_t[None] * seq_cols).reshape(nb * dk, R)
        upd = jnp.dot(stacked, iv, preferred_element_type=F32)
        gn = gn_ref[:, h * dv:(h + 1) * dv]
        for pr in range(nb // 2):
            rows = slice(pr * PAIR, (pr + 1) * PAIR)
            lhs = qe_scr[rows, sl]
            inter = []
            for b in (2 * pr, 2 * pr + 1):
                s_prev = s0_ref[0, b, h]
                inter.append(jnp.dot(lhs, s_prev.astype(BF16), preferred_element_type=F32))
                s_ref[0, b, h] = dec_t[:, b * T:b * T + 1] * s_prev + upd[b * dk:(b + 1) * dk, :]
            o2 = oc[rows, :] + jnp.where(first_of_pair, inter[0], inter[1])
            gate = p_ref[rows, 2 * fdim + heads * dv + h * dv:2 * fdim + heads * dv + (h + 1) * dv].astype(F32)
            on = o2 * lax.rsqrt(jnp.mean(o2 * o2, axis=1, keepdims=True) + EPS) * gn
            y_ref[rows, h * dv:(h + 1) * dv] = (on * (gate * jax.nn.sigmoid(gate))).astype(y_ref.dtype)


def hgrn_short_mixer(proj, lb, gn_g, s0, *, row0, batch, seq, nb):
    heads, dk, dv = s0.shape[-3], s0.shape[-2], s0.shape[-1]
    R = nb * seq
    assert batch % nb == 0 and row0 % R == 0
    nblocks = batch // nb
    blk0 = row0 // R
    fdim = heads * dk
    s0 = s0.reshape(nblocks, nb, heads, dk, dv)
    y, s_out = pl.pallas_call(
        functools.partial(_hgrn_short_kernel, nb, seq, heads, dk, dv),
        grid=(nblocks,),
        in_specs=[
            pl.BlockSpec((R, proj.shape[1]), lambda i: (blk0 + i, 0)),
            pl.BlockSpec((1, fdim), lambda i: (0, 0)),
            pl.BlockSpec((1, heads * dv), lambda i: (0, 0)),
            pl.BlockSpec((1, nb, heads, dk, dv), lambda i: (i, 0, 0, 0, 0)),
        ],
        out_specs=[
            pl.BlockSpec((R, heads * dv), lambda i: (i, 0)),
            pl.BlockSpec((1, nb, heads, dk, dv), lambda i: (i, 0, 0, 0, 0)),
        ],
        out_shape=[
            jax.ShapeDtypeStruct((batch * seq, heads * dv), BF16),
            jax.ShapeDtypeStruct(s0.shape, F32),
        ],
        scratch_shapes=[
            pltpu.VMEM((R, fdim), BF16), pltpu.VMEM((R, fdim), BF16), pltpu.VMEM((R, fdim), BF16),
            pltpu.VMEM((R, fdim), F32), pltpu.VMEM((R, fdim), F32),
        ],
        compiler_params=_params("arbitrary"),
        name="hgrn_short_mixer",
    )(proj, lb.reshape(1, fdim), gn_g.reshape(1, heads * dv), s0)
    return y, s_out.reshape(batch, heads, dk, dv)


def _hgrn_long_kernel(R, heads, dk, dv, levels,
                      p_ref, lb_ref, gn_ref, s0_ref, y_ref, s_ref,
                      st_scr, dec_scr, qd_scr, kd_scr, qe_scr, ke_scr, ql_scr, kl_scr, k_scr, a_scr):
    fdim = heads * dk
    eps_unscaled = EPS * dk
    W = 2 * dk
    last_step = pl.num_programs(1) - 1

    @pl.when(pl.program_id(1) == 0)
    def _():
        for h in range(heads):
            st_scr[h] = s0_ref[0, h].T

    r = lax.broadcasted_iota(jnp.int32, (R, R), 0)
    c = lax.broadcasted_iota(jnp.int32, (R, R), 1)
    tril = (c <= r).astype(BF16)
    diag_mask = ((r // H_SUB) == (c // H_SUB)) & (c <= r)
    level_masks = [((r // G) == (c // G)) & ((r % G) >= G // 2) & ((c % G) < G // 2) for G in levels]

    HB = 2 * H_SUB
    half = slice(0, H_SUB), slice(H_SUB, HB)
    zeros_half = jnp.zeros((H_SUB, W), BF16)
    zeros_block = jnp.zeros((HB, W), BF16)

    for g in range(fdim // W):
        sl = slice(g * W, (g + 1) * W)
        fpre = p_ref[:, fdim + g * W:fdim + (g + 1) * W].astype(F32)
        lb = lb_ref[:, sl]
        f = lb + (1.0 - lb) * jax.nn.sigmoid(fpre)
        k_scr[:, sl] = 1.0 - f
        a_scr[:, sl] = _cumsum_rows_exact(tril, jnp.log(f))
        a_last = a_scr[R - 1:R, sl]
        dec_scr[:, sl] = jnp.exp(a_last)

        for blk in range(R // HB):
            row0 = blk * HB
            rows = slice(row0, row0 + HB)
            q = p_ref[rows, sl].astype(F32)
            k = k_scr[rows, sl]
            a = a_scr[rows, sl]
            a3 = a.reshape(2, H_SUB, W)
            d = (a3 - a3[:, H_SUB // 2:H_SUB // 2 + 1, :]).reshape(HB, W)
            qd_scr[rows, sl] = (q * jnp.exp(d)).astype(BF16)
            kd_scr[rows, sl] = (k * jnp.exp(-d)).astype(BF16)
            qe_scr[rows, sl] = (q * jnp.exp(a)).astype(BF16)
            ke_scr[rows, sl] = (k * jnp.exp(a_last - a)).astype(BF16)
            for li, G in enumerate(levels):
                bnd = (row0 // G) * G + G // 2 - 1
                a_bnd = a_scr[bnd:bnd + 1, sl]
                if G == HB:
                    lo, up = half
                    kl_scr[li, row0:row0 + H_SUB, sl] = (k[lo] * jnp.exp(a_bnd - a[lo])).astype(BF16)
                    ql_scr[li, row0:row0 + H_SUB, sl] = zeros_half
                    ql_scr[li, row0 + H_SUB:row0 + HB, sl] = (q[up] * jnp.exp(a[up] - a_bnd)).astype(BF16)
                    kl_scr[li, row0 + H_SUB:row0 + HB, sl] = zeros_half
                elif row0 % G >= G // 2:
                    ql_scr[li, rows, sl] = (q * jnp.exp(a - a_bnd)).astype(BF16)
                    kl_scr[li, rows, sl] = zeros_block
                else:
                    kl_scr[li, rows, sl] = (k * jnp.exp(a_bnd - a)).astype(BF16)
                    ql_scr[li, rows, sl] = zeros_block

    for h in range(heads):
        sl = slice(h * dk, (h + 1) * dk)
        sc = jnp.where(diag_mask, _dot_nt(qd_scr[:, sl], kd_scr[:, sl]), 0.0)
        for li in range(len(levels)):
            sc = jnp.where(level_masks[li], _dot_nt(ql_scr[li, :, sl], kl_scr[li, :, sl]), sc)
        iv = p_ref[:, 2 * fdim + h * dv:2 * fdim + (h + 1) * dv]
        gate = p_ref[:, 2 * fdim + heads * dv + h * dv:2 * fdim + heads * dv + (h + 1) * dv].astype(F32)
        st = st_scr[h]
        oc = jnp.dot(sc.astype(BF16), iv, preferred_element_type=F32) + _dot_nt(qe_scr[:, sl], st.astype(BF16))
        st_scr[h] = st * dec_scr[:, sl] + _dot_tn(iv, ke_scr[:, sl])
        on = oc * lax.rsqrt(jnp.mean(oc * oc, axis=1, keepdims=True) + eps_unscaled) * gn_ref[:, h * dv:(h + 1) * dv]
        y_ref[:, h * dv:(h + 1) * dv] = (on * (gate * jax.nn.sigmoid(gate))).astype(y_ref.dtype)

    @pl.when(pl.program_id(1) == last_step)
    def _():
        for h in range(heads):
            s_ref[0, h] = st_scr[h].T


def hgrn_long_mixer(proj, lb, gn_g, s0, *, row0, batch, seq, rows):
    heads, dk, dv = s0.shape[-3], s0.shape[-2], s0.shape[-1]
    R = rows
    assert seq % R == 0 and R % (2 * H_SUB) == 0 and row0 % R == 0
    levels = []
    G = 2 * H_SUB
    while G <= R:
        levels.append(G)
        G *= 2
    assert levels[-1] == R
    nsteps = seq // R
    blk0 = row0 // R
    fdim = heads * dk
    wide = lambda: pltpu.VMEM((R, fdim), BF16)
    return pl.pallas_call(
        functools.partial(_hgrn_long_kernel, R, heads, dk, dv, tuple(levels)),
        grid=(batch, nsteps),
        in_specs=[
            pl.BlockSpec((R, proj.shape[1]), lambda i, c: (blk0 + i * nsteps + c, 0)),
            pl.BlockSpec((1, fdim), lambda i, c: (0, 0)),
            pl.BlockSpec((1, heads * dv), lambda i, c: (0, 0)),
            pl.BlockSpec((1, heads, dk, dv), lambda i, c: (i, 0, 0, 0)),
        ],
        out_specs=[
            pl.BlockSpec((R, heads * dv), lambda i, c: (i * nsteps + c, 0)),
            pl.BlockSpec((1, heads, dk, dv), lambda i, c: (i, 0, 0, 0)),
        ],
        out_shape=[
            jax.ShapeDtypeStruct((batch * seq, heads * dv), BF16),
            jax.ShapeDtypeStruct(s0.shape, F32),
        ],
        scratch_shapes=[
            pltpu.VMEM((heads, dv, dk), F32),
            pltpu.VMEM((1, fdim), F32),
            wide(), wide(), wide(), wide(),
            pltpu.VMEM((len(levels), R, fdim), BF16),
            pltpu.VMEM((len(levels), R, fdim), BF16),
            pltpu.VMEM((R, fdim), F32),
            pltpu.VMEM((R, fdim), F32),
        ],
        compiler_params=_params("arbitrary", "arbitrary"),
        name="hgrn_long_mixer",
    )(proj, lb.reshape(1, fdim), gn_g.reshape(1, heads * dv), s0)


def kernel(x_prompt, x_sample, state_mlstm_C, state_mlstm_n, state_mlstm_m, state_hgrn_S,
           norm_mix_g, norm_ffn_g, norm_final_g, mlstm_w_in, mlstm_b_gates, mlstm_head_norm_g,
           mlstm_w_out, hgrn_w_in, hgrn_lower_bounds, hgrn_g_norm_g, hgrn_w_out, ffn_w_up, ffn_w_down):
    bp, tp, d = x_prompt.shape
    bs, ts, _ = x_sample.shape
    np_, ns_ = bp * tp, bs * ts
    depth = norm_mix_g.shape[0]
    H = M_HEADS
    dqk, dv = state_mlstm_C.shape[-2], state_mlstm_C.shape[-1]
    hh, hdk, hdv = state_hgrn_S.shape[-3], state_hgrn_S.shape[-2], state_hgrn_S.shape[-1]

    xs = (x_prompt.reshape(np_, d), x_sample.reshape(ns_, d))

    lb_all = jnp.cumsum(jax.nn.softmax(hgrn_lower_bounds.astype(F32), axis=0), axis=0)
    lb_all = lb_all - lb_all[0]

    tm = 1024
    Cp, Np, Mp, Sp, Cs, Ns, Ms, Ss = [], [], [], [], [], [], [], []
    for i in range(depth):
        j = i // 2
        if i % 2 == 0:
            n_main = 2 * H * dqk + 2 * H * dv
            w_in_t = jnp.swapaxes(mlstm_w_in, 1, 2)
            w_gates = jnp.pad(w_in_t[j, n_main:], ((0, 128 - 2 * H), (0, 0)))
            proj, gate_pre = norm_matmul(xs, norm_mix_g[i], w_in_t, j, n_main, tm=tm, tn=1024,
                                         out_dtype=BF16, w_extra=w_gates, w_rows_are_outputs=True)
            args = (proj, gate_pre, mlstm_b_gates[j], mlstm_head_norm_g[j])
            yp, c_p, n_p, m_p = mlstm_long_mixer(
                *args, jnp.zeros((bp, H, dqk, dv), F32), jnp.zeros((bp, H, dqk), F32), jnp.zeros((bp, H), F32),
                row0=0, batch=bp, seq=tp, rows=256)
            ysm, c_s, n_s, m_s = mlstm_short_mixer(
                *args, state_mlstm_C[j], state_mlstm_n[j], state_mlstm_m[j],
                row0=np_, batch=bs, seq=ts, nb=8)
            Cp.append(c_p); Np.append(n_p); Mp.append(m_p)
            Cs.append(c_s); Ns.append(n_s); Ms.append(m_s)
            w_out = mlstm_w_out
        else:
            proj = norm_matmul(xs, norm_mix_g[i], hgrn_w_in, j, hgrn_w_in.shape[-1], tm=tm, tn=1024, out_dtype=BF16)
            args = (proj, lb_all[i], hgrn_g_norm_g[j])
            yp, s_p = hgrn_long_mixer(*args, jnp.zeros((bp, hh, hdk, hdv), F32),
                                      row0=0, batch=bp, seq=tp, rows=128)
            ysm, s_s = hgrn_short_mixer(*args, state_hgrn_S[j], row0=np_, batch=bs, seq=ts, nb=8)
            Sp.append(s_p); Ss.append(s_s)
            w_out = hgrn_w_out
        x = matmul_residual((yp, ysm), w_out, j, xs, tm=256, tn=d)
        act = norm_swiglu(x, norm_ffn_g[i], ffn_w_up, i, tm=tm, tn=512)
        x = matmul_residual((act,), ffn_w_down, i, (x,), tm=512, tn=512)
        xs = (x,)

    y_prompt = final_norm(x, norm_final_g, row0=0, nrows=np_, tm=512).reshape(bp, tp, d)
    y_sample = final_norm(x, norm_final_g, row0=np_, nrows=ns_, tm=512).reshape(bs, ts, d)
    cat = lambda parts: jnp.stack(parts) if len(parts) > 1 else parts[0][None]
    return (y_prompt, y_sample, cat(Cp), cat(Np), cat(Mp), cat(Sp), cat(Cs), cat(Ns), cat(Ms), cat(Ss))
```

```python
import functools

import jax
import jax.numpy as jnp
from jax import lax
from jax.experimental import pallas as pl
from jax.experimental.pallas import tpu as pltpu

F32 = jnp.float32
BF16 = jnp.bfloat16

EPS = 1e-6
GATE_CAP = 15.0
M_HEADS = 8
M_CHUNK = 64
M_ROW_BLOCK = 256
H_DK = 128
H_CHUNK = 32
H_SUB = 16

VMEM_LIMIT_BYTES = 56 * 1024 * 1024


def _params(*sem):
    return pltpu.CompilerParams(dimension_semantics=sem, vmem_limit_bytes=VMEM_LIMIT_BYTES)


def _row_tiles(xs, tm):
    for x in xs:
        assert x.shape[0] % tm == 0, (x.shape, tm)
    return tuple(x.shape[0] // tm for x in xs)


def _clamped_row_tile(first, count, row_of, col_of, *ids):
    return (jnp.clip(row_of(*ids) - first, 0, count - 1), col_of(*ids))


def _row_specs(tiles, block, row_of, col_of, single_tile_unbuffered=False):
    specs, first = [], 0
    for count in tiles:
        mode = dict(pipeline_mode=pl.Buffered(1)) if single_tile_unbuffered and count == 1 else {}
        specs.append(pl.BlockSpec(block, functools.partial(_clamped_row_tile, first, count, row_of, col_of), **mode))
        first += count
    return specs


def _on_owner(refs, tiles, i, fn):
    if len(refs) == 1:
        fn(refs[0])
        return
    first = 0
    for ref, count in zip(refs, tiles):
        pl.when((i >= first) & (i < first + count))(functools.partial(fn, ref))
        first += count


def _rms_normed(x, g):
    ms = jnp.mean(x * x, axis=-1, keepdims=True)
    return x * lax.rsqrt(ms + EPS) * g


def _norm_matmul_kernel(tiles, with_extra, w_rows_are_outputs, *refs):
    nx = len(tiles)
    x_refs, g_ref, w_ref = refs[:nx], refs[nx], refs[nx + 1]
    if with_extra:
        we_ref, o_ref, oe_ref, h_scr = refs[nx + 2:]
    else:
        o_ref, h_scr = refs[nx + 2:]
    matmul = _dot_nt if w_rows_are_outputs else functools.partial(jnp.dot, preferred_element_type=F32)

    @pl.when(pl.program_id(1) == 0)
    def _():
        def build(x_ref):
            h_scr[...] = _rms_normed(x_ref[...], g_ref[...]).astype(BF16)
            if with_extra:
                oe_ref[...] = matmul(h_scr[...], we_ref[...].astype(BF16))
        _on_owner(x_refs, tiles, pl.program_id(0), build)

    o_ref[...] = matmul(h_scr[...], w_ref[...].astype(BF16)).astype(o_ref.dtype)


def norm_matmul(xs, g, w, layer, n_cols, *, tm, tn, out_dtype, w_extra=None, w_rows_are_outputs=False):
    d = xs[0].shape[1]
    tiles = _row_tiles(xs, tm)
    m = tm * sum(tiles)
    row_of, col0 = (lambda i, j: i), (lambda i, j: 0)
    w_spec = (pl.BlockSpec((None, tn, d), lambda i, j: (layer, j, 0)) if w_rows_are_outputs
              else pl.BlockSpec((None, d, tn), lambda i, j: (layer, 0, j)))
    in_specs = (_row_specs(tiles, (tm, d), row_of, col0, single_tile_unbuffered=len(tiles) > 1)
                + [pl.BlockSpec((1, d), lambda i, j: (0, 0)), w_spec])
    out_specs = [pl.BlockSpec((tm, tn), lambda i, j: (i, j))]
    out_shape = [jax.ShapeDtypeStruct((m, n_cols), out_dtype)]
    args = list(xs) + [g.reshape(1, d), w]
    if w_extra is not None:
        ne = w_extra.shape[0 if w_rows_are_outputs else 1]
        in_specs.append(pl.BlockSpec(w_extra.shape, lambda i, j: (0, 0)))
        out_specs.append(pl.BlockSpec((tm, ne), lambda i, j: (i, 0)))
        out_shape.append(jax.ShapeDtypeStruct((m, ne), F32))
        args.append(w_extra)
    outs = pl.pallas_call(
        functools.partial(_norm_matmul_kernel, tiles, w_extra is not None, w_rows_are_outputs),
        grid=(m // tm, n_cols // tn),
        in_specs=in_specs,
        out_specs=out_specs,
        out_shape=out_shape,
        scratch_shapes=[pltpu.VMEM((tm, d), BF16)],
        compiler_params=_params("arbitrary", "arbitrary"),
        name="norm_matmul",
    )(*args)
    return outs if w_extra is not None else outs[0]


def _norm_swiglu_kernel(x_ref, g_ref, wa_ref, wu_ref, o_ref, h_scr):
    @pl.when(pl.program_id(1) == 0)
    def _():
        h_scr[...] = _rms_normed(x_ref[...], g_ref[...]).astype(BF16)

    h = h_scr[...]
    a = jnp.dot(h, wa_ref[...].astype(BF16), preferred_element_type=F32)
    u = jnp.dot(h, wu_ref[...].astype(BF16), preferred_element_type=F32)
    o_ref[...] = (a * jax.nn.sigmoid(a) * u).astype(o_ref.dtype)


def norm_swiglu(x, g, w_up, layer, *, tm, tn):
    m, d = x.shape
    ff = w_up.shape[-1] // 2
    nj = ff // tn
    return pl.pallas_call(
        _norm_swiglu_kernel,
        grid=(m // tm, nj),
        in_specs=[
            pl.BlockSpec((tm, d), lambda i, j: (i, 0)),
            pl.BlockSpec((1, d), lambda i, j: (0, 0)),
            pl.BlockSpec((None, d, tn), lambda i, j: (layer, 0, j)),
            pl.BlockSpec((None, d, tn), lambda i, j: (layer, 0, j + nj)),
        ],
        out_specs=pl.BlockSpec((tm, tn), lambda i, j: (i, j)),
        out_shape=jax.ShapeDtypeStruct((m, ff), BF16),
        scratch_shapes=[pltpu.VMEM((tm, d), BF16)],
        compiler_params=_params("arbitrary", "arbitrary"),
        name="norm_swiglu",
    )(x, g.reshape(1, d), w_up, w_up)


def _matmul_residual_kernel(y_tiles, r_tiles, *refs):
    ny, nr = len(y_tiles), len(r_tiles)
    y_refs, w_ref, r_refs = refs[:ny], refs[ny], refs[ny + 1:ny + 1 + nr]
    o_ref, w_scr = refs[ny + 1 + nr:]
    i = pl.program_id(1)

    @pl.when(i == 0)
    def _():
        w_scr[...] = w_ref[...].astype(BF16)

    def product(y_ref):
        o_ref[...] = jnp.dot(y_ref[...], w_scr[...], preferred_element_type=F32)

    def add_residual(r_ref):
        o_ref[...] = o_ref[...] + r_ref[...]

    _on_owner(y_refs, y_tiles, i, product)
    _on_owner(r_refs, r_tiles, i, add_residual)


def matmul_residual(ys, w, layer, rs, *, tm, tn):
    k, n = w.shape[-2], w.shape[-1]
    y_tiles, r_tiles = _row_tiles(ys, tm), _row_tiles(rs, tm)
    assert sum(y_tiles) == sum(r_tiles)
    m = tm * sum(y_tiles)
    row_of = lambda j, i: i
    w_mode = dict(pipeline_mode=pl.Buffered(1)) if tn == n else {}
    in_specs = (_row_specs(y_tiles, (tm, k), row_of, lambda j, i: 0)
                + [pl.BlockSpec((None, k, tn), lambda j, i: (layer, 0, j), **w_mode)]
                + _row_specs(r_tiles, (tm, tn), row_of, lambda j, i: j))
    return pl.pallas_call(
        functools.partial(_matmul_residual_kernel, y_tiles, r_tiles),
        grid=(n // tn, m // tm),
        in_specs=in_specs,
        out_specs=pl.BlockSpec((tm, tn), lambda j, i: (i, j)),
        out_shape=jax.ShapeDtypeStruct((m, n), F32),
        scratch_shapes=[pltpu.VMEM((k, tn), BF16)],
        compiler_params=_params("arbitrary", "arbitrary"),
        name="matmul_residual",
    )(*ys, w, *rs)


def _final_norm_kernel(x_ref, g_ref, o_ref):
    o_ref[...] = _rms_normed(x_ref[...], g_ref[...])


def final_norm(x, g, *, row0, nrows, tm):
    d = x.shape[1]
    assert row0 % tm == 0 and nrows % tm == 0
    return pl.pallas_call(
        _final_norm_kernel,
        grid=(nrows // tm,),
        in_specs=[pl.BlockSpec((tm, d), lambda i: (row0 // tm + i, 0)), pl.BlockSpec((1, d), lambda i: (0, 0))],
        out_specs=pl.BlockSpec((tm, d), lambda i: (i, 0)),
        out_shape=jax.ShapeDtypeStruct((nrows, d), F32),
        compiler_params=_params("arbitrary"),
        name="final_norm",
    )(x, g.reshape(1, d))


def _segment_masks(nseg, L):
    R = nseg * L
    r = lax.broadcasted_iota(jnp.int32, (R, R), 0)
    c = lax.broadcasted_iota(jnp.int32, (R, R), 1)
    same = (r // L) == (c // L)
    return r, c, same


def _dot_nt(a, b):
    return lax.dot_general(a, b, (((1,), (1,)), ((), ())), preferred_element_type=F32)


def _dot_tn(a, b):
    return lax.dot_general(a, b, (((0,), (0,)), ((), ())), preferred_element_type=F32)


def _log_sigmoid(x):
    return jnp.minimum(x, 0.0) - jnp.log1p(jnp.exp(-jnp.abs(x)))


def _cumsum_rows_exact(tril, x):
    hi = x.astype(BF16)
    r1 = x - hi.astype(F32)
    mid = r1.astype(BF16)
    lo = (r1 - mid.astype(F32)).astype(BF16)
    return (jnp.dot(tril, hi, preferred_element_type=F32) + jnp.dot(tril, mid, preferred_element_type=F32)
            + jnp.dot(tril, lo, preferred_element_type=F32))


def _mlstm_kernel(nseg, L, dqk, dv,
                  p_ref, gp_ref, bias_ref, hg_ref, c0_ref, n0_ref, m0_ref,
                  y_ref, c_ref, n_ref, m_ref):
    H = M_HEADS
    R = nseg * L
    s1 = H * dqk
    scale = dqk ** -0.5

    @pl.when(pl.program_id(1) == 0)
    def _():
        c_ref[...] = c0_ref[...]
        n_ref[...] = n0_ref[...]
        m_ref[...] = m0_ref[...]

    r, c, same = _segment_masks(nseg, L)
    causal = same & (c <= r)
    causal_t = same & (r <= c)
    eye = r == c
    row_seg = lax.broadcasted_iota(jnp.int32, (R, 1), 0) // L

    gates = gp_ref[:, :2 * H] + bias_ref[...]
    capped = GATE_CAP * jnp.tanh(gates / GATE_CAP)
    log_f = _log_sigmoid(capped)

    m_prev = m_ref[0, 0]
    m_new_all = m_prev
    seg_i = lax.broadcasted_iota(jnp.int32, (nseg, H), 0)
    head_i = lax.broadcasted_iota(jnp.int32, (nseg, H), 1)

    for h in range(H):
        q = p_ref[:, h * dqk:(h + 1) * dqk]
        k = p_ref[:, s1 + h * dqk:s1 + (h + 1) * dqk]
        v = p_ref[:, 2 * s1 + h * dv:2 * s1 + (h + 1) * dv]
        o = p_ref[:, 2 * s1 + H * dv + h * dv:2 * s1 + H * dv + (h + 1) * dv]
        qf = q.astype(F32)
        kf = k.astype(F32)
        vf = v.astype(F32)

        ig_col = capped[:, h:h + 1]
        lf_col = log_f[:, H + h:H + h + 1]
        lf_row = jnp.sum(jnp.where(eye, lf_col, 0.0), axis=0, keepdims=True)
        ig_row = jnp.sum(jnp.where(eye, ig_col, 0.0), axis=0, keepdims=True)
        b_col = jnp.sum(jnp.where(causal, lf_row, 0.0), axis=1, keepdims=True)
        b_row = jnp.sum(jnp.where(causal_t, lf_col, 0.0), axis=0, keepdims=True)
        b_last_col = jnp.sum(jnp.where(same, lf_row, 0.0), axis=1, keepdims=True)

        m_col = jnp.zeros((R, 1), F32)
        n_rows = jnp.zeros((R, dqk), F32)
        for s in range(nseg):
            m_s = m_prev[s:s + 1, h:h + 1]
            m_col = jnp.where(row_seg == s, m_s, m_col)
            n_rows = jnp.where(row_seg == s, n_ref[0, s, h:h + 1, :], n_rows)

        dlog = jnp.where(causal, b_col - b_row + ig_row, -jnp.inf)
        inter = b_col + m_col
        m_t = jnp.maximum(inter, jnp.max(dlog, axis=1, keepdims=True))
        w_intra = jnp.exp(dlog - m_t)
        w_inter = jnp.exp(inter - m_t)
        sc = _dot_nt(q, k) * scale * w_intra

        qc = jnp.zeros((R, dv), F32)
        for s in range(nseg):
            qc_s = jnp.dot(q, c_ref[0, s, h].astype(BF16), preferred_element_type=F32)
            qc = qc_s if nseg == 1 else jnp.where(row_seg == s, qc_s, qc)
        num = jnp.dot(sc.astype(BF16), v, preferred_element_type=F32) + w_inter * qc
        qn = jnp.sum(qf * n_rows, axis=1, keepdims=True)
        den = jnp.sum(sc, axis=1, keepdims=True) + w_inter * qn
        hc = num / jnp.maximum(jnp.abs(den), jnp.exp(-m_t))

        hn = hc * lax.rsqrt(jnp.mean(hc * hc, axis=1, keepdims=True) + EPS) * hg_ref[:, h * dv:(h + 1) * dv]
        y_ref[:, h * dv:(h + 1) * dv] = (hn * jax.nn.sigmoid(o.astype(F32))).astype(y_ref.dtype)

        g = b_last_col - b_col + ig_col
        m_new_col = jnp.zeros((R, 1), F32)
        decays = []
        for s in range(nseg):
            in_s = row_seg == s
            m_s = m_prev[s:s + 1, h:h + 1]
            b_last_s = jnp.sum(jnp.where(in_s, lf_col, 0.0), axis=0, keepdims=True)
            g_max_s = jnp.max(jnp.where(in_s, g, -jnp.inf), axis=0, keepdims=True)
            m_new_s = jnp.maximum(b_last_s + m_s, g_max_s)
            decays.append(jnp.exp(b_last_s + m_s - m_new_s))
            m_new_col = jnp.where(in_s, m_new_s, m_new_col)
            m_new_all = jnp.where((seg_i == s) & (head_i == h), m_new_s, m_new_all)
        w_k = jnp.exp(g - m_new_col) * scale
        wv = w_k * vf
        wk = w_k * kf
        for s in range(nseg):
            in_s = row_seg == s
            wv_s = wv if nseg == 1 else jnp.where(in_s, wv, 0.0)
            wk_s = wk if nseg == 1 else jnp.where(in_s, wk, 0.0)
            c_ref[0, s, h] = decays[s] * c_ref[0, s, h] + _dot_tn(k, wv_s.astype(BF16))
            n_ref[0, s, h:h + 1, :] = decays[s] * n_ref[0, s, h:h + 1, :] + jnp.sum(wk_s, axis=0, keepdims=True)

    m_ref[0, 0] = m_new_all


def mlstm_mixer(proj, gate_pre, bias, head_g, c0, n0, m0, *, row0, batch, seq, nseg, chunk):
    H = M_HEADS
    dqk, dv = c0.shape[-2], c0.shape[-1]
    R = nseg * chunk
    nchunks = seq // chunk
    assert nseg == 1 or nchunks == 1
    nblocks = batch // nseg
    blk0 = row0 // R
    rows = lambda i, c: (blk0 + i * nchunks + c, 0)
    width = proj.shape[1]
    c0 = c0.reshape(nblocks, nseg, H, dqk, dv)
    n0 = n0.reshape(nblocks, nseg, H, dqk)
    m0 = m0.reshape(nblocks, 1, nseg, H)
    y, c_out, n_out, m_out = pl.pallas_call(
        functools.partial(_mlstm_kernel, nseg, chunk, dqk, dv),
        grid=(nblocks, nchunks),
        in_specs=[
            pl.BlockSpec((R, width), rows),
            pl.BlockSpec((R, gate_pre.shape[1]), rows),
            pl.BlockSpec((1, 2 * H), lambda i, c: (0, 0)),
            pl.BlockSpec((1, H * dv), lambda i, c: (0, 0)),
            pl.BlockSpec((1, nseg, H, dqk, dv), lambda i, c: (i, 0, 0, 0, 0)),
            pl.BlockSpec((1, nseg, H, dqk), lambda i, c: (i, 0, 0, 0)),
            pl.BlockSpec((1, 1, nseg, H), lambda i, c: (i, 0, 0, 0)),
        ],
        out_specs=[
            pl.BlockSpec((R, H * dv), lambda i, c: (i * nchunks + c, 0)),
            pl.BlockSpec((1, nseg, H, dqk, dv), lambda i, c: (i, 0, 0, 0, 0)),
            pl.BlockSpec((1, nseg, H, dqk), lambda i, c: (i, 0, 0, 0)),
            pl.BlockSpec((1, 1, nseg, H), lambda i, c: (i, 0, 0, 0)),
        ],
        out_shape=[
            jax.ShapeDtypeStruct((batch * seq, H * dv), BF16),
            jax.ShapeDtypeStruct(c0.shape, F32),
            jax.ShapeDtypeStruct(n0.shape, F32),
            jax.ShapeDtypeStruct(m0.shape, F32),
        ],
        compiler_params=_params("arbitrary", "arbitrary"),
        name="mlstm_mixer",
    )(proj, gate_pre, bias.reshape(1, 2 * H), head_g.reshape(1, H * dv), c0, n0, m0)
    return (y, c_out.reshape(batch, H, dqk, dv), n_out.reshape(batch, H, dqk), m_out.reshape(batch, H))


def _mlstm_long_kernel(R, dqk, dv,
                       p_ref, gp_ref, bias_ref, hg_ref, c0_ref, n0_ref, m0_ref,
                       y_ref, c_ref, n_ref, m_ref, kt_scr):
    H = M_HEADS
    s1 = H * dqk
    scale = dqk ** -0.5

    @pl.when(pl.program_id(1) == 0)
    def _():
        c_ref[...] = c0_ref[...]
        n_ref[...] = n0_ref[...]
        m_ref[...] = m0_ref[...]

    r = lax.broadcasted_iota(jnp.int32, (R, R), 0)
    c = lax.broadcasted_iota(jnp.int32, (R, R), 1)
    causal = c <= r
    tril = causal.astype(BF16)

    capped = GATE_CAP * jnp.tanh((gp_ref[...] + bias_ref[...]) / GATE_CAP)
    cum = _cumsum_rows_exact(tril, _log_sigmoid(capped))
    ig_on_f = pltpu.roll(capped, H, axis=1)
    m_prev = m_ref[0]
    b_last = cum[R - 1:R, :]
    g_all = b_last - cum + ig_on_f
    m_new = jnp.maximum(b_last + m_prev, jnp.max(g_all, axis=0, keepdims=True))
    decay_all = jnp.exp(b_last + m_prev - m_new)
    wk_all = jnp.exp(g_all - m_new) * scale
    inter_all = cum + m_prev
    rows_t = (ig_on_f - cum).T

    RB = min(R, M_ROW_BLOCK)
    block_masks = [lax.broadcasted_iota(jnp.int32, (RB, (i + 1) * RB), 1)
                   <= lax.broadcasted_iota(jnp.int32, (RB, (i + 1) * RB), 0) + i * RB for i in range(R // RB)]

    for h in range(H):
        lane = H + h
        q_cols = slice(h * dqk, (h + 1) * dqk)
        k_cols = slice(s1 + h * dqk, s1 + (h + 1) * dqk)
        v_cols = slice(2 * s1 + h * dv, 2 * s1 + (h + 1) * dv)
        o_cols = slice(2 * s1 + H * dv + h * dv, 2 * s1 + H * dv + (h + 1) * dv)
        c_prev = c_ref[0, h]
        c_bf = c_prev.astype(BF16)
        n_prev = n_ref[0, h:h + 1, :]
        hg = hg_ref[:, h * dv:(h + 1) * dv]
        kt_scr[h] = p_ref[:, k_cols].T

        for i in range(R // RB):
            rows = slice(i * RB, (i + 1) * RB)
            ncols = (i + 1) * RB
            q = p_ref[rows, q_cols]
            gate_diff = cum[rows, lane:lane + 1] + rows_t[lane:lane + 1, :ncols]
            dlog = jnp.where(block_masks[i], gate_diff, -jnp.inf)
            inter = inter_all[rows, lane:lane + 1]
            m_t = jnp.maximum(inter, jnp.max(dlog, axis=1, keepdims=True))
            w_intra = jnp.exp(dlog - m_t)
            w_inter = jnp.exp(inter - m_t)
            sc = jnp.dot(q, kt_scr[h, :, :ncols], preferred_element_type=F32) * scale * w_intra
            num = (jnp.dot(sc.astype(BF16), p_ref[:ncols, v_cols], preferred_element_type=F32)
                   + w_inter * jnp.dot(q, c_bf, preferred_element_type=F32))
            qn = jnp.sum(q.astype(F32) * n_prev, axis=1, keepdims=True)
            den = jnp.sum(sc, axis=1, keepdims=True) + w_inter * qn
            inv = 1.0 / jnp.maximum(jnp.abs(den), jnp.exp(-m_t))
            norm = inv * lax.rsqrt(inv * inv * jnp.mean(num * num, axis=1, keepdims=True) + EPS)
            o = p_ref[rows, o_cols].astype(F32)
            y_ref[rows, h * dv:(h + 1) * dv] = (num * norm * hg * jax.nn.sigmoid(o)).astype(y_ref.dtype)

        k = p_ref[:, k_cols]
        w_k = wk_all[:, lane:lane + 1]
        decay = decay_all[:, lane:lane + 1]
        wv = (w_k * p_ref[:, v_cols].astype(F32)).astype(BF16)
        c_ref[0, h] = decay * c_prev + _dot_tn(k, wv)
        n_ref[0, h:h + 1, :] = decay * n_prev + jnp.sum(w_k * k.astype(F32), axis=0, keepdims=True)

    m_ref[0] = m_new


def mlstm_long_mixer(proj, gate_pre, bias, head_g, c0, n0, m0, *, row0, batch, seq, rows):
    H = M_HEADS
    dqk, dv = c0.shape[-2], c0.shape[-1]
    R = rows
    lanes = gate_pre.shape[1]
    assert seq % R == 0 and row0 % R == 0 and lanes >= 2 * H
    nsteps = seq // R
    blk0 = row0 // R
    rows_of = lambda i, c: (blk0 + i * nsteps + c, 0)
    bias_l = jnp.pad(bias.reshape(1, 2 * H), ((0, 0), (0, lanes - 2 * H)))
    m0_l = jnp.pad(m0, ((0, 0), (H, lanes - 2 * H))).reshape(batch, 1, lanes)
    y, c_out, n_out, m_out = pl.pallas_call(
        functools.partial(_mlstm_long_kernel, R, dqk, dv),
        grid=(batch, nsteps),
        in_specs=[
            pl.BlockSpec((R, proj.shape[1]), rows_of),
            pl.BlockSpec((R, lanes), rows_of),
            pl.BlockSpec((1, lanes), lambda i, c: (0, 0)),
            pl.BlockSpec((1, H * dv), lambda i, c: (0, 0)),
            pl.BlockSpec((1, H, dqk, dv), lambda i, c: (i, 0, 0, 0)),
            pl.BlockSpec((1, H, dqk), lambda i, c: (i, 0, 0)),
            pl.BlockSpec((1, 1, lanes), lambda i, c: (i, 0, 0)),
        ],
        out_specs=[
            pl.BlockSpec((R, H * dv), lambda i, c: (i * nsteps + c, 0)),
            pl.BlockSpec((1, H, dqk, dv), lambda i, c: (i, 0, 0, 0)),
            pl.BlockSpec((1, H, dqk), lambda i, c: (i, 0, 0)),
            pl.BlockSpec((1, 1, lanes), lambda i, c: (i, 0, 0)),
        ],
        out_shape=[
            jax.ShapeDtypeStruct((batch * seq, H * dv), BF16),
            jax.ShapeDtypeStruct(c0.shape, F32),
            jax.ShapeDtypeStruct(n0.shape, F32),
            jax.ShapeDtypeStruct(m0_l.shape, F32),
        ],
        scratch_shapes=[pltpu.VMEM((H, dqk, R), BF16)],
        compiler_params=_params("arbitrary", "arbitrary"),
        name="mlstm_long_mixer",
    )(proj, gate_pre, bias_l, head_g.reshape(1, H * dv), c0, n0, m0_l)
    return y, c_out, n_out, m_out[:, 0, H:2 * H]


def _mlstm_short_kernel(nb, T, dqk, dv,
                        p_ref, gp_ref, bias_ref, hg_ref, c0_ref, n0_ref, m0_ref,
                        y_ref, c_ref, n_ref, m_ref):
    H = M_HEADS
    R = nb * T
    PAIR = 2 * T
    s1 = H * dqk
    scale = dqk ** -0.5
    assert T % 8 == 0 and nb % 2 == 0

    r = lax.broadcasted_iota(jnp.int32, (R, R), 0)
    c = lax.broadcasted_iota(jnp.int32, (R, R), 1)
    causal = ((r // T) == (c // T)) & (c <= r)
    tril = causal.astype(BF16)
    seq_cols = (lax.broadcasted_iota(jnp.int32, (nb, 1, R), 2) // T
                == lax.broadcasted_iota(jnp.int32, (nb, 1, R), 0)).astype(BF16)
    first_of_pair = lax.broadcasted_iota(jnp.int32, (PAIR, 1), 0) < T

    def per_seq(x):
        return x.reshape(nb, T, x.shape[-1])

    def seq_rows(x3):
        return jnp.broadcast_to(x3, (nb, T, x3.shape[-1])).reshape(R, x3.shape[-1])

    capped = GATE_CAP * jnp.tanh((gp_ref[...] + bias_ref[...]) / GATE_CAP)
    cum = _cumsum_rows_exact(tril, _log_sigmoid(capped))
    ig_on_f = pltpu.roll(capped, H, axis=1)
    m_prev = m0_ref[...]
    b_last = seq_rows(per_seq(cum)[:, T - 1:T, :])
    g_all = b_last - cum + ig_on_f
    m_new = jnp.maximum(b_last + m_prev, seq_rows(jnp.max(per_seq(g_all), axis=1, keepdims=True)))
    decay_all = jnp.exp(b_last + m_prev - m_new)
    wk_all = jnp.exp(g_all - m_new) * scale
    inter_all = cum + m_prev
    rows_t = (ig_on_f - cum).T

    for h in range(H):
        lane = H + h
        q = p_ref[:, h * dqk:(h + 1) * dqk]
        k = p_ref[:, s1 + h * dqk:s1 + (h + 1) * dqk]
        v = p_ref[:, 2 * s1 + h * dv:2 * s1 + (h + 1) * dv]
        o = p_ref[:, 2 * s1 + H * dv + h * dv:2 * s1 + H * dv + (h + 1) * dv]
        kf = k.astype(F32)

        dlog = jnp.where(causal, cum[:, lane:lane + 1] + rows_t[lane:lane + 1, :], -jnp.inf)
        inter = inter_all[:, lane:lane + 1]
        m_t = jnp.maximum(inter, jnp.max(dlog, axis=1, keepdims=True))
        w_intra = jnp.exp(dlog - m_t)
        w_inter = jnp.exp(inter - m_t)
        sc = _dot_nt(q, k) * scale * w_intra
        num_intra = jnp.dot(sc.astype(BF16), v, preferred_element_type=F32)
        n_prev = n0_ref[h]
        qn = jnp.sum(q.astype(F32) * n_prev, axis=1, keepdims=True)
        den = jnp.sum(sc, axis=1, keepdims=True) + w_inter * qn
        inv = 1.0 / jnp.maximum(jnp.abs(den), jnp.exp(-m_t))
        hg = hg_ref[:, h * dv:(h + 1) * dv]

        w_k = wk_all[:, lane:lane + 1]
        k_t = kf.T.astype(BF16)
        stacked = (k_t[None] * seq_cols).reshape(nb * dqk, R)
        upd = jnp.dot(stacked, (w_k * v.astype(F32)).astype(BF16), preferred_element_type=F32)
        n_ref[h] = (decay_all[:, lane:lane + 1] * n_prev
                    + seq_rows(jnp.sum(per_seq(w_k * kf), axis=1, keepdims=True)))

        for pr in range(nb // 2):
            rows = slice(pr * PAIR, (pr + 1) * PAIR)
            lhs = p_ref[rows, h * dqk:(h + 1) * dqk]
            qc = []
            for b in (2 * pr, 2 * pr + 1):
                c_prev = c0_ref[0, b, h]
                qc.append(jnp.dot(lhs, c_prev.astype(BF16), preferred_element_type=F32))
                c_ref[0, b, h] = decay_all[b * T:b * T + 1, lane:lane + 1] * c_prev + upd[b * dqk:(b + 1) * dqk, :]
            num = num_intra[rows, :] + w_inter[rows, :] * jnp.where(first_of_pair, qc[0], qc[1])
            hc = num * inv[rows, :]
            hn = hc * lax.rsqrt(jnp.mean(hc * hc, axis=1, keepdims=True) + EPS) * hg
            y_ref[rows, h * dv:(h + 1) * dv] = (hn * jax.nn.sigmoid(o[rows, :].astype(F32))).astype(y_ref.dtype)

    m_ref[...] = m_new


def mlstm_short_mixer(proj, gate_pre, bias, head_g, c0, n0, m0, *, row0, batch, seq, nb):
    H = M_HEADS
    dqk, dv = c0.shape[-2], c0.shape[-1]
    R = nb * seq
    lanes = gate_pre.shape[1]
    assert batch % nb == 0 and row0 % R == 0 and lanes >= 2 * H
    nblocks = batch // nb
    blk0 = row0 // R
    bias_l = jnp.pad(bias.reshape(1, 2 * H), ((0, 0), (0, lanes - 2 * H)))
    m0_l = jnp.repeat(jnp.pad(m0, ((0, 0), (H, lanes - 2 * H))), seq, axis=0)
    n0_t = jnp.repeat(jnp.swapaxes(n0, 0, 1), seq, axis=1)
    y, c_out, n_out, m_out = pl.pallas_call(
        functools.partial(_mlstm_short_kernel, nb, seq, dqk, dv),
        grid=(nblocks,),
        in_specs=[
            pl.BlockSpec((R, proj.shape[1]), lambda i: (blk0 + i, 0)),
            pl.BlockSpec((R, lanes), lambda i: (blk0 + i, 0)),
            pl.BlockSpec((1, lanes), lambda i: (0, 0)),
            pl.BlockSpec((1, H * dv), lambda i: (0, 0)),
            pl.BlockSpec((1, nb, H, dqk, dv), lambda i: (i, 0, 0, 0, 0)),
            pl.BlockSpec((H, R, dqk), lambda i: (0, i, 0)),
            pl.BlockSpec((R, lanes), lambda i: (i, 0)),
        ],
        out_specs=[
            pl.BlockSpec((R, H * dv), lambda i: (i, 0)),
            pl.BlockSpec((1, nb, H, dqk, dv), lambda i: (i, 0, 0, 0, 0)),
            pl.BlockSpec((H, R, dqk), lambda i: (0, i, 0)),
            pl.BlockSpec((R, lanes), lambda i: (i, 0)),
        ],
        out_shape=[
            jax.ShapeDtypeStruct((batch * seq, H * dv), BF16),
            jax.ShapeDtypeStruct((nblocks, nb, H, dqk, dv), F32),
            jax.ShapeDtypeStruct(n0_t.shape, F32),
            jax.ShapeDtypeStruct(m0_l.shape, F32),
        ],
        compiler_params=_params("arbitrary"),
        name="mlstm_short_mixer",
    )(proj, gate_pre, bias_l, head_g.reshape(1, H * dv), c0.reshape(nblocks, nb, H, dqk, dv), n0_t, m0_l)
    return (y, c_out.reshape(batch, H, dqk, dv), jnp.swapaxes(n_out[:, ::seq, :], 0, 1), m_out[::seq, H:2 * H])


def _hgrn_kernel(nseg, L, heads, dk, dv,
                 p_ref, lb_ref, gn_ref, s0_ref, y_ref, s_ref):
    R = nseg * L
    fdim = heads * dk
    scale = dk ** -0.5
    sub = min(L, H_SUB)
    assert L % sub == 0 and L // sub in (1, 2)

    @pl.when(pl.program_id(1) == 0)
    def _():
        s_ref[...] = s0_ref[...]

    r, c, same = _segment_masks(nseg, L)
    causal = same & (c <= r)
    same_sub = (r // sub) == (c // sub)
    diag_mask = causal & same_sub
    tril = causal.astype(F32)

    row = lax.broadcasted_iota(jnp.int32, (R, 1), 0)
    row_seg = row // L
    pos = row % L
    mid_sel = (same_sub & ((c % sub) == (sub // 2))).astype(F32)
    last_sel = (same & ((c % L) == (L - 1))).astype(F32)
    if L // sub == 2:
        bnd_sel = (same & ((c % L) == (sub - 1))).astype(F32)
        upper = pos >= sub
    hp = lax.Precision.HIGHEST

    for h in range(heads):
        q = p_ref[:, h * dk:(h + 1) * dk].astype(F32) * scale
        fpre = p_ref[:, fdim + h * dk:fdim + (h + 1) * dk].astype(F32)
        iv = p_ref[:, 2 * fdim + h * dv:2 * fdim + (h + 1) * dv]
        gate = p_ref[:, 2 * fdim + heads * dv + h * dv:2 * fdim + heads * dv + (h + 1) * dv].astype(F32)
        lb = lb_ref[:, h * dk:(h + 1) * dk]
        f = lb + (1.0 - lb) * jax.nn.sigmoid(fpre)
        log_f = jnp.log(f)
        k = 1.0 - f

        a = jnp.dot(tril, log_f, preferred_element_type=F32, precision=hp)
        a_mid = jnp.dot(mid_sel, a, preferred_element_type=F32, precision=hp)
        a_last = jnp.dot(last_sel, a, preferred_element_type=F32, precision=hp)

        qd = (q * jnp.exp(a - a_mid)).astype(BF16)
        kd = (k * jnp.exp(a_mid - a)).astype(BF16)
        sc = jnp.where(diag_mask, _dot_nt(qd, kd), 0.0)
        if L // sub == 2:
            a_bnd = jnp.dot(bnd_sel, a, preferred_element_type=F32, precision=hp)
            qo = jnp.where(upper, q * jnp.exp(jnp.minimum(a - a_bnd, 0.0)), 0.0).astype(BF16)
            ko = jnp.where(upper, 0.0, k * jnp.exp(jnp.minimum(a_bnd - a, 0.0))).astype(BF16)
            sc = sc + jnp.where(same, _dot_nt(qo, ko), 0.0)

        qe = (q * jnp.exp(a)).astype(BF16)
        inter = jnp.zeros((R, dv), F32)
        for s in range(nseg):
            inter_s = jnp.dot(qe, s_ref[0, s, h].astype(BF16), preferred_element_type=F32)
            inter = inter_s if nseg == 1 else jnp.where(row_seg == s, inter_s, inter)
        oc = jnp.dot(sc.astype(BF16), iv, preferred_element_type=F32) + inter

        on = oc * lax.rsqrt(jnp.mean(oc * oc, axis=1, keepdims=True) + EPS) * gn_ref[:, h * dv:(h + 1) * dv]
        y_ref[:, h * dv:(h + 1) * dv] = (on * (gate * jax.nn.sigmoid(gate))).astype(y_ref.dtype)

        ke = k * jnp.exp(a_last - a)
        for s in range(nseg):
            in_s = row_seg == s
            ke_s = ke if nseg == 1 else jnp.where(in_s, ke, 0.0)
            a_last_s = a[s * L + L - 1:s * L + L, :]
            upd = _dot_tn(ke_s.astype(BF16), iv)
            decay_col = jnp.sum(jnp.where(lax.broadcasted_iota(jnp.int32, (dk, dk), 0)
                                          == lax.broadcasted_iota(jnp.int32, (dk, dk), 1),
                                          jnp.exp(a_last_s), 0.0), axis=1, keepdims=True)
            s_ref[0, s, h] = decay_col * s_ref[0, s, h] + upd


def hgrn_mixer(proj, lb, gn_g, s0, *, row0, batch, seq, nseg, chunk):
    heads, dk, dv = s0.shape[-3], s0.shape[-2], s0.shape[-1]
    R = nseg * chunk
    nchunks = seq // chunk
    assert nseg == 1 or nchunks == 1
    nblocks = batch // nseg
    blk0 = row0 // R
    rows = lambda i, c: (blk0 + i * nchunks + c, 0)
    s0 = s0.reshape(nblocks, nseg, heads, dk, dv)
    y, s_out = pl.pallas_call(
        functools.partial(_hgrn_kernel, nseg, chunk, heads, dk, dv),
        grid=(nblocks, nchunks),
        in_specs=[
            pl.BlockSpec((R, proj.shape[1]), rows),
            pl.BlockSpec((1, heads * dk), lambda i, c: (0, 0)),
            pl.BlockSpec((1, heads * dv), lambda i, c: (0, 0)),
            pl.BlockSpec((1, nseg, heads, dk, dv), lambda i, c: (i, 0, 0, 0, 0)),
        ],
        out_specs=[
            pl.BlockSpec((R, heads * dv), lambda i, c: (i * nchunks + c, 0)),
            pl.BlockSpec((1, nseg, heads, dk, dv), lambda i, c: (i, 0, 0, 0, 0)),
        ],
        out_shape=[
            jax.ShapeDtypeStruct((batch * seq, heads * dv), BF16),
            jax.ShapeDtypeStruct(s0.shape, F32),
        ],
        compiler_params=_params("arbitrary", "arbitrary"),
        name="hgrn_mixer",
    )(proj, lb.reshape(1, heads * dk), gn_g.reshape(1, heads * dv), s0)
    return y, s_out.reshape(batch, heads, dk, dv)


def _hgrn_short_kernel(nb, T, heads, dk, dv,
                       p_ref, lb_ref, gn_ref, s0_ref, y_ref, s_ref,
                       qd_scr, kd_scr, qe_scr, ke_scr, dec_scr):
    R = nb * T
    fdim = heads * dk
    scale = dk ** -0.5
    W = 2 * dk
    PAIR = 2 * T
    assert T % 8 == 0 and T <= H_SUB and nb % 2 == 0

    r = lax.broadcasted_iota(jnp.int32, (R, R), 0)
    c = lax.broadcasted_iota(jnp.int32, (R, R), 1)
    causal = ((r // T) == (c // T)) & (c <= r)
    tril = causal.astype(BF16)
    seq_cols = (lax.broadcasted_iota(jnp.int32, (nb, 1, R), 2) // T
                == lax.broadcasted_iota(jnp.int32, (nb, 1, R), 0)).astype(BF16)
    first_of_pair = lax.broadcasted_iota(jnp.int32, (PAIR, 1), 0) < T

    def minus_ref_row(a, row):
        a3 = a.reshape(nb, T, a.shape[-1])
        return (a3 - a3[:, row:row + 1, :]).reshape(a.shape)

    for g in range(fdim // W):
        sl = slice(g * W, (g + 1) * W)
        q = p_ref[:, sl].astype(F32) * scale
        fpre = p_ref[:, fdim + g * W:fdim + (g + 1) * W].astype(F32)
        lb = lb_ref[:, sl]
        f = lb + (1.0 - lb) * jax.nn.sigmoid(fpre)
        k = 1.0 - f
        a = _cumsum_rows_exact(tril, jnp.log(f))
        d = minus_ref_row(a, T // 2)
        to_last = -minus_ref_row(a, T - 1)
        qd_scr[:, sl] = (q * jnp.exp(d)).astype(BF16)
        kd_scr[:, sl] = (k * jnp.exp(-d)).astype(BF16)
        qe_scr[:, sl] = (q * jnp.exp(a)).astype(BF16)
        ke_scr[:, sl] = k * jnp.exp(to_last)
        dec_scr[:, sl] = jnp.exp(a + to_last)

    for h in range(heads):
        sl = slice(h * dk, (h + 1) * dk)
        iv = p_ref[:, 2 * fdim + h * dv:2 * fdim + (h + 1) * dv]
        sc = jnp.where(causal, _dot_nt(qd_scr[:, sl], kd_scr[:, sl]), 0.0)
        oc = jnp.dot(sc.astype(BF16), iv, preferred_element_type=F32)
        ke_t = ke_scr[:, sl].T.astype(BF16)
        dec_t = dec_scr[:, sl].T
        stacked = (ke_t[None] * seq_cols).reshape(nb * dk, R)
        upd = jnp.dot(stacked, iv, preferred_element_type=F32)
        gn = gn_ref[:, h * dv:(h + 1) * dv]
        for pr in range(nb // 2):
            rows = slice(pr * PAIR, (pr + 1) * PAIR)
            lhs = qe_scr[rows, sl]
            inter = []
            for b in (2 * pr, 2 * pr + 1):
                s_prev = s0_ref[0, b, h]
                inter.append(jnp.dot(lhs, s_prev.astype(BF16), preferred_element_type=F32))
                s_ref[0, b, h] = dec_t[:, b * T:b * T + 1] * s_prev + upd[b * dk:(b + 1) * dk, :]
            o2 = oc[rows, :] + jnp.where(first_of_pair, inter[0], inter[1])
            gate = p_ref[rows, 2 * fdim + heads * dv + h * dv:2 * fdim + heads * dv + (h + 1) * dv].astype(F32)
            on = o2 * lax.rsqrt(jnp.mean(o2 * o2, axis=1, keepdims=True) + EPS) * gn
            y_ref[rows, h * dv:(h + 1) * dv] = (on * (gate * jax.nn.sigmoid(gate))).astype(y_ref.dtype)


def hgrn_short_mixer(proj, lb, gn_g, s0, *, row0, batch, seq, nb):
    heads, dk, dv = s0.shape[-3], s0.shape[-2], s0.shape[-1]
    R = nb * seq
    assert batch % nb == 0 and row0 % R == 0
    nblocks = batch // nb
    blk0 = row0 // R
    fdim = heads * dk
    s0 = s0.reshape(nblocks, nb, heads, dk, dv)
    y, s_out = pl.pallas_call(
        functools.partial(_hgrn_short_kernel, nb, seq, heads, dk, dv),
        grid=(nblocks,),
        in_specs=[
            pl.BlockSpec((R, proj.shape[1]), lambda i: (blk0 + i, 0)),
            pl.BlockSpec((1, fdim), lambda i: (0, 0)),
            pl.BlockSpec((1, heads * dv), lambda i: (0, 0)),
            pl.BlockSpec((1, nb, heads, dk, dv), lambda i: (i, 0, 0, 0, 0)),
        ],
        out_specs=[
            pl.BlockSpec((R, heads * dv), lambda i: (i, 0)),
            pl.BlockSpec((1, nb, heads, dk, dv), lambda i: (i, 0, 0, 0, 0)),
        ],
        out_shape=[
            jax.ShapeDtypeStruct((batch * seq, heads * dv), BF16),
            jax.ShapeDtypeStruct(s0.shape, F32),
        ],
        scratch_shapes=[
            pltpu.VMEM((R, fdim), BF16), pltpu.VMEM((R, fdim), BF16), pltpu.VMEM((R, fdim), BF16),
            pltpu.VMEM((R, fdim), F32), pltpu.VMEM((R, fdim), F32),
        ],
        compiler_params=_params("arbitrary"),
        name="hgrn_short_mixer",
    )(proj, lb.reshape(1, fdim), gn_g.reshape(1, heads * dv), s0)
    return y, s_out.reshape(batch, heads, dk, dv)


def _hgrn_long_kernel(R, heads, dk, dv, levels,
                      p_ref, lb_ref, gn_ref, s0_ref, y_ref, s_ref,
                      st_scr, dec_scr, qd_scr, kd_scr, qe_scr, ke_scr, ql_scr, kl_scr, k_scr, a_scr,
                      kt_scr, sb_scr):
    fdim = heads * dk
    eps_unscaled = EPS * dk
    W = 2 * dk
    last_step = pl.num_programs(1) - 1

    @pl.when(pl.program_id(1) == 0)
    def _():
        for h in range(heads):
            st_scr[h] = s0_ref[0, h].T

    r = lax.broadcasted_iota(jnp.int32, (R, R), 0)
    c = lax.broadcasted_iota(jnp.int32, (R, R), 1)
    tril = (c <= r).astype(BF16)
    diag_mask = ((r // H_SUB) == (c // H_SUB)) & (c <= r)
    level_masks = [((r // G) == (c // G)) & ((r % G) >= G // 2) & ((c % G) < G // 2) for G in levels]

    HB = 2 * H_SUB
    half = slice(0, H_SUB), slice(H_SUB, HB)
    zeros_half = jnp.zeros((H_SUB, W), BF16)
    zeros_block = jnp.zeros((HB, W), BF16)

    for g in range(fdim // W):
        sl = slice(g * W, (g + 1) * W)
        fpre = p_ref[:, fdim + g * W:fdim + (g + 1) * W].astype(F32)
        lb = lb_ref[:, sl]
        f = lb + (1.0 - lb) * jax.nn.sigmoid(fpre)
        k_scr[:, sl] = 1.0 - f
        a_scr[:, sl] = _cumsum_rows_exact(tril, jnp.log(f))
        a_last = a_scr[R - 1:R, sl]
        dec_scr[:, sl] = jnp.exp(a_last)

        for blk in range(R // HB):
            row0 = blk * HB
            rows = slice(row0, row0 + HB)
            q = p_ref[rows, sl].astype(F32)
            k = k_scr[rows, sl]
            a = a_scr[rows, sl]
            a3 = a.reshape(2, H_SUB, W)
            d = (a3 - a3[:, H_SUB // 2:H_SUB // 2 + 1, :]).reshape(HB, W)
            qd_scr[rows, sl] = (q * jnp.exp(d)).astype(BF16)
            kd_scr[rows, sl] = (k * jnp.exp(-d)).astype(BF16)
            qe_scr[rows, sl] = (q * jnp.exp(a)).astype(BF16)
            ke_scr[rows, sl] = (k * jnp.exp(a_last - a)).astype(BF16)
            for li, G in enumerate(levels):
                bnd = (row0 // G) * G + G // 2 - 1
                a_bnd = a_scr[bnd:bnd + 1, sl]
                if G == HB:
                    lo, up = half
                    kl_scr[li, row0:row0 + H_SUB, sl] = (k[lo] * jnp.exp(a_bnd - a[lo])).astype(BF16)
                    ql_scr[li, row0:row0 + H_SUB, sl] = zeros_half
                    ql_scr[li, row0 + H_SUB:row0 + HB, sl] = (q[up] * jnp.exp(a[up] - a_bnd)).astype(BF16)
                    kl_scr[li, row0 + H_SUB:row0 + HB, sl] = zeros_half
                elif row0 % G >= G // 2:
                    ql_scr[li, rows, sl] = (q * jnp.exp(a - a_bnd)).astype(BF16)
                    kl_scr[li, rows, sl] = zeros_block
                else:
                    kl_scr[li, rows, sl] = (k * jnp.exp(a_bnd - a)).astype(BF16)
                    ql_scr[li, rows, sl] = zeros_block

    for h in range(heads):
        sl = slice(h * dk, (h + 1) * dk)
        st = st_scr[h]
        kt_scr[h, 0] = kd_scr[:, sl].T
        for li in range(len(levels)):
            kt_scr[h, 1 + li] = kl_scr[li, :, sl].T
        sb_scr[h] = st.astype(BF16).T
        sc = jnp.where(diag_mask, jnp.dot(qd_scr[:, sl], kt_scr[h, 0], preferred_element_type=F32), 0.0)
        for li in range(len(levels)):
            sc = jnp.where(level_masks[li],
                           jnp.dot(ql_scr[li, :, sl], kt_scr[h, 1 + li], preferred_element_type=F32), sc)
        iv = p_ref[:, 2 * fdim + h * dv:2 * fdim + (h + 1) * dv]
        gate = p_ref[:, 2 * fdim + heads * dv + h * dv:2 * fdim + heads * dv + (h + 1) * dv].astype(F32)
        oc = (jnp.dot(sc.astype(BF16), iv, preferred_element_type=F32)
              + jnp.dot(qe_scr[:, sl], sb_scr[h], preferred_element_type=F32))
        st_scr[h] = st * dec_scr[:, sl] + _dot_tn(iv, ke_scr[:, sl])
        on = oc * lax.rsqrt(jnp.mean(oc * oc, axis=1, keepdims=True) + eps_unscaled) * gn_ref[:, h * dv:(h + 1) * dv]
        y_ref[:, h * dv:(h + 1) * dv] = (on * (gate * jax.nn.sigmoid(gate))).astype(y_ref.dtype)

    @pl.when(pl.program_id(1) == last_step)
    def _():
        for h in range(heads):
            s_ref[0, h] = st_scr[h].T


def hgrn_long_mixer(proj, lb, gn_g, s0, *, row0, batch, seq, rows):
    heads, dk, dv = s0.shape[-3], s0.shape[-2], s0.shape[-1]
    R = rows
    assert seq % R == 0 and R % (2 * H_SUB) == 0 and row0 % R == 0
    levels = []
    G = 2 * H_SUB
    while G <= R:
        levels.append(G)
        G *= 2
    assert levels[-1] == R
    nsteps = seq // R
    blk0 = row0 // R
    fdim = heads * dk
    wide = lambda: pltpu.VMEM((R, fdim), BF16)
    return pl.pallas_call(
        functools.partial(_hgrn_long_kernel, R, heads, dk, dv, tuple(levels)),
        grid=(batch, nsteps),
        in_specs=[
            pl.BlockSpec((R, proj.shape[1]), lambda i, c: (blk0 + i * nsteps + c, 0)),
            pl.BlockSpec((1, fdim), lambda i, c: (0, 0)),
            pl.BlockSpec((1, heads * dv), lambda i, c: (0, 0)),
            pl.BlockSpec((1, heads, dk, dv), lambda i, c: (i, 0, 0, 0)),
        ],
        out_specs=[
            pl.BlockSpec((R, heads * dv), lambda i, c: (i * nsteps + c, 0)),
            pl.BlockSpec((1, heads, dk, dv), lambda i, c: (i, 0, 0, 0)),
        ],
        out_shape=[
            jax.ShapeDtypeStruct((batch * seq, heads * dv), BF16),
            jax.ShapeDtypeStruct(s0.shape, F32),
        ],
        scratch_shapes=[
            pltpu.VMEM((heads, dv, dk), F32),
            pltpu.VMEM((1, fdim), F32),
            wide(), wide(), wide(), wide(),
            pltpu.VMEM((len(levels), R, fdim), BF16),
            pltpu.VMEM((len(levels), R, fdim), BF16),
            pltpu.VMEM((R, fdim), F32),
            pltpu.VMEM((R, fdim), F32),
            pltpu.VMEM((heads, 1 + len(levels), dk, R), BF16),
            pltpu.VMEM((heads, dk, dv), BF16),
        ],
        compiler_params=_params("arbitrary", "arbitrary"),
        name="hgrn_long_mixer",
    )(proj, lb.reshape(1, fdim), gn_g.reshape(1, heads * dv), s0)


def kernel(x_prompt, x_sample, state_mlstm_C, state_mlstm_n, state_mlstm_m, state_hgrn_S,
           norm_mix_g, norm_ffn_g, norm_final_g, mlstm_w_in, mlstm_b_gates, mlstm_head_norm_g,
           mlstm_w_out, hgrn_w_in, hgrn_lower_bounds, hgrn_g_norm_g, hgrn_w_out, ffn_w_up, ffn_w_down):
    bp, tp, d = x_prompt.shape
    bs, ts, _ = x_sample.shape
    np_, ns_ = bp * tp, bs * ts
    depth = norm_mix_g.shape[0]
    H = M_HEADS
    dqk, dv = state_mlstm_C.shape[-2], state_mlstm_C.shape[-1]
    hh, hdk, hdv = state_hgrn_S.shape[-3], state_hgrn_S.shape[-2], state_hgrn_S.shape[-1]

    xs = (x_prompt.reshape(np_, d), x_sample.reshape(ns_, d))

    lb_all = jnp.cumsum(jax.nn.softmax(hgrn_lower_bounds.astype(F32), axis=0), axis=0)
    lb_all = lb_all - lb_all[0]

    tm = 1024
    Cp, Np, Mp, Sp, Cs, Ns, Ms, Ss = [], [], [], [], [], [], [], []
    for i in range(depth):
        j = i // 2
        if i % 2 == 0:
            n_main = 2 * H * dqk + 2 * H * dv
            w_in_t = jnp.swapaxes(mlstm_w_in, 1, 2)
            w_gates = jnp.pad(w_in_t[j, n_main:], ((0, 128 - 2 * H), (0, 0)))
            proj, gate_pre = norm_matmul(xs, norm_mix_g[i], w_in_t, j, n_main, tm=tm, tn=1024,
                                         out_dtype=BF16, w_extra=w_gates, w_rows_are_outputs=True)
            args = (proj, gate_pre, mlstm_b_gates[j], mlstm_head_norm_g[j])
            yp, c_p, n_p, m_p = mlstm_long_mixer(
                *args, jnp.zeros((bp, H, dqk, dv), F32), jnp.zeros((bp, H, dqk), F32), jnp.zeros((bp, H), F32),
                row0=0, batch=bp, seq=tp, rows=256)
            ysm, c_s, n_s, m_s = mlstm_short_mixer(
                *args, state_mlstm_C[j], state_mlstm_n[j], state_mlstm_m[j],
                row0=np_, batch=bs, seq=ts, nb=8)
            Cp.append(c_p); Np.append(n_p); Mp.append(m_p)
            Cs.append(c_s); Ns.append(n_s); Ms.append(m_s)
            w_out = mlstm_w_out
        else:
            proj = norm_matmul(xs, norm_mix_g[i], hgrn_w_in, j, hgrn_w_in.shape[-1], tm=tm, tn=1024, out_dtype=BF16)
            args = (proj, lb_all[i], hgrn_g_norm_g[j])
            yp, s_p = hgrn_long_mixer(*args, jnp.zeros((bp, hh, hdk, hdv), F32),
                                      row0=0, batch=bp, seq=tp, rows=128)
            ysm, s_s = hgrn_short_mixer(*args, state_hgrn_S[j], row0=np_, batch=bs, seq=ts, nb=8)
            Sp.append(s_p); Ss.append(s_s)
            w_out = hgrn_w_out
        x = matmul_residual((yp, ysm), w_out, j, xs, tm=256, tn=d)
        act = norm_swiglu(x, norm_ffn_g[i], ffn_w_up, i, tm=tm, tn=512)
        x = matmul_residual((act,), ffn_w_down, i, (x,), tm=512, tn=512)
        xs = (x,)

    y_prompt = final_norm(x, norm_final_g, row0=0, nrows=np_, tm=512).reshape(bp, tp, d)
    y_sample = final_norm(x, norm_final_g, row0=np_, nrows=ns_, tm=512).reshape(bs, ts, d)
    cat = lambda parts: jnp.stack(parts) if len(parts) > 1 else parts[0][None]
    return (y_prompt, y_sample, cat(Cp), cat(Np), cat(Mp), cat(Sp), cat(Cs), cat(Ns), cat(Ms), cat(Ss))
```

```python
import functools

import jax
import jax.numpy as jnp
from jax import lax
from jax.experimental import pallas as pl
from jax.experimental.pallas import tpu as pltpu

F32 = jnp.float32
BF16 = jnp.bfloat16

EPS = 1e-6
GATE_CAP = 15.0
M_HEADS = 8
M_CHUNK = 64
M_ROW_BLOCK = 256
H_DK = 128
H_CHUNK = 32
H_SUB = 16

VMEM_LIMIT_BYTES = 56 * 1024 * 1024


def _params(*sem):
    return pltpu.CompilerParams(dimension_semantics=sem, vmem_limit_bytes=VMEM_LIMIT_BYTES)


def _row_tiles(xs, tm):
    for x in xs:
        assert x.shape[0] % tm == 0, (x.shape, tm)
    return tuple(x.shape[0] // tm for x in xs)


def _clamped_row_tile(first, count, row_of, col_of, *ids):
    return (jnp.clip(row_of(*ids) - first, 0, count - 1), col_of(*ids))


def _row_specs(tiles, block, row_of, col_of, single_tile_unbuffered=False):
    specs, first = [], 0
    for count in tiles:
        mode = dict(pipeline_mode=pl.Buffered(1)) if single_tile_unbuffered and count == 1 else {}
        specs.append(pl.BlockSpec(block, functools.partial(_clamped_row_tile, first, count, row_of, col_of), **mode))
        first += count
    return specs


def _on_owner(refs, tiles, i, fn):
    if len(refs) == 1:
        fn(refs[0])
        return
    first = 0
    for ref, count in zip(refs, tiles):
        pl.when((i >= first) & (i < first + count))(functools.partial(fn, ref))
        first += count


def _rms_normed(x, g):
    ms = jnp.mean(x * x, axis=-1, keepdims=True)
    return x * lax.rsqrt(ms + EPS) * g


def _norm_matmul_kernel(tiles, with_extra, w_rows_are_outputs, *refs):
    nx = len(tiles)
    x_refs, g_ref, w_ref = refs[:nx], refs[nx], refs[nx + 1]
    if with_extra:
        we_ref, o_ref, oe_ref, h_scr = refs[nx + 2:]
    else:
        o_ref, h_scr = refs[nx + 2:]
    matmul = _dot_nt if w_rows_are_outputs else functools.partial(jnp.dot, preferred_element_type=F32)

    @pl.when(pl.program_id(1) == 0)
    def _():
        def build(x_ref):
            h_scr[...] = _rms_normed(x_ref[...], g_ref[...]).astype(BF16)
            if with_extra:
                oe_ref[...] = matmul(h_scr[...], we_ref[...].astype(BF16))
        _on_owner(x_refs, tiles, pl.program_id(0), build)

    o_ref[...] = matmul(h_scr[...], w_ref[...].astype(BF16)).astype(o_ref.dtype)


def norm_matmul(xs, g, w, layer, n_cols, *, tm, tn, out_dtype, w_extra=None, w_rows_are_outputs=False):
    d = xs[0].shape[1]
    tiles = _row_tiles(xs, tm)
    m = tm * sum(tiles)
    row_of, col0 = (lambda i, j: i), (lambda i, j: 0)
    w_spec = (pl.BlockSpec((None, tn, d), lambda i, j: (layer, j, 0)) if w_rows_are_outputs
              else pl.BlockSpec((None, d, tn), lambda i, j: (layer, 0, j)))
    in_specs = (_row_specs(tiles, (tm, d), row_of, col0, single_tile_unbuffered=len(tiles) > 1)
                + [pl.BlockSpec((1, d), lambda i, j: (0, 0)), w_spec])
    out_specs = [pl.BlockSpec((tm, tn), lambda i, j: (i, j))]
    out_shape = [jax.ShapeDtypeStruct((m, n_cols), out_dtype)]
    args = list(xs) + [g.reshape(1, d), w]
    if w_extra is not None:
        ne = w_extra.shape[0 if w_rows_are_outputs else 1]
        in_specs.append(pl.BlockSpec(w_extra.shape, lambda i, j: (0, 0)))
        out_specs.append(pl.BlockSpec((tm, ne), lambda i, j: (i, 0)))
        out_shape.append(jax.ShapeDtypeStruct((m, ne), F32))
        args.append(w_extra)
    outs = pl.pallas_call(
        functools.partial(_norm_matmul_kernel, tiles, w_extra is not None, w_rows_are_outputs),
        grid=(m // tm, n_cols // tn),
        in_specs=in_specs,
        out_specs=out_specs,
        out_shape=out_shape,
        scratch_shapes=[pltpu.VMEM((tm, d), BF16)],
        compiler_params=_params("arbitrary", "arbitrary"),
        name="norm_matmul",
    )(*args)
    return outs if w_extra is not None else outs[0]


def _norm_swiglu_kernel(x_ref, g_ref, wa_ref, wu_ref, o_ref, h_scr):
    @pl.when(pl.program_id(1) == 0)
    def _():
        h_scr[...] = _rms_normed(x_ref[...], g_ref[...]).astype(BF16)

    h = h_scr[...]
    a = jnp.dot(h, wa_ref[...].astype(BF16), preferred_element_type=F32)
    u = jnp.dot(h, wu_ref[...].astype(BF16), preferred_element_type=F32)
    o_ref[...] = (a * jax.nn.sigmoid(a) * u).astype(o_ref.dtype)


def norm_swiglu(x, g, w_up, layer, *, tm, tn):
    m, d = x.shape
    ff = w_up.shape[-1] // 2
    nj = ff // tn
    return pl.pallas_call(
        _norm_swiglu_kernel,
        grid=(m // tm, nj),
        in_specs=[
            pl.BlockSpec((tm, d), lambda i, j: (i, 0)),
            pl.BlockSpec((1, d), lambda i, j: (0, 0)),
            pl.BlockSpec((None, d, tn), lambda i, j: (layer, 0, j)),
            pl.BlockSpec((None, d, tn), lambda i, j: (layer, 0, j + nj)),
        ],
        out_specs=pl.BlockSpec((tm, tn), lambda i, j: (i, j)),
        out_shape=jax.ShapeDtypeStruct((m, ff), BF16),
        scratch_shapes=[pltpu.VMEM((tm, d), BF16)],
        compiler_params=_params("arbitrary", "arbitrary"),
        name="norm_swiglu",
    )(x, g.reshape(1, d), w_up, w_up)


def _matmul_residual_kernel(y_tiles, r_tiles, *refs):
    ny, nr = len(y_tiles), len(r_tiles)
    y_refs, w_ref, r_refs = refs[:ny], refs[ny], refs[ny + 1:ny + 1 + nr]
    o_ref, w_scr = refs[ny + 1 + nr:]
    i = pl.program_id(1)

    @pl.when(i == 0)
    def _():
        w_scr[...] = w_ref[...].astype(BF16)

    def product(y_ref):
        o_ref[...] = jnp.dot(y_ref[...], w_scr[...], preferred_element_type=F32)

    def add_residual(r_ref):
        o_ref[...] = o_ref[...] + r_ref[...]

    _on_owner(y_refs, y_tiles, i, product)
    _on_owner(r_refs, r_tiles, i, add_residual)


def matmul_residual(ys, w, layer, rs, *, tm, tn):
    k, n = w.shape[-2], w.shape[-1]
    y_tiles, r_tiles = _row_tiles(ys, tm), _row_tiles(rs, tm)
    assert sum(y_tiles) == sum(r_tiles)
    m = tm * sum(y_tiles)
    row_of = lambda j, i: i
    w_mode = dict(pipeline_mode=pl.Buffered(1)) if tn == n else {}
    in_specs = (_row_specs(y_tiles, (tm, k), row_of, lambda j, i: 0)
                + [pl.BlockSpec((None, k, tn), lambda j, i: (layer, 0, j), **w_mode)]
                + _row_specs(r_tiles, (tm, tn), row_of, lambda j, i: j))
    return pl.pallas_call(
        functools.partial(_matmul_residual_kernel, y_tiles, r_tiles),
        grid=(n // tn, m // tm),
        in_specs=in_specs,
        out_specs=pl.BlockSpec((tm, tn), lambda j, i: (i, j)),
        out_shape=jax.ShapeDtypeStruct((m, n), F32),
        scratch_shapes=[pltpu.VMEM((k, tn), BF16)],
        compiler_params=_params("arbitrary", "arbitrary"),
        name="matmul_residual",
    )(*ys, w, *rs)


def _final_norm_kernel(x_ref, g_ref, o_ref):
    o_ref[...] = _rms_normed(x_ref[...], g_ref[...])


def final_norm(x, g, *, row0, nrows, tm):
    d = x.shape[1]
    assert row0 % tm == 0 and nrows % tm == 0
    return pl.pallas_call(
        _final_norm_kernel,
        grid=(nrows // tm,),
        in_specs=[pl.BlockSpec((tm, d), lambda i: (row0 // tm + i, 0)), pl.BlockSpec((1, d), lambda i: (0, 0))],
        out_specs=pl.BlockSpec((tm, d), lambda i: (i, 0)),
        out_shape=jax.ShapeDtypeStruct((nrows, d), F32),
        compiler_params=_params("arbitrary"),
        name="final_norm",
    )(x, g.reshape(1, d))


def _segment_masks(nseg, L):
    R = nseg * L
    r = lax.broadcasted_iota(jnp.int32, (R, R), 0)
    c = lax.broadcasted_iota(jnp.int32, (R, R), 1)
    same = (r // L) == (c // L)
    return r, c, same


def _dot_nt(a, b):
    return lax.dot_general(a, b, (((1,), (1,)), ((), ())), preferred_element_type=F32)


def _dot_tn(a, b):
    return lax.dot_general(a, b, (((0,), (0,)), ((), ())), preferred_element_type=F32)


def _log_sigmoid(x):
    return jnp.minimum(x, 0.0) - jnp.log1p(jnp.exp(-jnp.abs(x)))


def _silu(x):
    return 0.5 * x * (1.0 + jnp.tanh(0.5 * x))


def _cumsum_rows_exact(tril, x):
    hi = x.astype(BF16)
    r1 = x - hi.astype(F32)
    mid = r1.astype(BF16)
    lo = (r1 - mid.astype(F32)).astype(BF16)
    return (jnp.dot(tril, hi, preferred_element_type=F32) + jnp.dot(tril, mid, preferred_element_type=F32)
            + jnp.dot(tril, lo, preferred_element_type=F32))


def _mlstm_kernel(nseg, L, dqk, dv,
                  p_ref, gp_ref, bias_ref, hg_ref, c0_ref, n0_ref, m0_ref,
                  y_ref, c_ref, n_ref, m_ref):
    H = M_HEADS
    R = nseg * L
    s1 = H * dqk
    scale = dqk ** -0.5

    @pl.when(pl.program_id(1) == 0)
    def _():
        c_ref[...] = c0_ref[...]
        n_ref[...] = n0_ref[...]
        m_ref[...] = m0_ref[...]

    r, c, same = _segment_masks(nseg, L)
    causal = same & (c <= r)
    causal_t = same & (r <= c)
    eye = r == c
    row_seg = lax.broadcasted_iota(jnp.int32, (R, 1), 0) // L

    gates = gp_ref[:, :2 * H] + bias_ref[...]
    capped = GATE_CAP * jnp.tanh(gates / GATE_CAP)
    log_f = _log_sigmoid(capped)

    m_prev = m_ref[0, 0]
    m_new_all = m_prev
    seg_i = lax.broadcasted_iota(jnp.int32, (nseg, H), 0)
    head_i = lax.broadcasted_iota(jnp.int32, (nseg, H), 1)

    for h in range(H):
        q = p_ref[:, h * dqk:(h + 1) * dqk]
        k = p_ref[:, s1 + h * dqk:s1 + (h + 1) * dqk]
        v = p_ref[:, 2 * s1 + h * dv:2 * s1 + (h + 1) * dv]
        o = p_ref[:, 2 * s1 + H * dv + h * dv:2 * s1 + H * dv + (h + 1) * dv]
        qf = q.astype(F32)
        kf = k.astype(F32)
        vf = v.astype(F32)

        ig_col = capped[:, h:h + 1]
        lf_col = log_f[:, H + h:H + h + 1]
        lf_row = jnp.sum(jnp.where(eye, lf_col, 0.0), axis=0, keepdims=True)
        ig_row = jnp.sum(jnp.where(eye, ig_col, 0.0), axis=0, keepdims=True)
        b_col = jnp.sum(jnp.where(causal, lf_row, 0.0), axis=1, keepdims=True)
        b_row = jnp.sum(jnp.where(causal_t, lf_col, 0.0), axis=0, keepdims=True)
        b_last_col = jnp.sum(jnp.where(same, lf_row, 0.0), axis=1, keepdims=True)

        m_col = jnp.zeros((R, 1), F32)
        n_rows = jnp.zeros((R, dqk), F32)
        for s in range(nseg):
            m_s = m_prev[s:s + 1, h:h + 1]
            m_col = jnp.where(row_seg == s, m_s, m_col)
            n_rows = jnp.where(row_seg == s, n_ref[0, s, h:h + 1, :], n_rows)

        dlog = jnp.where(causal, b_col - b_row + ig_row, -jnp.inf)
        inter = b_col + m_col
        m_t = jnp.maximum(inter, jnp.max(dlog, axis=1, keepdims=True))
        w_intra = jnp.exp(dlog - m_t)
        w_inter = jnp.exp(inter - m_t)
        sc = _dot_nt(q, k) * scale * w_intra

        qc = jnp.zeros((R, dv), F32)
        for s in range(nseg):
            qc_s = jnp.dot(q, c_ref[0, s, h].astype(BF16), preferred_element_type=F32)
            qc = qc_s if nseg == 1 else jnp.where(row_seg == s, qc_s, qc)
        num = jnp.dot(sc.astype(BF16), v, preferred_element_type=F32) + w_inter * qc
        qn = jnp.sum(qf * n_rows, axis=1, keepdims=True)
        den = jnp.sum(sc, axis=1, keepdims=True) + w_inter * qn
        hc = num / jnp.maximum(jnp.abs(den), jnp.exp(-m_t))

        hn = hc * lax.rsqrt(jnp.mean(hc * hc, axis=1, keepdims=True) + EPS) * hg_ref[:, h * dv:(h + 1) * dv]
        y_ref[:, h * dv:(h + 1) * dv] = (hn * jax.nn.sigmoid(o.astype(F32))).astype(y_ref.dtype)

        g = b_last_col - b_col + ig_col
        m_new_col = jnp.zeros((R, 1), F32)
        decays = []
        for s in range(nseg):
            in_s = row_seg == s
            m_s = m_prev[s:s + 1, h:h + 1]
            b_last_s = jnp.sum(jnp.where(in_s, lf_col, 0.0), axis=0, keepdims=True)
            g_max_s = jnp.max(jnp.where(in_s, g, -jnp.inf), axis=0, keepdims=True)
            m_new_s = jnp.maximum(b_last_s + m_s, g_max_s)
            decays.append(jnp.exp(b_last_s + m_s - m_new_s))
            m_new_col = jnp.where(in_s, m_new_s, m_new_col)
            m_new_all = jnp.where((seg_i == s) & (head_i == h), m_new_s, m_new_all)
        w_k = jnp.exp(g - m_new_col) * scale
        wv = w_k * vf
        wk = w_k * kf
        for s in range(nseg):
            in_s = row_seg == s
            wv_s = wv if nseg == 1 else jnp.where(in_s, wv, 0.0)
            wk_s = wk if nseg == 1 else jnp.where(in_s, wk, 0.0)
            c_ref[0, s, h] = decays[s] * c_ref[0, s, h] + _dot_tn(k, wv_s.astype(BF16))
            n_ref[0, s, h:h + 1, :] = decays[s] * n_ref[0, s, h:h + 1, :] + jnp.sum(wk_s, axis=0, keepdims=True)

    m_ref[0, 0] = m_new_all


def mlstm_mixer(proj, gate_pre, bias, head_g, c0, n0, m0, *, row0, batch, seq, nseg, chunk):
    H = M_HEADS
    dqk, dv = c0.shape[-2], c0.shape[-1]
    R = nseg * chunk
    nchunks = seq // chunk
    assert nseg == 1 or nchunks == 1
    nblocks = batch // nseg
    blk0 = row0 // R
    rows = lambda i, c: (blk0 + i * nchunks + c, 0)
    width = proj.shape[1]
    c0 = c0.reshape(nblocks, nseg, H, dqk, dv)
    n0 = n0.reshape(nblocks, nseg, H, dqk)
    m0 = m0.reshape(nblocks, 1, nseg, H)
    y, c_out, n_out, m_out = pl.pallas_call(
        functools.partial(_mlstm_kernel, nseg, chunk, dqk, dv),
        grid=(nblocks, nchunks),
        in_specs=[
            pl.BlockSpec((R, width), rows),
            pl.BlockSpec((R, gate_pre.shape[1]), rows),
            pl.BlockSpec((1, 2 * H), lambda i, c: (0, 0)),
            pl.BlockSpec((1, H * dv), lambda i, c: (0, 0)),
            pl.BlockSpec((1, nseg, H, dqk, dv), lambda i, c: (i, 0, 0, 0, 0)),
            pl.BlockSpec((1, nseg, H, dqk), lambda i, c: (i, 0, 0, 0)),
            pl.BlockSpec((1, 1, nseg, H), lambda i, c: (i, 0, 0, 0)),
        ],
        out_specs=[
            pl.BlockSpec((R, H * dv), lambda i, c: (i * nchunks + c, 0)),
            pl.BlockSpec((1, nseg, H, dqk, dv), lambda i, c: (i, 0, 0, 0, 0)),
            pl.BlockSpec((1, nseg, H, dqk), lambda i, c: (i, 0, 0, 0)),
            pl.BlockSpec((1, 1, nseg, H), lambda i, c: (i, 0, 0, 0)),
        ],
        out_shape=[
            jax.ShapeDtypeStruct((batch * seq, H * dv), BF16),
            jax.ShapeDtypeStruct(c0.shape, F32),
            jax.ShapeDtypeStruct(n0.shape, F32),
            jax.ShapeDtypeStruct(m0.shape, F32),
        ],
        compiler_params=_params("arbitrary", "arbitrary"),
        name="mlstm_mixer",
    )(proj, gate_pre, bias.reshape(1, 2 * H), head_g.reshape(1, H * dv), c0, n0, m0)
    return (y, c_out.reshape(batch, H, dqk, dv), n_out.reshape(batch, H, dqk), m_out.reshape(batch, H))


def _mlstm_long_kernel(R, dqk, dv,
                       p_ref, gp_ref, bias_ref, hg_ref, c0_ref, n0_ref, m0_ref,
                       y_ref, c_ref, n_ref, m_ref, kt_scr):
    H = M_HEADS
    s1 = H * dqk
    scale = dqk ** -0.5

    @pl.when(pl.program_id(1) == 0)
    def _():
        c_ref[...] = c0_ref[...]
        n_ref[...] = n0_ref[...]
        m_ref[...] = m0_ref[...]

    r = lax.broadcasted_iota(jnp.int32, (R, R), 0)
    c = lax.broadcasted_iota(jnp.int32, (R, R), 1)
    causal = c <= r
    tril = causal.astype(BF16)

    capped = GATE_CAP * jnp.tanh((gp_ref[...] + bias_ref[...]) / GATE_CAP)
    cum = _cumsum_rows_exact(tril, _log_sigmoid(capped))
    ig_on_f = pltpu.roll(capped, H, axis=1)
    m_prev = m_ref[0]
    b_last = cum[R - 1:R, :]
    g_all = b_last - cum + ig_on_f
    m_new = jnp.maximum(b_last + m_prev, jnp.max(g_all, axis=0, keepdims=True))
    decay_all = jnp.exp(b_last + m_prev - m_new)
    wk_all = jnp.exp(g_all - m_new) * scale
    inter_all = cum + m_prev
    rows_t = (ig_on_f - cum).T

    RB = min(R, M_ROW_BLOCK)
    block_masks = [lax.broadcasted_iota(jnp.int32, (RB, (i + 1) * RB), 1)
                   <= lax.broadcasted_iota(jnp.int32, (RB, (i + 1) * RB), 0) + i * RB for i in range(R // RB)]

    for h in range(H):
        lane = H + h
        q_cols = slice(h * dqk, (h + 1) * dqk)
        k_cols = slice(s1 + h * dqk, s1 + (h + 1) * dqk)
        v_cols = slice(2 * s1 + h * dv, 2 * s1 + (h + 1) * dv)
        o_cols = slice(2 * s1 + H * dv + h * dv, 2 * s1 + H * dv + (h + 1) * dv)
        c_prev = c_ref[0, h]
        c_bf = c_prev.astype(BF16)
        n_prev = n_ref[0, h:h + 1, :]
        hg = hg_ref[:, h * dv:(h + 1) * dv]
        kt_scr[h] = p_ref[:, k_cols].T

        for i in range(R // RB):
            rows = slice(i * RB, (i + 1) * RB)
            ncols = (i + 1) * RB
            q = p_ref[rows, q_cols]
            gate_diff = cum[rows, lane:lane + 1] + rows_t[lane:lane + 1, :ncols]
            dlog = jnp.where(block_masks[i], gate_diff, -jnp.inf)
            inter = inter_all[rows, lane:lane + 1]
            m_t = jnp.maximum(inter, jnp.max(dlog, axis=1, keepdims=True))
            w_intra = jnp.exp(dlog - m_t)
            w_inter = jnp.exp(inter - m_t)
            sc = jnp.dot(q, kt_scr[h, :, :ncols], preferred_element_type=F32) * scale * w_intra
            num = (jnp.dot(sc.astype(BF16), p_ref[:ncols, v_cols], preferred_element_type=F32)
                   + w_inter * jnp.dot(q, c_bf, preferred_element_type=F32))
            qn = jnp.sum(q.astype(F32) * n_prev, axis=1, keepdims=True)
            den = jnp.sum(sc, axis=1, keepdims=True) + w_inter * qn
            inv = 1.0 / jnp.maximum(jnp.abs(den), jnp.exp(-m_t))
            norm = inv * lax.rsqrt(inv * inv * jnp.mean(num * num, axis=1, keepdims=True) + EPS)
            o = p_ref[rows, o_cols].astype(F32)
            y_ref[rows, h * dv:(h + 1) * dv] = (num * norm * hg * jax.nn.sigmoid(o)).astype(y_ref.dtype)

        k = p_ref[:, k_cols]
        w_k = wk_all[:, lane:lane + 1]
        decay = decay_all[:, lane:lane + 1]
        wv = (w_k * p_ref[:, v_cols].astype(F32)).astype(BF16)
        c_ref[0, h] = decay * c_prev + _dot_tn(k, wv)
        n_ref[0, h:h + 1, :] = decay * n_prev + jnp.sum(w_k * k.astype(F32), axis=0, keepdims=True)

    m_ref[0] = m_new


def mlstm_long_mixer(proj, gate_pre, bias, head_g, c0, n0, m0, *, row0, batch, seq, rows):
    H = M_HEADS
    dqk, dv = c0.shape[-2], c0.shape[-1]
    R = rows
    lanes = gate_pre.shape[1]
    assert seq % R == 0 and row0 % R == 0 and lanes >= 2 * H
    nsteps = seq // R
    blk0 = row0 // R
    rows_of = lambda i, c: (blk0 + i * nsteps + c, 0)
    bias_l = jnp.pad(bias.reshape(1, 2 * H), ((0, 0), (0, lanes - 2 * H)))
    m0_l = jnp.pad(m0, ((0, 0), (H, lanes - 2 * H))).reshape(batch, 1, lanes)
    y, c_out, n_out, m_out = pl.pallas_call(
        functools.partial(_mlstm_long_kernel, R, dqk, dv),
        grid=(batch, nsteps),
        in_specs=[
            pl.BlockSpec((R, proj.shape[1]), rows_of),
            pl.BlockSpec((R, lanes), rows_of),
            pl.BlockSpec((1, lanes), lambda i, c: (0, 0)),
            pl.BlockSpec((1, H * dv), lambda i, c: (0, 0)),
            pl.BlockSpec((1, H, dqk, dv), lambda i, c: (i, 0, 0, 0)),
            pl.BlockSpec((1, H, dqk), lambda i, c: (i, 0, 0)),
            pl.BlockSpec((1, 1, lanes), lambda i, c: (i, 0, 0)),
        ],
        out_specs=[
            pl.BlockSpec((R, H * dv), lambda i, c: (i * nsteps + c, 0)),
            pl.BlockSpec((1, H, dqk, dv), lambda i, c: (i, 0, 0, 0)),
            pl.BlockSpec((1, H, dqk), lambda i, c: (i, 0, 0)),
            pl.BlockSpec((1, 1, lanes), lambda i, c: (i, 0, 0)),
        ],
        out_shape=[
            jax.ShapeDtypeStruct((batch * seq, H * dv), BF16),
            jax.ShapeDtypeStruct(c0.shape, F32),
            jax.ShapeDtypeStruct(n0.shape, F32),
            jax.ShapeDtypeStruct(m0_l.shape, F32),
        ],
        scratch_shapes=[pltpu.VMEM((H, dqk, R), BF16)],
        compiler_params=_params("arbitrary", "arbitrary"),
        name="mlstm_long_mixer",
    )(proj, gate_pre, bias_l, head_g.reshape(1, H * dv), c0, n0, m0_l)
    return y, c_out, n_out, m_out[:, 0, H:2 * H]


def _mlstm_short_kernel(nb, T, dqk, dv,
                        p_ref, gp_ref, bias_ref, hg_ref, c0_ref, n0_ref, m0_ref,
                        y_ref, c_ref, n_ref, m_ref):
    H = M_HEADS
    R = nb * T
    PAIR = 2 * T
    s1 = H * dqk
    scale = dqk ** -0.5
    assert T % 8 == 0 and nb % 2 == 0

    r = lax.broadcasted_iota(jnp.int32, (R, R), 0)
    c = lax.broadcasted_iota(jnp.int32, (R, R), 1)
    causal = ((r // T) == (c // T)) & (c <= r)
    tril = causal.astype(BF16)
    seq_cols = (lax.broadcasted_iota(jnp.int32, (nb, 1, R), 2) // T
                == lax.broadcasted_iota(jnp.int32, (nb, 1, R), 0)).astype(BF16)
    first_of_pair = lax.broadcasted_iota(jnp.int32, (PAIR, 1), 0) < T

    def per_seq(x):
        return x.reshape(nb, T, x.shape[-1])

    def seq_rows(x3):
        return jnp.broadcast_to(x3, (nb, T, x3.shape[-1])).reshape(R, x3.shape[-1])

    capped = GATE_CAP * jnp.tanh((gp_ref[...] + bias_ref[...]) / GATE_CAP)
    cum = _cumsum_rows_exact(tril, _log_sigmoid(capped))
    ig_on_f = pltpu.roll(capped, H, axis=1)
    m_prev = m0_ref[...]
    b_last = seq_rows(per_seq(cum)[:, T - 1:T, :])
    g_all = b_last - cum + ig_on_f
    m_new = jnp.maximum(b_last + m_prev, seq_rows(jnp.max(per_seq(g_all), axis=1, keepdims=True)))
    decay_all = jnp.exp(b_last + m_prev - m_new)
    wk_all = jnp.exp(g_all - m_new) * scale
    inter_all = cum + m_prev
    rows_t = (ig_on_f - cum).T

    for h in range(H):
        lane = H + h
        q = p_ref[:, h * dqk:(h + 1) * dqk]
        k = p_ref[:, s1 + h * dqk:s1 + (h + 1) * dqk]
        v = p_ref[:, 2 * s1 + h * dv:2 * s1 + (h + 1) * dv]
        o = p_ref[:, 2 * s1 + H * dv + h * dv:2 * s1 + H * dv + (h + 1) * dv]
        kf = k.astype(F32)

        dlog = jnp.where(causal, cum[:, lane:lane + 1] + rows_t[lane:lane + 1, :], -jnp.inf)
        inter = inter_all[:, lane:lane + 1]
        m_t = jnp.maximum(inter, jnp.max(dlog, axis=1, keepdims=True))
        w_intra = jnp.exp(dlog - m_t)
        w_inter = jnp.exp(inter - m_t)
        sc = _dot_nt(q, k) * scale * w_intra
        num_intra = jnp.dot(sc.astype(BF16), v, preferred_element_type=F32)
        n_prev = n0_ref[h]
        qn = jnp.sum(q.astype(F32) * n_prev, axis=1, keepdims=True)
        den = jnp.sum(sc, axis=1, keepdims=True) + w_inter * qn
        inv = 1.0 / jnp.maximum(jnp.abs(den), jnp.exp(-m_t))
        hg = hg_ref[:, h * dv:(h + 1) * dv]

        w_k = wk_all[:, lane:lane + 1]
        k_t = kf.T.astype(BF16)
        stacked = (k_t[None] * seq_cols).reshape(nb * dqk, R)
        upd = jnp.dot(stacked, (w_k * v.astype(F32)).astype(BF16), preferred_element_type=F32)
        n_ref[h] = (decay_all[:, lane:lane + 1] * n_prev
                    + seq_rows(jnp.sum(per_seq(w_k * kf), axis=1, keepdims=True)))

        for pr in range(nb // 2):
            rows = slice(pr * PAIR, (pr + 1) * PAIR)
            lhs = p_ref[rows, h * dqk:(h + 1) * dqk]
            qc = []
            for b in (2 * pr, 2 * pr + 1):
                c_prev = c0_ref[0, b, h]
                qc.append(jnp.dot(lhs, c_prev.astype(BF16), preferred_element_type=F32))
                c_ref[0, b, h] = decay_all[b * T:b * T + 1, lane:lane + 1] * c_prev + upd[b * dqk:(b + 1) * dqk, :]
            num = num_intra[rows, :] + w_inter[rows, :] * jnp.where(first_of_pair, qc[0], qc[1])
            hc = num * inv[rows, :]
            hn = hc * lax.rsqrt(jnp.mean(hc * hc, axis=1, keepdims=True) + EPS) * hg
            y_ref[rows, h * dv:(h + 1) * dv] = (hn * jax.nn.sigmoid(o[rows, :].astype(F32))).astype(y_ref.dtype)

    m_ref[...] = m_new


def mlstm_short_mixer(proj, gate_pre, bias, head_g, c0, n0, m0, *, row0, batch, seq, nb):
    H = M_HEADS
    dqk, dv = c0.shape[-2], c0.shape[-1]
    R = nb * seq
    lanes = gate_pre.shape[1]
    assert batch % nb == 0 and row0 % R == 0 and lanes >= 2 * H
    nblocks = batch // nb
    blk0 = row0 // R
    bias_l = jnp.pad(bias.reshape(1, 2 * H), ((0, 0), (0, lanes - 2 * H)))
    m0_l = jnp.repeat(jnp.pad(m0, ((0, 0), (H, lanes - 2 * H))), seq, axis=0)
    n0_t = jnp.repeat(jnp.swapaxes(n0, 0, 1), seq, axis=1)
    y, c_out, n_out, m_out = pl.pallas_call(
        functools.partial(_mlstm_short_kernel, nb, seq, dqk, dv),
        grid=(nblocks,),
        in_specs=[
            pl.BlockSpec((R, proj.shape[1]), lambda i: (blk0 + i, 0)),
            pl.BlockSpec((R, lanes), lambda i: (blk0 + i, 0)),
            pl.BlockSpec((1, lanes), lambda i: (0, 0)),
            pl.BlockSpec((1, H * dv), lambda i: (0, 0)),
            pl.BlockSpec((1, nb, H, dqk, dv), lambda i: (i, 0, 0, 0, 0)),
            pl.BlockSpec((H, R, dqk), lambda i: (0, i, 0)),
            pl.BlockSpec((R, lanes), lambda i: (i, 0)),
        ],
        out_specs=[
            pl.BlockSpec((R, H * dv), lambda i: (i, 0)),
            pl.BlockSpec((1, nb, H, dqk, dv), lambda i: (i, 0, 0, 0, 0)),
            pl.BlockSpec((H, R, dqk), lambda i: (0, i, 0)),
            pl.BlockSpec((R, lanes), lambda i: (i, 0)),
        ],
        out_shape=[
            jax.ShapeDtypeStruct((batch * seq, H * dv), BF16),
            jax.ShapeDtypeStruct((nblocks, nb, H, dqk, dv), F32),
            jax.ShapeDtypeStruct(n0_t.shape, F32),
            jax.ShapeDtypeStruct(m0_l.shape, F32),
        ],
        compiler_params=_params("arbitrary"),
        name="mlstm_short_mixer",
    )(proj, gate_pre, bias_l, head_g.reshape(1, H * dv), c0.reshape(nblocks, nb, H, dqk, dv), n0_t, m0_l)
    return (y, c_out.reshape(batch, H, dqk, dv), jnp.swapaxes(n_out[:, ::seq, :], 0, 1), m_out[::seq, H:2 * H])


def _hgrn_kernel(nseg, L, heads, dk, dv,
                 p_ref, lb_ref, gn_ref, s0_ref, y_ref, s_ref):
    R = nseg * L
    fdim = heads * dk
    scale = dk ** -0.5
    sub = min(L, H_SUB)
    assert L % sub == 0 and L // sub in (1, 2)

    @pl.when(pl.program_id(1) == 0)
    def _():
        s_ref[...] = s0_ref[...]

    r, c, same = _segment_masks(nseg, L)
    causal = same & (c <= r)
    same_sub = (r // sub) == (c // sub)
    diag_mask = causal & same_sub
    tril = causal.astype(F32)

    row = lax.broadcasted_iota(jnp.int32, (R, 1), 0)
    row_seg = row // L
    pos = row % L
    mid_sel = (same_sub & ((c % sub) == (sub // 2))).astype(F32)
    last_sel = (same & ((c % L) == (L - 1))).astype(F32)
    if L // sub == 2:
        bnd_sel = (same & ((c % L) == (sub - 1))).astype(F32)
        upper = pos >= sub
    hp = lax.Precision.HIGHEST

    for h in range(heads):
        q = p_ref[:, h * dk:(h + 1) * dk].astype(F32) * scale
        fpre = p_ref[:, fdim + h * dk:fdim + (h + 1) * dk].astype(F32)
        iv = p_ref[:, 2 * fdim + h * dv:2 * fdim + (h + 1) * dv]
        gate = p_ref[:, 2 * fdim + heads * dv + h * dv:2 * fdim + heads * dv + (h + 1) * dv].astype(F32)
        lb = lb_ref[:, h * dk:(h + 1) * dk]
        f = lb + (1.0 - lb) * jax.nn.sigmoid(fpre)
        log_f = jnp.log(f)
        k = 1.0 - f

        a = jnp.dot(tril, log_f, preferred_element_type=F32, precision=hp)
        a_mid = jnp.dot(mid_sel, a, preferred_element_type=F32, precision=hp)
        a_last = jnp.dot(last_sel, a, preferred_element_type=F32, precision=hp)

        qd = (q * jnp.exp(a - a_mid)).astype(BF16)
        kd = (k * jnp.exp(a_mid - a)).astype(BF16)
        sc = jnp.where(diag_mask, _dot_nt(qd, kd), 0.0)
        if L // sub == 2:
            a_bnd = jnp.dot(bnd_sel, a, preferred_element_type=F32, precision=hp)
            qo = jnp.where(upper, q * jnp.exp(jnp.minimum(a - a_bnd, 0.0)), 0.0).astype(BF16)
            ko = jnp.where(upper, 0.0, k * jnp.exp(jnp.minimum(a_bnd - a, 0.0))).astype(BF16)
            sc = sc + jnp.where(same, _dot_nt(qo, ko), 0.0)

        qe = (q * jnp.exp(a)).astype(BF16)
        inter = jnp.zeros((R, dv), F32)
        for s in range(nseg):
            inter_s = jnp.dot(qe, s_ref[0, s, h].astype(BF16), preferred_element_type=F32)
            inter = inter_s if nseg == 1 else jnp.where(row_seg == s, inter_s, inter)
        oc = jnp.dot(sc.astype(BF16), iv, preferred_element_type=F32) + inter

        on = oc * lax.rsqrt(jnp.mean(oc * oc, axis=1, keepdims=True) + EPS) * gn_ref[:, h * dv:(h + 1) * dv]
        y_ref[:, h * dv:(h + 1) * dv] = (on * _silu(gate)).astype(y_ref.dtype)

        ke = k * jnp.exp(a_last - a)
        for s in range(nseg):
            in_s = row_seg == s
            ke_s = ke if nseg == 1 else jnp.where(in_s, ke, 0.0)
            a_last_s = a[s * L + L - 1:s * L + L, :]
            upd = _dot_tn(ke_s.astype(BF16), iv)
            decay_col = jnp.sum(jnp.where(lax.broadcasted_iota(jnp.int32, (dk, dk), 0)
                                          == lax.broadcasted_iota(jnp.int32, (dk, dk), 1),
                                          jnp.exp(a_last_s), 0.0), axis=1, keepdims=True)
            s_ref[0, s, h] = decay_col * s_ref[0, s, h] + upd


def hgrn_mixer(proj, lb, gn_g, s0, *, row0, batch, seq, nseg, chunk):
    heads, dk, dv = s0.shape[-3], s0.shape[-2], s0.shape[-1]
    R = nseg * chunk
    nchunks = seq // chunk
    assert nseg == 1 or nchunks == 1
    nblocks = batch // nseg
    blk0 = row0 // R
    rows = lambda i, c: (blk0 + i * nchunks + c, 0)
    s0 = s0.reshape(nblocks, nseg, heads, dk, dv)
    y, s_out = pl.pallas_call(
        functools.partial(_hgrn_kernel, nseg, chunk, heads, dk, dv),
        grid=(nblocks, nchunks),
        in_specs=[
            pl.BlockSpec((R, proj.shape[1]), rows),
            pl.BlockSpec((1, heads * dk), lambda i, c: (0, 0)),
            pl.BlockSpec((1, heads * dv), lambda i, c: (0, 0)),
            pl.BlockSpec((1, nseg, heads, dk, dv), lambda i, c: (i, 0, 0, 0, 0)),
        ],
        out_specs=[
            pl.BlockSpec((R, heads * dv), lambda i, c: (i * nchunks + c, 0)),
            pl.BlockSpec((1, nseg, heads, dk, dv), lambda i, c: (i, 0, 0, 0, 0)),
        ],
        out_shape=[
            jax.ShapeDtypeStruct((batch * seq, heads * dv), BF16),
            jax.ShapeDtypeStruct(s0.shape, F32),
        ],
        compiler_params=_params("arbitrary", "arbitrary"),
        name="hgrn_mixer",
    )(proj, lb.reshape(1, heads * dk), gn_g.reshape(1, heads * dv), s0)
    return y, s_out.reshape(batch, heads, dk, dv)


def _hgrn_short_kernel(nb, T, heads, dk, dv,
                       p_ref, lb_ref, gn_ref, s0_ref, y_ref, s_ref,
                       qd_scr, kd_scr, qe_scr, ke_scr, dec_scr):
    R = nb * T
    fdim = heads * dk
    scale = dk ** -0.5
    W = 2 * dk
    PAIR = 2 * T
    assert T % 8 == 0 and T <= H_SUB and nb % 2 == 0

    r = lax.broadcasted_iota(jnp.int32, (R, R), 0)
    c = lax.broadcasted_iota(jnp.int32, (R, R), 1)
    causal = ((r // T) == (c // T)) & (c <= r)
    tril = causal.astype(BF16)
    seq_cols = (lax.broadcasted_iota(jnp.int32, (nb, 1, R), 2) // T
                == lax.broadcasted_iota(jnp.int32, (nb, 1, R), 0)).astype(BF16)
    first_of_pair = lax.broadcasted_iota(jnp.int32, (PAIR, 1), 0) < T

    def minus_ref_row(a, row):
        a3 = a.reshape(nb, T, a.shape[-1])
        return (a3 - a3[:, row:row + 1, :]).reshape(a.shape)

    for g in range(fdim // W):
        sl = slice(g * W, (g + 1) * W)
        q = p_ref[:, sl].astype(F32) * scale
        fpre = p_ref[:, fdim + g * W:fdim + (g + 1) * W].astype(F32)
        lb = lb_ref[:, sl]
        f = lb + (1.0 - lb) * jax.nn.sigmoid(fpre)
        k = 1.0 - f
        a = _cumsum_rows_exact(tril, jnp.log(f))
        d = minus_ref_row(a, T // 2)
        to_last = -minus_ref_row(a, T - 1)
        qd_scr[:, sl] = (q * jnp.exp(d)).astype(BF16)
        kd_scr[:, sl] = (k * jnp.exp(-d)).astype(BF16)
        qe_scr[:, sl] = (q * jnp.exp(a)).astype(BF16)
        ke_scr[:, sl] = k * jnp.exp(to_last)
        dec_scr[:, sl] = jnp.exp(a + to_last)

    for h in range(heads):
        sl = slice(h * dk, (h + 1) * dk)
        iv = p_ref[:, 2 * fdim + h * dv:2 * fdim + (h + 1) * dv]
        sc = jnp.where(causal, _dot_nt(qd_scr[:, sl], kd_scr[:, sl]), 0.0)
        oc = jnp.dot(sc.astype(BF16), iv, preferred_element_type=F32)
        ke_t = ke_scr[:, sl].T.astype(BF16)
        dec_t = dec_scr[:, sl].T
        stacked = (ke_t[None] * seq_cols).reshape(nb * dk, R)
        upd = jnp.dot(stacked, iv, preferred_element_type=F32)
        gn = gn_ref[:, h * dv:(h + 1) * dv]
        for pr in range(nb // 2):
            rows = slice(pr * PAIR, (pr + 1) * PAIR)
            lhs = qe_scr[rows, sl]
            inter = []
            for b in (2 * pr, 2 * pr + 1):
                s_prev = s0_ref[0, b, h]
                inter.append(jnp.dot(lhs, s_prev.astype(BF16), preferred_element_type=F32))
                s_ref[0, b, h] = dec_t[:, b * T:b * T + 1] * s_prev + upd[b * dk:(b + 1) * dk, :]
            o2 = oc[rows, :] + jnp.where(first_of_pair, inter[0], inter[1])
            gate = p_ref[rows, 2 * fdim + heads * dv + h * dv:2 * fdim + heads * dv + (h + 1) * dv].astype(F32)
            on = o2 * lax.rsqrt(jnp.mean(o2 * o2, axis=1, keepdims=True) + EPS) * gn
            y_ref[rows, h * dv:(h + 1) * dv] = (on * _silu(gate)).astype(y_ref.dtype)


def hgrn_short_mixer(proj, lb, gn_g, s0, *, row0, batch, seq, nb):
    heads, dk, dv = s0.shape[-3], s0.shape[-2], s0.shape[-1]
    R = nb * seq
    assert batch % nb == 0 and row0 % R == 0
    nblocks = batch // nb
    blk0 = row0 // R
    fdim = heads * dk
    s0 = s0.reshape(nblocks, nb, heads, dk, dv)
    y, s_out = pl.pallas_call(
        functools.partial(_hgrn_short_kernel, nb, seq, heads, dk, dv),
        grid=(nblocks,),
        in_specs=[
            pl.BlockSpec((R, proj.shape[1]), lambda i: (blk0 + i, 0)),
            pl.BlockSpec((1, fdim), lambda i: (0, 0)),
            pl.BlockSpec((1, heads * dv), lambda i: (0, 0)),
            pl.BlockSpec((1, nb, heads, dk, dv), lambda i: (i, 0, 0, 0, 0)),
        ],
        out_specs=[
            pl.BlockSpec((R, heads * dv), lambda i: (i, 0)),
            pl.BlockSpec((1, nb, heads, dk, dv), lambda i: (i, 0, 0, 0, 0)),
        ],
        out_shape=[
            jax.ShapeDtypeStruct((batch * seq, heads * dv), BF16),
            jax.ShapeDtypeStruct(s0.shape, F32),
        ],
        scratch_shapes=[
            pltpu.VMEM((R, fdim), BF16), pltpu.VMEM((R, fdim), BF16), pltpu.VMEM((R, fdim), BF16),
            pltpu.VMEM((R, fdim), F32), pltpu.VMEM((R, fdim), F32),
        ],
        compiler_params=_params("arbitrary"),
        name="hgrn_short_mixer",
    )(proj, lb.reshape(1, fdim), gn_g.reshape(1, heads * dv), s0)
    return y, s_out.reshape(batch, heads, dk, dv)


def _hgrn_long_kernel(R, heads, dk, dv, levels,
                      p_ref, lb_ref, gn_ref, s0_ref, y_ref, s_ref,
                      st_scr, dec_scr, qd_scr, kd_scr, qe_scr, ke_scr, ql_scr, kl_scr, k_scr, a_scr,
                      kt_scr, sb_scr):
    fdim = heads * dk
    eps_unscaled = EPS * dk
    W = 2 * dk
    last_step = pl.num_programs(1) - 1

    @pl.when(pl.program_id(1) == 0)
    def _():
        for h in range(heads):
            st_scr[h] = s0_ref[0, h].T

    r = lax.broadcasted_iota(jnp.int32, (R, R), 0)
    c = lax.broadcasted_iota(jnp.int32, (R, R), 1)
    tril = (c <= r).astype(BF16)
    diag_mask = ((r // H_SUB) == (c // H_SUB)) & (c <= r)
    level_masks = [((r // G) == (c // G)) & ((r % G) >= G // 2) & ((c % G) < G // 2) for G in levels]

    NSB = R // H_SUB
    zeros_sub = jnp.zeros((H_SUB, dk), BF16)

    for g in range(fdim // W):
        fpre = p_ref[:, fdim + g * W:fdim + (g + 1) * W].astype(F32)
        lb = lb_ref[:, g * W:(g + 1) * W]
        f = lb + (1.0 - lb) * jax.nn.sigmoid(fpre)
        k = 1.0 - f
        a = _cumsum_rows_exact(tril, jnp.log(f))
        for j in range(W // dk):
            k_scr[g * (W // dk) + j] = k[:, j * dk:(j + 1) * dk]
            a_scr[g * (W // dk) + j] = a[:, j * dk:(j + 1) * dk]

    def rows_of(ref, h, start, count, stride):
        if count == 1:
            return ref[h, start:start + 1, :]
        return ref.at[h][pl.ds(start, count, stride=stride), :]

    for h in range(heads):
        sl = slice(h * dk, (h + 1) * dk)
        a_last = a_scr[h, R - 1:R, :]
        dec_scr[:, sl] = jnp.exp(a_last)
        mids = rows_of(a_scr, h, H_SUB // 2, NSB, H_SUB)
        c_qe = jnp.exp(mids)
        c_ke = jnp.exp(a_last - mids)
        c_lv = []
        for G in levels:
            per = G // H_SUB
            bnd = rows_of(a_scr, h, G // 2 - 1, R // G, G)
            consts = []
            for o in range(per):
                mid_o = rows_of(a_scr, h, o * H_SUB + H_SUB // 2, R // G, G)
                consts.append(jnp.exp(mid_o - bnd) if o >= per // 2 else jnp.exp(bnd - mid_o))
            c_lv.append(consts)
        for s in range(NSB):
            rows = slice(s * H_SUB, (s + 1) * H_SUB)
            d = a_scr[h, rows, :] - mids[s:s + 1, :]
            qd = p_ref[rows, sl].astype(F32) * jnp.exp(d)
            kd = k_scr[h, rows, :] * jnp.exp(-d)
            qd_scr[rows, sl] = qd.astype(BF16)
            kd_scr[rows, sl] = kd.astype(BF16)
            qe_scr[rows, sl] = (qd * c_qe[s:s + 1, :]).astype(BF16)
            ke_scr[rows, sl] = (kd * c_ke[s:s + 1, :]).astype(BF16)
            for li, G in enumerate(levels):
                per = G // H_SUB
                o, gi = s % per, s // per
                const = c_lv[li][o][gi:gi + 1, :]
                if o >= per // 2:
                    ql_scr[li, rows, sl] = (qd * const).astype(BF16)
                    kl_scr[li, rows, sl] = zeros_sub
                else:
                    kl_scr[li, rows, sl] = (kd * const).astype(BF16)
                    ql_scr[li, rows, sl] = zeros_sub

    for h in range(heads):
        sl = slice(h * dk, (h + 1) * dk)
        st = st_scr[h]
        kt_scr[h, 0] = kd_scr[:, sl].T
        for li in range(len(levels)):
            kt_scr[h, 1 + li] = kl_scr[li, :, sl].T
        sb_scr[h] = st.astype(BF16).T
        sc = jnp.where(diag_mask, jnp.dot(qd_scr[:, sl], kt_scr[h, 0], preferred_element_type=F32), 0.0)
        for li in range(len(levels)):
            sc = jnp.where(level_masks[li],
                           jnp.dot(ql_scr[li, :, sl], kt_scr[h, 1 + li], preferred_element_type=F32), sc)
        iv = p_ref[:, 2 * fdim + h * dv:2 * fdim + (h + 1) * dv]
        gate = p_ref[:, 2 * fdim + heads * dv + h * dv:2 * fdim + heads * dv + (h + 1) * dv].astype(F32)
        oc = (jnp.dot(sc.astype(BF16), iv, preferred_element_type=F32)
              + jnp.dot(qe_scr[:, sl], sb_scr[h], preferred_element_type=F32))
        st_scr[h] = st * dec_scr[:, sl] + _dot_tn(iv, ke_scr[:, sl])
        on = oc * lax.rsqrt(jnp.mean(oc * oc, axis=1, keepdims=True) + eps_unscaled) * gn_ref[:, h * dv:(h + 1) * dv]
        y_ref[:, h * dv:(h + 1) * dv] = (on * _silu(gate)).astype(y_ref.dtype)

    @pl.when(pl.program_id(1) == last_step)
    def _():
        for h in range(heads):
            s_ref[0, h] = st_scr[h].T


def hgrn_long_mixer(proj, lb, gn_g, s0, *, row0, batch, seq, rows):
    heads, dk, dv = s0.shape[-3], s0.shape[-2], s0.shape[-1]
    R = rows
    assert seq % R == 0 and R % (2 * H_SUB) == 0 and row0 % R == 0
    levels = []
    G = 2 * H_SUB
    while G <= R:
        levels.append(G)
        G *= 2
    assert levels[-1] == R
    nsteps = seq // R
    blk0 = row0 // R
    fdim = heads * dk
    wide = lambda: pltpu.VMEM((R, fdim), BF16)
    return pl.pallas_call(
        functools.partial(_hgrn_long_kernel, R, heads, dk, dv, tuple(levels)),
        grid=(batch, nsteps),
        in_specs=[
            pl.BlockSpec((R, proj.shape[1]), lambda i, c: (blk0 + i * nsteps + c, 0)),
            pl.BlockSpec((1, fdim), lambda i, c: (0, 0)),
            pl.BlockSpec((1, heads * dv), lambda i, c: (0, 0)),
            pl.BlockSpec((1, heads, dk, dv), lambda i, c: (i, 0, 0, 0)),
        ],
        out_specs=[
            pl.BlockSpec((R, heads * dv), lambda i, c: (i * nsteps + c, 0)),
            pl.BlockSpec((1, heads, dk, dv), lambda i, c: (i, 0, 0, 0)),
        ],
        out_shape=[
            jax.ShapeDtypeStruct((batch * seq, heads * dv), BF16),
            jax.ShapeDtypeStruct(s0.shape, F32),
        ],
        scratch_shapes=[
            pltpu.VMEM((heads, dv, dk), F32),
            pltpu.VMEM((1, fdim), F32),
            wide(), wide(), wide(), wide(),
            pltpu.VMEM((len(levels), R, fdim), BF16),
            pltpu.VMEM((len(levels), R, fdim), BF16),
            pltpu.VMEM((heads, R, dk), F32),
            pltpu.VMEM((heads, R, dk), F32),
            pltpu.VMEM((heads, 1 + len(levels), dk, R), BF16),
            pltpu.VMEM((heads, dk, dv), BF16),
        ],
        compiler_params=_params("arbitrary", "arbitrary"),
        name="hgrn_long_mixer",
    )(proj, lb.reshape(1, fdim), gn_g.reshape(1, heads * dv), s0)


def kernel(x_prompt, x_sample, state_mlstm_C, state_mlstm_n, state_mlstm_m, state_hgrn_S,
           norm_mix_g, norm_ffn_g, norm_final_g, mlstm_w_in, mlstm_b_gates, mlstm_head_norm_g,
           mlstm_w_out, hgrn_w_in, hgrn_lower_bounds, hgrn_g_norm_g, hgrn_w_out, ffn_w_up, ffn_w_down):
    bp, tp, d = x_prompt.shape
    bs, ts, _ = x_sample.shape
    np_, ns_ = bp * tp, bs * ts
    depth = norm_mix_g.shape[0]
    H = M_HEADS
    dqk, dv = state_mlstm_C.shape[-2], state_mlstm_C.shape[-1]
    hh, hdk, hdv = state_hgrn_S.shape[-3], state_hgrn_S.shape[-2], state_hgrn_S.shape[-1]

    xs = (x_prompt.reshape(np_, d), x_sample.reshape(ns_, d))

    lb_all = jnp.cumsum(jax.nn.softmax(hgrn_lower_bounds.astype(F32), axis=0), axis=0)
    lb_all = lb_all - lb_all[0]

    tm = 1024
    Cp, Np, Mp, Sp, Cs, Ns, Ms, Ss = [], [], [], [], [], [], [], []
    for i in range(depth):
        j = i // 2
        if i % 2 == 0:
            n_main = 2 * H * dqk + 2 * H * dv
            w_in_t = jnp.swapaxes(mlstm_w_in, 1, 2)
            w_gates = jnp.pad(w_in_t[j, n_main:], ((0, 128 - 2 * H), (0, 0)))
            proj, gate_pre = norm_matmul(xs, norm_mix_g[i], w_in_t, j, n_main, tm=tm, tn=1024,
                                         out_dtype=BF16, w_extra=w_gates, w_rows_are_outputs=True)
            args = (proj, gate_pre, mlstm_b_gates[j], mlstm_head_norm_g[j])
            yp, c_p, n_p, m_p = mlstm_long_mixer(
                *args, jnp.zeros((bp, H, dqk, dv), F32), jnp.zeros((bp, H, dqk), F32), jnp.zeros((bp, H), F32),
                row0=0, batch=bp, seq=tp, rows=256)
            ysm, c_s, n_s, m_s = mlstm_short_mixer(
                *args, state_mlstm_C[j], state_mlstm_n[j], state_mlstm_m[j],
                row0=np_, batch=bs, seq=ts, nb=8)
            Cp.append(c_p); Np.append(n_p); Mp.append(m_p)
            Cs.append(c_s); Ns.append(n_s); Ms.append(m_s)
            w_out = mlstm_w_out
        else:
            proj = norm_matmul(xs, norm_mix_g[i], hgrn_w_in, j, hgrn_w_in.shape[-1], tm=tm, tn=1024, out_dtype=BF16)
            args = (proj, lb_all[i], hgrn_g_norm_g[j])
            yp, s_p = hgrn_long_mixer(*args, jnp.zeros((bp, hh, hdk, hdv), F32),
                                      row0=0, batch=bp, seq=tp, rows=128)
            ysm, s_s = hgrn_short_mixer(*args, state_hgrn_S[j], row0=np_, batch=bs, seq=ts, nb=8)
            Sp.append(s_p); Ss.append(s_s)
            w_out = hgrn_w_out
        x = matmul_residual((yp, ysm), w_out, j, xs, tm=256, tn=d)
        act = norm_swiglu(x, norm_ffn_g[i], ffn_w_up, i, tm=tm, tn=512)
        x = matmul_residual((act,), ffn_w_down, i, (x,), tm=512, tn=512)
        xs = (x,)

    y_prompt = final_norm(x, norm_final_g, row0=0, nrows=np_, tm=512).reshape(bp, tp, d)
    y_sample = final_norm(x, norm_final_g, row0=np_, nrows=ns_, tm=512).reshape(bs, ts, d)
    cat = lambda parts: jnp.stack(parts) if len(parts) > 1 else parts[0][None]
    return (y_prompt, y_sample, cat(Cp), cat(Np), cat(Mp), cat(Sp), cat(Cs), cat(Ns), cat(Ms), cat(Ss))
```

```python
import functools

import jax
import jax.numpy as jnp
from jax import lax
from jax.experimental import pallas as pl
from jax.experimental.pallas import tpu as pltpu

F32 = jnp.float32
BF16 = jnp.bfloat16

EPS = 1e-6
GATE_CAP = 15.0
M_HEADS = 8
M_CHUNK = 64
M_ROW_BLOCK = 256
H_DK = 128
H_CHUNK = 32
H_SUB = 16

VMEM_LIMIT_BYTES = 56 * 1024 * 1024


def _params(*sem):
    return pltpu.CompilerParams(dimension_semantics=sem, vmem_limit_bytes=VMEM_LIMIT_BYTES)


def _row_tiles(xs, tm):
    for x in xs:
        assert x.shape[0] % tm == 0, (x.shape, tm)
    return tuple(x.shape[0] // tm for x in xs)


def _clamped_row_tile(first, count, row_of, col_of, *ids):
    return (jnp.clip(row_of(*ids) - first, 0, count - 1), col_of(*ids))


def _row_specs(tiles, block, row_of, col_of, single_tile_unbuffered=False):
    specs, first = [], 0
    for count in tiles:
        mode = dict(pipeline_mode=pl.Buffered(1)) if single_tile_unbuffered and count == 1 else {}
        specs.append(pl.BlockSpec(block, functools.partial(_clamped_row_tile, first, count, row_of, col_of), **mode))
        first += count
    return specs


def _on_owner(refs, tiles, i, fn):
    if len(refs) == 1:
        fn(refs[0])
        return
    first = 0
    for ref, count in zip(refs, tiles):
        pl.when((i >= first) & (i < first + count))(functools.partial(fn, ref))
        first += count


def _rms_normed(x, g):
    ms = jnp.mean(x * x, axis=-1, keepdims=True)
    return x * lax.rsqrt(ms + EPS) * g


def _norm_matmul_kernel(tiles, with_extra, w_rows_are_outputs, *refs):
    nx = len(tiles)
    x_refs, g_ref, w_ref = refs[:nx], refs[nx], refs[nx + 1]
    if with_extra:
        we_ref, o_ref, oe_ref, h_scr = refs[nx + 2:]
    else:
        o_ref, h_scr = refs[nx + 2:]
    matmul = _dot_nt if w_rows_are_outputs else functools.partial(jnp.dot, preferred_element_type=F32)

    @pl.when(pl.program_id(1) == 0)
    def _():
        def build(x_ref):
            h_scr[...] = _rms_normed(x_ref[...], g_ref[...]).astype(BF16)
            if with_extra:
                oe_ref[...] = matmul(h_scr[...], we_ref[...].astype(BF16))
        _on_owner(x_refs, tiles, pl.program_id(0), build)

    o_ref[...] = matmul(h_scr[...], w_ref[...].astype(BF16)).astype(o_ref.dtype)


def norm_matmul(xs, g, w, layer, n_cols, *, tm, tn, out_dtype, w_extra=None, w_rows_are_outputs=False):
    d = xs[0].shape[1]
    tiles = _row_tiles(xs, tm)
    m = tm * sum(tiles)
    row_of, col0 = (lambda i, j: i), (lambda i, j: 0)
    w_spec = (pl.BlockSpec((None, tn, d), lambda i, j: (layer, j, 0)) if w_rows_are_outputs
              else pl.BlockSpec((None, d, tn), lambda i, j: (layer, 0, j)))
    in_specs = (_row_specs(tiles, (tm, d), row_of, col0, single_tile_unbuffered=len(tiles) > 1)
                + [pl.BlockSpec((1, d), lambda i, j: (0, 0)), w_spec])
    out_specs = [pl.BlockSpec((tm, tn), lambda i, j: (i, j))]
    out_shape = [jax.ShapeDtypeStruct((m, n_cols), out_dtype)]
    args = list(xs) + [g.reshape(1, d), w]
    if w_extra is not None:
        ne = w_extra.shape[0 if w_rows_are_outputs else 1]
        in_specs.append(pl.BlockSpec(w_extra.shape, lambda i, j: (0, 0)))
        out_specs.append(pl.BlockSpec((tm, ne), lambda i, j: (i, 0)))
        out_shape.append(jax.ShapeDtypeStruct((m, ne), F32))
        args.append(w_extra)
    outs = pl.pallas_call(
        functools.partial(_norm_matmul_kernel, tiles, w_extra is not None, w_rows_are_outputs),
        grid=(m // tm, n_cols // tn),
        in_specs=in_specs,
        out_specs=out_specs,
        out_shape=out_shape,
        scratch_shapes=[pltpu.VMEM((tm, d), BF16)],
        compiler_params=_params("arbitrary", "arbitrary"),
        name="norm_matmul",
    )(*args)
    return outs if w_extra is not None else outs[0]


def _norm_swiglu_kernel(x_ref, g_ref, wa_ref, wu_ref, o_ref, h_scr):
    @pl.when(pl.program_id(1) == 0)
    def _():
        h_scr[...] = _rms_normed(x_ref[...], g_ref[...]).astype(BF16)

    h = h_scr[...]
    a = jnp.dot(h, wa_ref[...].astype(BF16), preferred_element_type=F32)
    u = jnp.dot(h, wu_ref[...].astype(BF16), preferred_element_type=F32)
    o_ref[...] = (a * jax.nn.sigmoid(a) * u).astype(o_ref.dtype)


def norm_swiglu(x, g, w_up, layer, *, tm, tn):
    m, d = x.shape
    ff = w_up.shape[-1] // 2
    nj = ff // tn
    return pl.pallas_call(
        _norm_swiglu_kernel,
        grid=(m // tm, nj),
        in_specs=[
            pl.BlockSpec((tm, d), lambda i, j: (i, 0)),
            pl.BlockSpec((1, d), lambda i, j: (0, 0)),
            pl.BlockSpec((None, d, tn), lambda i, j: (layer, 0, j)),
            pl.BlockSpec((None, d, tn), lambda i, j: (layer, 0, j + nj)),
        ],
        out_specs=pl.BlockSpec((tm, tn), lambda i, j: (i, j)),
        out_shape=jax.ShapeDtypeStruct((m, ff), BF16),
        scratch_shapes=[pltpu.VMEM((tm, d), BF16)],
        compiler_params=_params("arbitrary", "arbitrary"),
        name="norm_swiglu",
    )(x, g.reshape(1, d), w_up, w_up)


def _matmul_residual_kernel(y_tiles, r_tiles, *refs):
    ny, nr = len(y_tiles), len(r_tiles)
    y_refs, w_ref, r_refs = refs[:ny], refs[ny], refs[ny + 1:ny + 1 + nr]
    o_ref, w_scr = refs[ny + 1 + nr:]
    i = pl.program_id(1)

    @pl.when(i == 0)
    def _():
        w_scr[...] = w_ref[...].astype(BF16)

    def product(y_ref):
        o_ref[...] = jnp.dot(y_ref[...], w_scr[...], preferred_element_type=F32)

    def add_residual(r_ref):
        o_ref[...] = o_ref[...] + r_ref[...]

    _on_owner(y_refs, y_tiles, i, product)
    _on_owner(r_refs, r_tiles, i, add_residual)


def matmul_residual(ys, w, layer, rs, *, tm, tn):
    k, n = w.shape[-2], w.shape[-1]
    y_tiles, r_tiles = _row_tiles(ys, tm), _row_tiles(rs, tm)
    assert sum(y_tiles) == sum(r_tiles)
    m = tm * sum(y_tiles)
    row_of = lambda j, i: i
    w_mode = dict(pipeline_mode=pl.Buffered(1)) if tn == n else {}
    in_specs = (_row_specs(y_tiles, (tm, k), row_of, lambda j, i: 0)
                + [pl.BlockSpec((None, k, tn), lambda j, i: (layer, 0, j), **w_mode)]
                + _row_specs(r_tiles, (tm, tn), row_of, lambda j, i: j))
    return pl.pallas_call(
        functools.partial(_matmul_residual_kernel, y_tiles, r_tiles),
        grid=(n // tn, m // tm),
        in_specs=in_specs,
        out_specs=pl.BlockSpec((tm, tn), lambda j, i: (i, j)),
        out_shape=jax.ShapeDtypeStruct((m, n), F32),
        scratch_shapes=[pltpu.VMEM((k, tn), BF16)],
        compiler_params=_params("arbitrary", "arbitrary"),
        name="matmul_residual",
    )(*ys, w, *rs)


def _final_norm_kernel(x_ref, g_ref, o_ref):
    o_ref[...] = _rms_normed(x_ref[...], g_ref[...])


def final_norm(x, g, *, row0, nrows, tm):
    d = x.shape[1]
    assert row0 % tm == 0 and nrows % tm == 0
    return pl.pallas_call(
        _final_norm_kernel,
        grid=(nrows // tm,),
        in_specs=[pl.BlockSpec((tm, d), lambda i: (row0 // tm + i, 0)), pl.BlockSpec((1, d), lambda i: (0, 0))],
        out_specs=pl.BlockSpec((tm, d), lambda i: (i, 0)),
        out_shape=jax.ShapeDtypeStruct((nrows, d), F32),
        compiler_params=_params("arbitrary"),
        name="final_norm",
    )(x, g.reshape(1, d))


def _segment_masks(nseg, L):
    R = nseg * L
    r = lax.broadcasted_iota(jnp.int32, (R, R), 0)
    c = lax.broadcasted_iota(jnp.int32, (R, R), 1)
    same = (r // L) == (c // L)
    return r, c, same


def _dot_nt(a, b):
    return lax.dot_general(a, b, (((1,), (1,)), ((), ())), preferred_element_type=F32)


def _dot_tn(a, b):
    return lax.dot_general(a, b, (((0,), (0,)), ((), ())), preferred_element_type=F32)


def _log_sigmoid(x):
    return jnp.minimum(x, 0.0) - jnp.log1p(jnp.exp(-jnp.abs(x)))


def _silu(x):
    return 0.5 * x * (1.0 + jnp.tanh(0.5 * x))


def _cumsum_rows_exact(tril, x):
    hi = x.astype(BF16)
    r1 = x - hi.astype(F32)
    mid = r1.astype(BF16)
    lo = (r1 - mid.astype(F32)).astype(BF16)
    return (jnp.dot(tril, hi, preferred_element_type=F32) + jnp.dot(tril, mid, preferred_element_type=F32)
            + jnp.dot(tril, lo, preferred_element_type=F32))


def _mlstm_kernel(nseg, L, dqk, dv,
                  p_ref, gp_ref, bias_ref, hg_ref, c0_ref, n0_ref, m0_ref,
                  y_ref, c_ref, n_ref, m_ref):
    H = M_HEADS
    R = nseg * L
    s1 = H * dqk
    scale = dqk ** -0.5

    @pl.when(pl.program_id(1) == 0)
    def _():
        c_ref[...] = c0_ref[...]
        n_ref[...] = n0_ref[...]
        m_ref[...] = m0_ref[...]

    r, c, same = _segment_masks(nseg, L)
    causal = same & (c <= r)
    causal_t = same & (r <= c)
    eye = r == c
    row_seg = lax.broadcasted_iota(jnp.int32, (R, 1), 0) // L

    gates = gp_ref[:, :2 * H] + bias_ref[...]
    capped = GATE_CAP * jnp.tanh(gates / GATE_CAP)
    log_f = _log_sigmoid(capped)

    m_prev = m_ref[0, 0]
    m_new_all = m_prev
    seg_i = lax.broadcasted_iota(jnp.int32, (nseg, H), 0)
    head_i = lax.broadcasted_iota(jnp.int32, (nseg, H), 1)

    for h in range(H):
        q = p_ref[:, h * dqk:(h + 1) * dqk]
        k = p_ref[:, s1 + h * dqk:s1 + (h + 1) * dqk]
        v = p_ref[:, 2 * s1 + h * dv:2 * s1 + (h + 1) * dv]
        o = p_ref[:, 2 * s1 + H * dv + h * dv:2 * s1 + H * dv + (h + 1) * dv]
        qf = q.astype(F32)
        kf = k.astype(F32)
        vf = v.astype(F32)

        ig_col = capped[:, h:h + 1]
        lf_col = log_f[:, H + h:H + h + 1]
        lf_row = jnp.sum(jnp.where(eye, lf_col, 0.0), axis=0, keepdims=True)
        ig_row = jnp.sum(jnp.where(eye, ig_col, 0.0), axis=0, keepdims=True)
        b_col = jnp.sum(jnp.where(causal, lf_row, 0.0), axis=1, keepdims=True)
        b_row = jnp.sum(jnp.where(causal_t, lf_col, 0.0), axis=0, keepdims=True)
        b_last_col = jnp.sum(jnp.where(same, lf_row, 0.0), axis=1, keepdims=True)

        m_col = jnp.zeros((R, 1), F32)
        n_rows = jnp.zeros((R, dqk), F32)
        for s in range(nseg):
            m_s = m_prev[s:s + 1, h:h + 1]
            m_col = jnp.where(row_seg == s, m_s, m_col)
            n_rows = jnp.where(row_seg == s, n_ref[0, s, h:h + 1, :], n_rows)

        dlog = jnp.where(causal, b_col - b_row + ig_row, -jnp.inf)
        inter = b_col + m_col
        m_t = jnp.maximum(inter, jnp.max(dlog, axis=1, keepdims=True))
        w_intra = jnp.exp(dlog - m_t)
        w_inter = jnp.exp(inter - m_t)
        sc = _dot_nt(q, k) * scale * w_intra

        qc = jnp.zeros((R, dv), F32)
        for s in range(nseg):
            qc_s = jnp.dot(q, c_ref[0, s, h].astype(BF16), preferred_element_type=F32)
            qc = qc_s if nseg == 1 else jnp.where(row_seg == s, qc_s, qc)
        num = jnp.dot(sc.astype(BF16), v, preferred_element_type=F32) + w_inter * qc
        qn = jnp.sum(qf * n_rows, axis=1, keepdims=True)
        den = jnp.sum(sc, axis=1, keepdims=True) + w_inter * qn
        hc = num / jnp.maximum(jnp.abs(den), jnp.exp(-m_t))

        hn = hc * lax.rsqrt(jnp.mean(hc * hc, axis=1, keepdims=True) + EPS) * hg_ref[:, h * dv:(h + 1) * dv]
        y_ref[:, h * dv:(h + 1) * dv] = (hn * jax.nn.sigmoid(o.astype(F32))).astype(y_ref.dtype)

        g = b_last_col - b_col + ig_col
        m_new_col = jnp.zeros((R, 1), F32)
        decays = []
        for s in range(nseg):
            in_s = row_seg == s
            m_s = m_prev[s:s + 1, h:h + 1]
            b_last_s = jnp.sum(jnp.where(in_s, lf_col, 0.0), axis=0, keepdims=True)
            g_max_s = jnp.max(jnp.where(in_s, g, -jnp.inf), axis=0, keepdims=True)
            m_new_s = jnp.maximum(b_last_s + m_s, g_max_s)
            decays.append(jnp.exp(b_last_s + m_s - m_new_s))
            m_new_col = jnp.where(in_s, m_new_s, m_new_col)
            m_new_all = jnp.where((seg_i == s) & (head_i == h), m_new_s, m_new_all)
        w_k = jnp.exp(g - m_new_col) * scale
        wv = w_k * vf
        wk = w_k * kf
        for s in range(nseg):
            in_s = row_seg == s
            wv_s = wv if nseg == 1 else jnp.where(in_s, wv, 0.0)
            wk_s = wk if nseg == 1 else jnp.where(in_s, wk, 0.0)
            c_ref[0, s, h] = decays[s] * c_ref[0, s, h] + _dot_tn(k, wv_s.astype(BF16))
            n_ref[0, s, h:h + 1, :] = decays[s] * n_ref[0, s, h:h + 1, :] + jnp.sum(wk_s, axis=0, keepdims=True)

    m_ref[0, 0] = m_new_all


def mlstm_mixer(proj, gate_pre, bias, head_g, c0, n0, m0, *, row0, batch, seq, nseg, chunk):
    H = M_HEADS
    dqk, dv = c0.shape[-2], c0.shape[-1]
    R = nseg * chunk
    nchunks = seq // chunk
    assert nseg == 1 or nchunks == 1
    nblocks = batch // nseg
    blk0 = row0 // R
    rows = lambda i, c: (blk0 + i * nchunks + c, 0)
    width = proj.shape[1]
    c0 = c0.reshape(nblocks, nseg, H, dqk, dv)
    n0 = n0.reshape(nblocks, nseg, H, dqk)
    m0 = m0.reshape(nblocks, 1, nseg, H)
    y, c_out, n_out, m_out = pl.pallas_call(
        functools.partial(_mlstm_kernel, nseg, chunk, dqk, dv),
        grid=(nblocks, nchunks),
        in_specs=[
            pl.BlockSpec((R, width), rows),
            pl.BlockSpec((R, gate_pre.shape[1]), rows),
            pl.BlockSpec((1, 2 * H), lambda i, c: (0, 0)),
            pl.BlockSpec((1, H * dv), lambda i, c: (0, 0)),
            pl.BlockSpec((1, nseg, H, dqk, dv), lambda i, c: (i, 0, 0, 0, 0)),
            pl.BlockSpec((1, nseg, H, dqk), lambda i, c: (i, 0, 0, 0)),
            pl.BlockSpec((1, 1, nseg, H), lambda i, c: (i, 0, 0, 0)),
        ],
        out_specs=[
            pl.BlockSpec((R, H * dv), lambda i, c: (i * nchunks + c, 0)),
            pl.BlockSpec((1, nseg, H, dqk, dv), lambda i, c: (i, 0, 0, 0, 0)),
            pl.BlockSpec((1, nseg, H, dqk), lambda i, c: (i, 0, 0, 0)),
            pl.BlockSpec((1, 1, nseg, H), lambda i, c: (i, 0, 0, 0)),
        ],
        out_shape=[
            jax.ShapeDtypeStruct((batch * seq, H * dv), BF16),
            jax.ShapeDtypeStruct(c0.shape, F32),
            jax.ShapeDtypeStruct(n0.shape, F32),
            jax.ShapeDtypeStruct(m0.shape, F32),
        ],
        compiler_params=_params("arbitrary", "arbitrary"),
        name="mlstm_mixer",
    )(proj, gate_pre, bias.reshape(1, 2 * H), head_g.reshape(1, H * dv), c0, n0, m0)
    return (y, c_out.reshape(batch, H, dqk, dv), n_out.reshape(batch, H, dqk), m_out.reshape(batch, H))


def _mlstm_long_kernel(R, dqk, dv,
                       p_ref, gp_ref, bias_ref, hg_ref, c0_ref, n0_ref, m0_ref,
                       y_ref, c_ref, n_ref, m_ref, kt_scr):
    H = M_HEADS
    s1 = H * dqk
    scale = dqk ** -0.5

    @pl.when(pl.program_id(1) == 0)
    def _():
        c_ref[...] = c0_ref[...]
        n_ref[...] = n0_ref[...]
        m_ref[...] = m0_ref[...]

    r = lax.broadcasted_iota(jnp.int32, (R, R), 0)
    c = lax.broadcasted_iota(jnp.int32, (R, R), 1)
    causal = c <= r
    tril = causal.astype(BF16)

    capped = GATE_CAP * jnp.tanh((gp_ref[...] + bias_ref[...]) / GATE_CAP)
    cum = _cumsum_rows_exact(tril, _log_sigmoid(capped))
    ig_on_f = pltpu.roll(capped, H, axis=1)
    m_prev = m_ref[0]
    b_last = cum[R - 1:R, :]
    g_all = b_last - cum + ig_on_f
    m_new = jnp.maximum(b_last + m_prev, jnp.max(g_all, axis=0, keepdims=True))
    decay_all = jnp.exp(b_last + m_prev - m_new)
    wk_all = jnp.exp(g_all - m_new) * scale
    inter_all = cum + m_prev
    rows_t = (ig_on_f - cum).T

    RB = min(R, M_ROW_BLOCK)
    block_masks = [lax.broadcasted_iota(jnp.int32, (RB, (i + 1) * RB), 1)
                   <= lax.broadcasted_iota(jnp.int32, (RB, (i + 1) * RB), 0) + i * RB for i in range(R // RB)]

    for h in range(H):
        lane = H + h
        q_cols = slice(h * dqk, (h + 1) * dqk)
        k_cols = slice(s1 + h * dqk, s1 + (h + 1) * dqk)
        v_cols = slice(2 * s1 + h * dv, 2 * s1 + (h + 1) * dv)
        o_cols = slice(2 * s1 + H * dv + h * dv, 2 * s1 + H * dv + (h + 1) * dv)
        c_prev = c_ref[0, h]
        c_bf = c_prev.astype(BF16)
        n_prev = n_ref[0, h:h + 1, :]
        hg = hg_ref[:, h * dv:(h + 1) * dv]
        kt_scr[h] = p_ref[:, k_cols].T

        for i in range(R // RB):
            rows = slice(i * RB, (i + 1) * RB)
            ncols = (i + 1) * RB
            q = p_ref[rows, q_cols]
            gate_diff = cum[rows, lane:lane + 1] + rows_t[lane:lane + 1, :ncols]
            dlog = jnp.where(block_masks[i], gate_diff, -jnp.inf)
            inter = inter_all[rows, lane:lane + 1]
            m_t = jnp.maximum(inter, jnp.max(dlog, axis=1, keepdims=True))
            w_intra = jnp.exp(dlog - m_t)
            w_inter = jnp.exp(inter - m_t)
            sc = jnp.dot(q, kt_scr[h, :, :ncols], preferred_element_type=F32) * scale * w_intra
            num = (jnp.dot(sc.astype(BF16), p_ref[:ncols, v_cols], preferred_element_type=F32)
                   + w_inter * jnp.dot(q, c_bf, preferred_element_type=F32))
            qn = jnp.sum(q.astype(F32) * n_prev, axis=1, keepdims=True)
            den = jnp.sum(sc, axis=1, keepdims=True) + w_inter * qn
            inv = 1.0 / jnp.maximum(jnp.abs(den), jnp.exp(-m_t))
            norm = inv * lax.rsqrt(inv * inv * jnp.mean(num * num, axis=1, keepdims=True) + EPS)
            o = p_ref[rows, o_cols].astype(F32)
            y_ref[rows, h * dv:(h + 1) * dv] = (num * norm * hg * jax.nn.sigmoid(o)).astype(y_ref.dtype)

        k = p_ref[:, k_cols]
        w_k = wk_all[:, lane:lane + 1]
        decay = decay_all[:, lane:lane + 1]
        wv = (w_k * p_ref[:, v_cols].astype(F32)).astype(BF16)
        c_ref[0, h] = decay * c_prev + _dot_tn(k, wv)
        n_ref[0, h:h + 1, :] = decay * n_prev + jnp.sum(w_k * k.astype(F32), axis=0, keepdims=True)

    m_ref[0] = m_new


def mlstm_long_mixer(proj, gate_pre, bias, head_g, c0, n0, m0, *, row0, batch, seq, rows):
    H = M_HEADS
    dqk, dv = c0.shape[-2], c0.shape[-1]
    R = rows
    lanes = gate_pre.shape[1]
    assert seq % R == 0 and row0 % R == 0 and lanes >= 2 * H
    nsteps = seq // R
    blk0 = row0 // R
    rows_of = lambda i, c: (blk0 + i * nsteps + c, 0)
    bias_l = jnp.pad(bias.reshape(1, 2 * H), ((0, 0), (0, lanes - 2 * H)))
    m0_l = jnp.pad(m0, ((0, 0), (H, lanes - 2 * H))).reshape(batch, 1, lanes)
    y, c_out, n_out, m_out = pl.pallas_call(
        functools.partial(_mlstm_long_kernel, R, dqk, dv),
        grid=(batch, nsteps),
        in_specs=[
            pl.BlockSpec((R, proj.shape[1]), rows_of),
            pl.BlockSpec((R, lanes), rows_of),
            pl.BlockSpec((1, lanes), lambda i, c: (0, 0)),
            pl.BlockSpec((1, H * dv), lambda i, c: (0, 0)),
            pl.BlockSpec((1, H, dqk, dv), lambda i, c: (i, 0, 0, 0)),
            pl.BlockSpec((1, H, dqk), lambda i, c: (i, 0, 0)),
            pl.BlockSpec((1, 1, lanes), lambda i, c: (i, 0, 0)),
        ],
        out_specs=[
            pl.BlockSpec((R, H * dv), lambda i, c: (i * nsteps + c, 0)),
            pl.BlockSpec((1, H, dqk, dv), lambda i, c: (i, 0, 0, 0)),
            pl.BlockSpec((1, H, dqk), lambda i, c: (i, 0, 0)),
            pl.BlockSpec((1, 1, lanes), lambda i, c: (i, 0, 0)),
        ],
        out_shape=[
            jax.ShapeDtypeStruct((batch * seq, H * dv), BF16),
            jax.ShapeDtypeStruct(c0.shape, F32),
            jax.ShapeDtypeStruct(n0.shape, F32),
            jax.ShapeDtypeStruct(m0_l.shape, F32),
        ],
        scratch_shapes=[pltpu.VMEM((H, dqk, R), BF16)],
        compiler_params=_params("arbitrary", "arbitrary"),
        name="mlstm_long_mixer",
    )(proj, gate_pre, bias_l, head_g.reshape(1, H * dv), c0, n0, m0_l)
    return y, c_out, n_out, m_out[:, 0, H:2 * H]


def _mlstm_short_kernel(nb, T, dqk, dv,
                        p_ref, gp_ref, bias_ref, hg_ref, c0_ref, n0_ref, m0_ref,
                        y_ref, c_ref, n_ref, m_ref):
    H = M_HEADS
    R = nb * T
    PAIR = 2 * T
    s1 = H * dqk
    scale = dqk ** -0.5
    assert T % 8 == 0 and nb % 2 == 0

    r = lax.broadcasted_iota(jnp.int32, (R, R), 0)
    c = lax.broadcasted_iota(jnp.int32, (R, R), 1)
    causal = ((r // T) == (c // T)) & (c <= r)
    tril = causal.astype(BF16)
    seq_cols = (lax.broadcasted_iota(jnp.int32, (nb, 1, R), 2) // T
                == lax.broadcasted_iota(jnp.int32, (nb, 1, R), 0)).astype(BF16)
    first_of_pair = lax.broadcasted_iota(jnp.int32, (PAIR, 1), 0) < T

    def per_seq(x):
        return x.reshape(nb, T, x.shape[-1])

    def seq_rows(x3):
        return jnp.broadcast_to(x3, (nb, T, x3.shape[-1])).reshape(R, x3.shape[-1])

    capped = GATE_CAP * jnp.tanh((gp_ref[...] + bias_ref[...]) / GATE_CAP)
    cum = _cumsum_rows_exact(tril, _log_sigmoid(capped))
    ig_on_f = pltpu.roll(capped, H, axis=1)
    m_prev = m0_ref[...]
    b_last = seq_rows(per_seq(cum)[:, T - 1:T, :])
    g_all = b_last - cum + ig_on_f
    m_new = jnp.maximum(b_last + m_prev, seq_rows(jnp.max(per_seq(g_all), axis=1, keepdims=True)))
    decay_all = jnp.exp(b_last + m_prev - m_new)
    wk_all = jnp.exp(g_all - m_new) * scale
    inter_all = cum + m_prev
    rows_t = (ig_on_f - cum).T

    for h in range(H):
        lane = H + h
        q = p_ref[:, h * dqk:(h + 1) * dqk]
        k = p_ref[:, s1 + h * dqk:s1 + (h + 1) * dqk]
        v = p_ref[:, 2 * s1 + h * dv:2 * s1 + (h + 1) * dv]
        o = p_ref[:, 2 * s1 + H * dv + h * dv:2 * s1 + H * dv + (h + 1) * dv]
        kf = k.astype(F32)

        dlog = jnp.where(causal, cum[:, lane:lane + 1] + rows_t[lane:lane + 1, :], -jnp.inf)
        inter = inter_all[:, lane:lane + 1]
        m_t = jnp.maximum(inter, jnp.max(dlog, axis=1, keepdims=True))
        w_intra = jnp.exp(dlog - m_t)
        w_inter = jnp.exp(inter - m_t)
        sc = _dot_nt(q, k) * scale * w_intra
        num_intra = jnp.dot(sc.astype(BF16), v, preferred_element_type=F32)
        n_prev = n0_ref[h]
        qn = jnp.sum(q.astype(F32) * n_prev, axis=1, keepdims=True)
        den = jnp.sum(sc, axis=1, keepdims=True) + w_inter * qn
        inv = 1.0 / jnp.maximum(jnp.abs(den), jnp.exp(-m_t))
        hg = hg_ref[:, h * dv:(h + 1) * dv]

        w_k = wk_all[:, lane:lane + 1]
        k_t = kf.T.astype(BF16)
        stacked = (k_t[None] * seq_cols).reshape(nb * dqk, R)
        upd = jnp.dot(stacked, (w_k * v.astype(F32)).astype(BF16), preferred_element_type=F32)
        n_ref[h] = (decay_all[:, lane:lane + 1] * n_prev
                    + seq_rows(jnp.sum(per_seq(w_k * kf), axis=1, keepdims=True)))

        for pr in range(nb // 2):
            rows = slice(pr * PAIR, (pr + 1) * PAIR)
            lhs = p_ref[rows, h * dqk:(h + 1) * dqk]
            qc = []
            for b in (2 * pr, 2 * pr + 1):
                c_prev = c0_ref[0, b, h]
                qc.append(jnp.dot(lhs, c_prev.astype(BF16), preferred_element_type=F32))
                c_ref[0, b, h] = decay_all[b * T:b * T + 1, lane:lane + 1] * c_prev + upd[b * dqk:(b + 1) * dqk, :]
            num = num_intra[rows, :] + w_inter[rows, :] * jnp.where(first_of_pair, qc[0], qc[1])
            hc = num * inv[rows, :]
            hn = hc * lax.rsqrt(jnp.mean(hc * hc, axis=1, keepdims=True) + EPS) * hg
            y_ref[rows, h * dv:(h + 1) * dv] = (hn * jax.nn.sigmoid(o[rows, :].astype(F32))).astype(y_ref.dtype)

    m_ref[...] = m_new


def mlstm_short_mixer(proj, gate_pre, bias, head_g, c0, n0, m0, *, row0, batch, seq, nb):
    H = M_HEADS
    dqk, dv = c0.shape[-2], c0.shape[-1]
    R = nb * seq
    lanes = gate_pre.shape[1]
    assert batch % nb == 0 and row0 % R == 0 and lanes >= 2 * H
    nblocks = batch // nb
    blk0 = row0 // R
    bias_l = jnp.pad(bias.reshape(1, 2 * H), ((0, 0), (0, lanes - 2 * H)))
    m0_l = jnp.repeat(jnp.pad(m0, ((0, 0), (H, lanes - 2 * H))), seq, axis=0)
    n0_t = jnp.repeat(jnp.swapaxes(n0, 0, 1), seq, axis=1)
    y, c_out, n_out, m_out = pl.pallas_call(
        functools.partial(_mlstm_short_kernel, nb, seq, dqk, dv),
        grid=(nblocks,),
        in_specs=[
            pl.BlockSpec((R, proj.shape[1]), lambda i: (blk0 + i, 0)),
            pl.BlockSpec((R, lanes), lambda i: (blk0 + i, 0)),
            pl.BlockSpec((1, lanes), lambda i: (0, 0)),
            pl.BlockSpec((1, H * dv), lambda i: (0, 0)),
            pl.BlockSpec((1, nb, H, dqk, dv), lambda i: (i, 0, 0, 0, 0)),
            pl.BlockSpec((H, R, dqk), lambda i: (0, i, 0)),
            pl.BlockSpec((R, lanes), lambda i: (i, 0)),
        ],
        out_specs=[
            pl.BlockSpec((R, H * dv), lambda i: (i, 0)),
            pl.BlockSpec((1, nb, H, dqk, dv), lambda i: (i, 0, 0, 0, 0)),
            pl.BlockSpec((H, R, dqk), lambda i: (0, i, 0)),
            pl.BlockSpec((R, lanes), lambda i: (i, 0)),
        ],
        out_shape=[
            jax.ShapeDtypeStruct((batch * seq, H * dv), BF16),
            jax.ShapeDtypeStruct((nblocks, nb, H, dqk, dv), F32),
            jax.ShapeDtypeStruct(n0_t.shape, F32),
            jax.ShapeDtypeStruct(m0_l.shape, F32),
        ],
        compiler_params=_params("arbitrary"),
        name="mlstm_short_mixer",
    )(proj, gate_pre, bias_l, head_g.reshape(1, H * dv), c0.reshape(nblocks, nb, H, dqk, dv), n0_t, m0_l)
    return (y, c_out.reshape(batch, H, dqk, dv), jnp.swapaxes(n_out[:, ::seq, :], 0, 1), m_out[::seq, H:2 * H])


def _hgrn_kernel(nseg, L, heads, dk, dv,
                 p_ref, lb_ref, gn_ref, s0_ref, y_ref, s_ref):
    R = nseg * L
    fdim = heads * dk
    scale = dk ** -0.5
    sub = min(L, H_SUB)
    assert L % sub == 0 and L // sub in (1, 2)

    @pl.when(pl.program_id(1) == 0)
    def _():
        s_ref[...] = s0_ref[...]

    r, c, same = _segment_masks(nseg, L)
    causal = same & (c <= r)
    same_sub = (r // sub) == (c // sub)
    diag_mask = causal & same_sub
    tril = causal.astype(F32)

    row = lax.broadcasted_iota(jnp.int32, (R, 1), 0)
    row_seg = row // L
    pos = row % L
    mid_sel = (same_sub & ((c % sub) == (sub // 2))).astype(F32)
    last_sel = (same & ((c % L) == (L - 1))).astype(F32)
    if L // sub == 2:
        bnd_sel = (same & ((c % L) == (sub - 1))).astype(F32)
        upper = pos >= sub
    hp = lax.Precision.HIGHEST

    for h in range(heads):
        q = p_ref[:, h * dk:(h + 1) * dk].astype(F32) * scale
        fpre = p_ref[:, fdim + h * dk:fdim + (h + 1) * dk].astype(F32)
        iv = p_ref[:, 2 * fdim + h * dv:2 * fdim + (h + 1) * dv]
        gate = p_ref[:, 2 * fdim + heads * dv + h * dv:2 * fdim + heads * dv + (h + 1) * dv].astype(F32)
        lb = lb_ref[:, h * dk:(h + 1) * dk]
        f = lb + (1.0 - lb) * jax.nn.sigmoid(fpre)
        log_f = jnp.log(f)
        k = 1.0 - f

        a = jnp.dot(tril, log_f, preferred_element_type=F32, precision=hp)
        a_mid = jnp.dot(mid_sel, a, preferred_element_type=F32, precision=hp)
        a_last = jnp.dot(last_sel, a, preferred_element_type=F32, precision=hp)

        qd = (q * jnp.exp(a - a_mid)).astype(BF16)
        kd = (k * jnp.exp(a_mid - a)).astype(BF16)
        sc = jnp.where(diag_mask, _dot_nt(qd, kd), 0.0)
        if L // sub == 2:
            a_bnd = jnp.dot(bnd_sel, a, preferred_element_type=F32, precision=hp)
            qo = jnp.where(upper, q * jnp.exp(jnp.minimum(a - a_bnd, 0.0)), 0.0).astype(BF16)
            ko = jnp.where(upper, 0.0, k * jnp.exp(jnp.minimum(a_bnd - a, 0.0))).astype(BF16)
            sc = sc + jnp.where(same, _dot_nt(qo, ko), 0.0)

        qe = (q * jnp.exp(a)).astype(BF16)
        inter = jnp.zeros((R, dv), F32)
        for s in range(nseg):
            inter_s = jnp.dot(qe, s_ref[0, s, h].astype(BF16), preferred_element_type=F32)
            inter = inter_s if nseg == 1 else jnp.where(row_seg == s, inter_s, inter)
        oc = jnp.dot(sc.astype(BF16), iv, preferred_element_type=F32) + inter

        on = oc * lax.rsqrt(jnp.mean(oc * oc, axis=1, keepdims=True) + EPS) * gn_ref[:, h * dv:(h + 1) * dv]
        y_ref[:, h * dv:(h + 1) * dv] = (on * _silu(gate)).astype(y_ref.dtype)

        ke = k * jnp.exp(a_last - a)
        for s in range(nseg):
            in_s = row_seg == s
            ke_s = ke if nseg == 1 else jnp.where(in_s, ke, 0.0)
            a_last_s = a[s * L + L - 1:s * L + L, :]
            upd = _dot_tn(ke_s.astype(BF16), iv)
            decay_col = jnp.sum(jnp.where(lax.broadcasted_iota(jnp.int32, (dk, dk), 0)
                                          == lax.broadcasted_iota(jnp.int32, (dk, dk), 1),
                                          jnp.exp(a_last_s), 0.0), axis=1, keepdims=True)
            s_ref[0, s, h] = decay_col * s_ref[0, s, h] + upd


def hgrn_mixer(proj, lb, gn_g, s0, *, row0, batch, seq, nseg, chunk):
    heads, dk, dv = s0.shape[-3], s0.shape[-2], s0.shape[-1]
    R = nseg * chunk
    nchunks = seq // chunk
    assert nseg == 1 or nchunks == 1
    nblocks = batch // nseg
    blk0 = row0 // R
    rows = lambda i, c: (blk0 + i * nchunks + c, 0)
    s0 = s0.reshape(nblocks, nseg, heads, dk, dv)
    y, s_out = pl.pallas_call(
        functools.partial(_hgrn_kernel, nseg, chunk, heads, dk, dv),
        grid=(nblocks, nchunks),
        in_specs=[
            pl.BlockSpec((R, proj.shape[1]), rows),
            pl.BlockSpec((1, heads * dk), lambda i, c: (0, 0)),
            pl.BlockSpec((1, heads * dv), lambda i, c: (0, 0)),
            pl.BlockSpec((1, nseg, heads, dk, dv), lambda i, c: (i, 0, 0, 0, 0)),
        ],
        out_specs=[
            pl.BlockSpec((R, heads * dv), lambda i, c: (i * nchunks + c, 0)),
            pl.BlockSpec((1, nseg, heads, dk, dv), lambda i, c: (i, 0, 0, 0, 0)),
        ],
        out_shape=[
            jax.ShapeDtypeStruct((batch * seq, heads * dv), BF16),
            jax.ShapeDtypeStruct(s0.shape, F32),
        ],
        compiler_params=_params("arbitrary", "arbitrary"),
        name="hgrn_mixer",
    )(proj, lb.reshape(1, heads * dk), gn_g.reshape(1, heads * dv), s0)
    return y, s_out.reshape(batch, heads, dk, dv)


def _hgrn_short_kernel(nb, T, heads, dk, dv,
                       p_ref, lb_ref, gn_ref, s0_ref, y_ref, s_ref,
                       qd_scr, kd_scr, qe_scr, ke_scr, dec_scr):
    R = nb * T
    fdim = heads * dk
    scale = dk ** -0.5
    W = 2 * dk
    PAIR = 2 * T
    assert T % 8 == 0 and T <= H_SUB and nb % 2 == 0

    r = lax.broadcasted_iota(jnp.int32, (R, R), 0)
    c = lax.broadcasted_iota(jnp.int32, (R, R), 1)
    causal = ((r // T) == (c // T)) & (c <= r)
    tril = causal.astype(BF16)
    seq_cols = (lax.broadcasted_iota(jnp.int32, (nb, 1, R), 2) // T
                == lax.broadcasted_iota(jnp.int32, (nb, 1, R), 0)).astype(BF16)
    first_of_pair = lax.broadcasted_iota(jnp.int32, (PAIR, 1), 0) < T

    def minus_ref_row(a, row):
        a3 = a.reshape(nb, T, a.shape[-1])
        return (a3 - a3[:, row:row + 1, :]).reshape(a.shape)

    for g in range(fdim // W):
        sl = slice(g * W, (g + 1) * W)
        q = p_ref[:, sl].astype(F32) * scale
        fpre = p_ref[:, fdim + g * W:fdim + (g + 1) * W].astype(F32)
        lb = lb_ref[:, sl]
        f = lb + (1.0 - lb) * jax.nn.sigmoid(fpre)
        k = 1.0 - f
        a = _cumsum_rows_exact(tril, jnp.log(f))
        d = minus_ref_row(a, T // 2)
        to_last = -minus_ref_row(a, T - 1)
        qd_scr[:, sl] = (q * jnp.exp(d)).astype(BF16)
        kd_scr[:, sl] = (k * jnp.exp(-d)).astype(BF16)
        qe_scr[:, sl] = (q * jnp.exp(a)).astype(BF16)
        ke_scr[:, sl] = k * jnp.exp(to_last)
        dec_scr[:, sl] = jnp.exp(a + to_last)

    for h in range(heads):
        sl = slice(h * dk, (h + 1) * dk)
        iv = p_ref[:, 2 * fdim + h * dv:2 * fdim + (h + 1) * dv]
        sc = jnp.where(causal, _dot_nt(qd_scr[:, sl], kd_scr[:, sl]), 0.0)
        oc = jnp.dot(sc.astype(BF16), iv, preferred_element_type=F32)
        ke_t = ke_scr[:, sl].T.astype(BF16)
        dec_t = dec_scr[:, sl].T
        stacked = (ke_t[None] * seq_cols).reshape(nb * dk, R)
        upd = jnp.dot(stacked, iv, preferred_element_type=F32)
        gn = gn_ref[:, h * dv:(h + 1) * dv]
        for pr in range(nb // 2):
            rows = slice(pr * PAIR, (pr + 1) * PAIR)
            lhs = qe_scr[rows, sl]
            inter = []
            for b in (2 * pr, 2 * pr + 1):
                s_prev = s0_ref[0, b, h]
                inter.append(jnp.dot(lhs, s_prev.astype(BF16), preferred_element_type=F32))
                s_ref[0, b, h] = dec_t[:, b * T:b * T + 1] * s_prev + upd[b * dk:(b + 1) * dk, :]
            o2 = oc[rows, :] + jnp.where(first_of_pair, inter[0], inter[1])
            gate = p_ref[rows, 2 * fdim + heads * dv + h * dv:2 * fdim + heads * dv + (h + 1) * dv].astype(F32)
            on = o2 * lax.rsqrt(jnp.mean(o2 * o2, axis=1, keepdims=True) + EPS) * gn
            y_ref[rows, h * dv:(h + 1) * dv] = (on * _silu(gate)).astype(y_ref.dtype)


def hgrn_short_mixer(proj, lb, gn_g, s0, *, row0, batch, seq, nb):
    heads, dk, dv = s0.shape[-3], s0.shape[-2], s0.shape[-1]
    R = nb * seq
    assert batch % nb == 0 and row0 % R == 0
    nblocks = batch // nb
    blk0 = row0 // R
    fdim = heads * dk
    s0 = s0.reshape(nblocks, nb, heads, dk, dv)
    y, s_out = pl.pallas_call(
        functools.partial(_hgrn_short_kernel, nb, seq, heads, dk, dv),
        grid=(nblocks,),
        in_specs=[
            pl.BlockSpec((R, proj.shape[1]), lambda i: (blk0 + i, 0)),
            pl.BlockSpec((1, fdim), lambda i: (0, 0)),
            pl.BlockSpec((1, heads * dv), lambda i: (0, 0)),
            pl.BlockSpec((1, nb, heads, dk, dv), lambda i: (i, 0, 0, 0, 0)),
        ],
        out_specs=[
            pl.BlockSpec((R, heads * dv), lambda i: (i, 0)),
            pl.BlockSpec((1, nb, heads, dk, dv), lambda i: (i, 0, 0, 0, 0)),
        ],
        out_shape=[
            jax.ShapeDtypeStruct((batch * seq, heads * dv), BF16),
            jax.ShapeDtypeStruct(s0.shape, F32),
        ],
        scratch_shapes=[
            pltpu.VMEM((R, fdim), BF16), pltpu.VMEM((R, fdim), BF16), pltpu.VMEM((R, fdim), BF16),
            pltpu.VMEM((R, fdim), F32), pltpu.VMEM((R, fdim), F32),
        ],
        compiler_params=_params("arbitrary"),
        name="hgrn_short_mixer",
    )(proj, lb.reshape(1, fdim), gn_g.reshape(1, heads * dv), s0)
    return y, s_out.reshape(batch, heads, dk, dv)


def _hgrn_long_kernel(R, heads, dk, dv, levels,
                      p_ref, lb_ref, gn_ref, s0_ref, y_ref, s_ref,
                      st_scr, dec_scr, qd_scr, kd_scr, qe_scr, ke_scr, ql_scr, kl_scr, k_scr, a_scr,
                      kt_scr, sb_scr, ivt_scr):
    fdim = heads * dk
    eps_unscaled = EPS * dk
    W = 2 * dk
    last_step = pl.num_programs(1) - 1

    @pl.when(pl.program_id(1) == 0)
    def _():
        for h in range(heads):
            st_scr[h] = s0_ref[0, h].T

    r = lax.broadcasted_iota(jnp.int32, (R, R), 0)
    c = lax.broadcasted_iota(jnp.int32, (R, R), 1)
    tril = (c <= r).astype(BF16)
    diag_mask = ((r // H_SUB) == (c // H_SUB)) & (c <= r)
    level_masks = [((r // G) == (c // G)) & ((r % G) >= G // 2) & ((c % G) < G // 2) for G in levels]

    NSB = R // H_SUB
    zeros_sub = jnp.zeros((H_SUB, dk), BF16)

    for g in range(fdim // W):
        fpre = p_ref[:, fdim + g * W:fdim + (g + 1) * W].astype(F32)
        lb = lb_ref[:, g * W:(g + 1) * W]
        f = lb + (1.0 - lb) * jax.nn.sigmoid(fpre)
        k = 1.0 - f
        a = _cumsum_rows_exact(tril, jnp.log(f))
        for j in range(W // dk):
            k_scr[g * (W // dk) + j] = k[:, j * dk:(j + 1) * dk]
            a_scr[g * (W // dk) + j] = a[:, j * dk:(j + 1) * dk]

    def rows_of(ref, h, start, count, stride):
        if count == 1:
            return ref[h, start:start + 1, :]
        return ref.at[h][pl.ds(start, count, stride=stride), :]

    for h in range(heads):
        sl = slice(h * dk, (h + 1) * dk)
        a_last = a_scr[h, R - 1:R, :]
        dec_scr[:, sl] = jnp.exp(a_last)
        mids = rows_of(a_scr, h, H_SUB // 2, NSB, H_SUB)
        c_qe = jnp.exp(mids)
        c_ke = jnp.exp(a_last - mids)
        c_lv = []
        for G in levels:
            per = G // H_SUB
            bnd = rows_of(a_scr, h, G // 2 - 1, R // G, G)
            consts = []
            for o in range(per):
                mid_o = rows_of(a_scr, h, o * H_SUB + H_SUB // 2, R // G, G)
                consts.append(jnp.exp(mid_o - bnd) if o >= per // 2 else jnp.exp(bnd - mid_o))
            c_lv.append(consts)
        for s in range(NSB):
            rows = slice(s * H_SUB, (s + 1) * H_SUB)
            d = a_scr[h, rows, :] - mids[s:s + 1, :]
            qd = p_ref[rows, sl].astype(F32) * jnp.exp(d)
            kd = k_scr[h, rows, :] * jnp.exp(-d)
            qd_scr[rows, sl] = qd.astype(BF16)
            kd_scr[rows, sl] = kd.astype(BF16)
            qe_scr[rows, sl] = (qd * c_qe[s:s + 1, :]).astype(BF16)
            ke_scr[rows, sl] = (kd * c_ke[s:s + 1, :]).astype(BF16)
            for li, G in enumerate(levels):
                per = G // H_SUB
                o, gi = s % per, s // per
                const = c_lv[li][o][gi:gi + 1, :]
                if o >= per // 2:
                    ql_scr[li, rows, sl] = (qd * const).astype(BF16)
                    kl_scr[li, rows, sl] = zeros_sub
                else:
                    kl_scr[li, rows, sl] = (kd * const).astype(BF16)
                    ql_scr[li, rows, sl] = zeros_sub

    def paired(lhs_a, lhs_b, rhs_ab):
        out = jnp.dot(jnp.concatenate([lhs_a, lhs_b], axis=0), rhs_ab, preferred_element_type=F32)
        ra, ca = lhs_a.shape[0], rhs_ab.shape[1] // 2
        return out[:ra, :ca], out[ra:, ca:]

    assert dk == dv
    for p in range(heads // 2):
        ha, hb = 2 * p, 2 * p + 1
        sa, sb = slice(ha * dk, (ha + 1) * dk), slice(hb * dk, (hb + 1) * dk)
        sab = slice(ha * dk, (hb + 1) * dk)
        st_a, st_b = st_scr[ha], st_scr[hb]
        kt_scr[p, 0, :, :R] = kd_scr[:, sa].T
        kt_scr[p, 0, :, R:] = kd_scr[:, sb].T
        for li in range(len(levels)):
            kt_scr[p, 1 + li, :, :R] = kl_scr[li, :, sa].T
            kt_scr[p, 1 + li, :, R:] = kl_scr[li, :, sb].T
        sb_scr[p, :, :dv] = st_a.astype(BF16).T
        sb_scr[p, :, dv:] = st_b.astype(BF16).T
        da, db = paired(qd_scr[:, sa], qd_scr[:, sb], kt_scr[p, 0])
        sc_a, sc_b = jnp.where(diag_mask, da, 0.0), jnp.where(diag_mask, db, 0.0)
        for li in range(len(levels)):
            la, lb_ = paired(ql_scr[li, :, sa], ql_scr[li, :, sb], kt_scr[p, 1 + li])
            sc_a, sc_b = jnp.where(level_masks[li], la, sc_a), jnp.where(level_masks[li], lb_, sc_b)
        iv_ab = p_ref[:, 2 * fdim + ha * dv:2 * fdim + (hb + 1) * dv]
        intra_a, intra_b = paired(sc_a.astype(BF16), sc_b.astype(BF16), iv_ab)
        inter_a, inter_b = paired(qe_scr[:, sa], qe_scr[:, sb], sb_scr[p])
        ivt_scr[p] = iv_ab.T
        upd = jnp.dot(ivt_scr[p], ke_scr[:, sab], preferred_element_type=F32)
        st_scr[ha] = st_a * dec_scr[:, sa] + upd[:dv, :dk]
        st_scr[hb] = st_b * dec_scr[:, sb] + upd[dv:, dk:]
        for h, oc in ((ha, intra_a + inter_a), (hb, intra_b + inter_b)):
            gate = p_ref[:, 2 * fdim + heads * dv + h * dv:2 * fdim + heads * dv + (h + 1) * dv].astype(F32)
            on = (oc * lax.rsqrt(jnp.mean(oc * oc, axis=1, keepdims=True) + eps_unscaled)
                  * gn_ref[:, h * dv:(h + 1) * dv])
            y_ref[:, h * dv:(h + 1) * dv] = (on * _silu(gate)).astype(y_ref.dtype)

    @pl.when(pl.program_id(1) == last_step)
    def _():
        for h in range(heads):
            s_ref[0, h] = st_scr[h].T


def hgrn_long_mixer(proj, lb, gn_g, s0, *, row0, batch, seq, rows):
    heads, dk, dv = s0.shape[-3], s0.shape[-2], s0.shape[-1]
    R = rows
    assert seq % R == 0 and R % (2 * H_SUB) == 0 and row0 % R == 0
    levels = []
    G = 2 * H_SUB
    while G <= R:
        levels.append(G)
        G *= 2
    assert levels[-1] == R
    nsteps = seq // R
    blk0 = row0 // R
    fdim = heads * dk
    wide = lambda: pltpu.VMEM((R, fdim), BF16)
    return pl.pallas_call(
        functools.partial(_hgrn_long_kernel, R, heads, dk, dv, tuple(levels)),
        grid=(batch, nsteps),
        in_specs=[
            pl.BlockSpec((R, proj.shape[1]), lambda i, c: (blk0 + i * nsteps + c, 0)),
            pl.BlockSpec((1, fdim), lambda i, c: (0, 0)),
            pl.BlockSpec((1, heads * dv), lambda i, c: (0, 0)),
            pl.BlockSpec((1, heads, dk, dv), lambda i, c: (i, 0, 0, 0)),
        ],
        out_specs=[
            pl.BlockSpec((R, heads * dv), lambda i, c: (i * nsteps + c, 0)),
            pl.BlockSpec((1, heads, dk, dv), lambda i, c: (i, 0, 0, 0)),
        ],
        out_shape=[
            jax.ShapeDtypeStruct((batch * seq, heads * dv), BF16),
            jax.ShapeDtypeStruct(s0.shape, F32),
        ],
        scratch_shapes=[
            pltpu.VMEM((heads, dv, dk), F32),
            pltpu.VMEM((1, fdim), F32),
            wide(), wide(), wide(), wide(),
            pltpu.VMEM((len(levels), R, fdim), BF16),
            pltpu.VMEM((len(levels), R, fdim), BF16),
            pltpu.VMEM((heads, R, dk), F32),
            pltpu.VMEM((heads, R, dk), F32),
            pltpu.VMEM((heads // 2, 1 + len(levels), dk, 2 * R), BF16),
            pltpu.VMEM((heads // 2, dk, 2 * dv), BF16),
            pltpu.VMEM((heads // 2, 2 * dv, R), BF16),
        ],
        compiler_params=_params("arbitrary", "arbitrary"),
        name="hgrn_long_mixer",
    )(proj, lb.reshape(1, fdim), gn_g.reshape(1, heads * dv), s0)


def kernel(x_prompt, x_sample, state_mlstm_C, state_mlstm_n, state_mlstm_m, state_hgrn_S,
           norm_mix_g, norm_ffn_g, norm_final_g, mlstm_w_in, mlstm_b_gates, mlstm_head_norm_g,
           mlstm_w_out, hgrn_w_in, hgrn_lower_bounds, hgrn_g_norm_g, hgrn_w_out, ffn_w_up, ffn_w_down):
    bp, tp, d = x_prompt.shape
    bs, ts, _ = x_sample.shape
    np_, ns_ = bp * tp, bs * ts
    depth = norm_mix_g.shape[0]
    H = M_HEADS
    dqk, dv = state_mlstm_C.shape[-2], state_mlstm_C.shape[-1]
    hh, hdk, hdv = state_hgrn_S.shape[-3], state_hgrn_S.shape[-2], state_hgrn_S.shape[-1]

    xs = (x_prompt.reshape(np_, d), x_sample.reshape(ns_, d))

    lb_all = jnp.cumsum(jax.nn.softmax(hgrn_lower_bounds.astype(F32), axis=0), axis=0)
    lb_all = lb_all - lb_all[0]

    tm = 1024
    Cp, Np, Mp, Sp, Cs, Ns, Ms, Ss = [], [], [], [], [], [], [], []
    for i in range(depth):
        j = i // 2
        if i % 2 == 0:
            n_main = 2 * H * dqk + 2 * H * dv
            w_in_t = jnp.swapaxes(mlstm_w_in, 1, 2)
            w_gates = jnp.pad(w_in_t[j, n_main:], ((0, 128 - 2 * H), (0, 0)))
            proj, gate_pre = norm_matmul(xs, norm_mix_g[i], w_in_t, j, n_main, tm=tm, tn=1024,
                                         out_dtype=BF16, w_extra=w_gates, w_rows_are_outputs=True)
            args = (proj, gate_pre, mlstm_b_gates[j], mlstm_head_norm_g[j])
            yp, c_p, n_p, m_p = mlstm_long_mixer(
                *args, jnp.zeros((bp, H, dqk, dv), F32), jnp.zeros((bp, H, dqk), F32), jnp.zeros((bp, H), F32),
                row0=0, batch=bp, seq=tp, rows=256)
            ysm, c_s, n_s, m_s = mlstm_short_mixer(
                *args, state_mlstm_C[j], state_mlstm_n[j], state_mlstm_m[j],
                row0=np_, batch=bs, seq=ts, nb=8)
            Cp.append(c_p); Np.append(n_p); Mp.append(m_p)
            Cs.append(c_s); Ns.append(n_s); Ms.append(m_s)
            w_out = mlstm_w_out
        else:
            proj = norm_matmul(xs, norm_mix_g[i], hgrn_w_in, j, hgrn_w_in.shape[-1], tm=tm, tn=1024, out_dtype=BF16)
            args = (proj, lb_all[i], hgrn_g_norm_g[j])
            yp, s_p = hgrn_long_mixer(*args, jnp.zeros((bp, hh, hdk, hdv), F32),
                                      row0=0, batch=bp, seq=tp, rows=128)
            ysm, s_s = hgrn_short_mixer(*args, state_hgrn_S[j], row0=np_, batch=bs, seq=ts, nb=8)
            Sp.append(s_p); Ss.append(s_s)
            w_out = hgrn_w_out
        x = matmul_residual((yp, ysm), w_out, j, xs, tm=256, tn=d)
        act = norm_swiglu(x, norm_ffn_g[i], ffn_w_up, i, tm=tm, tn=512)
        x = matmul_residual((act,), ffn_w_down, i, (x,), tm=512, tn=512)
        xs = (x,)

    y_prompt = final_norm(x, norm_final_g, row0=0, nrows=np_, tm=512).reshape(bp, tp, d)
    y_sample = final_norm(x, norm_final_g, row0=np_, nrows=ns_, tm=512).reshape(bs, ts, d)
    cat = lambda parts: jnp.stack(parts) if len(parts) > 1 else parts[0][None]
    return (y_prompt, y_sample, cat(Cp), cat(Np), cat(Mp), cat(Sp), cat(Cs), cat(Ns), cat(Ms), cat(Ss))
```

```python
import functools

import jax
import jax.numpy as jnp
from jax import lax
from jax.experimental import pallas as pl
from jax.experimental.pallas import tpu as pltpu

F32 = jnp.float32
BF16 = jnp.bfloat16

EPS = 1e-6
GATE_CAP = 15.0
M_HEADS = 8
M_CHUNK = 64
M_ROW_BLOCK = 256
H_DK = 128
H_CHUNK = 32
H_SUB = 16

VMEM_LIMIT_BYTES = 56 * 1024 * 1024


def _params(*sem):
    return pltpu.CompilerParams(dimension_semantics=sem, vmem_limit_bytes=VMEM_LIMIT_BYTES)


def _row_tiles(xs, tm):
    for x in xs:
        assert x.shape[0] % tm == 0, (x.shape, tm)
    return tuple(x.shape[0] // tm for x in xs)


def _clamped_row_tile(first, count, row_of, col_of, *ids):
    return (jnp.clip(row_of(*ids) - first, 0, count - 1), col_of(*ids))


def _row_specs(tiles, block, row_of, col_of, single_tile_unbuffered=False):
    specs, first = [], 0
    for count in tiles:
        mode = dict(pipeline_mode=pl.Buffered(1)) if single_tile_unbuffered and count == 1 else {}
        specs.append(pl.BlockSpec(block, functools.partial(_clamped_row_tile, first, count, row_of, col_of), **mode))
        first += count
    return specs


def _on_owner(refs, tiles, i, fn):
    if len(refs) == 1:
        fn(refs[0])
        return
    first = 0
    for ref, count in zip(refs, tiles):
        pl.when((i >= first) & (i < first + count))(functools.partial(fn, ref))
        first += count


def _rms_normed(x, g):
    ms = jnp.mean(x * x, axis=-1, keepdims=True)
    return x * lax.rsqrt(ms + EPS) * g


def _norm_matmul_kernel(tiles, with_extra, w_rows_are_outputs, *refs):
    nx = len(tiles)
    x_refs, g_ref, w_ref = refs[:nx], refs[nx], refs[nx + 1]
    if with_extra:
        we_ref, o_ref, oe_ref, h_scr = refs[nx + 2:]
    else:
        o_ref, h_scr = refs[nx + 2:]
    matmul = _dot_nt if w_rows_are_outputs else functools.partial(jnp.dot, preferred_element_type=F32)

    @pl.when(pl.program_id(1) == 0)
    def _():
        def build(x_ref):
            h_scr[...] = _rms_normed(x_ref[...], g_ref[...]).astype(BF16)
            if with_extra:
                oe_ref[...] = matmul(h_scr[...], we_ref[...].astype(BF16))
        _on_owner(x_refs, tiles, pl.program_id(0), build)

    o_ref[...] = matmul(h_scr[...], w_ref[...].astype(BF16)).astype(o_ref.dtype)


def norm_matmul(xs, g, w, layer, n_cols, *, tm, tn, out_dtype, w_extra=None, w_rows_are_outputs=False):
    d = xs[0].shape[1]
    tiles = _row_tiles(xs, tm)
    m = tm * sum(tiles)
    row_of, col0 = (lambda i, j: i), (lambda i, j: 0)
    w_spec = (pl.BlockSpec((None, tn, d), lambda i, j: (layer, j, 0)) if w_rows_are_outputs
              else pl.BlockSpec((None, d, tn), lambda i, j: (layer, 0, j)))
    in_specs = (_row_specs(tiles, (tm, d), row_of, col0, single_tile_unbuffered=len(tiles) > 1)
                + [pl.BlockSpec((1, d), lambda i, j: (0, 0)), w_spec])
    out_specs = [pl.BlockSpec((tm, tn), lambda i, j: (i, j))]
    out_shape = [jax.ShapeDtypeStruct((m, n_cols), out_dtype)]
    args = list(xs) + [g.reshape(1, d), w]
    if w_extra is not None:
        ne = w_extra.shape[0 if w_rows_are_outputs else 1]
        in_specs.append(pl.BlockSpec(w_extra.shape, lambda i, j: (0, 0)))
        out_specs.append(pl.BlockSpec((tm, ne), lambda i, j: (i, 0)))
        out_shape.append(jax.ShapeDtypeStruct((m, ne), F32))
        args.append(w_extra)
    outs = pl.pallas_call(
        functools.partial(_norm_matmul_kernel, tiles, w_extra is not None, w_rows_are_outputs),
        grid=(m // tm, n_cols // tn),
        in_specs=in_specs,
        out_specs=out_specs,
        out_shape=out_shape,
        scratch_shapes=[pltpu.VMEM((tm, d), BF16)],
        compiler_params=_params("arbitrary", "arbitrary"),
        name="norm_matmul",
    )(*args)
    return outs if w_extra is not None else outs[0]


def _norm_swiglu_kernel(x_ref, g_ref, wa_ref, wu_ref, o_ref, h_scr):
    @pl.when(pl.program_id(1) == 0)
    def _():
        h_scr[...] = _rms_normed(x_ref[...], g_ref[...]).astype(BF16)

    h = h_scr[...]
    a = jnp.dot(h, wa_ref[...].astype(BF16), preferred_element_type=F32)
    u = jnp.dot(h, wu_ref[...].astype(BF16), preferred_element_type=F32)
    o_ref[...] = (a * jax.nn.sigmoid(a) * u).astype(o_ref.dtype)


def norm_swiglu(x, g, w_up, layer, *, tm, tn):
    m, d = x.shape
    ff = w_up.shape[-1] // 2
    nj = ff // tn
    return pl.pallas_call(
        _norm_swiglu_kernel,
        grid=(m // tm, nj),
        in_specs=[
            pl.BlockSpec((tm, d), lambda i, j: (i, 0)),
            pl.BlockSpec((1, d), lambda i, j: (0, 0)),
            pl.BlockSpec((None, d, tn), lambda i, j: (layer, 0, j)),
            pl.BlockSpec((None, d, tn), lambda i, j: (layer, 0, j + nj)),
        ],
        out_specs=pl.BlockSpec((tm, tn), lambda i, j: (i, j)),
        out_shape=jax.ShapeDtypeStruct((m, ff), BF16),
        scratch_shapes=[pltpu.VMEM((tm, d), BF16)],
        compiler_params=_params("arbitrary", "arbitrary"),
        name="norm_swiglu",
    )(x, g.reshape(1, d), w_up, w_up)


def _matmul_residual_kernel(y_tiles, r_tiles, *refs):
    ny, nr = len(y_tiles), len(r_tiles)
    y_refs, w_ref, r_refs = refs[:ny], refs[ny], refs[ny + 1:ny + 1 + nr]
    o_ref, w_scr = refs[ny + 1 + nr:]
    i = pl.program_id(1)

    @pl.when(i == 0)
    def _():
        w_scr[...] = w_ref[...].astype(BF16)

    def product(y_ref):
        o_ref[...] = jnp.dot(y_ref[...], w_scr[...], preferred_element_type=F32)

    def add_residual(r_ref):
        o_ref[...] = o_ref[...] + r_ref[...]

    _on_owner(y_refs, y_tiles, i, product)
    _on_owner(r_refs, r_tiles, i, add_residual)


def matmul_residual(ys, w, layer, rs, *, tm, tn):
    k, n = w.shape[-2], w.shape[-1]
    y_tiles, r_tiles = _row_tiles(ys, tm), _row_tiles(rs, tm)
    assert sum(y_tiles) == sum(r_tiles)
    m = tm * sum(y_tiles)
    row_of = lambda j, i: i
    w_mode = dict(pipeline_mode=pl.Buffered(1)) if tn == n else {}
    in_specs = (_row_specs(y_tiles, (tm, k), row_of, lambda j, i: 0)
                + [pl.BlockSpec((None, k, tn), lambda j, i: (layer, 0, j), **w_mode)]
                + _row_specs(r_tiles, (tm, tn), row_of, lambda j, i: j))
    return pl.pallas_call(
        functools.partial(_matmul_residual_kernel, y_tiles, r_tiles),
        grid=(n // tn, m // tm),
        in_specs=in_specs,
        out_specs=pl.BlockSpec((tm, tn), lambda j, i: (i, j)),
        out_shape=jax.ShapeDtypeStruct((m, n), F32),
        scratch_shapes=[pltpu.VMEM((k, tn), BF16)],
        compiler_params=_params("arbitrary", "arbitrary"),
        name="matmul_residual",
    )(*ys, w, *rs)


def _final_norm_kernel(x_ref, g_ref, o_ref):
    o_ref[...] = _rms_normed(x_ref[...], g_ref[...])


def final_norm(x, g, *, row0, nrows, tm):
    d = x.shape[1]
    assert row0 % tm == 0 and nrows % tm == 0
    return pl.pallas_call(
        _final_norm_kernel,
        grid=(nrows // tm,),
        in_specs=[pl.BlockSpec((tm, d), lambda i: (row0 // tm + i, 0)), pl.BlockSpec((1, d), lambda i: (0, 0))],
        out_specs=pl.BlockSpec((tm, d), lambda i: (i, 0)),
        out_shape=jax.ShapeDtypeStruct((nrows, d), F32),
        compiler_params=_params("arbitrary"),
        name="final_norm",
    )(x, g.reshape(1, d))


def _segment_masks(nseg, L):
    R = nseg * L
    r = lax.broadcasted_iota(jnp.int32, (R, R), 0)
    c = lax.broadcasted_iota(jnp.int32, (R, R), 1)
    same = (r // L) == (c // L)
    return r, c, same


def _dot_nt(a, b):
    return lax.dot_general(a, b, (((1,), (1,)), ((), ())), preferred_element_type=F32)


def _dot_tn(a, b):
    return lax.dot_general(a, b, (((0,), (0,)), ((), ())), preferred_element_type=F32)


def _log_sigmoid(x):
    return jnp.minimum(x, 0.0) - jnp.log1p(jnp.exp(-jnp.abs(x)))


def _silu(x):
    return 0.5 * x * (1.0 + jnp.tanh(0.5 * x))


def _cumsum_rows_exact(tril, x):
    hi = x.astype(BF16)
    r1 = x - hi.astype(F32)
    mid = r1.astype(BF16)
    lo = (r1 - mid.astype(F32)).astype(BF16)
    return (jnp.dot(tril, hi, preferred_element_type=F32) + jnp.dot(tril, mid, preferred_element_type=F32)
            + jnp.dot(tril, lo, preferred_element_type=F32))


def _mlstm_kernel(nseg, L, dqk, dv,
                  p_ref, gp_ref, bias_ref, hg_ref, c0_ref, n0_ref, m0_ref,
                  y_ref, c_ref, n_ref, m_ref):
    H = M_HEADS
    R = nseg * L
    s1 = H * dqk
    scale = dqk ** -0.5

    @pl.when(pl.program_id(1) == 0)
    def _():
        c_ref[...] = c0_ref[...]
        n_ref[...] = n0_ref[...]
        m_ref[...] = m0_ref[...]

    r, c, same = _segment_masks(nseg, L)
    causal = same & (c <= r)
    causal_t = same & (r <= c)
    eye = r == c
    row_seg = lax.broadcasted_iota(jnp.int32, (R, 1), 0) // L

    gates = gp_ref[:, :2 * H] + bias_ref[...]
    capped = GATE_CAP * jnp.tanh(gates / GATE_CAP)
    log_f = _log_sigmoid(capped)

    m_prev = m_ref[0, 0]
    m_new_all = m_prev
    seg_i = lax.broadcasted_iota(jnp.int32, (nseg, H), 0)
    head_i = lax.broadcasted_iota(jnp.int32, (nseg, H), 1)

    for h in range(H):
        q = p_ref[:, h * dqk:(h + 1) * dqk]
        k = p_ref[:, s1 + h * dqk:s1 + (h + 1) * dqk]
        v = p_ref[:, 2 * s1 + h * dv:2 * s1 + (h + 1) * dv]
        o = p_ref[:, 2 * s1 + H * dv + h * dv:2 * s1 + H * dv + (h + 1) * dv]
        qf = q.astype(F32)
        kf = k.astype(F32)
        vf = v.astype(F32)

        ig_col = capped[:, h:h + 1]
        lf_col = log_f[:, H + h:H + h + 1]
        lf_row = jnp.sum(jnp.where(eye, lf_col, 0.0), axis=0, keepdims=True)
        ig_row = jnp.sum(jnp.where(eye, ig_col, 0.0), axis=0, keepdims=True)
        b_col = jnp.sum(jnp.where(causal, lf_row, 0.0), axis=1, keepdims=True)
        b_row = jnp.sum(jnp.where(causal_t, lf_col, 0.0), axis=0, keepdims=True)
        b_last_col = jnp.sum(jnp.where(same, lf_row, 0.0), axis=1, keepdims=True)

        m_col = jnp.zeros((R, 1), F32)
        n_rows = jnp.zeros((R, dqk), F32)
        for s in range(nseg):
            m_s = m_prev[s:s + 1, h:h + 1]
            m_col = jnp.where(row_seg == s, m_s, m_col)
            n_rows = jnp.where(row_seg == s, n_ref[0, s, h:h + 1, :], n_rows)

        dlog = jnp.where(causal, b_col - b_row + ig_row, -jnp.inf)
        inter = b_col + m_col
        m_t = jnp.maximum(inter, jnp.max(dlog, axis=1, keepdims=True))
        w_intra = jnp.exp(dlog - m_t)
        w_inter = jnp.exp(inter - m_t)
        sc = _dot_nt(q, k) * scale * w_intra

        qc = jnp.zeros((R, dv), F32)
        for s in range(nseg):
            qc_s = jnp.dot(q, c_ref[0, s, h].astype(BF16), preferred_element_type=F32)
            qc = qc_s if nseg == 1 else jnp.where(row_seg == s, qc_s, qc)
        num = jnp.dot(sc.astype(BF16), v, preferred_element_type=F32) + w_inter * qc
        qn = jnp.sum(qf * n_rows, axis=1, keepdims=True)
        den = jnp.sum(sc, axis=1, keepdims=True) + w_inter * qn
        hc = num / jnp.maximum(jnp.abs(den), jnp.exp(-m_t))

        hn = hc * lax.rsqrt(jnp.mean(hc * hc, axis=1, keepdims=True) + EPS) * hg_ref[:, h * dv:(h + 1) * dv]
        y_ref[:, h * dv:(h + 1) * dv] = (hn * jax.nn.sigmoid(o.astype(F32))).astype(y_ref.dtype)

        g = b_last_col - b_col + ig_col
        m_new_col = jnp.zeros((R, 1), F32)
        decays = []
        for s in range(nseg):
            in_s = row_seg == s
            m_s = m_prev[s:s + 1, h:h + 1]
            b_last_s = jnp.sum(jnp.where(in_s, lf_col, 0.0), axis=0, keepdims=True)
            g_max_s = jnp.max(jnp.where(in_s, g, -jnp.inf), axis=0, keepdims=True)
            m_new_s = jnp.maximum(b_last_s + m_s, g_max_s)
            decays.append(jnp.exp(b_last_s + m_s - m_new_s))
            m_new_col = jnp.where(in_s, m_new_s, m_new_col)
            m_new_all = jnp.where((seg_i == s) & (head_i == h), m_new_s, m_new_all)
        w_k = jnp.exp(g - m_new_col) * scale
        wv = w_k * vf
        wk = w_k * kf
        for s in range(nseg):
            in_s = row_seg == s
            wv_s = wv if nseg == 1 else jnp.where(in_s, wv, 0.0)
            wk_s = wk if nseg == 1 else jnp.where(in_s, wk, 0.0)
            c_ref[0, s, h] = decays[s] * c_ref[0, s, h] + _dot_tn(k, wv_s.astype(BF16))
            n_ref[0, s, h:h + 1, :] = decays[s] * n_ref[0, s, h:h + 1, :] + jnp.sum(wk_s, axis=0, keepdims=True)

    m_ref[0, 0] = m_new_all


def mlstm_mixer(proj, gate_pre, bias, head_g, c0, n0, m0, *, row0, batch, seq, nseg, chunk):
    H = M_HEADS
    dqk, dv = c0.shape[-2], c0.shape[-1]
    R = nseg * chunk
    nchunks = seq // chunk
    assert nseg == 1 or nchunks == 1
    nblocks = batch // nseg
    blk0 = row0 // R
    rows = lambda i, c: (blk0 + i * nchunks + c, 0)
    width = proj.shape[1]
    c0 = c0.reshape(nblocks, nseg, H, dqk, dv)
    n0 = n0.reshape(nblocks, nseg, H, dqk)
    m0 = m0.reshape(nblocks, 1, nseg, H)
    y, c_out, n_out, m_out = pl.pallas_call(
        functools.partial(_mlstm_kernel, nseg, chunk, dqk, dv),
        grid=(nblocks, nchunks),
        in_specs=[
            pl.BlockSpec((R, width), rows),
            pl.BlockSpec((R, gate_pre.shape[1]), rows),
            pl.BlockSpec((1, 2 * H), lambda i, c: (0, 0)),
            pl.BlockSpec((1, H * dv), lambda i, c: (0, 0)),
            pl.BlockSpec((1, nseg, H, dqk, dv), lambda i, c: (i, 0, 0, 0, 0)),
            pl.BlockSpec((1, nseg, H, dqk), lambda i, c: (i, 0, 0, 0)),
            pl.BlockSpec((1, 1, nseg, H), lambda i, c: (i, 0, 0, 0)),
        ],
        out_specs=[
            pl.BlockSpec((R, H * dv), lambda i, c: (i * nchunks + c, 0)),
            pl.BlockSpec((1, nseg, H, dqk, dv), lambda i, c: (i, 0, 0, 0, 0)),
            pl.BlockSpec((1, nseg, H, dqk), lambda i, c: (i, 0, 0, 0)),
            pl.BlockSpec((1, 1, nseg, H), lambda i, c: (i, 0, 0, 0)),
        ],
        out_shape=[
            jax.ShapeDtypeStruct((batch * seq, H * dv), BF16),
            jax.ShapeDtypeStruct(c0.shape, F32),
            jax.ShapeDtypeStruct(n0.shape, F32),
            jax.ShapeDtypeStruct(m0.shape, F32),
        ],
        compiler_params=_params("arbitrary", "arbitrary"),
        name="mlstm_mixer",
    )(proj, gate_pre, bias.reshape(1, 2 * H), head_g.reshape(1, H * dv), c0, n0, m0)
    return (y, c_out.reshape(batch, H, dqk, dv), n_out.reshape(batch, H, dqk), m_out.reshape(batch, H))


def _mlstm_long_kernel(R, dqk, dv,
                       p_ref, gp_ref, bias_ref, hg_ref, c0_ref, n0_ref, m0_ref,
                       y_ref, c_ref, n_ref, m_ref, kt_scr):
    H = M_HEADS
    s1 = H * dqk
    scale = dqk ** -0.5

    @pl.when(pl.program_id(1) == 0)
    def _():
        c_ref[...] = c0_ref[...]
        n_ref[...] = n0_ref[...]
        m_ref[...] = m0_ref[...]

    r = lax.broadcasted_iota(jnp.int32, (R, R), 0)
    c = lax.broadcasted_iota(jnp.int32, (R, R), 1)
    causal = c <= r
    tril = causal.astype(BF16)

    capped = GATE_CAP * jnp.tanh((gp_ref[...] + bias_ref[...]) / GATE_CAP)
    cum = _cumsum_rows_exact(tril, _log_sigmoid(capped))
    ig_on_f = pltpu.roll(capped, H, axis=1)
    m_prev = m_ref[0]
    b_last = cum[R - 1:R, :]
    g_all = b_last - cum + ig_on_f
    m_new = jnp.maximum(b_last + m_prev, jnp.max(g_all, axis=0, keepdims=True))
    decay_all = jnp.exp(b_last + m_prev - m_new)
    wk_all = jnp.exp(g_all - m_new) * scale
    inter_all = cum + m_prev
    rows_t = (ig_on_f - cum).T

    RB = min(R, M_ROW_BLOCK)
    block_masks = [lax.broadcasted_iota(jnp.int32, (RB, (i + 1) * RB), 1)
                   <= lax.broadcasted_iota(jnp.int32, (RB, (i + 1) * RB), 0) + i * RB for i in range(R // RB)]

    for h in range(H):
        lane = H + h
        q_cols = slice(h * dqk, (h + 1) * dqk)
        k_cols = slice(s1 + h * dqk, s1 + (h + 1) * dqk)
        v_cols = slice(2 * s1 + h * dv, 2 * s1 + (h + 1) * dv)
        o_cols = slice(2 * s1 + H * dv + h * dv, 2 * s1 + H * dv + (h + 1) * dv)
        c_prev = c_ref[0, h]
        c_bf = c_prev.astype(BF16)
        n_prev = n_ref[0, h:h + 1, :]
        hg = hg_ref[:, h * dv:(h + 1) * dv]
        kt_scr[h] = p_ref[:, k_cols].T

        for i in range(R // RB):
            rows = slice(i * RB, (i + 1) * RB)
            ncols = (i + 1) * RB
            q = p_ref[rows, q_cols]
            gate_diff = cum[rows, lane:lane + 1] + rows_t[lane:lane + 1, :ncols]
            dlog = jnp.where(block_masks[i], gate_diff, -jnp.inf)
            inter = inter_all[rows, lane:lane + 1]
            m_t = jnp.maximum(inter, jnp.max(dlog, axis=1, keepdims=True))
            w_intra = jnp.exp(dlog - m_t)
            w_inter = jnp.exp(inter - m_t)
            sc = jnp.dot(q, kt_scr[h, :, :ncols], preferred_element_type=F32) * scale * w_intra
            num = (jnp.dot(sc.astype(BF16), p_ref[:ncols, v_cols], preferred_element_type=F32)
                   + w_inter * jnp.dot(q, c_bf, preferred_element_type=F32))
            qn = jnp.sum(q.astype(F32) * n_prev, axis=1, keepdims=True)
            den = jnp.sum(sc, axis=1, keepdims=True) + w_inter * qn
            inv = 1.0 / jnp.maximum(jnp.abs(den), jnp.exp(-m_t))
            norm = inv * lax.rsqrt(inv * inv * jnp.mean(num * num, axis=1, keepdims=True) + EPS)
            o = p_ref[rows, o_cols].astype(F32)
            y_ref[rows, h * dv:(h + 1) * dv] = (num * norm * hg * jax.nn.sigmoid(o)).astype(y_ref.dtype)

        k = p_ref[:, k_cols]
        w_k = wk_all[:, lane:lane + 1]
        decay = decay_all[:, lane:lane + 1]
        wv = (w_k * p_ref[:, v_cols].astype(F32)).astype(BF16)
        c_ref[0, h] = decay * c_prev + _dot_tn(k, wv)
        n_ref[0, h:h + 1, :] = decay * n_prev + jnp.sum(w_k * k.astype(F32), axis=0, keepdims=True)

    m_ref[0] = m_new


def mlstm_long_mixer(proj, gate_pre, bias, head_g, c0, n0, m0, *, row0, batch, seq, rows):
    H = M_HEADS
    dqk, dv = c0.shape[-2], c0.shape[-1]
    R = rows
    lanes = gate_pre.shape[1]
    assert seq % R == 0 and row0 % R == 0 and lanes >= 2 * H
    nsteps = seq // R
    blk0 = row0 // R
    rows_of = lambda i, c: (blk0 + i * nsteps + c, 0)
    bias_l = jnp.pad(bias.reshape(1, 2 * H), ((0, 0), (0, lanes - 2 * H)))
    m0_l = jnp.pad(m0, ((0, 0), (H, lanes - 2 * H))).reshape(batch, 1, lanes)
    y, c_out, n_out, m_out = pl.pallas_call(
        functools.partial(_mlstm_long_kernel, R, dqk, dv),
        grid=(batch, nsteps),
        in_specs=[
            pl.BlockSpec((R, proj.shape[1]), rows_of),
            pl.BlockSpec((R, lanes), rows_of),
            pl.BlockSpec((1, lanes), lambda i, c: (0, 0)),
            pl.BlockSpec((1, H * dv), lambda i, c: (0, 0)),
            pl.BlockSpec((1, H, dqk, dv), lambda i, c: (i, 0, 0, 0)),
            pl.BlockSpec((1, H, dqk), lambda i, c: (i, 0, 0)),
            pl.BlockSpec((1, 1, lanes), lambda i, c: (i, 0, 0)),
        ],
        out_specs=[
            pl.BlockSpec((R, H * dv), lambda i, c: (i * nsteps + c, 0)),
            pl.BlockSpec((1, H, dqk, dv), lambda i, c: (i, 0, 0, 0)),
            pl.BlockSpec((1, H, dqk), lambda i, c: (i, 0, 0)),
            pl.BlockSpec((1, 1, lanes), lambda i, c: (i, 0, 0)),
        ],
        out_shape=[
            jax.ShapeDtypeStruct((batch * seq, H * dv), BF16),
            jax.ShapeDtypeStruct(c0.shape, F32),
            jax.ShapeDtypeStruct(n0.shape, F32),
            jax.ShapeDtypeStruct(m0_l.shape, F32),
        ],
        scratch_shapes=[pltpu.VMEM((H, dqk, R), BF16)],
        compiler_params=_params("arbitrary", "arbitrary"),
        name="mlstm_long_mixer",
    )(proj, gate_pre, bias_l, head_g.reshape(1, H * dv), c0, n0, m0_l)
    return y, c_out, n_out, m_out[:, 0, H:2 * H]


def _mlstm_short_kernel(nb, T, dqk, dv,
                        p_ref, gp_ref, bias_ref, hg_ref, c0_ref, n0_ref, m0_ref,
                        y_ref, c_ref, n_ref, m_ref):
    H = M_HEADS
    R = nb * T
    PAIR = 2 * T
    s1 = H * dqk
    scale = dqk ** -0.5
    assert T % 8 == 0 and nb % 2 == 0

    r = lax.broadcasted_iota(jnp.int32, (R, R), 0)
    c = lax.broadcasted_iota(jnp.int32, (R, R), 1)
    causal = ((r // T) == (c // T)) & (c <= r)
    tril = causal.astype(BF16)
    seq_cols = (lax.broadcasted_iota(jnp.int32, (nb, 1, R), 2) // T
                == lax.broadcasted_iota(jnp.int32, (nb, 1, R), 0)).astype(BF16)
    first_of_pair = lax.broadcasted_iota(jnp.int32, (PAIR, 1), 0) < T

    def per_seq(x):
        return x.reshape(nb, T, x.shape[-1])

    def seq_rows(x3):
        return jnp.broadcast_to(x3, (nb, T, x3.shape[-1])).reshape(R, x3.shape[-1])

    capped = GATE_CAP * jnp.tanh((gp_ref[...] + bias_ref[...]) / GATE_CAP)
    cum = _cumsum_rows_exact(tril, _log_sigmoid(capped))
    ig_on_f = pltpu.roll(capped, H, axis=1)
    m_prev = m0_ref[...]
    b_last = seq_rows(per_seq(cum)[:, T - 1:T, :])
    g_all = b_last - cum + ig_on_f
    m_new = jnp.maximum(b_last + m_prev, seq_rows(jnp.max(per_seq(g_all), axis=1, keepdims=True)))
    decay_all = jnp.exp(b_last + m_prev - m_new)
    wk_all = jnp.exp(g_all - m_new) * scale
    inter_all = cum + m_prev
    rows_t = (ig_on_f - cum).T

    for h in range(H):
        lane = H + h
        q = p_ref[:, h * dqk:(h + 1) * dqk]
        k = p_ref[:, s1 + h * dqk:s1 + (h + 1) * dqk]
        v = p_ref[:, 2 * s1 + h * dv:2 * s1 + (h + 1) * dv]
        o = p_ref[:, 2 * s1 + H * dv + h * dv:2 * s1 + H * dv + (h + 1) * dv]
        kf = k.astype(F32)

        dlog = jnp.where(causal, cum[:, lane:lane + 1] + rows_t[lane:lane + 1, :], -jnp.inf)
        inter = inter_all[:, lane:lane + 1]
        m_t = jnp.maximum(inter, jnp.max(dlog, axis=1, keepdims=True))
        w_intra = jnp.exp(dlog - m_t)
        w_inter = jnp.exp(inter - m_t)
        sc = _dot_nt(q, k) * scale * w_intra
        num_intra = jnp.dot(sc.astype(BF16), v, preferred_element_type=F32)
        n_prev = n0_ref[h]
        qn = jnp.sum(q.astype(F32) * n_prev, axis=1, keepdims=True)
        den = jnp.sum(sc, axis=1, keepdims=True) + w_inter * qn
        inv = 1.0 / jnp.maximum(jnp.abs(den), jnp.exp(-m_t))
        hg = hg_ref[:, h * dv:(h + 1) * dv]

        w_k = wk_all[:, lane:lane + 1]
        k_t = kf.T.astype(BF16)
        stacked = (k_t[None] * seq_cols).reshape(nb * dqk, R)
        upd = jnp.dot(stacked, (w_k * v.astype(F32)).astype(BF16), preferred_element_type=F32)
        n_ref[h] = (decay_all[:, lane:lane + 1] * n_prev
                    + seq_rows(jnp.sum(per_seq(w_k * kf), axis=1, keepdims=True)))

        for pr in range(nb // 2):
            rows = slice(pr * PAIR, (pr + 1) * PAIR)
            lhs = p_ref[rows, h * dqk:(h + 1) * dqk]
            qc = []
            for b in (2 * pr, 2 * pr + 1):
                c_prev = c0_ref[0, b, h]
                qc.append(jnp.dot(lhs, c_prev.astype(BF16), preferred_element_type=F32))
                c_ref[0, b, h] = decay_all[b * T:b * T + 1, lane:lane + 1] * c_prev + upd[b * dqk:(b + 1) * dqk, :]
            num = num_intra[rows, :] + w_inter[rows, :] * jnp.where(first_of_pair, qc[0], qc[1])
            hc = num * inv[rows, :]
            hn = hc * lax.rsqrt(jnp.mean(hc * hc, axis=1, keepdims=True) + EPS) * hg
            y_ref[rows, h * dv:(h + 1) * dv] = (hn * jax.nn.sigmoid(o[rows, :].astype(F32))).astype(y_ref.dtype)

    m_ref[...] = m_new


def mlstm_short_mixer(proj, gate_pre, bias, head_g, c0, n0, m0, *, row0, batch, seq, nb):
    H = M_HEADS
    dqk, dv = c0.shape[-2], c0.shape[-1]
    R = nb * seq
    lanes = gate_pre.shape[1]
    assert batch % nb == 0 and row0 % R == 0 and lanes >= 2 * H
    nblocks = batch // nb
    blk0 = row0 // R
    bias_l = jnp.pad(bias.reshape(1, 2 * H), ((0, 0), (0, lanes - 2 * H)))
    m0_l = jnp.repeat(jnp.pad(m0, ((0, 0), (H, lanes - 2 * H))), seq, axis=0)
    n0_t = jnp.repeat(jnp.swapaxes(n0, 0, 1), seq, axis=1)
    y, c_out, n_out, m_out = pl.pallas_call(
        functools.partial(_mlstm_short_kernel, nb, seq, dqk, dv),
        grid=(nblocks,),
        in_specs=[
            pl.BlockSpec((R, proj.shape[1]), lambda i: (blk0 + i, 0)),
            pl.BlockSpec((R, lanes), lambda i: (blk0 + i, 0)),
            pl.BlockSpec((1, lanes), lambda i: (0, 0)),
            pl.BlockSpec((1, H * dv), lambda i: (0, 0)),
            pl.BlockSpec((1, nb, H, dqk, dv), lambda i: (i, 0, 0, 0, 0)),
            pl.BlockSpec((H, R, dqk), lambda i: (0, i, 0)),
            pl.BlockSpec((R, lanes), lambda i: (i, 0)),
        ],
        out_specs=[
            pl.BlockSpec((R, H * dv), lambda i: (i, 0)),
            pl.BlockSpec((1, nb, H, dqk, dv), lambda i: (i, 0, 0, 0, 0)),
            pl.BlockSpec((H, R, dqk), lambda i: (0, i, 0)),
            pl.BlockSpec((R, lanes), lambda i: (i, 0)),
        ],
        out_shape=[
            jax.ShapeDtypeStruct((batch * seq, H * dv), BF16),
            jax.ShapeDtypeStruct((nblocks, nb, H, dqk, dv), F32),
            jax.ShapeDtypeStruct(n0_t.shape, F32),
            jax.ShapeDtypeStruct(m0_l.shape, F32),
        ],
        compiler_params=_params("arbitrary"),
        name="mlstm_short_mixer",
    )(proj, gate_pre, bias_l, head_g.reshape(1, H * dv), c0.reshape(nblocks, nb, H, dqk, dv), n0_t, m0_l)
    return (y, c_out.reshape(batch, H, dqk, dv), jnp.swapaxes(n_out[:, ::seq, :], 0, 1), m_out[::seq, H:2 * H])


def _hgrn_kernel(nseg, L, heads, dk, dv,
                 p_ref, lb_ref, gn_ref, s0_ref, y_ref, s_ref):
    R = nseg * L
    fdim = heads * dk
    scale = dk ** -0.5
    sub = min(L, H_SUB)
    assert L % sub == 0 and L // sub in (1, 2)

    @pl.when(pl.program_id(1) == 0)
    def _():
        s_ref[...] = s0_ref[...]

    r, c, same = _segment_masks(nseg, L)
    causal = same & (c <= r)
    same_sub = (r // sub) == (c // sub)
    diag_mask = causal & same_sub
    tril = causal.astype(F32)

    row = lax.broadcasted_iota(jnp.int32, (R, 1), 0)
    row_seg = row // L
    pos = row % L
    mid_sel = (same_sub & ((c % sub) == (sub // 2))).astype(F32)
    last_sel = (same & ((c % L) == (L - 1))).astype(F32)
    if L // sub == 2:
        bnd_sel = (same & ((c % L) == (sub - 1))).astype(F32)
        upper = pos >= sub
    hp = lax.Precision.HIGHEST

    for h in range(heads):
        q = p_ref[:, h * dk:(h + 1) * dk].astype(F32) * scale
        fpre = p_ref[:, fdim + h * dk:fdim + (h + 1) * dk].astype(F32)
        iv = p_ref[:, 2 * fdim + h * dv:2 * fdim + (h + 1) * dv]
        gate = p_ref[:, 2 * fdim + heads * dv + h * dv:2 * fdim + heads * dv + (h + 1) * dv].astype(F32)
        lb = lb_ref[:, h * dk:(h + 1) * dk]
        f = lb + (1.0 - lb) * jax.nn.sigmoid(fpre)
        log_f = jnp.log(f)
        k = 1.0 - f

        a = jnp.dot(tril, log_f, preferred_element_type=F32, precision=hp)
        a_mid = jnp.dot(mid_sel, a, preferred_element_type=F32, precision=hp)
        a_last = jnp.dot(last_sel, a, preferred_element_type=F32, precision=hp)

        qd = (q * jnp.exp(a - a_mid)).astype(BF16)
        kd = (k * jnp.exp(a_mid - a)).astype(BF16)
        sc = jnp.where(diag_mask, _dot_nt(qd, kd), 0.0)
        if L // sub == 2:
            a_bnd = jnp.dot(bnd_sel, a, preferred_element_type=F32, precision=hp)
            qo = jnp.where(upper, q * jnp.exp(jnp.minimum(a - a_bnd, 0.0)), 0.0).astype(BF16)
            ko = jnp.where(upper, 0.0, k * jnp.exp(jnp.minimum(a_bnd - a, 0.0))).astype(BF16)
            sc = sc + jnp.where(same, _dot_nt(qo, ko), 0.0)

        qe = (q * jnp.exp(a)).astype(BF16)
        inter = jnp.zeros((R, dv), F32)
        for s in range(nseg):
            inter_s = jnp.dot(qe, s_ref[0, s, h].astype(BF16), preferred_element_type=F32)
            inter = inter_s if nseg == 1 else jnp.where(row_seg == s, inter_s, inter)
        oc = jnp.dot(sc.astype(BF16), iv, preferred_element_type=F32) + inter

        on = oc * lax.rsqrt(jnp.mean(oc * oc, axis=1, keepdims=True) + EPS) * gn_ref[:, h * dv:(h + 1) * dv]
        y_ref[:, h * dv:(h + 1) * dv] = (on * _silu(gate)).astype(y_ref.dtype)

        ke = k * jnp.exp(a_last - a)
        for s in range(nseg):
            in_s = row_seg == s
            ke_s = ke if nseg == 1 else jnp.where(in_s, ke, 0.0)
            a_last_s = a[s * L + L - 1:s * L + L, :]
            upd = _dot_tn(ke_s.astype(BF16), iv)
            decay_col = jnp.sum(jnp.where(lax.broadcasted_iota(jnp.int32, (dk, dk), 0)
                                          == lax.broadcasted_iota(jnp.int32, (dk, dk), 1),
                                          jnp.exp(a_last_s), 0.0), axis=1, keepdims=True)
            s_ref[0, s, h] = decay_col * s_ref[0, s, h] + upd


def hgrn_mixer(proj, lb, gn_g, s0, *, row0, batch, seq, nseg, chunk):
    heads, dk, dv = s0.shape[-3], s0.shape[-2], s0.shape[-1]
    R = nseg * chunk
    nchunks = seq // chunk
    assert nseg == 1 or nchunks == 1
    nblocks = batch // nseg
    blk0 = row0 // R
    rows = lambda i, c: (blk0 + i * nchunks + c, 0)
    s0 = s0.reshape(nblocks, nseg, heads, dk, dv)
    y, s_out = pl.pallas_call(
        functools.partial(_hgrn_kernel, nseg, chunk, heads, dk, dv),
        grid=(nblocks, nchunks),
        in_specs=[
            pl.BlockSpec((R, proj.shape[1]), rows),
            pl.BlockSpec((1, heads * dk), lambda i, c: (0, 0)),
            pl.BlockSpec((1, heads * dv), lambda i, c: (0, 0)),
            pl.BlockSpec((1, nseg, heads, dk, dv), lambda i, c: (i, 0, 0, 0, 0)),
        ],
        out_specs=[
            pl.BlockSpec((R, heads * dv), lambda i, c: (i * nchunks + c, 0)),
            pl.BlockSpec((1, nseg, heads, dk, dv), lambda i, c: (i, 0, 0, 0, 0)),
        ],
        out_shape=[
            jax.ShapeDtypeStruct((batch * seq, heads * dv), BF16),
            jax.ShapeDtypeStruct(s0.shape, F32),
        ],
        compiler_params=_params("arbitrary", "arbitrary"),
        name="hgrn_mixer",
    )(proj, lb.reshape(1, heads * dk), gn_g.reshape(1, heads * dv), s0)
    return y, s_out.reshape(batch, heads, dk, dv)


def _hgrn_short_kernel(nb, T, heads, dk, dv,
                       p_ref, lb_ref, gn_ref, s0_ref, y_ref, s_ref,
                       qd_scr, kd_scr, qe_scr, ke_scr, dec_scr):
    R = nb * T
    fdim = heads * dk
    scale = dk ** -0.5
    W = 2 * dk
    PAIR = 2 * T
    assert T % 8 == 0 and T <= H_SUB and nb % 2 == 0

    r = lax.broadcasted_iota(jnp.int32, (R, R), 0)
    c = lax.broadcasted_iota(jnp.int32, (R, R), 1)
    causal = ((r // T) == (c // T)) & (c <= r)
    tril = causal.astype(BF16)
    seq_cols = (lax.broadcasted_iota(jnp.int32, (nb, 1, R), 2) // T
                == lax.broadcasted_iota(jnp.int32, (nb, 1, R), 0)).astype(BF16)
    first_of_pair = lax.broadcasted_iota(jnp.int32, (PAIR, 1), 0) < T

    def minus_ref_row(a, row):
        a3 = a.reshape(nb, T, a.shape[-1])
        return (a3 - a3[:, row:row + 1, :]).reshape(a.shape)

    for g in range(fdim // W):
        sl = slice(g * W, (g + 1) * W)
        q = p_ref[:, sl].astype(F32) * scale
        fpre = p_ref[:, fdim + g * W:fdim + (g + 1) * W].astype(F32)
        lb = lb_ref[:, sl]
        f = lb + (1.0 - lb) * jax.nn.sigmoid(fpre)
        k = 1.0 - f
        a = _cumsum_rows_exact(tril, jnp.log(f))
        d = minus_ref_row(a, T // 2)
        to_last = -minus_ref_row(a, T - 1)
        qd_scr[:, sl] = (q * jnp.exp(d)).astype(BF16)
        kd_scr[:, sl] = (k * jnp.exp(-d)).astype(BF16)
        qe_scr[:, sl] = (q * jnp.exp(a)).astype(BF16)
        ke_scr[:, sl] = k * jnp.exp(to_last)
        dec_scr[:, sl] = jnp.exp(a + to_last)

    for h in range(heads):
        sl = slice(h * dk, (h + 1) * dk)
        iv = p_ref[:, 2 * fdim + h * dv:2 * fdim + (h + 1) * dv]
        sc = jnp.where(causal, _dot_nt(qd_scr[:, sl], kd_scr[:, sl]), 0.0)
        oc = jnp.dot(sc.astype(BF16), iv, preferred_element_type=F32)
        ke_t = ke_scr[:, sl].T.astype(BF16)
        dec_t = dec_scr[:, sl].T
        stacked = (ke_t[None] * seq_cols).reshape(nb * dk, R)
        upd = jnp.dot(stacked, iv, preferred_element_type=F32)
        gn = gn_ref[:, h * dv:(h + 1) * dv]
        for pr in range(nb // 2):
            rows = slice(pr * PAIR, (pr + 1) * PAIR)
            lhs = qe_scr[rows, sl]
            inter = []
            for b in (2 * pr, 2 * pr + 1):
                s_prev = s0_ref[0, b, h]
                inter.append(jnp.dot(lhs, s_prev.astype(BF16), preferred_element_type=F32))
                s_ref[0, b, h] = dec_t[:, b * T:b * T + 1] * s_prev + upd[b * dk:(b + 1) * dk, :]
            o2 = oc[rows, :] + jnp.where(first_of_pair, inter[0], inter[1])
            gate = p_ref[rows, 2 * fdim + heads * dv + h * dv:2 * fdim + heads * dv + (h + 1) * dv].astype(F32)
            on = o2 * lax.rsqrt(jnp.mean(o2 * o2, axis=1, keepdims=True) + EPS) * gn
            y_ref[rows, h * dv:(h + 1) * dv] = (on * _silu(gate)).astype(y_ref.dtype)


def hgrn_short_mixer(proj, lb, gn_g, s0, *, row0, batch, seq, nb):
    heads, dk, dv = s0.shape[-3], s0.shape[-2], s0.shape[-1]
    R = nb * seq
    assert batch % nb == 0 and row0 % R == 0
    nblocks = batch // nb
    blk0 = row0 // R
    fdim = heads * dk
    s0 = s0.reshape(nblocks, nb, heads, dk, dv)
    y, s_out = pl.pallas_call(
        functools.partial(_hgrn_short_kernel, nb, seq, heads, dk, dv),
        grid=(nblocks,),
        in_specs=[
            pl.BlockSpec((R, proj.shape[1]), lambda i: (blk0 + i, 0)),
            pl.BlockSpec((1, fdim), lambda i: (0, 0)),
            pl.BlockSpec((1, heads * dv), lambda i: (0, 0)),
            pl.BlockSpec((1, nb, heads, dk, dv), lambda i: (i, 0, 0, 0, 0)),
        ],
        out_specs=[
            pl.BlockSpec((R, heads * dv), lambda i: (i, 0)),
            pl.BlockSpec((1, nb, heads, dk, dv), lambda i: (i, 0, 0, 0, 0)),
        ],
        out_shape=[
            jax.ShapeDtypeStruct((batch * seq, heads * dv), BF16),
            jax.ShapeDtypeStruct(s0.shape, F32),
        ],
        scratch_shapes=[
            pltpu.VMEM((R, fdim), BF16), pltpu.VMEM((R, fdim), BF16), pltpu.VMEM((R, fdim), BF16),
            pltpu.VMEM((R, fdim), F32), pltpu.VMEM((R, fdim), F32),
        ],
        compiler_params=_params("arbitrary"),
        name="hgrn_short_mixer",
    )(proj, lb.reshape(1, fdim), gn_g.reshape(1, heads * dv), s0)
    return y, s_out.reshape(batch, heads, dk, dv)


def _hgrn_long_kernel(R, heads, dk, dv, levels,
                      p_ref, lb_ref, gn_ref, s0_ref, y_ref, s_ref,
                      st_scr, dec_scr, qd_scr, kd_scr, qe_scr, ke_scr, ql_scr, kl_scr, k_scr, a_scr,
                      kt_scr, sb_scr, ivt_scr, sc_scr, oc_scr):
    fdim = heads * dk
    eps_unscaled = EPS * dk
    W = 2 * dk
    last_step = pl.num_programs(1) - 1

    @pl.when(pl.program_id(1) == 0)
    def _():
        for h in range(heads):
            st_scr[h] = s0_ref[0, h].T

    r = lax.broadcasted_iota(jnp.int32, (R, R), 0)
    c = lax.broadcasted_iota(jnp.int32, (R, R), 1)
    tril = (c <= r).astype(BF16)
    diag_mask = ((r // H_SUB) == (c // H_SUB)) & (c <= r)
    level_masks = [((r // G) == (c // G)) & ((r % G) >= G // 2) & ((c % G) < G // 2) for G in levels]

    NSB = R // H_SUB
    zeros_sub = jnp.zeros((H_SUB, dk), BF16)

    for g in range(fdim // W):
        fpre = p_ref[:, fdim + g * W:fdim + (g + 1) * W].astype(F32)
        lb = lb_ref[:, g * W:(g + 1) * W]
        f = lb + (1.0 - lb) * jax.nn.sigmoid(fpre)
        k = 1.0 - f
        a = _cumsum_rows_exact(tril, jnp.log(f))
        for j in range(W // dk):
            k_scr[g * (W // dk) + j] = k[:, j * dk:(j + 1) * dk]
            a_scr[g * (W // dk) + j] = a[:, j * dk:(j + 1) * dk]

    def rows_of(ref, h, start, count, stride):
        if count == 1:
            return ref[h, start:start + 1, :]
        return ref.at[h][pl.ds(start, count, stride=stride), :]

    for h in range(heads):
        sl = slice(h * dk, (h + 1) * dk)
        a_last = a_scr[h, R - 1:R, :]
        dec_scr[:, sl] = jnp.exp(a_last)
        mids = rows_of(a_scr, h, H_SUB // 2, NSB, H_SUB)
        c_qe = jnp.exp(mids)
        c_ke = jnp.exp(a_last - mids)
        c_lv = []
        for G in levels:
            per = G // H_SUB
            bnd = rows_of(a_scr, h, G // 2 - 1, R // G, G)
            consts = []
            for o in range(per):
                mid_o = rows_of(a_scr, h, o * H_SUB + H_SUB // 2, R // G, G)
                consts.append(jnp.exp(mid_o - bnd) if o >= per // 2 else jnp.exp(bnd - mid_o))
            c_lv.append(consts)
        for s in range(NSB):
            rows = slice(s * H_SUB, (s + 1) * H_SUB)
            d = a_scr[h, rows, :] - mids[s:s + 1, :]
            qd = p_ref[rows, sl].astype(F32) * jnp.exp(d)
            kd = k_scr[h, rows, :] * jnp.exp(-d)
            qd_scr[rows, sl] = qd.astype(BF16)
            kd_scr[rows, sl] = kd.astype(BF16)
            qe_scr[rows, sl] = (qd * c_qe[s:s + 1, :]).astype(BF16)
            ke_scr[rows, sl] = (kd * c_ke[s:s + 1, :]).astype(BF16)
            for li, G in enumerate(levels):
                per = G // H_SUB
                o, gi = s % per, s // per
                const = c_lv[li][o][gi:gi + 1, :]
                if o >= per // 2:
                    ql_scr[li, rows, sl] = (qd * const).astype(BF16)
                    kl_scr[li, rows, sl] = zeros_sub
                else:
                    kl_scr[li, rows, sl] = (kd * const).astype(BF16)
                    ql_scr[li, rows, sl] = zeros_sub

    def paired(lhs_a, lhs_b, rhs_ab):
        out = jnp.dot(jnp.concatenate([lhs_a, lhs_b], axis=0), rhs_ab, preferred_element_type=F32)
        ra, ca = lhs_a.shape[0], rhs_ab.shape[1] // 2
        return out[:ra, :ca], out[ra:, ca:]

    assert dk == dv
    npairs = heads // 2
    lanes_a = [slice(2 * p * dk, (2 * p + 1) * dk) for p in range(npairs)]
    lanes_b = [slice((2 * p + 1) * dk, (2 * p + 2) * dk) for p in range(npairs)]
    lanes_ab = [slice(2 * p * dk, (2 * p + 2) * dk) for p in range(npairs)]
    iv_of = lambda p: p_ref[:, 2 * fdim + 2 * p * dv:2 * fdim + (2 * p + 2) * dv]

    for p in range(npairs):
        kt_scr[p, 0, :, :R] = kd_scr[:, lanes_a[p]].T
        kt_scr[p, 0, :, R:] = kd_scr[:, lanes_b[p]].T
        for li in range(len(levels)):
            kt_scr[p, 1 + li, :, :R] = kl_scr[li, :, lanes_a[p]].T
            kt_scr[p, 1 + li, :, R:] = kl_scr[li, :, lanes_b[p]].T
        sb_scr[p, :, :dv] = st_scr[2 * p].astype(BF16).T
        sb_scr[p, :, dv:] = st_scr[2 * p + 1].astype(BF16).T
        ivt_scr[p] = iv_of(p).T
    for p in range(npairs):
        da, db = paired(qd_scr[:, lanes_a[p]], qd_scr[:, lanes_b[p]], kt_scr[p, 0])
        sc_a, sc_b = jnp.where(diag_mask, da, 0.0), jnp.where(diag_mask, db, 0.0)
        for li in range(len(levels)):
            la, lb_ = paired(ql_scr[li, :, lanes_a[p]], ql_scr[li, :, lanes_b[p]], kt_scr[p, 1 + li])
            sc_a, sc_b = jnp.where(level_masks[li], la, sc_a), jnp.where(level_masks[li], lb_, sc_b)
        sc_scr[2 * p] = sc_a.astype(BF16)
        sc_scr[2 * p + 1] = sc_b.astype(BF16)
    for p in range(npairs):
        intra_a, intra_b = paired(sc_scr[2 * p], sc_scr[2 * p + 1], iv_of(p))
        inter_a, inter_b = paired(qe_scr[:, lanes_a[p]], qe_scr[:, lanes_b[p]], sb_scr[p])
        oc_scr[:, lanes_a[p]] = intra_a + inter_a
        oc_scr[:, lanes_b[p]] = intra_b + inter_b
        upd = jnp.dot(ivt_scr[p], ke_scr[:, lanes_ab[p]], preferred_element_type=F32)
        st_scr[2 * p] = st_scr[2 * p] * dec_scr[:, lanes_a[p]] + upd[:dv, :dk]
        st_scr[2 * p + 1] = st_scr[2 * p + 1] * dec_scr[:, lanes_b[p]] + upd[dv:, dk:]
    for h in range(heads):
        sl = slice(h * dv, (h + 1) * dv)
        oc = oc_scr[:, sl]
        gate = p_ref[:, 2 * fdim + heads * dv + h * dv:2 * fdim + heads * dv + (h + 1) * dv].astype(F32)
        on = oc * lax.rsqrt(jnp.mean(oc * oc, axis=1, keepdims=True) + eps_unscaled) * gn_ref[:, sl]
        y_ref[:, sl] = (on * _silu(gate)).astype(y_ref.dtype)

    @pl.when(pl.program_id(1) == last_step)
    def _():
        for h in range(heads):
            s_ref[0, h] = st_scr[h].T


def hgrn_long_mixer(proj, lb, gn_g, s0, *, row0, batch, seq, rows):
    heads, dk, dv = s0.shape[-3], s0.shape[-2], s0.shape[-1]
    R = rows
    assert seq % R == 0 and R % (2 * H_SUB) == 0 and row0 % R == 0
    levels = []
    G = 2 * H_SUB
    while G <= R:
        levels.append(G)
        G *= 2
    assert levels[-1] == R
    nsteps = seq // R
    blk0 = row0 // R
    fdim = heads * dk
    wide = lambda: pltpu.VMEM((R, fdim), BF16)
    return pl.pallas_call(
        functools.partial(_hgrn_long_kernel, R, heads, dk, dv, tuple(levels)),
        grid=(batch, nsteps),
        in_specs=[
            pl.BlockSpec((R, proj.shape[1]), lambda i, c: (blk0 + i * nsteps + c, 0)),
            pl.BlockSpec((1, fdim), lambda i, c: (0, 0)),
            pl.BlockSpec((1, heads * dv), lambda i, c: (0, 0)),
            pl.BlockSpec((1, heads, dk, dv), lambda i, c: (i, 0, 0, 0)),
        ],
        out_specs=[
            pl.BlockSpec((R, heads * dv), lambda i, c: (i * nsteps + c, 0)),
            pl.BlockSpec((1, heads, dk, dv), lambda i, c: (i, 0, 0, 0)),
        ],
        out_shape=[
            jax.ShapeDtypeStruct((batch * seq, heads * dv), BF16),
            jax.ShapeDtypeStruct(s0.shape, F32),
        ],
        scratch_shapes=[
            pltpu.VMEM((heads, dv, dk), F32),
            pltpu.VMEM((1, fdim), F32),
            wide(), wide(), wide(), wide(),
            pltpu.VMEM((len(levels), R, fdim), BF16),
            pltpu.VMEM((len(levels), R, fdim), BF16),
            pltpu.VMEM((heads, R, dk), F32),
            pltpu.VMEM((heads, R, dk), F32),
            pltpu.VMEM((heads // 2, 1 + len(levels), dk, 2 * R), BF16),
            pltpu.VMEM((heads // 2, dk, 2 * dv), BF16),
            pltpu.VMEM((heads // 2, 2 * dv, R), BF16),
            pltpu.VMEM((heads, R, R), BF16),
            pltpu.VMEM((R, heads * dv), F32),
        ],
        compiler_params=_params("arbitrary", "arbitrary"),
        name="hgrn_long_mixer",
    )(proj, lb.reshape(1, fdim), gn_g.reshape(1, heads * dv), s0)


def kernel(x_prompt, x_sample, state_mlstm_C, state_mlstm_n, state_mlstm_m, state_hgrn_S,
           norm_mix_g, norm_ffn_g, norm_final_g, mlstm_w_in, mlstm_b_gates, mlstm_head_norm_g,
           mlstm_w_out, hgrn_w_in, hgrn_lower_bounds, hgrn_g_norm_g, hgrn_w_out, ffn_w_up, ffn_w_down):
    bp, tp, d = x_prompt.shape
    bs, ts, _ = x_sample.shape
    np_, ns_ = bp * tp, bs * ts
    depth = norm_mix_g.shape[0]
    H = M_HEADS
    dqk, dv = state_mlstm_C.shape[-2], state_mlstm_C.shape[-1]
    hh, hdk, hdv = state_hgrn_S.shape[-3], state_hgrn_S.shape[-2], state_hgrn_S.shape[-1]

    xs = (x_prompt.reshape(np_, d), x_sample.reshape(ns_, d))

    lb_all = jnp.cumsum(jax.nn.softmax(hgrn_lower_bounds.astype(F32), axis=0), axis=0)
    lb_all = lb_all - lb_all[0]

    tm = 1024
    Cp, Np, Mp, Sp, Cs, Ns, Ms, Ss = [], [], [], [], [], [], [], []
    for i in range(depth):
        j = i // 2
        if i % 2 == 0:
            n_main = 2 * H * dqk + 2 * H * dv
            w_in_t = jnp.swapaxes(mlstm_w_in, 1, 2)
            w_gates = jnp.pad(w_in_t[j, n_main:], ((0, 128 - 2 * H), (0, 0)))
            proj, gate_pre = norm_matmul(xs, norm_mix_g[i], w_in_t, j, n_main, tm=tm, tn=1024,
                                         out_dtype=BF16, w_extra=w_gates, w_rows_are_outputs=True)
            args = (proj, gate_pre, mlstm_b_gates[j], mlstm_head_norm_g[j])
            yp, c_p, n_p, m_p = mlstm_long_mixer(
                *args, jnp.zeros((bp, H, dqk, dv), F32), jnp.zeros((bp, H, dqk), F32), jnp.zeros((bp, H), F32),
                row0=0, batch=bp, seq=tp, rows=256)
            ysm, c_s, n_s, m_s = mlstm_short_mixer(
                *args, state_mlstm_C[j], state_mlstm_n[j], state_mlstm_m[j],
                row0=np_, batch=bs, seq=ts, nb=8)
            Cp.append(c_p); Np.append(n_p); Mp.append(m_p)
            Cs.append(c_s); Ns.append(n_s); Ms.append(m_s)
            w_out = mlstm_w_out
        else:
            proj = norm_matmul(xs, norm_mix_g[i], hgrn_w_in, j, hgrn_w_in.shape[-1], tm=tm, tn=1024, out_dtype=BF16)
            args = (proj, lb_all[i], hgrn_g_norm_g[j])
            yp, s_p = hgrn_long_mixer(*args, jnp.zeros((bp, hh, hdk, hdv), F32),
                                      row0=0, batch=bp, seq=tp, rows=128)
            ysm, s_s = hgrn_short_mixer(*args, state_hgrn_S[j], row0=np_, batch=bs, seq=ts, nb=8)
            Sp.append(s_p); Ss.append(s_s)
            w_out = hgrn_w_out
        x = matmul_residual((yp, ysm), w_out, j, xs, tm=256, tn=d)
        act = norm_swiglu(x, norm_ffn_g[i], ffn_w_up, i, tm=tm, tn=512)
        x = matmul_residual((act,), ffn_w_down, i, (x,), tm=512, tn=512)
        xs = (x,)

    y_prompt = final_norm(x, norm_final_g, row0=0, nrows=np_, tm=512).reshape(bp, tp, d)
    y_sample = final_norm(x, norm_final_g, row0=np_, nrows=ns_, tm=512).reshape(bs, ts, d)
    cat = lambda parts: jnp.stack(parts) if len(parts) > 1 else parts[0][None]
    return (y_prompt, y_sample, cat(Cp), cat(Np), cat(Mp), cat(Sp), cat(Cs), cat(Ns), cat(Ms), cat(Ss))
```

```python
import functools

import jax
import jax.numpy as jnp
from jax import lax
from jax.experimental import pallas as pl
from jax.experimental.pallas import tpu as pltpu

F32 = jnp.float32
BF16 = jnp.bfloat16

EPS = 1e-6
GATE_CAP = 15.0
M_HEADS = 8
M_CHUNK = 64
H_DK = 128
H_CHUNK = 32
H_SUB = 16

VMEM_LIMIT_BYTES = 56 * 1024 * 1024


def _params(*sem):
    return pltpu.CompilerParams(dimension_semantics=sem, vmem_limit_bytes=VMEM_LIMIT_BYTES)


def _row_tiles(xs, tm):
    for x in xs:
        assert x.shape[0] % tm == 0, (x.shape, tm)
    return tuple(x.shape[0] // tm for x in xs)


def _clamped_row_tile(first, count, row_of, col_of, *ids):
    return (jnp.clip(row_of(*ids) - first, 0, count - 1), col_of(*ids))


def _row_specs(tiles, block, row_of, col_of, single_tile_unbuffered=False):
    specs, first = [], 0
    for count in tiles:
        mode = dict(pipeline_mode=pl.Buffered(1)) if single_tile_unbuffered and count == 1 else {}
        specs.append(pl.BlockSpec(block, functools.partial(_clamped_row_tile, first, count, row_of, col_of), **mode))
        first += count
    return specs


def _on_owner(refs, tiles, i, fn):
    if len(refs) == 1:
        fn(refs[0])
        return
    first = 0
    for ref, count in zip(refs, tiles):
        pl.when((i >= first) & (i < first + count))(functools.partial(fn, ref))
        first += count


def _rms_normed(x, g):
    ms = jnp.mean(x * x, axis=-1, keepdims=True)
    return x * lax.rsqrt(ms + EPS) * g


def _norm_matmul_kernel(tiles, with_extra, w_rows_are_outputs, *refs):
    nx = len(tiles)
    x_refs, g_ref, w_ref = refs[:nx], refs[nx], refs[nx + 1]
    if with_extra:
        we_ref, o_ref, oe_ref, h_scr = refs[nx + 2:]
    else:
        o_ref, h_scr = refs[nx + 2:]
    matmul = _dot_nt if w_rows_are_outputs else functools.partial(jnp.dot, preferred_element_type=F32)

    @pl.when(pl.program_id(1) == 0)
    def _():
        def build(x_ref):
            h_scr[...] = _rms_normed(x_ref[...], g_ref[...]).astype(BF16)
            if with_extra:
                oe_ref[...] = matmul(h_scr[...], we_ref[...].astype(BF16))
        _on_owner(x_refs, tiles, pl.program_id(0), build)

    o_ref[...] = matmul(h_scr[...], w_ref[...].astype(BF16)).astype(o_ref.dtype)


def norm_matmul(xs, g, w, layer, n_cols, *, tm, tn, out_dtype, w_extra=None, w_rows_are_outputs=False):
    d = xs[0].shape[1]
    tiles = _row_tiles(xs, tm)
    m = tm * sum(tiles)
    row_of, col0 = (lambda i, j: i), (lambda i, j: 0)
    w_spec = (pl.BlockSpec((None, tn, d), lambda i, j: (layer, j, 0)) if w_rows_are_outputs
              else pl.BlockSpec((None, d, tn), lambda i, j: (layer, 0, j)))
    in_specs = (_row_specs(tiles, (tm, d), row_of, col0, single_tile_unbuffered=len(tiles) > 1)
                + [pl.BlockSpec((1, d), lambda i, j: (0, 0)), w_spec])
    out_specs = [pl.BlockSpec((tm, tn), lambda i, j: (i, j))]
    out_shape = [jax.ShapeDtypeStruct((m, n_cols), out_dtype)]
    args = list(xs) + [g.reshape(1, d), w]
    if w_extra is not None:
        ne = w_extra.shape[0 if w_rows_are_outputs else 1]
        in_specs.append(pl.BlockSpec(w_extra.shape, lambda i, j: (0, 0)))
        out_specs.append(pl.BlockSpec((tm, ne), lambda i, j: (i, 0)))
        out_shape.append(jax.ShapeDtypeStruct((m, ne), F32))
        args.append(w_extra)
    outs = pl.pallas_call(
        functools.partial(_norm_matmul_kernel, tiles, w_extra is not None, w_rows_are_outputs),
        grid=(m // tm, n_cols // tn),
        in_specs=in_specs,
        out_specs=out_specs,
        out_shape=out_shape,
        scratch_shapes=[pltpu.VMEM((tm, d), BF16)],
        compiler_params=_params("arbitrary", "arbitrary"),
        name="norm_matmul",
    )(*args)
    return outs if w_extra is not None else outs[0]


def _norm_swiglu_kernel(x_ref, g_ref, wa_ref, wu_ref, o_ref, h_scr):
    @pl.when(pl.program_id(1) == 0)
    def _():
        h_scr[...] = _rms_normed(x_ref[...], g_ref[...]).astype(BF16)

    h = h_scr[...]
    a = jnp.dot(h, wa_ref[...].astype(BF16), preferred_element_type=F32)
    u = jnp.dot(h, wu_ref[...].astype(BF16), preferred_element_type=F32)
    o_ref[...] = (a * jax.nn.sigmoid(a) * u).astype(o_ref.dtype)


def norm_swiglu(x, g, w_up, layer, *, tm, tn):
    m, d = x.shape
    ff = w_up.shape[-1] // 2
    nj = ff // tn
    return pl.pallas_call(
        _norm_swiglu_kernel,
        grid=(m // tm, nj),
        in_specs=[
            pl.BlockSpec((tm, d), lambda i, j: (i, 0)),
            pl.BlockSpec((1, d), lambda i, j: (0, 0)),
            pl.BlockSpec((None, d, tn), lambda i, j: (layer, 0, j)),
            pl.BlockSpec((None, d, tn), lambda i, j: (layer, 0, j + nj)),
        ],
        out_specs=pl.BlockSpec((tm, tn), lambda i, j: (i, j)),
        out_shape=jax.ShapeDtypeStruct((m, ff), BF16),
        scratch_shapes=[pltpu.VMEM((tm, d), BF16)],
        compiler_params=_params("arbitrary", "arbitrary"),
        name="norm_swiglu",
    )(x, g.reshape(1, d), w_up, w_up)


def _matmul_residual_kernel(y_tiles, r_tiles, *refs):
    ny, nr = len(y_tiles), len(r_tiles)
    y_refs, w_ref, r_refs = refs[:ny], refs[ny], refs[ny + 1:ny + 1 + nr]
    o_ref, w_scr = refs[ny + 1 + nr:]
    i = pl.program_id(1)

    @pl.when(i == 0)
    def _():
        w_scr[...] = w_ref[...].astype(BF16)

    def product(y_ref):
        o_ref[...] = jnp.dot(y_ref[...], w_scr[...], preferred_element_type=F32)

    def add_residual(r_ref):
        o_ref[...] = o_ref[...] + r_ref[...]

    _on_owner(y_refs, y_tiles, i, product)
    _on_owner(r_refs, r_tiles, i, add_residual)


def matmul_residual(ys, w, layer, rs, *, tm, tn):
    k, n = w.shape[-2], w.shape[-1]
    y_tiles, r_tiles = _row_tiles(ys, tm), _row_tiles(rs, tm)
    assert sum(y_tiles) == sum(r_tiles)
    m = tm * sum(y_tiles)
    row_of = lambda j, i: i
    w_mode = dict(pipeline_mode=pl.Buffered(1)) if tn == n else {}
    in_specs = (_row_specs(y_tiles, (tm, k), row_of, lambda j, i: 0)
                + [pl.BlockSpec((None, k, tn), lambda j, i: (layer, 0, j), **w_mode)]
                + _row_specs(r_tiles, (tm, tn), row_of, lambda j, i: j))
    return pl.pallas_call(
        functools.partial(_matmul_residual_kernel, y_tiles, r_tiles),
        grid=(n // tn, m // tm),
        in_specs=in_specs,
        out_specs=pl.BlockSpec((tm, tn), lambda j, i: (i, j)),
        out_shape=jax.ShapeDtypeStruct((m, n), F32),
        scratch_shapes=[pltpu.VMEM((k, tn), BF16)],
        compiler_params=_params("arbitrary", "arbitrary"),
        name="matmul_residual",
    )(*ys, w, *rs)


def _final_norm_kernel(x_ref, g_ref, o_ref):
    o_ref[...] = _rms_normed(x_ref[...], g_ref[...])


def final_norm(x, g, *, row0, nrows, tm):
    d = x.shape[1]
    assert row0 % tm == 0 and nrows % tm == 0
    return pl.pallas_call(
        _final_norm_kernel,
        grid=(nrows // tm,),
        in_specs=[pl.BlockSpec((tm, d), lambda i: (row0 // tm + i, 0)), pl.BlockSpec((1, d), lambda i: (0, 0))],
        out_specs=pl.BlockSpec((tm, d), lambda i: (i, 0)),
        out_shape=jax.ShapeDtypeStruct((nrows, d), F32),
        compiler_params=_params("arbitrary"),
        name="final_norm",
    )(x, g.reshape(1, d))


def _segment_masks(nseg, L):
    R = nseg * L
    r = lax.broadcasted_iota(jnp.int32, (R, R), 0)
    c = lax.broadcasted_iota(jnp.int32, (R, R), 1)
    same = (r // L) == (c // L)
    return r, c, same


def _dot_nt(a, b):
    return lax.dot_general(a, b, (((1,), (1,)), ((), ())), preferred_element_type=F32)


def _dot_tn(a, b):
    return lax.dot_general(a, b, (((0,), (0,)), ((), ())), preferred_element_type=F32)


def _log_sigmoid(x):
    return jnp.minimum(x, 0.0) - jnp.log1p(jnp.exp(-jnp.abs(x)))


def _silu(x):
    return 0.5 * x * (1.0 + jnp.tanh(0.5 * x))


def _cumsum_rows_exact(tril, x):
    hi = x.astype(BF16)
    r1 = x - hi.astype(F32)
    mid = r1.astype(BF16)
    lo = (r1 - mid.astype(F32)).astype(BF16)
    return (jnp.dot(tril, hi, preferred_element_type=F32) + jnp.dot(tril, mid, preferred_element_type=F32)
            + jnp.dot(tril, lo, preferred_element_type=F32))


def _mlstm_kernel(nseg, L, dqk, dv,
                  p_ref, gp_ref, bias_ref, hg_ref, c0_ref, n0_ref, m0_ref,
                  y_ref, c_ref, n_ref, m_ref):
    H = M_HEADS
    R = nseg * L
    s1 = H * dqk
    scale = dqk ** -0.5

    @pl.when(pl.program_id(1) == 0)
    def _():
        c_ref[...] = c0_ref[...]
        n_ref[...] = n0_ref[...]
        m_ref[...] = m0_ref[...]

    r, c, same = _segment_masks(nseg, L)
    causal = same & (c <= r)
    causal_t = same & (r <= c)
    eye = r == c
    row_seg = lax.broadcasted_iota(jnp.int32, (R, 1), 0) // L

    gates = gp_ref[:, :2 * H] + bias_ref[...]
    capped = GATE_CAP * jnp.tanh(gates / GATE_CAP)
    log_f = _log_sigmoid(capped)

    m_prev = m_ref[0, 0]
    m_new_all = m_prev
    seg_i = lax.broadcasted_iota(jnp.int32, (nseg, H), 0)
    head_i = lax.broadcasted_iota(jnp.int32, (nseg, H), 1)

    for h in range(H):
        q = p_ref[:, h * dqk:(h + 1) * dqk]
        k = p_ref[:, s1 + h * dqk:s1 + (h + 1) * dqk]
        v = p_ref[:, 2 * s1 + h * dv:2 * s1 + (h + 1) * dv]
        o = p_ref[:, 2 * s1 + H * dv + h * dv:2 * s1 + H * dv + (h + 1) * dv]
        qf = q.astype(F32)
        kf = k.astype(F32)
        vf = v.astype(F32)

        ig_col = capped[:, h:h + 1]
        lf_col = log_f[:, H + h:H + h + 1]
        lf_row = jnp.sum(jnp.where(eye, lf_col, 0.0), axis=0, keepdims=True)
        ig_row = jnp.sum(jnp.where(eye, ig_col, 0.0), axis=0, keepdims=True)
        b_col = jnp.sum(jnp.where(causal, lf_row, 0.0), axis=1, keepdims=True)
        b_row = jnp.sum(jnp.where(causal_t, lf_col, 0.0), axis=0, keepdims=True)
        b_last_col = jnp.sum(jnp.where(same, lf_row, 0.0), axis=1, keepdims=True)

        m_col = jnp.zeros((R, 1), F32)
        n_rows = jnp.zeros((R, dqk), F32)
        for s in range(nseg):
            m_s = m_prev[s:s + 1, h:h + 1]
            m_col = jnp.where(row_seg == s, m_s, m_col)
            n_rows = jnp.where(row_seg == s, n_ref[0, s, h:h + 1, :], n_rows)

        dlog = jnp.where(causal, b_col - b_row + ig_row, -jnp.inf)
        inter = b_col + m_col
        m_t = jnp.maximum(inter, jnp.max(dlog, axis=1, keepdims=True))
        w_intra = jnp.exp(dlog - m_t)
        w_inter = jnp.exp(inter - m_t)
        sc = _dot_nt(q, k) * scale * w_intra

        qc = jnp.zeros((R, dv), F32)
        for s in range(nseg):
            qc_s = jnp.dot(q, c_ref[0, s, h].astype(BF16), preferred_element_type=F32)
            qc = qc_s if nseg == 1 else jnp.where(row_seg == s, qc_s, qc)
        num = jnp.dot(sc.astype(BF16), v, preferred_element_type=F32) + w_inter * qc
        qn = jnp.sum(qf * n_rows, axis=1, keepdims=True)
        den = jnp.sum(sc, axis=1, keepdims=True) + w_inter * qn
        hc = num / jnp.maximum(jnp.abs(den), jnp.exp(-m_t))

        hn = hc * lax.rsqrt(jnp.mean(hc * hc, axis=1, keepdims=True) + EPS) * hg_ref[:, h * dv:(h + 1) * dv]
        y_ref[:, h * dv:(h + 1) * dv] = (hn * jax.nn.sigmoid(o.astype(F32))).astype(y_ref.dtype)

        g = b_last_col - b_col + ig_col
        m_new_col = jnp.zeros((R, 1), F32)
        decays = []
        for s in range(nseg):
            in_s = row_seg == s
            m_s = m_prev[s:s + 1, h:h + 1]
            b_last_s = jnp.sum(jnp.where(in_s, lf_col, 0.0), axis=0, keepdims=True)
            g_max_s = jnp.max(jnp.where(in_s, g, -jnp.inf), axis=0, keepdims=True)
            m_new_s = jnp.maximum(b_last_s + m_s, g_max_s)
            decays.append(jnp.exp(b_last_s + m_s - m_new_s))
            m_new_col = jnp.where(in_s, m_new_s, m_new_col)
            m_new_all = jnp.where((seg_i == s) & (head_i == h), m_new_s, m_new_all)
        w_k = jnp.exp(g - m_new_col) * scale
        wv = w_k * vf
        wk = w_k * kf
        for s in range(nseg):
            in_s = row_seg == s
            wv_s = wv if nseg == 1 else jnp.where(in_s, wv, 0.0)
            wk_s = wk if nseg == 1 else jnp.where(in_s, wk, 0.0)
            c_ref[0, s, h] = decays[s] * c_ref[0, s, h] + _dot_tn(k, wv_s.astype(BF16))
            n_ref[0, s, h:h + 1, :] = decays[s] * n_ref[0, s, h:h + 1, :] + jnp.sum(wk_s, axis=0, keepdims=True)

    m_ref[0, 0] = m_new_all


def mlstm_mixer(proj, gate_pre, bias, head_g, c0, n0, m0, *, row0, batch, seq, nseg, chunk):
    H = M_HEADS
    dqk, dv = c0.shape[-2], c0.shape[-1]
    R = nseg * chunk
    nchunks = seq // chunk
    assert nseg == 1 or nchunks == 1
    nblocks = batch // nseg
    blk0 = row0 // R
    rows = lambda i, c: (blk0 + i * nchunks + c, 0)
    width = proj.shape[1]
    c0 = c0.reshape(nblocks, nseg, H, dqk, dv)
    n0 = n0.reshape(nblocks, nseg, H, dqk)
    m0 = m0.reshape(nblocks, 1, nseg, H)
    y, c_out, n_out, m_out = pl.pallas_call(
        functools.partial(_mlstm_kernel, nseg, chunk, dqk, dv),
        grid=(nblocks, nchunks),
        in_specs=[
            pl.BlockSpec((R, width), rows),
            pl.BlockSpec((R, gate_pre.shape[1]), rows),
            pl.BlockSpec((1, 2 * H), lambda i, c: (0, 0)),
            pl.BlockSpec((1, H * dv), lambda i, c: (0, 0)),
            pl.BlockSpec((1, nseg, H, dqk, dv), lambda i, c: (i, 0, 0, 0, 0)),
            pl.BlockSpec((1, nseg, H, dqk), lambda i, c: (i, 0, 0, 0)),
            pl.BlockSpec((1, 1, nseg, H), lambda i, c: (i, 0, 0, 0)),
        ],
        out_specs=[
            pl.BlockSpec((R, H * dv), lambda i, c: (i * nchunks + c, 0)),
            pl.BlockSpec((1, nseg, H, dqk, dv), lambda i, c: (i, 0, 0, 0, 0)),
            pl.BlockSpec((1, nseg, H, dqk), lambda i, c: (i, 0, 0, 0)),
            pl.BlockSpec((1, 1, nseg, H), lambda i, c: (i, 0, 0, 0)),
        ],
        out_shape=[
            jax.ShapeDtypeStruct((batch * seq, H * dv), BF16),
            jax.ShapeDtypeStruct(c0.shape, F32),
            jax.ShapeDtypeStruct(n0.shape, F32),
            jax.ShapeDtypeStruct(m0.shape, F32),
        ],
        compiler_params=_params("arbitrary", "arbitrary"),
        name="mlstm_mixer",
    )(proj, gate_pre, bias.reshape(1, 2 * H), head_g.reshape(1, H * dv), c0, n0, m0)
    return (y, c_out.reshape(batch, H, dqk, dv), n_out.reshape(batch, H, dqk), m_out.reshape(batch, H))


def _mlstm_long_kernel(R, dqk, dv,
                       p_ref, gp_ref, bias_ref, hg_ref, c0_ref, n0_ref, m0_ref,
                       y_ref, c_ref, n_ref, m_ref, kt_scr):
    H = M_HEADS
    s1 = H * dqk
    scale = dqk ** -0.5

    @pl.when(pl.program_id(1) == 0)
    def _():
        c_ref[...] = c0_ref[...]
        n_ref[...] = n0_ref[...]
        m_ref[...] = m0_ref[...]

    r = lax.broadcasted_iota(jnp.int32, (R, R), 0)
    c = lax.broadcasted_iota(jnp.int32, (R, R), 1)
    causal = c <= r
    tril = causal.astype(BF16)

    capped = GATE_CAP * jnp.tanh((gp_ref[...] + bias_ref[...]) / GATE_CAP)
    cum = _cumsum_rows_exact(tril, _log_sigmoid(capped))
    ig_on_f = pltpu.roll(capped, H, axis=1)
    m_prev = m_ref[0]
    b_last = cum[R - 1:R, :]
    g_all = b_last - cum + ig_on_f
    m_new = jnp.maximum(b_last + m_prev, jnp.max(g_all, axis=0, keepdims=True))
    decay_all = jnp.exp(b_last + m_prev - m_new)
    wk_all = jnp.exp(g_all - m_new) * scale
    inter_all = cum + m_prev
    rows_t = (ig_on_f - cum).T

    q_cols = [slice(h * dqk, (h + 1) * dqk) for h in range(H)]
    k_cols = [slice(s1 + h * dqk, s1 + (h + 1) * dqk) for h in range(H)]
    v_cols = [slice(2 * s1 + h * dv, 2 * s1 + (h + 1) * dv) for h in range(H)]
    o_cols = [slice(2 * s1 + H * dv + h * dv, 2 * s1 + H * dv + (h + 1) * dv) for h in range(H)]

    for h in range(H):
        lane = H + h
        q = p_ref[:, q_cols[h]]
        c_prev = c_ref[0, h]
        c_bf = c_prev.astype(BF16)
        n_prev = n_ref[0, h:h + 1, :]
        kt_scr[h] = p_ref[:, k_cols[h]].T

        dlog = jnp.where(causal, cum[:, lane:lane + 1] + rows_t[lane:lane + 1, :], -jnp.inf)
        inter = inter_all[:, lane:lane + 1]
        m_t = jnp.maximum(inter, jnp.max(dlog, axis=1, keepdims=True))
        w_inter = jnp.exp(inter - m_t)
        sc = jnp.dot(q, kt_scr[h], preferred_element_type=F32) * scale * jnp.exp(dlog - m_t)
        num = (jnp.dot(sc.astype(BF16), p_ref[:, v_cols[h]], preferred_element_type=F32)
               + w_inter * jnp.dot(q, c_bf, preferred_element_type=F32))
        qn = jnp.sum(q.astype(F32) * n_prev, axis=1, keepdims=True)
        den = jnp.sum(sc, axis=1, keepdims=True) + w_inter * qn
        inv = 1.0 / jnp.maximum(jnp.abs(den), jnp.exp(-m_t))
        norm = inv * lax.rsqrt(inv * inv * jnp.mean(num * num, axis=1, keepdims=True) + EPS)
        o = p_ref[:, o_cols[h]].astype(F32)
        y_ref[:, h * dv:(h + 1) * dv] = (num * norm * hg_ref[:, h * dv:(h + 1) * dv]
                                         * jax.nn.sigmoid(o)).astype(y_ref.dtype)

        k = p_ref[:, k_cols[h]]
        w_k = wk_all[:, lane:lane + 1]
        decay = decay_all[:, lane:lane + 1]
        wv = (w_k * p_ref[:, v_cols[h]].astype(F32)).astype(BF16)
        c_ref[0, h] = decay * c_prev + _dot_tn(k, wv)
        n_ref[0, h:h + 1, :] = decay * n_prev + jnp.sum(w_k * k.astype(F32), axis=0, keepdims=True)

    m_ref[0] = m_new


def mlstm_long_mixer(proj, gate_pre, bias, head_g, c0, n0, m0, *, row0, batch, seq, rows):
    H = M_HEADS
    dqk, dv = c0.shape[-2], c0.shape[-1]
    R = rows
    lanes = gate_pre.shape[1]
    assert seq % R == 0 and row0 % R == 0 and lanes >= 2 * H
    nsteps = seq // R
    blk0 = row0 // R
    rows_of = lambda i, c: (blk0 + i * nsteps + c, 0)
    bias_l = jnp.pad(bias.reshape(1, 2 * H), ((0, 0), (0, lanes - 2 * H)))
    m0_l = jnp.pad(m0, ((0, 0), (H, lanes - 2 * H))).reshape(batch, 1, lanes)
    y, c_out, n_out, m_out = pl.pallas_call(
        functools.partial(_mlstm_long_kernel, R, dqk, dv),
        grid=(batch, nsteps),
        in_specs=[
            pl.BlockSpec((R, proj.shape[1]), rows_of),
            pl.BlockSpec((R, lanes), rows_of),
            pl.BlockSpec((1, lanes), lambda i, c: (0, 0)),
            pl.BlockSpec((1, H * dv), lambda i, c: (0, 0)),
            pl.BlockSpec((1, H, dqk, dv), lambda i, c: (i, 0, 0, 0)),
            pl.BlockSpec((1, H, dqk), lambda i, c: (i, 0, 0)),
            pl.BlockSpec((1, 1, lanes), lambda i, c: (i, 0, 0)),
        ],
        out_specs=[
            pl.BlockSpec((R, H * dv), lambda i, c: (i * nsteps + c, 0)),
            pl.BlockSpec((1, H, dqk, dv), lambda i, c: (i, 0, 0, 0)),
            pl.BlockSpec((1, H, dqk), lambda i, c: (i, 0, 0)),
            pl.BlockSpec((1, 1, lanes), lambda i, c: (i, 0, 0)),
        ],
        out_shape=[
            jax.ShapeDtypeStruct((batch * seq, H * dv), BF16),
            jax.ShapeDtypeStruct(c0.shape, F32),
            jax.ShapeDtypeStruct(n0.shape, F32),
            jax.ShapeDtypeStruct(m0_l.shape, F32),
        ],
        scratch_shapes=[pltpu.VMEM((H, dqk, R), BF16)],
        compiler_params=_params("arbitrary", "arbitrary"),
        name="mlstm_long_mixer",
    )(proj, gate_pre, bias_l, head_g.reshape(1, H * dv), c0, n0, m0_l)
    return y, c_out, n_out, m_out[:, 0, H:2 * H]


def _mlstm_short_kernel(nb, T, dqk, dv,
                        p_ref, gp_ref, bias_ref, hg_ref, c0_ref, n0_ref, m0_ref,
                        y_ref, c_ref, n_ref, m_ref):
    H = M_HEADS
    R = nb * T
    PAIR = 2 * T
    s1 = H * dqk
    scale = dqk ** -0.5
    assert T % 8 == 0 and nb % 2 == 0

    r = lax.broadcasted_iota(jnp.int32, (R, R), 0)
    c = lax.broadcasted_iota(jnp.int32, (R, R), 1)
    causal = ((r // T) == (c // T)) & (c <= r)
    tril = causal.astype(BF16)
    seq_cols = (lax.broadcasted_iota(jnp.int32, (nb, 1, R), 2) // T
                == lax.broadcasted_iota(jnp.int32, (nb, 1, R), 0)).astype(BF16)
    first_of_pair = lax.broadcasted_iota(jnp.int32, (PAIR, 1), 0) < T

    def per_seq(x):
        return x.reshape(nb, T, x.shape[-1])

    def seq_rows(x3):
        return jnp.broadcast_to(x3, (nb, T, x3.shape[-1])).reshape(R, x3.shape[-1])

    capped = GATE_CAP * jnp.tanh((gp_ref[...] + bias_ref[...]) / GATE_CAP)
    cum = _cumsum_rows_exact(tril, _log_sigmoid(capped))
    ig_on_f = pltpu.roll(capped, H, axis=1)
    m_prev = m0_ref[...]
    b_last = seq_rows(per_seq(cum)[:, T - 1:T, :])
    g_all = b_last - cum + ig_on_f
    m_new = jnp.maximum(b_last + m_prev, seq_rows(jnp.max(per_seq(g_all), axis=1, keepdims=True)))
    decay_all = jnp.exp(b_last + m_prev - m_new)
    wk_all = jnp.exp(g_all - m_new) * scale
    inter_all = cum + m_prev
    rows_t = (ig_on_f - cum).T

    for h in range(H):
        lane = H + h
        q = p_ref[:, h * dqk:(h + 1) * dqk]
        k = p_ref[:, s1 + h * dqk:s1 + (h + 1) * dqk]
        v = p_ref[:, 2 * s1 + h * dv:2 * s1 + (h + 1) * dv]
        o = p_ref[:, 2 * s1 + H * dv + h * dv:2 * s1 + H * dv + (h + 1) * dv]
        kf = k.astype(F32)

        dlog = jnp.where(causal, cum[:, lane:lane + 1] + rows_t[lane:lane + 1, :], -jnp.inf)
        inter = inter_all[:, lane:lane + 1]
        m_t = jnp.maximum(inter, jnp.max(dlog, axis=1, keepdims=True))
        w_intra = jnp.exp(dlog - m_t)
        w_inter = jnp.exp(inter - m_t)
        sc = _dot_nt(q, k) * scale * w_intra
        num_intra = jnp.dot(sc.astype(BF16), v, preferred_element_type=F32)
        n_prev = n0_ref[h]
        qn = jnp.sum(q.astype(F32) * n_prev, axis=1, keepdims=True)
        den = jnp.sum(sc, axis=1, keepdims=True) + w_inter * qn
        inv = 1.0 / jnp.maximum(jnp.abs(den), jnp.exp(-m_t))
        hg = hg_ref[:, h * dv:(h + 1) * dv]

        w_k = wk_all[:, lane:lane + 1]
        k_t = kf.T.astype(BF16)
        stacked = (k_t[None] * seq_cols).reshape(nb * dqk, R)
        upd = jnp.dot(stacked, (w_k * v.astype(F32)).astype(BF16), preferred_element_type=F32)
        n_ref[h] = (decay_all[:, lane:lane + 1] * n_prev
                    + seq_rows(jnp.sum(per_seq(w_k * kf), axis=1, keepdims=True)))

        for pr in range(nb // 2):
            rows = slice(pr * PAIR, (pr + 1) * PAIR)
            lhs = p_ref[rows, h * dqk:(h + 1) * dqk]
            qc = []
            for b in (2 * pr, 2 * pr + 1):
                c_prev = c0_ref[0, b, h]
                qc.append(jnp.dot(lhs, c_prev.astype(BF16), preferred_element_type=F32))
                c_ref[0, b, h] = decay_all[b * T:b * T + 1, lane:lane + 1] * c_prev + upd[b * dqk:(b + 1) * dqk, :]
            num = num_intra[rows, :] + w_inter[rows, :] * jnp.where(first_of_pair, qc[0], qc[1])
            hc = num * inv[rows, :]
            hn = hc * lax.rsqrt(jnp.mean(hc * hc, axis=1, keepdims=True) + EPS) * hg
            y_ref[rows, h * dv:(h + 1) * dv] = (hn * jax.nn.sigmoid(o[rows, :].astype(F32))).astype(y_ref.dtype)

    m_ref[...] = m_new


def mlstm_short_mixer(proj, gate_pre, bias, head_g, c0, n0, m0, *, row0, batch, seq, nb):
    H = M_HEADS
    dqk, dv = c0.shape[-2], c0.shape[-1]
    R = nb * seq
    lanes = gate_pre.shape[1]
    assert batch % nb == 0 and row0 % R == 0 and lanes >= 2 * H
    nblocks = batch // nb
    blk0 = row0 // R
    bias_l = jnp.pad(bias.reshape(1, 2 * H), ((0, 0), (0, lanes - 2 * H)))
    m0_l = jnp.repeat(jnp.pad(m0, ((0, 0), (H, lanes - 2 * H))), seq, axis=0)
    n0_t = jnp.repeat(jnp.swapaxes(n0, 0, 1), seq, axis=1)
    y, c_out, n_out, m_out = pl.pallas_call(
        functools.partial(_mlstm_short_kernel, nb, seq, dqk, dv),
        grid=(nblocks,),
        in_specs=[
            pl.BlockSpec((R, proj.shape[1]), lambda i: (blk0 + i, 0)),
            pl.BlockSpec((R, lanes), lambda i: (blk0 + i, 0)),
            pl.BlockSpec((1, lanes), lambda i: (0, 0)),
            pl.BlockSpec((1, H * dv), lambda i: (0, 0)),
            pl.BlockSpec((1, nb, H, dqk, dv), lambda i: (i, 0, 0, 0, 0)),
            pl.BlockSpec((H, R, dqk), lambda i: (0, i, 0)),
            pl.BlockSpec((R, lanes), lambda i: (i, 0)),
        ],
        out_specs=[
            pl.BlockSpec((R, H * dv), lambda i: (i, 0)),
            pl.BlockSpec((1, nb, H, dqk, dv), lambda i: (i, 0, 0, 0, 0)),
            pl.BlockSpec((H, R, dqk), lambda i: (0, i, 0)),
            pl.BlockSpec((R, lanes), lambda i: (i, 0)),
        ],
        out_shape=[
            jax.ShapeDtypeStruct((batch * seq, H * dv), BF16),
            jax.ShapeDtypeStruct((nblocks, nb, H, dqk, dv), F32),
            jax.ShapeDtypeStruct(n0_t.shape, F32),
            jax.ShapeDtypeStruct(m0_l.shape, F32),
        ],
        compiler_params=_params("arbitrary"),
        name="mlstm_short_mixer",
    )(proj, gate_pre, bias_l, head_g.reshape(1, H * dv), c0.reshape(nblocks, nb, H, dqk, dv), n0_t, m0_l)
    return (y, c_out.reshape(batch, H, dqk, dv), jnp.swapaxes(n_out[:, ::seq, :], 0, 1), m_out[::seq, H:2 * H])


def _hgrn_kernel(nseg, L, heads, dk, dv,
                 p_ref, lb_ref, gn_ref, s0_ref, y_ref, s_ref):
    R = nseg * L
    fdim = heads * dk
    scale = dk ** -0.5
    sub = min(L, H_SUB)
    assert L % sub == 0 and L // sub in (1, 2)

    @pl.when(pl.program_id(1) == 0)
    def _():
        s_ref[...] = s0_ref[...]

    r, c, same = _segment_masks(nseg, L)
    causal = same & (c <= r)
    same_sub = (r // sub) == (c // sub)
    diag_mask = causal & same_sub
    tril = causal.astype(F32)

    row = lax.broadcasted_iota(jnp.int32, (R, 1), 0)
    row_seg = row // L
    pos = row % L
    mid_sel = (same_sub & ((c % sub) == (sub // 2))).astype(F32)
    last_sel = (same & ((c % L) == (L - 1))).astype(F32)
    if L // sub == 2:
        bnd_sel = (same & ((c % L) == (sub - 1))).astype(F32)
        upper = pos >= sub
    hp = lax.Precision.HIGHEST

    for h in range(heads):
        q = p_ref[:, h * dk:(h + 1) * dk].astype(F32) * scale
        fpre = p_ref[:, fdim + h * dk:fdim + (h + 1) * dk].astype(F32)
        iv = p_ref[:, 2 * fdim + h * dv:2 * fdim + (h + 1) * dv]
        gate = p_ref[:, 2 * fdim + heads * dv + h * dv:2 * fdim + heads * dv + (h + 1) * dv].astype(F32)
        lb = lb_ref[:, h * dk:(h + 1) * dk]
        f = lb + (1.0 - lb) * jax.nn.sigmoid(fpre)
        log_f = jnp.log(f)
        k = 1.0 - f

        a = jnp.dot(tril, log_f, preferred_element_type=F32, precision=hp)
        a_mid = jnp.dot(mid_sel, a, preferred_element_type=F32, precision=hp)
        a_last = jnp.dot(last_sel, a, preferred_element_type=F32, precision=hp)

        qd = (q * jnp.exp(a - a_mid)).astype(BF16)
        kd = (k * jnp.exp(a_mid - a)).astype(BF16)
        sc = jnp.where(diag_mask, _dot_nt(qd, kd), 0.0)
        if L // sub == 2:
            a_bnd = jnp.dot(bnd_sel, a, preferred_element_type=F32, precision=hp)
            qo = jnp.where(upper, q * jnp.exp(jnp.minimum(a - a_bnd, 0.0)), 0.0).astype(BF16)
            ko = jnp.where(upper, 0.0, k * jnp.exp(jnp.minimum(a_bnd - a, 0.0))).astype(BF16)
            sc = sc + jnp.where(same, _dot_nt(qo, ko), 0.0)

        qe = (q * jnp.exp(a)).astype(BF16)
        inter = jnp.zeros((R, dv), F32)
        for s in range(nseg):
            inter_s = jnp.dot(qe, s_ref[0, s, h].astype(BF16), preferred_element_type=F32)
            inter = inter_s if nseg == 1 else jnp.where(row_seg == s, inter_s, inter)
        oc = jnp.dot(sc.astype(BF16), iv, preferred_element_type=F32) + inter

        on = oc * lax.rsqrt(jnp.mean(oc * oc, axis=1, keepdims=True) + EPS) * gn_ref[:, h * dv:(h + 1) * dv]
        y_ref[:, h * dv:(h + 1) * dv] = (on * _silu(gate)).astype(y_ref.dtype)

        ke = k * jnp.exp(a_last - a)
        for s in range(nseg):
            in_s = row_seg == s
            ke_s = ke if nseg == 1 else jnp.where(in_s, ke, 0.0)
            a_last_s = a[s * L + L - 1:s * L + L, :]
            upd = _dot_tn(ke_s.astype(BF16), iv)
            decay_col = jnp.sum(jnp.where(lax.broadcasted_iota(jnp.int32, (dk, dk), 0)
                                          == lax.broadcasted_iota(jnp.int32, (dk, dk), 1),
                                          jnp.exp(a_last_s), 0.0), axis=1, keepdims=True)
            s_ref[0, s, h] = decay_col * s_ref[0, s, h] + upd


def hgrn_mixer(proj, lb, gn_g, s0, *, row0, batch, seq, nseg, chunk):
    heads, dk, dv = s0.shape[-3], s0.shape[-2], s0.shape[-1]
    R = nseg * chunk
    nchunks = seq // chunk
    assert nseg == 1 or nchunks == 1
    nblocks = batch // nseg
    blk0 = row0 // R
    rows = lambda i, c: (blk0 + i * nchunks + c, 0)
    s0 = s0.reshape(nblocks, nseg, heads, dk, dv)
    y, s_out = pl.pallas_call(
        functools.partial(_hgrn_kernel, nseg, chunk, heads, dk, dv),
        grid=(nblocks, nchunks),
        in_specs=[
            pl.BlockSpec((R, proj.shape[1]), rows),
            pl.BlockSpec((1, heads * dk), lambda i, c: (0, 0)),
            pl.BlockSpec((1, heads * dv), lambda i, c: (0, 0)),
            pl.BlockSpec((1, nseg, heads, dk, dv), lambda i, c: (i, 0, 0, 0, 0)),
        ],
        out_specs=[
            pl.BlockSpec((R, heads * dv), lambda i, c: (i * nchunks + c, 0)),
            pl.BlockSpec((1, nseg, heads, dk, dv), lambda i, c: (i, 0, 0, 0, 0)),
        ],
        out_shape=[
            jax.ShapeDtypeStruct((batch * seq, heads * dv), BF16),
            jax.ShapeDtypeStruct(s0.shape, F32),
        ],
        compiler_params=_params("arbitrary", "arbitrary"),
        name="hgrn_mixer",
    )(proj, lb.reshape(1, heads * dk), gn_g.reshape(1, heads * dv), s0)
    return y, s_out.reshape(batch, heads, dk, dv)


def _hgrn_short_kernel(nb, T, heads, dk, dv,
                       p_ref, lb_ref, gn_ref, s0_ref, y_ref, s_ref,
                       qd_scr, kd_scr, qe_scr, ke_scr, dec_scr):
    R = nb * T
    fdim = heads * dk
    scale = dk ** -0.5
    W = 2 * dk
    PAIR = 2 * T
    assert T % 8 == 0 and T <= H_SUB and nb % 2 == 0

    r = lax.broadcasted_iota(jnp.int32, (R, R), 0)
    c = lax.broadcasted_iota(jnp.int32, (R, R), 1)
    causal = ((r // T) == (c // T)) & (c <= r)
    tril = causal.astype(BF16)
    seq_cols = (lax.broadcasted_iota(jnp.int32, (nb, 1, R), 2) // T
                == lax.broadcasted_iota(jnp.int32, (nb, 1, R), 0)).astype(BF16)
    first_of_pair = lax.broadcasted_iota(jnp.int32, (PAIR, 1), 0) < T

    def minus_ref_row(a, row):
        a3 = a.reshape(nb, T, a.shape[-1])
        return (a3 - a3[:, row:row + 1, :]).reshape(a.shape)

    for g in range(fdim // W):
        sl = slice(g * W, (g + 1) * W)
        q = p_ref[:, sl].astype(F32) * scale
        fpre = p_ref[:, fdim + g * W:fdim + (g + 1) * W].astype(F32)
        lb = lb_ref[:, sl]
        f = lb + (1.0 - lb) * jax.nn.sigmoid(fpre)
        k = 1.0 - f
        a = _cumsum_rows_exact(tril, jnp.log(f))
        d = minus_ref_row(a, T // 2)
        to_last = -minus_ref_row(a, T - 1)
        qd_scr[:, sl] = (q * jnp.exp(d)).astype(BF16)
        kd_scr[:, sl] = (k * jnp.exp(-d)).astype(BF16)
        qe_scr[:, sl] = (q * jnp.exp(a)).astype(BF16)
        ke_scr[:, sl] = k * jnp.exp(to_last)
        dec_scr[:, sl] = jnp.exp(a + to_last)

    for h in range(heads):
        sl = slice(h * dk, (h + 1) * dk)
        iv = p_ref[:, 2 * fdim + h * dv:2 * fdim + (h + 1) * dv]
        sc = jnp.where(causal, _dot_nt(qd_scr[:, sl], kd_scr[:, sl]), 0.0)
        oc = jnp.dot(sc.astype(BF16), iv, preferred_element_type=F32)
        ke_t = ke_scr[:, sl].T.astype(BF16)
        dec_t = dec_scr[:, sl].T
        stacked = (ke_t[None] * seq_cols).reshape(nb * dk, R)
        upd = jnp.dot(stacked, iv, preferred_element_type=F32)
        gn = gn_ref[:, h * dv:(h + 1) * dv]
        for pr in range(nb // 2):
            rows = slice(pr * PAIR, (pr + 1) * PAIR)
            lhs = qe_scr[rows, sl]
            inter = []
            for b in (2 * pr, 2 * pr + 1):
                s_prev = s0_ref[0, b, h]
                inter.append(jnp.dot(lhs, s_prev.astype(BF16), preferred_element_type=F32))
                s_ref[0, b, h] = dec_t[:, b * T:b * T + 1] * s_prev + upd[b * dk:(b + 1) * dk, :]
            o2 = oc[rows, :] + jnp.where(first_of_pair, inter[0], inter[1])
            gate = p_ref[rows, 2 * fdim + heads * dv + h * dv:2 * fdim + heads * dv + (h + 1) * dv].astype(F32)
            on = o2 * lax.rsqrt(jnp.mean(o2 * o2, axis=1, keepdims=True) + EPS) * gn
            y_ref[rows, h * dv:(h + 1) * dv] = (on * _silu(gate)).astype(y_ref.dtype)


def hgrn_short_mixer(proj, lb, gn_g, s0, *, row0, batch, seq, nb):
    heads, dk, dv = s0.shape[-3], s0.shape[-2], s0.shape[-1]
    R = nb * seq
    assert batch % nb == 0 and row0 % R == 0
    nblocks = batch // nb
    blk0 = row0 // R
    fdim = heads * dk
    s0 = s0.reshape(nblocks, nb, heads, dk, dv)
    y, s_out = pl.pallas_call(
        functools.partial(_hgrn_short_kernel, nb, seq, heads, dk, dv),
        grid=(nblocks,),
        in_specs=[
            pl.BlockSpec((R, proj.shape[1]), lambda i: (blk0 + i, 0)),
            pl.BlockSpec((1, fdim), lambda i: (0, 0)),
            pl.BlockSpec((1, heads * dv), lambda i: (0, 0)),
            pl.BlockSpec((1, nb, heads, dk, dv), lambda i: (i, 0, 0, 0, 0)),
        ],
        out_specs=[
            pl.BlockSpec((R, heads * dv), lambda i: (i, 0)),
            pl.BlockSpec((1, nb, heads, dk, dv), lambda i: (i, 0, 0, 0, 0)),
        ],
        out_shape=[
            jax.ShapeDtypeStruct((batch * seq, heads * dv), BF16),
            jax.ShapeDtypeStruct(s0.shape, F32),
        ],
        scratch_shapes=[
            pltpu.VMEM((R, fdim), BF16), pltpu.VMEM((R, fdim), BF16), pltpu.VMEM((R, fdim), BF16),
            pltpu.VMEM((R, fdim), F32), pltpu.VMEM((R, fdim), F32),
        ],
        compiler_params=_params("arbitrary"),
        name="hgrn_short_mixer",
    )(proj, lb.reshape(1, fdim), gn_g.reshape(1, heads * dv), s0)
    return y, s_out.reshape(batch, heads, dk, dv)


def _hgrn_long_kernel(R, heads, dk, dv, levels,
                      p_ref, lb_ref, gn_ref, s0_ref, y_ref, s_ref,
                      st_scr, dec_scr, qd_scr, kd_scr, qe_scr, ke_scr, ql_scr, kl_scr, k_scr, a_scr,
                      kt_scr, sb_scr, ivt_scr, sc_scr, oc_scr):
    fdim = heads * dk
    eps_unscaled = EPS * dk
    W = 2 * dk
    last_step = pl.num_programs(1) - 1

    @pl.when(pl.program_id(1) == 0)
    def _():
        for h in range(heads):
            st_scr[h] = s0_ref[0, h].T

    r = lax.broadcasted_iota(jnp.int32, (R, R), 0)
    c = lax.broadcasted_iota(jnp.int32, (R, R), 1)
    tril = (c <= r).astype(BF16)
    diag_mask = ((r // H_SUB) == (c // H_SUB)) & (c <= r)
    level_masks = [((r // G) == (c // G)) & ((r % G) >= G // 2) & ((c % G) < G // 2) for G in levels]

    NSB = R // H_SUB
    zeros_sub = jnp.zeros((H_SUB, dk), BF16)

    for g in range(fdim // W):
        fpre = p_ref[:, fdim + g * W:fdim + (g + 1) * W].astype(F32)
        lb = lb_ref[:, g * W:(g + 1) * W]
        f = lb + (1.0 - lb) * jax.nn.sigmoid(fpre)
        k = 1.0 - f
        a = _cumsum_rows_exact(tril, jnp.log(f))
        for j in range(W // dk):
            k_scr[g * (W // dk) + j] = k[:, j * dk:(j + 1) * dk]
            a_scr[g * (W // dk) + j] = a[:, j * dk:(j + 1) * dk]

    def rows_of(ref, h, start, count, stride):
        if count == 1:
            return ref[h, start:start + 1, :]
        return ref.at[h][pl.ds(start, count, stride=stride), :]

    for h in range(heads):
        sl = slice(h * dk, (h + 1) * dk)
        a_last = a_scr[h, R - 1:R, :]
        dec_scr[:, sl] = jnp.exp(a_last)
        mids = rows_of(a_scr, h, H_SUB // 2, NSB, H_SUB)
        c_qe = jnp.exp(mids)
        c_ke = jnp.exp(a_last - mids)
        c_lv = []
        for G in levels:
            per = G // H_SUB
            bnd = rows_of(a_scr, h, G // 2 - 1, R // G, G)
            consts = []
            for o in range(per):
                mid_o = rows_of(a_scr, h, o * H_SUB + H_SUB // 2, R // G, G)
                consts.append(jnp.exp(mid_o - bnd) if o >= per // 2 else jnp.exp(bnd - mid_o))
            c_lv.append(consts)
        for s in range(NSB):
            rows = slice(s * H_SUB, (s + 1) * H_SUB)
            d = a_scr[h, rows, :] - mids[s:s + 1, :]
            qd = p_ref[rows, sl].astype(F32) * jnp.exp(d)
            kd = k_scr[h, rows, :] * jnp.exp(-d)
            qd_scr[rows, sl] = qd.astype(BF16)
            kd_scr[rows, sl] = kd.astype(BF16)
            qe_scr[rows, sl] = (qd * c_qe[s:s + 1, :]).astype(BF16)
            ke_scr[rows, sl] = (kd * c_ke[s:s + 1, :]).astype(BF16)
            for li, G in enumerate(levels):
                per = G // H_SUB
                o, gi = s % per, s // per
                const = c_lv[li][o][gi:gi + 1, :]
                if o >= per // 2:
                    ql_scr[li, rows, sl] = (qd * const).astype(BF16)
                    kl_scr[li, rows, sl] = zeros_sub
                else:
                    kl_scr[li, rows, sl] = (kd * const).astype(BF16)
                    ql_scr[li, rows, sl] = zeros_sub

    def paired(lhs_a, lhs_b, rhs_ab):
        ca = rhs_ab.shape[1] // 2
        return (jnp.dot(lhs_a, rhs_ab[:, :ca], preferred_element_type=F32),
                jnp.dot(lhs_b, rhs_ab[:, ca:], preferred_element_type=F32))

    assert dk == dv
    npairs = heads // 2
    lanes_a = [slice(2 * p * dk, (2 * p + 1) * dk) for p in range(npairs)]
    lanes_b = [slice((2 * p + 1) * dk, (2 * p + 2) * dk) for p in range(npairs)]
    lanes_ab = [slice(2 * p * dk, (2 * p + 2) * dk) for p in range(npairs)]
    iv_of = lambda p: p_ref[:, 2 * fdim + 2 * p * dv:2 * fdim + (2 * p + 2) * dv]

    for p in range(npairs):
        kt_scr[p, 0, :, :R] = kd_scr[:, lanes_a[p]].T
        kt_scr[p, 0, :, R:] = kd_scr[:, lanes_b[p]].T
        for li in range(len(levels)):
            kt_scr[p, 1 + li, :, :R] = kl_scr[li, :, lanes_a[p]].T
            kt_scr[p, 1 + li, :, R:] = kl_scr[li, :, lanes_b[p]].T
        sb_scr[p, :, :dv] = st_scr[2 * p].astype(BF16).T
        sb_scr[p, :, dv:] = st_scr[2 * p + 1].astype(BF16).T
        ivt_scr[p] = iv_of(p).T
    for p in range(npairs):
        da, db = paired(qd_scr[:, lanes_a[p]], qd_scr[:, lanes_b[p]], kt_scr[p, 0])
        sc_a, sc_b = jnp.where(diag_mask, da, 0.0), jnp.where(diag_mask, db, 0.0)
        for li in range(len(levels)):
            la, lb_ = paired(ql_scr[li, :, lanes_a[p]], ql_scr[li, :, lanes_b[p]], kt_scr[p, 1 + li])
            sc_a, sc_b = jnp.where(level_masks[li], la, sc_a), jnp.where(level_masks[li], lb_, sc_b)
        sc_scr[2 * p] = sc_a.astype(BF16)
        sc_scr[2 * p + 1] = sc_b.astype(BF16)
    for p in range(npairs):
        intra_a, intra_b = paired(sc_scr[2 * p], sc_scr[2 * p + 1], iv_of(p))
        inter_a, inter_b = paired(qe_scr[:, lanes_a[p]], qe_scr[:, lanes_b[p]], sb_scr[p])
        oc_scr[:, lanes_a[p]] = intra_a + inter_a
        oc_scr[:, lanes_b[p]] = intra_b + inter_b
        upd_a = jnp.dot(ivt_scr[p, :dv, :], ke_scr[:, lanes_a[p]], preferred_element_type=F32)
        upd_b = jnp.dot(ivt_scr[p, dv:, :], ke_scr[:, lanes_b[p]], preferred_element_type=F32)
        st_scr[2 * p] = st_scr[2 * p] * dec_scr[:, lanes_a[p]] + upd_a
        st_scr[2 * p + 1] = st_scr[2 * p + 1] * dec_scr[:, lanes_b[p]] + upd_b
    for h in range(heads):
        sl = slice(h * dv, (h + 1) * dv)
        oc = oc_scr[:, sl]
        gate = p_ref[:, 2 * fdim + heads * dv + h * dv:2 * fdim + heads * dv + (h + 1) * dv].astype(F32)
        on = oc * lax.rsqrt(jnp.mean(oc * oc, axis=1, keepdims=True) + eps_unscaled) * gn_ref[:, sl]
        y_ref[:, sl] = (on * _silu(gate)).astype(y_ref.dtype)

    @pl.when(pl.program_id(1) == last_step)
    def _():
        for h in range(heads):
            s_ref[0, h] = st_scr[h].T


def hgrn_long_mixer(proj, lb, gn_g, s0, *, row0, batch, seq, rows):
    heads, dk, dv = s0.shape[-3], s0.shape[-2], s0.shape[-1]
    R = rows
    assert seq % R == 0 and R % (2 * H_SUB) == 0 and row0 % R == 0
    levels = []
    G = 2 * H_SUB
    while G <= R:
        levels.append(G)
        G *= 2
    assert levels[-1] == R
    nsteps = seq // R
    blk0 = row0 // R
    fdim = heads * dk
    wide = lambda: pltpu.VMEM((R, fdim), BF16)
    return pl.pallas_call(
        functools.partial(_hgrn_long_kernel, R, heads, dk, dv, tuple(levels)),
        grid=(batch, nsteps),
        in_specs=[
            pl.BlockSpec((R, proj.shape[1]), lambda i, c: (blk0 + i * nsteps + c, 0)),
            pl.BlockSpec((1, fdim), lambda i, c: (0, 0)),
            pl.BlockSpec((1, heads * dv), lambda i, c: (0, 0)),
            pl.BlockSpec((1, heads, dk, dv), lambda i, c: (i, 0, 0, 0)),
        ],
        out_specs=[
            pl.BlockSpec((R, heads * dv), lambda i, c: (i * nsteps + c, 0)),
            pl.BlockSpec((1, heads, dk, dv), lambda i, c: (i, 0, 0, 0)),
        ],
        out_shape=[
            jax.ShapeDtypeStruct((batch * seq, heads * dv), BF16),
            jax.ShapeDtypeStruct(s0.shape, F32),
        ],
        scratch_shapes=[
            pltpu.VMEM((heads, dv, dk), F32),
            pltpu.VMEM((1, fdim), F32),
            wide(), wide(), wide(), wide(),
            pltpu.VMEM((len(levels), R, fdim), BF16),
            pltpu.VMEM((len(levels), R, fdim), BF16),
            pltpu.VMEM((heads, R, dk), F32),
            pltpu.VMEM((heads, R, dk), F32),
            pltpu.VMEM((heads // 2, 1 + len(levels), dk, 2 * R), BF16),
            pltpu.VMEM((heads // 2, dk, 2 * dv), BF16),
            pltpu.VMEM((heads // 2, 2 * dv, R), BF16),
            pltpu.VMEM((heads, R, R), BF16),
            pltpu.VMEM((R, heads * dv), F32),
        ],
        compiler_params=_params("arbitrary", "arbitrary"),
        name="hgrn_long_mixer",
    )(proj, lb.reshape(1, fdim), gn_g.reshape(1, heads * dv), s0)


def kernel(x_prompt, x_sample, state_mlstm_C, state_mlstm_n, state_mlstm_m, state_hgrn_S,
           norm_mix_g, norm_ffn_g, norm_final_g, mlstm_w_in, mlstm_b_gates, mlstm_head_norm_g,
           mlstm_w_out, hgrn_w_in, hgrn_lower_bounds, hgrn_g_norm_g, hgrn_w_out, ffn_w_up, ffn_w_down):
    bp, tp, d = x_prompt.shape
    bs, ts, _ = x_sample.shape
    np_, ns_ = bp * tp, bs * ts
    depth = norm_mix_g.shape[0]
    H = M_HEADS
    dqk, dv = state_mlstm_C.shape[-2], state_mlstm_C.shape[-1]
    hh, hdk, hdv = state_hgrn_S.shape[-3], state_hgrn_S.shape[-2], state_hgrn_S.shape[-1]

    xs = (x_prompt.reshape(np_, d), x_sample.reshape(ns_, d))

    lb_all = jnp.cumsum(jax.nn.softmax(hgrn_lower_bounds.astype(F32), axis=0), axis=0)
    lb_all = lb_all - lb_all[0]

    tm = 1024
    Cp, Np, Mp, Sp, Cs, Ns, Ms, Ss = [], [], [], [], [], [], [], []
    for i in range(depth):
        j = i // 2
        if i % 2 == 0:
            n_main = 2 * H * dqk + 2 * H * dv
            w_in_t = jnp.swapaxes(mlstm_w_in, 1, 2)
            w_gates = jnp.pad(w_in_t[j, n_main:], ((0, 128 - 2 * H), (0, 0)))
            proj, gate_pre = norm_matmul(xs, norm_mix_g[i], w_in_t, j, n_main, tm=tm, tn=1024,
                                         out_dtype=BF16, w_extra=w_gates, w_rows_are_outputs=True)
            args = (proj, gate_pre, mlstm_b_gates[j], mlstm_head_norm_g[j])
            yp, c_p, n_p, m_p = mlstm_long_mixer(
                *args, jnp.zeros((bp, H, dqk, dv), F32), jnp.zeros((bp, H, dqk), F32), jnp.zeros((bp, H), F32),
                row0=0, batch=bp, seq=tp, rows=256)
            ysm, c_s, n_s, m_s = mlstm_short_mixer(
                *args, state_mlstm_C[j], state_mlstm_n[j], state_mlstm_m[j],
                row0=np_, batch=bs, seq=ts, nb=8)
            Cp.append(c_p); Np.append(n_p); Mp.append(m_p)
            Cs.append(c_s); Ns.append(n_s); Ms.append(m_s)
            w_out = mlstm_w_out
        else:
            proj = norm_matmul(xs, norm_mix_g[i], hgrn_w_in, j, hgrn_w_in.shape[-1], tm=tm, tn=1024, out_dtype=BF16)
            args = (proj, lb_all[i], hgrn_g_norm_g[j])
            yp, s_p = hgrn_long_mixer(*args, jnp.zeros((bp, hh, hdk, hdv), F32),
                                      row0=0, batch=bp, seq=tp, rows=128)
            ysm, s_s = hgrn_short_mixer(*args, state_hgrn_S[j], row0=np_, batch=bs, seq=ts, nb=8)
            Sp.append(s_p); Ss.append(s_s)
            w_out = hgrn_w_out
        x = matmul_residual((yp, ysm), w_out, j, xs, tm=256, tn=d)
        act = norm_swiglu(x, norm_ffn_g[i], ffn_w_up, i, tm=tm, tn=512)
        x = matmul_residual((act,), ffn_w_down, i, (x,), tm=512, tn=512)
        xs = (x,)

    y_prompt = final_norm(x, norm_final_g, row0=0, nrows=np_, tm=512).reshape(bp, tp, d)
    y_sample = final_norm(x, norm_final_g, row0=np_, nrows=ns_, tm=512).reshape(bs, ts, d)
    cat = lambda parts: jnp.stack(parts) if len(parts) > 1 else parts[0][None]
    return (y_prompt, y_sample, cat(Cp), cat(Np), cat(Mp), cat(Sp), cat(Cs), cat(Ns), cat(Ms), cat(Ss))
```

```python
import functools

import jax
import jax.numpy as jnp
from jax import lax
from jax.experimental import pallas as pl
from jax.experimental.pallas import tpu as pltpu

F32 = jnp.float32
BF16 = jnp.bfloat16

EPS = 1e-6
GATE_CAP = 15.0
M_HEADS = 8
H_SUB = 16

LANES = 128

PROJ_ROWS = 1024
PROJ_COLS = 1024
SWIGLU_COLS = 512
OUT_ROWS = 256
DOWN_ROWS = 512
DOWN_COLS = 512
NORM_ROWS = 512
MLSTM_STEP_ROWS = 256
HGRN_STEP_ROWS = 128
SAMPLE_SEQS = 8

VMEM_LIMIT_BYTES = 56 * 1024 * 1024


def _params(*sem):
    return pltpu.CompilerParams(dimension_semantics=sem, vmem_limit_bytes=VMEM_LIMIT_BYTES)


def _row_tiles(xs, tm):
    for x in xs:
        assert x.shape[0] % tm == 0, (x.shape, tm)
    return tuple(x.shape[0] // tm for x in xs)


def _clamped_row_tile(first, count, row_of, col_of, *ids):
    return (jnp.clip(row_of(*ids) - first, 0, count - 1), col_of(*ids))


def _row_specs(tiles, block, row_of, col_of, single_tile_unbuffered=False):
    specs, first = [], 0
    for count in tiles:
        mode = dict(pipeline_mode=pl.Buffered(1)) if single_tile_unbuffered and count == 1 else {}
        specs.append(pl.BlockSpec(block, functools.partial(_clamped_row_tile, first, count, row_of, col_of), **mode))
        first += count
    return specs


def _on_owner(refs, tiles, i, fn):
    if len(refs) == 1:
        fn(refs[0])
        return
    first = 0
    for ref, count in zip(refs, tiles):
        pl.when((i >= first) & (i < first + count))(functools.partial(fn, ref))
        first += count


def _rms_normed(x, g):
    ms = jnp.mean(x * x, axis=-1, keepdims=True)
    return x * lax.rsqrt(ms + EPS) * g


def _norm_matmul_kernel(tiles, with_extra, w_rows_are_outputs, *refs):
    nx = len(tiles)
    x_refs, g_ref, w_ref = refs[:nx], refs[nx], refs[nx + 1]
    if with_extra:
        we_ref, o_ref, oe_ref, h_scr = refs[nx + 2:nx + 6]
    else:
        o_ref, h_scr = refs[nx + 2:nx + 4]
    matmul = _dot_nt if w_rows_are_outputs else functools.partial(jnp.dot, preferred_element_type=F32)

    @pl.when(pl.program_id(1) == 0)
    def _():
        def build(x_ref):
            h_scr[...] = _rms_normed(x_ref[...], g_ref[...]).astype(BF16)
            if with_extra:
                oe_ref[...] = matmul(h_scr[...], we_ref[...].astype(BF16))
        _on_owner(x_refs, tiles, pl.program_id(0), build)

    if w_rows_are_outputs:
        wt_scr = refs[-1]
        wt_scr[...] = w_ref[...].astype(BF16).T
        o_ref[...] = jnp.dot(h_scr[...], wt_scr[...], preferred_element_type=F32).astype(o_ref.dtype)
    else:
        o_ref[...] = jnp.dot(h_scr[...], w_ref[...].astype(BF16), preferred_element_type=F32).astype(o_ref.dtype)


def norm_matmul(xs, g, w, layer, n_cols, *, tm, tn, out_dtype, w_extra=None, w_rows_are_outputs=False):
    d = xs[0].shape[1]
    tiles = _row_tiles(xs, tm)
    m = tm * sum(tiles)
    row_of, col0 = (lambda i, j: i), (lambda i, j: 0)
    w_spec = (pl.BlockSpec((None, tn, d), lambda i, j: (layer, j, 0)) if w_rows_are_outputs
              else pl.BlockSpec((None, d, tn), lambda i, j: (layer, 0, j)))
    in_specs = (_row_specs(tiles, (tm, d), row_of, col0, single_tile_unbuffered=len(tiles) > 1)
                + [pl.BlockSpec((1, d), lambda i, j: (0, 0)), w_spec])
    out_specs = [pl.BlockSpec((tm, tn), lambda i, j: (i, j))]
    out_shape = [jax.ShapeDtypeStruct((m, n_cols), out_dtype)]
    args = list(xs) + [g.reshape(1, d), w]
    if w_extra is not None:
        ne = w_extra.shape[0 if w_rows_are_outputs else 1]
        in_specs.append(pl.BlockSpec(w_extra.shape, lambda i, j: (0, 0)))
        out_specs.append(pl.BlockSpec((tm, ne), lambda i, j: (i, 0)))
        out_shape.append(jax.ShapeDtypeStruct((m, ne), F32))
        args.append(w_extra)
    outs = pl.pallas_call(
        functools.partial(_norm_matmul_kernel, tiles, w_extra is not None, w_rows_are_outputs),
        grid=(m // tm, n_cols // tn),
        in_specs=in_specs,
        out_specs=out_specs,
        out_shape=out_shape,
        scratch_shapes=[pltpu.VMEM((tm, d), BF16)] + ([pltpu.VMEM((d, tn), BF16)] if w_rows_are_outputs else []),
        compiler_params=_params("arbitrary", "arbitrary"),
        name="norm_matmul",
    )(*args)
    return outs if w_extra is not None else outs[0]


def _norm_swiglu_kernel(x_ref, g_ref, wa_ref, wu_ref, o_ref, h_scr):
    @pl.when(pl.program_id(1) == 0)
    def _():
        h_scr[...] = _rms_normed(x_ref[...], g_ref[...]).astype(BF16)

    h = h_scr[...]
    a = jnp.dot(h, wa_ref[...].astype(BF16), preferred_element_type=F32)
    u = jnp.dot(h, wu_ref[...].astype(BF16), preferred_element_type=F32)
    o_ref[...] = (a * jax.nn.sigmoid(a) * u).astype(o_ref.dtype)


def norm_swiglu(x, g, w_up, layer, *, tm, tn):
    m, d = x.shape
    ff = w_up.shape[-1] // 2
    nj = ff // tn
    return pl.pallas_call(
        _norm_swiglu_kernel,
        grid=(m // tm, nj),
        in_specs=[
            pl.BlockSpec((tm, d), lambda i, j: (i, 0)),
            pl.BlockSpec((1, d), lambda i, j: (0, 0)),
            pl.BlockSpec((None, d, tn), lambda i, j: (layer, 0, j)),
            pl.BlockSpec((None, d, tn), lambda i, j: (layer, 0, j + nj)),
        ],
        out_specs=pl.BlockSpec((tm, tn), lambda i, j: (i, j)),
        out_shape=jax.ShapeDtypeStruct((m, ff), BF16),
        scratch_shapes=[pltpu.VMEM((tm, d), BF16)],
        compiler_params=_params("arbitrary", "arbitrary"),
        name="norm_swiglu",
    )(x, g.reshape(1, d), w_up, w_up)


def _matmul_residual_kernel(y_tiles, r_tiles, *refs):
    ny, nr = len(y_tiles), len(r_tiles)
    y_refs, w_ref, r_refs = refs[:ny], refs[ny], refs[ny + 1:ny + 1 + nr]
    o_ref, w_scr = refs[ny + 1 + nr:]
    i = pl.program_id(1)

    @pl.when(i == 0)
    def _():
        w_scr[...] = w_ref[...].astype(BF16)

    def product(y_ref):
        o_ref[...] = jnp.dot(y_ref[...], w_scr[...], preferred_element_type=F32)

    def add_residual(r_ref):
        o_ref[...] = o_ref[...] + r_ref[...]

    _on_owner(y_refs, y_tiles, i, product)
    _on_owner(r_refs, r_tiles, i, add_residual)


def matmul_residual(ys, w, layer, rs, *, tm, tn):
    k, n = w.shape[-2], w.shape[-1]
    y_tiles, r_tiles = _row_tiles(ys, tm), _row_tiles(rs, tm)
    assert sum(y_tiles) == sum(r_tiles)
    m = tm * sum(y_tiles)
    row_of = lambda j, i: i
    w_mode = dict(pipeline_mode=pl.Buffered(1)) if tn == n else {}
    in_specs = (_row_specs(y_tiles, (tm, k), row_of, lambda j, i: 0)
                + [pl.BlockSpec((None, k, tn), lambda j, i: (layer, 0, j), **w_mode)]
                + _row_specs(r_tiles, (tm, tn), row_of, lambda j, i: j))
    return pl.pallas_call(
        functools.partial(_matmul_residual_kernel, y_tiles, r_tiles),
        grid=(n // tn, m // tm),
        in_specs=in_specs,
        out_specs=pl.BlockSpec((tm, tn), lambda j, i: (i, j)),
        out_shape=jax.ShapeDtypeStruct((m, n), F32),
        scratch_shapes=[pltpu.VMEM((k, tn), BF16)],
        compiler_params=_params("arbitrary", "arbitrary"),
        name="matmul_residual",
    )(*ys, w, *rs)


def _final_norm_kernel(x_ref, g_ref, o_ref):
    o_ref[...] = _rms_normed(x_ref[...], g_ref[...])


def final_norm(x, g, *, row0, nrows, tm):
    d = x.shape[1]
    assert row0 % tm == 0 and nrows % tm == 0
    return pl.pallas_call(
        _final_norm_kernel,
        grid=(nrows // tm,),
        in_specs=[pl.BlockSpec((tm, d), lambda i: (row0 // tm + i, 0)), pl.BlockSpec((1, d), lambda i: (0, 0))],
        out_specs=pl.BlockSpec((tm, d), lambda i: (i, 0)),
        out_shape=jax.ShapeDtypeStruct((nrows, d), F32),
        compiler_params=_params("arbitrary"),
        name="final_norm",
    )(x, g.reshape(1, d))


def _dot_nt(a, b):
    return lax.dot_general(a, b, (((1,), (1,)), ((), ())), preferred_element_type=F32)


def _dot_tn(a, b):
    return lax.dot_general(a, b, (((0,), (0,)), ((), ())), preferred_element_type=F32)


def _log_sigmoid(x):
    return jnp.minimum(x, 0.0) - jnp.log1p(jnp.exp(-jnp.abs(x)))


def _silu(x):
    return 0.5 * x * (1.0 + jnp.tanh(0.5 * x))


def _cumsum_rows_exact(tril, x):
    hi = x.astype(BF16)
    r1 = x - hi.astype(F32)
    mid = r1.astype(BF16)
    lo = (r1 - mid.astype(F32)).astype(BF16)
    return (jnp.dot(tril, hi, preferred_element_type=F32) + jnp.dot(tril, mid, preferred_element_type=F32)
            + jnp.dot(tril, lo, preferred_element_type=F32))


def _mlstm_long_kernel(R, dqk, dv,
                       p_ref, gp_ref, bias_ref, hg_ref, c0_ref, n0_ref, m0_ref,
                       y_ref, c_ref, n_ref, m_ref, kt_scr):
    H = M_HEADS
    s1 = H * dqk
    scale = dqk ** -0.5

    @pl.when(pl.program_id(1) == 0)
    def _():
        c_ref[...] = c0_ref[...]
        n_ref[...] = n0_ref[...]
        m_ref[...] = m0_ref[...]

    r = lax.broadcasted_iota(jnp.int32, (R, R), 0)
    c = lax.broadcasted_iota(jnp.int32, (R, R), 1)
    causal = c <= r
    tril = causal.astype(BF16)

    capped = GATE_CAP * jnp.tanh((gp_ref[...] + bias_ref[...]) / GATE_CAP)
    cum = _cumsum_rows_exact(tril, _log_sigmoid(capped))
    ig_on_f = pltpu.roll(capped, H, axis=1)
    m_prev = m_ref[0]
    b_last = cum[R - 1:R, :]
    g_all = b_last - cum + ig_on_f
    m_new = jnp.maximum(b_last + m_prev, jnp.max(g_all, axis=0, keepdims=True))
    decay_all = jnp.exp(b_last + m_prev - m_new)
    wk_all = jnp.exp(g_all - m_new) * scale
    inter_all = cum + m_prev
    rows_t = (ig_on_f - cum).T

    q_cols = [slice(h * dqk, (h + 1) * dqk) for h in range(H)]
    k_cols = [slice(s1 + h * dqk, s1 + (h + 1) * dqk) for h in range(H)]
    v_cols = [slice(2 * s1 + h * dv, 2 * s1 + (h + 1) * dv) for h in range(H)]
    o_cols = [slice(2 * s1 + H * dv + h * dv, 2 * s1 + H * dv + (h + 1) * dv) for h in range(H)]

    for h in range(H):
        lane = H + h
        q = p_ref[:, q_cols[h]]
        c_prev = c_ref[0, h]
        c_bf = c_prev.astype(BF16)
        n_prev = n_ref[0, h:h + 1, :]
        kt_scr[h] = p_ref[:, k_cols[h]].T

        dlog = jnp.where(causal, cum[:, lane:lane + 1] + rows_t[lane:lane + 1, :], -jnp.inf)
        inter = inter_all[:, lane:lane + 1]
        m_t = jnp.maximum(inter, jnp.max(dlog, axis=1, keepdims=True))
        w_inter = jnp.exp(inter - m_t)
        sc = jnp.dot(q, kt_scr[h], preferred_element_type=F32) * scale * jnp.exp(dlog - m_t)
        num = (jnp.dot(sc.astype(BF16), p_ref[:, v_cols[h]], preferred_element_type=F32)
               + w_inter * jnp.dot(q, c_bf, preferred_element_type=F32))
        qn = jnp.sum(q.astype(F32) * n_prev, axis=1, keepdims=True)
        den = jnp.sum(sc, axis=1, keepdims=True) + w_inter * qn
        inv = 1.0 / jnp.maximum(jnp.abs(den), jnp.exp(-m_t))
        norm = inv * lax.rsqrt(inv * inv * jnp.mean(num * num, axis=1, keepdims=True) + EPS)
        o = p_ref[:, o_cols[h]].astype(F32)
        y_ref[:, h * dv:(h + 1) * dv] = (num * norm * hg_ref[:, h * dv:(h + 1) * dv]
                                         * jax.nn.sigmoid(o)).astype(y_ref.dtype)

        k = p_ref[:, k_cols[h]]
        w_k = wk_all[:, lane:lane + 1]
        decay = decay_all[:, lane:lane + 1]
        wv = (w_k * p_ref[:, v_cols[h]].astype(F32)).astype(BF16)
        c_ref[0, h] = decay * c_prev + _dot_tn(k, wv)
        n_ref[0, h:h + 1, :] = decay * n_prev + jnp.sum(w_k * k.astype(F32), axis=0, keepdims=True)

    m_ref[0] = m_new


def mlstm_long_mixer(proj, gate_pre, bias, head_g, c0, n0, m0, *, row0, batch, seq, rows):
    H = M_HEADS
    dqk, dv = c0.shape[-2], c0.shape[-1]
    R = rows
    lanes = gate_pre.shape[1]
    assert seq % R == 0 and row0 % R == 0 and lanes >= 2 * H
    nsteps = seq // R
    blk0 = row0 // R
    rows_of = lambda i, c: (blk0 + i * nsteps + c, 0)
    bias_l = jnp.pad(bias.reshape(1, 2 * H), ((0, 0), (0, lanes - 2 * H)))
    m0_l = jnp.pad(m0, ((0, 0), (H, lanes - 2 * H))).reshape(batch, 1, lanes)
    y, c_out, n_out, m_out = pl.pallas_call(
        functools.partial(_mlstm_long_kernel, R, dqk, dv),
        grid=(batch, nsteps),
        in_specs=[
            pl.BlockSpec((R, proj.shape[1]), rows_of),
            pl.BlockSpec((R, lanes), rows_of),
            pl.BlockSpec((1, lanes), lambda i, c: (0, 0)),
            pl.BlockSpec((1, H * dv), lambda i, c: (0, 0)),
            pl.BlockSpec((1, H, dqk, dv), lambda i, c: (i, 0, 0, 0)),
            pl.BlockSpec((1, H, dqk), lambda i, c: (i, 0, 0)),
            pl.BlockSpec((1, 1, lanes), lambda i, c: (i, 0, 0)),
        ],
        out_specs=[
            pl.BlockSpec((R, H * dv), lambda i, c: (i * nsteps + c, 0)),
            pl.BlockSpec((1, H, dqk, dv), lambda i, c: (i, 0, 0, 0)),
            pl.BlockSpec((1, H, dqk), lambda i, c: (i, 0, 0)),
            pl.BlockSpec((1, 1, lanes), lambda i, c: (i, 0, 0)),
        ],
        out_shape=[
            jax.ShapeDtypeStruct((batch * seq, H * dv), BF16),
            jax.ShapeDtypeStruct(c0.shape, F32),
            jax.ShapeDtypeStruct(n0.shape, F32),
            jax.ShapeDtypeStruct(m0_l.shape, F32),
        ],
        scratch_shapes=[pltpu.VMEM((H, dqk, R), BF16)],
        compiler_params=_params("arbitrary", "arbitrary"),
        name="mlstm_long_mixer",
    )(proj, gate_pre, bias_l, head_g.reshape(1, H * dv), c0, n0, m0_l)
    return y, c_out, n_out, m_out[:, 0, H:2 * H]


def _mlstm_short_kernel(nb, T, dqk, dv,
                        p_ref, gp_ref, bias_ref, hg_ref, c0_ref, n0_ref, m0_ref,
                        y_ref, c_ref, n_ref, m_ref):
    H = M_HEADS
    R = nb * T
    PAIR = 2 * T
    s1 = H * dqk
    scale = dqk ** -0.5
    assert T % 8 == 0 and nb % 2 == 0

    r = lax.broadcasted_iota(jnp.int32, (R, R), 0)
    c = lax.broadcasted_iota(jnp.int32, (R, R), 1)
    causal = ((r // T) == (c // T)) & (c <= r)
    tril = causal.astype(BF16)
    seq_cols = (lax.broadcasted_iota(jnp.int32, (nb, 1, R), 2) // T
                == lax.broadcasted_iota(jnp.int32, (nb, 1, R), 0)).astype(BF16)
    first_of_pair = lax.broadcasted_iota(jnp.int32, (PAIR, 1), 0) < T

    def per_seq(x):
        return x.reshape(nb, T, x.shape[-1])

    def seq_rows(x3):
        return jnp.broadcast_to(x3, (nb, T, x3.shape[-1])).reshape(R, x3.shape[-1])

    capped = GATE_CAP * jnp.tanh((gp_ref[...] + bias_ref[...]) / GATE_CAP)
    cum = _cumsum_rows_exact(tril, _log_sigmoid(capped))
    ig_on_f = pltpu.roll(capped, H, axis=1)
    m_prev = m0_ref[...]
    b_last = seq_rows(per_seq(cum)[:, T - 1:T, :])
    g_all = b_last - cum + ig_on_f
    m_new = jnp.maximum(b_last + m_prev, seq_rows(jnp.max(per_seq(g_all), axis=1, keepdims=True)))
    decay_all = jnp.exp(b_last + m_prev - m_new)
    wk_all = jnp.exp(g_all - m_new) * scale
    inter_all = cum + m_prev
    rows_t = (ig_on_f - cum).T

    for h in range(H):
        lane = H + h
        q = p_ref[:, h * dqk:(h + 1) * dqk]
        k = p_ref[:, s1 + h * dqk:s1 + (h + 1) * dqk]
        v = p_ref[:, 2 * s1 + h * dv:2 * s1 + (h + 1) * dv]
        o = p_ref[:, 2 * s1 + H * dv + h * dv:2 * s1 + H * dv + (h + 1) * dv]
        kf = k.astype(F32)

        dlog = jnp.where(causal, cum[:, lane:lane + 1] + rows_t[lane:lane + 1, :], -jnp.inf)
        inter = inter_all[:, lane:lane + 1]
        m_t = jnp.maximum(inter, jnp.max(dlog, axis=1, keepdims=True))
        w_intra = jnp.exp(dlog - m_t)
        w_inter = jnp.exp(inter - m_t)
        sc = _dot_nt(q, k) * scale * w_intra
        num_intra = jnp.dot(sc.astype(BF16), v, preferred_element_type=F32)
        n_prev = n0_ref[h]
        qn = jnp.sum(q.astype(F32) * n_prev, axis=1, keepdims=True)
        den = jnp.sum(sc, axis=1, keepdims=True) + w_inter * qn
        inv = 1.0 / jnp.maximum(jnp.abs(den), jnp.exp(-m_t))
        hg = hg_ref[:, h * dv:(h + 1) * dv]

        w_k = wk_all[:, lane:lane + 1]
        k_t = kf.T.astype(BF16)
        stacked = (k_t[None] * seq_cols).reshape(nb * dqk, R)
        upd = jnp.dot(stacked, (w_k * v.astype(F32)).astype(BF16), preferred_element_type=F32)
        n_ref[h] = (decay_all[:, lane:lane + 1] * n_prev
                    + seq_rows(jnp.sum(per_seq(w_k * kf), axis=1, keepdims=True)))

        for pr in range(nb // 2):
            rows = slice(pr * PAIR, (pr + 1) * PAIR)
            lhs = p_ref[rows, h * dqk:(h + 1) * dqk]
            qc = []
            for b in (2 * pr, 2 * pr + 1):
                c_prev = c0_ref[0, b, h]
                qc.append(jnp.dot(lhs, c_prev.astype(BF16), preferred_element_type=F32))
                c_ref[0, b, h] = decay_all[b * T:b * T + 1, lane:lane + 1] * c_prev + upd[b * dqk:(b + 1) * dqk, :]
            num = num_intra[rows, :] + w_inter[rows, :] * jnp.where(first_of_pair, qc[0], qc[1])
            hc = num * inv[rows, :]
            hn = hc * lax.rsqrt(jnp.mean(hc * hc, axis=1, keepdims=True) + EPS) * hg
            y_ref[rows, h * dv:(h + 1) * dv] = (hn * jax.nn.sigmoid(o[rows, :].astype(F32))).astype(y_ref.dtype)

    m_ref[...] = m_new


def mlstm_short_mixer(proj, gate_pre, bias, head_g, c0, n0, m0, *, row0, batch, seq, nb):
    H = M_HEADS
    dqk, dv = c0.shape[-2], c0.shape[-1]
    R = nb * seq
    lanes = gate_pre.shape[1]
    assert batch % nb == 0 and row0 % R == 0 and lanes >= 2 * H
    nblocks = batch // nb
    blk0 = row0 // R
    bias_l = jnp.pad(bias.reshape(1, 2 * H), ((0, 0), (0, lanes - 2 * H)))
    m0_l = jnp.repeat(jnp.pad(m0, ((0, 0), (H, lanes - 2 * H))), seq, axis=0)
    n0_t = jnp.repeat(jnp.swapaxes(n0, 0, 1), seq, axis=1)
    y, c_out, n_out, m_out = pl.pallas_call(
        functools.partial(_mlstm_short_kernel, nb, seq, dqk, dv),
        grid=(nblocks,),
        in_specs=[
            pl.BlockSpec((R, proj.shape[1]), lambda i: (blk0 + i, 0)),
            pl.BlockSpec((R, lanes), lambda i: (blk0 + i, 0)),
            pl.BlockSpec((1, lanes), lambda i: (0, 0)),
            pl.BlockSpec((1, H * dv), lambda i: (0, 0)),
            pl.BlockSpec((1, nb, H, dqk, dv), lambda i: (i, 0, 0, 0, 0)),
            pl.BlockSpec((H, R, dqk), lambda i: (0, i, 0)),
            pl.BlockSpec((R, lanes), lambda i: (i, 0)),
        ],
        out_specs=[
            pl.BlockSpec((R, H * dv), lambda i: (i, 0)),
            pl.BlockSpec((1, nb, H, dqk, dv), lambda i: (i, 0, 0, 0, 0)),
            pl.BlockSpec((H, R, dqk), lambda i: (0, i, 0)),
            pl.BlockSpec((R, lanes), lambda i: (i, 0)),
        ],
        out_shape=[
            jax.ShapeDtypeStruct((batch * seq, H * dv), BF16),
            jax.ShapeDtypeStruct((nblocks, nb, H, dqk, dv), F32),
            jax.ShapeDtypeStruct(n0_t.shape, F32),
            jax.ShapeDtypeStruct(m0_l.shape, F32),
        ],
        compiler_params=_params("arbitrary"),
        name="mlstm_short_mixer",
    )(proj, gate_pre, bias_l, head_g.reshape(1, H * dv), c0.reshape(nblocks, nb, H, dqk, dv), n0_t, m0_l)
    return (y, c_out.reshape(batch, H, dqk, dv), jnp.swapaxes(n_out[:, ::seq, :], 0, 1), m_out[::seq, H:2 * H])


def _hgrn_short_kernel(nb, T, heads, dk, dv,
                       p_ref, lb_ref, gn_ref, s0_ref, y_ref, s_ref,
                       qd_scr, kd_scr, qe_scr, ke_scr, dec_scr):
    R = nb * T
    fdim = heads * dk
    scale = dk ** -0.5
    W = 2 * dk
    PAIR = 2 * T
    assert T % 8 == 0 and T <= H_SUB and nb % 2 == 0

    r = lax.broadcasted_iota(jnp.int32, (R, R), 0)
    c = lax.broadcasted_iota(jnp.int32, (R, R), 1)
    causal = ((r // T) == (c // T)) & (c <= r)
    tril = causal.astype(BF16)
    seq_cols = (lax.broadcasted_iota(jnp.int32, (nb, 1, R), 2) // T
                == lax.broadcasted_iota(jnp.int32, (nb, 1, R), 0)).astype(BF16)
    first_of_pair = lax.broadcasted_iota(jnp.int32, (PAIR, 1), 0) < T

    def minus_ref_row(a, row):
        a3 = a.reshape(nb, T, a.shape[-1])
        return (a3 - a3[:, row:row + 1, :]).reshape(a.shape)

    for g in range(fdim // W):
        sl = slice(g * W, (g + 1) * W)
        q = p_ref[:, sl].astype(F32) * scale
        fpre = p_ref[:, fdim + g * W:fdim + (g + 1) * W].astype(F32)
        lb = lb_ref[:, sl]
        f = lb + (1.0 - lb) * jax.nn.sigmoid(fpre)
        k = 1.0 - f
        a = _cumsum_rows_exact(tril, jnp.log(f))
        d = minus_ref_row(a, T // 2)
        to_last = -minus_ref_row(a, T - 1)
        qd_scr[:, sl] = (q * jnp.exp(d)).astype(BF16)
        kd_scr[:, sl] = (k * jnp.exp(-d)).astype(BF16)
        qe_scr[:, sl] = (q * jnp.exp(a)).astype(BF16)
        ke_scr[:, sl] = k * jnp.exp(to_last)
        dec_scr[:, sl] = jnp.exp(a + to_last)

    for h in range(heads):
        sl = slice(h * dk, (h + 1) * dk)
        iv = p_ref[:, 2 * fdim + h * dv:2 * fdim + (h + 1) * dv]
        sc = jnp.where(causal, _dot_nt(qd_scr[:, sl], kd_scr[:, sl]), 0.0)
        oc = jnp.dot(sc.astype(BF16), iv, preferred_element_type=F32)
        ke_t = ke_scr[:, sl].T.astype(BF16)
        dec_t = dec_scr[:, sl].T
        stacked = (ke_t[None] * seq_cols).reshape(nb * dk, R)
        upd = jnp.dot(stacked, iv, preferred_element_type=F32)
        gn = gn_ref[:, h * dv:(h + 1) * dv]
        for pr in range(nb // 2):
            rows = slice(pr * PAIR, (pr + 1) * PAIR)
            lhs = qe_scr[rows, sl]
            inter = []
            for b in (2 * pr, 2 * pr + 1):
                s_prev = s0_ref[0, b, h]
                inter.append(jnp.dot(lhs, s_prev.astype(BF16), preferred_element_type=F32))
                s_ref[0, b, h] = dec_t[:, b * T:b * T + 1] * s_prev + upd[b * dk:(b + 1) * dk, :]
            o2 = oc[rows, :] + jnp.where(first_of_pair, inter[0], inter[1])
            gate = p_ref[rows, 2 * fdim + heads * dv + h * dv:2 * fdim + heads * dv + (h + 1) * dv].astype(F32)
            on = o2 * lax.rsqrt(jnp.mean(o2 * o2, axis=1, keepdims=True) + EPS) * gn
            y_ref[rows, h * dv:(h + 1) * dv] = (on * _silu(gate)).astype(y_ref.dtype)


def hgrn_short_mixer(proj, lb, gn_g, s0, *, row0, batch, seq, nb):
    heads, dk, dv = s0.shape[-3], s0.shape[-2], s0.shape[-1]
    R = nb * seq
    assert batch % nb == 0 and row0 % R == 0
    nblocks = batch // nb
    blk0 = row0 // R
    fdim = heads * dk
    s0 = s0.reshape(nblocks, nb, heads, dk, dv)
    y, s_out = pl.pallas_call(
        functools.partial(_hgrn_short_kernel, nb, seq, heads, dk, dv),
        grid=(nblocks,),
        in_specs=[
            pl.BlockSpec((R, proj.shape[1]), lambda i: (blk0 + i, 0)),
            pl.BlockSpec((1, fdim), lambda i: (0, 0)),
            pl.BlockSpec((1, heads * dv), lambda i: (0, 0)),
            pl.BlockSpec((1, nb, heads, dk, dv), lambda i: (i, 0, 0, 0, 0)),
        ],
        out_specs=[
            pl.BlockSpec((R, heads * dv), lambda i: (i, 0)),
            pl.BlockSpec((1, nb, heads, dk, dv), lambda i: (i, 0, 0, 0, 0)),
        ],
        out_shape=[
            jax.ShapeDtypeStruct((batch * seq, heads * dv), BF16),
            jax.ShapeDtypeStruct(s0.shape, F32),
        ],
        scratch_shapes=[
            pltpu.VMEM((R, fdim), BF16), pltpu.VMEM((R, fdim), BF16), pltpu.VMEM((R, fdim), BF16),
            pltpu.VMEM((R, fdim), F32), pltpu.VMEM((R, fdim), F32),
        ],
        compiler_params=_params("arbitrary"),
        name="hgrn_short_mixer",
    )(proj, lb.reshape(1, fdim), gn_g.reshape(1, heads * dv), s0)
    return y, s_out.reshape(batch, heads, dk, dv)


def _hgrn_long_kernel(R, heads, dk, dv, levels,
                      p_ref, lb_ref, gn_ref, s0_ref, y_ref, s_ref,
                      st_scr, dec_scr, qd_scr, kd_scr, qe_scr, ke_scr, ql_scr, kl_scr, k_scr, a_scr,
                      kt_scr, sb_scr, ivt_scr, sc_scr, oc_scr):
    fdim = heads * dk
    eps_unscaled = EPS * dk
    W = 2 * dk
    last_step = pl.num_programs(1) - 1

    @pl.when(pl.program_id(1) == 0)
    def _():
        for h in range(heads):
            st_scr[h] = s0_ref[0, h].T

    r = lax.broadcasted_iota(jnp.int32, (R, R), 0)
    c = lax.broadcasted_iota(jnp.int32, (R, R), 1)
    tril = (c <= r).astype(BF16)
    diag_mask = ((r // H_SUB) == (c // H_SUB)) & (c <= r)
    level_masks = [((r // G) == (c // G)) & ((r % G) >= G // 2) & ((c % G) < G // 2) for G in levels]

    NSB = R // H_SUB
    zeros_sub = jnp.zeros((H_SUB, dk), BF16)

    for g in range(fdim // W):
        fpre = p_ref[:, fdim + g * W:fdim + (g + 1) * W].astype(F32)
        lb = lb_ref[:, g * W:(g + 1) * W]
        f = lb + (1.0 - lb) * jax.nn.sigmoid(fpre)
        k = 1.0 - f
        a = _cumsum_rows_exact(tril, jnp.log(f))
        for j in range(W // dk):
            k_scr[g * (W // dk) + j] = k[:, j * dk:(j + 1) * dk]
            a_scr[g * (W // dk) + j] = a[:, j * dk:(j + 1) * dk]

    def rows_of(ref, h, start, count, stride):
        if count == 1:
            return ref[h, start:start + 1, :]
        return ref.at[h][pl.ds(start, count, stride=stride), :]

    for h in range(heads):
        sl = slice(h * dk, (h + 1) * dk)
        a_last = a_scr[h, R - 1:R, :]
        dec_scr[:, sl] = jnp.exp(a_last)
        mids = rows_of(a_scr, h, H_SUB // 2, NSB, H_SUB)
        c_qe = jnp.exp(mids)
        c_ke = jnp.exp(a_last - mids)
        c_lv = []
        for G in levels:
            per = G // H_SUB
            bnd = rows_of(a_scr, h, G // 2 - 1, R // G, G)
            consts = []
            for o in range(per):
                mid_o = rows_of(a_scr, h, o * H_SUB + H_SUB // 2, R // G, G)
                consts.append(jnp.exp(mid_o - bnd) if o >= per // 2 else jnp.exp(bnd - mid_o))
            c_lv.append(consts)
        for s in range(NSB):
            rows = slice(s * H_SUB, (s + 1) * H_SUB)
            d = a_scr[h, rows, :] - mids[s:s + 1, :]
            qd = p_ref[rows, sl].astype(F32) * jnp.exp(d)
            kd = k_scr[h, rows, :] * jnp.exp(-d)
            qd_scr[rows, sl] = qd.astype(BF16)
            kd_scr[rows, sl] = kd.astype(BF16)
            qe_scr[rows, sl] = (qd * c_qe[s:s + 1, :]).astype(BF16)
            ke_scr[rows, sl] = (kd * c_ke[s:s + 1, :]).astype(BF16)
            for li, G in enumerate(levels):
                per = G // H_SUB
                o, gi = s % per, s // per
                const = c_lv[li][o][gi:gi + 1, :]
                if o >= per // 2:
                    ql_scr[li, rows, sl] = (qd * const).astype(BF16)
                    kl_scr[li, rows, sl] = zeros_sub
                else:
                    kl_scr[li, rows, sl] = (kd * const).astype(BF16)
                    ql_scr[li, rows, sl] = zeros_sub

    def paired(lhs_a, lhs_b, rhs_ab):
        ca = rhs_ab.shape[1] // 2
        return (jnp.dot(lhs_a, rhs_ab[:, :ca], preferred_element_type=F32),
                jnp.dot(lhs_b, rhs_ab[:, ca:], preferred_element_type=F32))

    assert dk == dv
    npairs = heads // 2
    lanes_a = [slice(2 * p * dk, (2 * p + 1) * dk) for p in range(npairs)]
    lanes_b = [slice((2 * p + 1) * dk, (2 * p + 2) * dk) for p in range(npairs)]
    lanes_ab = [slice(2 * p * dk, (2 * p + 2) * dk) for p in range(npairs)]
    iv_of = lambda p: p_ref[:, 2 * fdim + 2 * p * dv:2 * fdim + (2 * p + 2) * dv]

    for p in range(npairs):
        kt_scr[p, 0, :, :R] = kd_scr[:, lanes_a[p]].T
        kt_scr[p, 0, :, R:] = kd_scr[:, lanes_b[p]].T
        for li in range(len(levels)):
            kt_scr[p, 1 + li, :, :R] = kl_scr[li, :, lanes_a[p]].T
            kt_scr[p, 1 + li, :, R:] = kl_scr[li, :, lanes_b[p]].T
        sb_scr[p, :, :dv] = st_scr[2 * p].astype(BF16).T
        sb_scr[p, :, dv:] = st_scr[2 * p + 1].astype(BF16).T
        ivt_scr[p] = iv_of(p).T
    for p in range(npairs):
        da, db = paired(qd_scr[:, lanes_a[p]], qd_scr[:, lanes_b[p]], kt_scr[p, 0])
        sc_a, sc_b = jnp.where(diag_mask, da, 0.0), jnp.where(diag_mask, db, 0.0)
        for li in range(len(levels)):
            la, lb_ = paired(ql_scr[li, :, lanes_a[p]], ql_scr[li, :, lanes_b[p]], kt_scr[p, 1 + li])
            sc_a, sc_b = jnp.where(level_masks[li], la, sc_a), jnp.where(level_masks[li], lb_, sc_b)
        sc_scr[2 * p] = sc_a.astype(BF16)
        sc_scr[2 * p + 1] = sc_b.astype(BF16)
    for p in range(npairs):
        intra_a, intra_b = paired(sc_scr[2 * p], sc_scr[2 * p + 1], iv_of(p))
        inter_a, inter_b = paired(qe_scr[:, lanes_a[p]], qe_scr[:, lanes_b[p]], sb_scr[p])
        oc_scr[:, lanes_a[p]] = intra_a + inter_a
        oc_scr[:, lanes_b[p]] = intra_b + inter_b
        upd_a = jnp.dot(ivt_scr[p, :dv, :], ke_scr[:, lanes_a[p]], preferred_element_type=F32)
        upd_b = jnp.dot(ivt_scr[p, dv:, :], ke_scr[:, lanes_b[p]], preferred_element_type=F32)
        st_scr[2 * p] = st_scr[2 * p] * dec_scr[:, lanes_a[p]] + upd_a
        st_scr[2 * p + 1] = st_scr[2 * p + 1] * dec_scr[:, lanes_b[p]] + upd_b
    for h in range(heads):
        sl = slice(h * dv, (h + 1) * dv)
        oc = oc_scr[:, sl]
        gate = p_ref[:, 2 * fdim + heads * dv + h * dv:2 * fdim + heads * dv + (h + 1) * dv].astype(F32)
        on = oc * lax.rsqrt(jnp.mean(oc * oc, axis=1, keepdims=True) + eps_unscaled) * gn_ref[:, sl]
        y_ref[:, sl] = (on * _silu(gate)).astype(y_ref.dtype)

    @pl.when(pl.program_id(1) == last_step)
    def _():
        for h in range(heads):
            s_ref[0, h] = st_scr[h].T


def hgrn_long_mixer(proj, lb, gn_g, s0, *, row0, batch, seq, rows):
    heads, dk, dv = s0.shape[-3], s0.shape[-2], s0.shape[-1]
    R = rows
    assert seq % R == 0 and R % (2 * H_SUB) == 0 and row0 % R == 0
    levels = []
    G = 2 * H_SUB
    while G <= R:
        levels.append(G)
        G *= 2
    assert levels[-1] == R
    nsteps = seq // R
    blk0 = row0 // R
    fdim = heads * dk
    wide = lambda: pltpu.VMEM((R, fdim), BF16)
    return pl.pallas_call(
        functools.partial(_hgrn_long_kernel, R, heads, dk, dv, tuple(levels)),
        grid=(batch, nsteps),
        in_specs=[
            pl.BlockSpec((R, proj.shape[1]), lambda i, c: (blk0 + i * nsteps + c, 0)),
            pl.BlockSpec((1, fdim), lambda i, c: (0, 0)),
            pl.BlockSpec((1, heads * dv), lambda i, c: (0, 0)),
            pl.BlockSpec((1, heads, dk, dv), lambda i, c: (i, 0, 0, 0)),
        ],
        out_specs=[
            pl.BlockSpec((R, heads * dv), lambda i, c: (i * nsteps + c, 0)),
            pl.BlockSpec((1, heads, dk, dv), lambda i, c: (i, 0, 0, 0)),
        ],
        out_shape=[
            jax.ShapeDtypeStruct((batch * seq, heads * dv), BF16),
            jax.ShapeDtypeStruct(s0.shape, F32),
        ],
        scratch_shapes=[
            pltpu.VMEM((heads, dv, dk), F32),
            pltpu.VMEM((1, fdim), F32),
            wide(), wide(), wide(), wide(),
            pltpu.VMEM((len(levels), R, fdim), BF16),
            pltpu.VMEM((len(levels), R, fdim), BF16),
            pltpu.VMEM((heads, R, dk), F32),
            pltpu.VMEM((heads, R, dk), F32),
            pltpu.VMEM((heads // 2, 1 + len(levels), dk, 2 * R), BF16),
            pltpu.VMEM((heads // 2, dk, 2 * dv), BF16),
            pltpu.VMEM((heads // 2, 2 * dv, R), BF16),
            pltpu.VMEM((heads, R, R), BF16),
            pltpu.VMEM((R, heads * dv), F32),
        ],
        compiler_params=_params("arbitrary", "arbitrary"),
        name="hgrn_long_mixer",
    )(proj, lb.reshape(1, fdim), gn_g.reshape(1, heads * dv), s0)


def kernel(x_prompt, x_sample, state_mlstm_C, state_mlstm_n, state_mlstm_m, state_hgrn_S,
           norm_mix_g, norm_ffn_g, norm_final_g, mlstm_w_in, mlstm_b_gates, mlstm_head_norm_g,
           mlstm_w_out, hgrn_w_in, hgrn_lower_bounds, hgrn_g_norm_g, hgrn_w_out, ffn_w_up, ffn_w_down):
    bp, tp, d = x_prompt.shape
    bs, ts, _ = x_sample.shape
    np_, ns_ = bp * tp, bs * ts
    depth = norm_mix_g.shape[0]
    H = M_HEADS
    dqk, dv = state_mlstm_C.shape[-2], state_mlstm_C.shape[-1]
    hh, hdk, hdv = state_hgrn_S.shape[-3], state_hgrn_S.shape[-2], state_hgrn_S.shape[-1]

    xs = (x_prompt.reshape(np_, d), x_sample.reshape(ns_, d))

    lb_all = jnp.cumsum(jax.nn.softmax(hgrn_lower_bounds.astype(F32), axis=0), axis=0)
    lb_all = lb_all - lb_all[0]

    Cp, Np, Mp, Sp, Cs, Ns, Ms, Ss = [], [], [], [], [], [], [], []
    for i in range(depth):
        j = i // 2
        if i % 2 == 0:
            n_main = 2 * H * dqk + 2 * H * dv
            w_in_t = jnp.swapaxes(mlstm_w_in, 1, 2)
            w_gates = jnp.pad(w_in_t[j, n_main:], ((0, LANES - 2 * H), (0, 0)))
            proj, gate_pre = norm_matmul(xs, norm_mix_g[i], w_in_t, j, n_main, tm=PROJ_ROWS, tn=PROJ_COLS,
                                         out_dtype=BF16, w_extra=w_gates, w_rows_are_outputs=True)
            args = (proj, gate_pre, mlstm_b_gates[j], mlstm_head_norm_g[j])
            yp, c_p, n_p, m_p = mlstm_long_mixer(
                *args, jnp.zeros((bp, H, dqk, dv), F32), jnp.zeros((bp, H, dqk), F32), jnp.zeros((bp, H), F32),
                row0=0, batch=bp, seq=tp, rows=MLSTM_STEP_ROWS)
            ysm, c_s, n_s, m_s = mlstm_short_mixer(
                *args, state_mlstm_C[j], state_mlstm_n[j], state_mlstm_m[j],
                row0=np_, batch=bs, seq=ts, nb=SAMPLE_SEQS)
            Cp.append(c_p); Np.append(n_p); Mp.append(m_p)
            Cs.append(c_s); Ns.append(n_s); Ms.append(m_s)
            w_out = mlstm_w_out
        else:
            proj = norm_matmul(xs, norm_mix_g[i], hgrn_w_in, j, hgrn_w_in.shape[-1],
                               tm=PROJ_ROWS, tn=PROJ_COLS, out_dtype=BF16)
            args = (proj, lb_all[i], hgrn_g_norm_g[j])
            yp, s_p = hgrn_long_mixer(*args, jnp.zeros((bp, hh, hdk, hdv), F32),
                                      row0=0, batch=bp, seq=tp, rows=HGRN_STEP_ROWS)
            ysm, s_s = hgrn_short_mixer(*args, state_hgrn_S[j], row0=np_, batch=bs, seq=ts, nb=SAMPLE_SEQS)
            Sp.append(s_p); Ss.append(s_s)
            w_out = hgrn_w_out
        x = matmul_residual((yp, ysm), w_out, j, xs, tm=OUT_ROWS, tn=d)
        act = norm_swiglu(x, norm_ffn_g[i], ffn_w_up, i, tm=PROJ_ROWS, tn=SWIGLU_COLS)
        x = matmul_residual((act,), ffn_w_down, i, (x,), tm=DOWN_ROWS, tn=DOWN_COLS)
        xs = (x,)

    y_prompt = final_norm(x, norm_final_g, row0=0, nrows=np_, tm=NORM_ROWS).reshape(bp, tp, d)
    y_sample = final_norm(x, norm_final_g, row0=np_, nrows=ns_, tm=NORM_ROWS).reshape(bs, ts, d)
    cat = lambda parts: jnp.stack(parts) if len(parts) > 1 else parts[0][None]
    return (y_prompt, y_sample, cat(Cp), cat(Np), cat(Mp), cat(Sp), cat(Cs), cat(Ns), cat(Ms), cat(Ss))
```

```python
import functools

import jax
import jax.numpy as jnp
from jax import lax
from jax.experimental import pallas as pl
from jax.experimental.pallas import tpu as pltpu

F32 = jnp.float32
BF16 = jnp.bfloat16

EPS = 1e-6
GATE_CAP = 15.0
M_HEADS = 8
H_SUB = 16

LANES = 128

PROJ_ROWS = 1024
PROJ_COLS = 1024
SWIGLU_COLS = 512
OUT_ROWS = 256
DOWN_ROWS = 512
DOWN_COLS = 512
NORM_ROWS = 512
MLSTM_STEP_ROWS = 256
HGRN_STEP_ROWS = 128
SAMPLE_SEQS = 8

VMEM_LIMIT_BYTES = 56 * 1024 * 1024


def _params(*sem):
    return pltpu.CompilerParams(dimension_semantics=sem, vmem_limit_bytes=VMEM_LIMIT_BYTES)


def _row_tiles(xs, tm):
    for x in xs:
        assert x.shape[0] % tm == 0, (x.shape, tm)
    return tuple(x.shape[0] // tm for x in xs)


def _clamped_row_tile(first, count, row_of, col_of, *ids):
    return (jnp.clip(row_of(*ids) - first, 0, count - 1), col_of(*ids))


def _row_specs(tiles, block, row_of, col_of, single_tile_unbuffered=False):
    specs, first = [], 0
    for count in tiles:
        mode = dict(pipeline_mode=pl.Buffered(1)) if single_tile_unbuffered and count == 1 else {}
        specs.append(pl.BlockSpec(block, functools.partial(_clamped_row_tile, first, count, row_of, col_of), **mode))
        first += count
    return specs


def _on_owner(refs, tiles, i, fn):
    if len(refs) == 1:
        fn(refs[0])
        return
    first = 0
    for ref, count in zip(refs, tiles):
        pl.when((i >= first) & (i < first + count))(functools.partial(fn, ref))
        first += count


def _rms_normed(x, g):
    ms = jnp.mean(x * x, axis=-1, keepdims=True)
    return x * lax.rsqrt(ms + EPS) * g


def _norm_matmul_kernel(tiles, with_extra, w_rows_are_outputs, *refs):
    nx = len(tiles)
    x_refs, g_ref, w_ref = refs[:nx], refs[nx], refs[nx + 1]
    if with_extra:
        we_ref, o_ref, oe_ref, h_scr = refs[nx + 2:]
    else:
        o_ref, h_scr = refs[nx + 2:]
    matmul = _dot_nt if w_rows_are_outputs else functools.partial(jnp.dot, preferred_element_type=F32)

    @pl.when(pl.program_id(1) == 0)
    def _():
        def build(x_ref):
            h_scr[...] = _rms_normed(x_ref[...], g_ref[...]).astype(BF16)
            if with_extra:
                oe_ref[...] = matmul(h_scr[...], we_ref[...].astype(BF16))
        _on_owner(x_refs, tiles, pl.program_id(0), build)

    o_ref[...] = matmul(h_scr[...], w_ref[...].astype(BF16)).astype(o_ref.dtype)


def norm_matmul(xs, g, w, layer, n_cols, *, tm, tn, out_dtype, w_extra=None, w_rows_are_outputs=False):
    d = xs[0].shape[1]
    tiles = _row_tiles(xs, tm)
    m = tm * sum(tiles)
    row_of, col0 = (lambda i, j: i), (lambda i, j: 0)
    w_spec = (pl.BlockSpec((None, tn, d), lambda i, j: (layer, j, 0)) if w_rows_are_outputs
              else pl.BlockSpec((None, d, tn), lambda i, j: (layer, 0, j)))
    in_specs = (_row_specs(tiles, (tm, d), row_of, col0, single_tile_unbuffered=len(tiles) > 1)
                + [pl.BlockSpec((1, d), lambda i, j: (0, 0)), w_spec])
    out_specs = [pl.BlockSpec((tm, tn), lambda i, j: (i, j))]
    out_shape = [jax.ShapeDtypeStruct((m, n_cols), out_dtype)]
    args = list(xs) + [g.reshape(1, d), w]
    if w_extra is not None:
        ne = w_extra.shape[0 if w_rows_are_outputs else 1]
        in_specs.append(pl.BlockSpec(w_extra.shape, lambda i, j: (0, 0)))
        out_specs.append(pl.BlockSpec((tm, ne), lambda i, j: (i, 0)))
        out_shape.append(jax.ShapeDtypeStruct((m, ne), F32))
        args.append(w_extra)
    outs = pl.pallas_call(
        functools.partial(_norm_matmul_kernel, tiles, w_extra is not None, w_rows_are_outputs),
        grid=(m // tm, n_cols // tn),
        in_specs=in_specs,
        out_specs=out_specs,
        out_shape=out_shape,
        scratch_shapes=[pltpu.VMEM((tm, d), BF16)],
        compiler_params=_params("arbitrary", "arbitrary"),
        name="norm_matmul",
    )(*args)
    return outs if w_extra is not None else outs[0]


def _norm_swiglu_kernel(x_ref, g_ref, wa_ref, wu_ref, o_ref, h_scr):
    @pl.when(pl.program_id(1) == 0)
    def _():
        h_scr[...] = _rms_normed(x_ref[...], g_ref[...]).astype(BF16)

    h = h_scr[...]
    a = jnp.dot(h, wa_ref[...].astype(BF16), preferred_element_type=F32)
    u = jnp.dot(h, wu_ref[...].astype(BF16), preferred_element_type=F32)
    o_ref[...] = (a * jax.nn.sigmoid(a) * u).astype(o_ref.dtype)


def norm_swiglu(x, g, w_up, layer, *, tm, tn):
    m, d = x.shape
    ff = w_up.shape[-1] // 2
    nj = ff // tn
    return pl.pallas_call(
        _norm_swiglu_kernel,
        grid=(m // tm, nj),
        in_specs=[
            pl.BlockSpec((tm, d), lambda i, j: (i, 0)),
            pl.BlockSpec((1, d), lambda i, j: (0, 0)),
            pl.BlockSpec((None, d, tn), lambda i, j: (layer, 0, j)),
            pl.BlockSpec((None, d, tn), lambda i, j: (layer, 0, j + nj)),
        ],
        out_specs=pl.BlockSpec((tm, tn), lambda i, j: (i, j)),
        out_shape=jax.ShapeDtypeStruct((m, ff), BF16),
        scratch_shapes=[pltpu.VMEM((tm, d), BF16)],
        compiler_params=_params("arbitrary", "arbitrary"),
        name="norm_swiglu",
    )(x, g.reshape(1, d), w_up, w_up)


def _matmul_residual_kernel(y_tiles, r_tiles, *refs):
    ny, nr = len(y_tiles), len(r_tiles)
    y_refs, w_ref, r_refs = refs[:ny], refs[ny], refs[ny + 1:ny + 1 + nr]
    o_ref, w_scr = refs[ny + 1 + nr:]
    i = pl.program_id(1)

    @pl.when(i == 0)
    def _():
        w_scr[...] = w_ref[...].astype(BF16)

    def product(y_ref):
        o_ref[...] = jnp.dot(y_ref[...], w_scr[...], preferred_element_type=F32)

    def add_residual(r_ref):
        o_ref[...] = o_ref[...] + r_ref[...]

    _on_owner(y_refs, y_tiles, i, product)
    _on_owner(r_refs, r_tiles, i, add_residual)


def matmul_residual(ys, w, layer, rs, *, tm, tn):
    k, n = w.shape[-2], w.shape[-1]
    y_tiles, r_tiles = _row_tiles(ys, tm), _row_tiles(rs, tm)
    assert sum(y_tiles) == sum(r_tiles)
    m = tm * sum(y_tiles)
    row_of = lambda j, i: i
    w_mode = dict(pipeline_mode=pl.Buffered(1)) if tn == n else {}
    in_specs = (_row_specs(y_tiles, (tm, k), row_of, lambda j, i: 0)
                + [pl.BlockSpec((None, k, tn), lambda j, i: (layer, 0, j), **w_mode)]
                + _row_specs(r_tiles, (tm, tn), row_of, lambda j, i: j))
    return pl.pallas_call(
        functools.partial(_matmul_residual_kernel, y_tiles, r_tiles),
        grid=(n // tn, m // tm),
        in_specs=in_specs,
        out_specs=pl.BlockSpec((tm, tn), lambda j, i: (i, j)),
        out_shape=jax.ShapeDtypeStruct((m, n), F32),
        scratch_shapes=[pltpu.VMEM((k, tn), BF16)],
        compiler_params=_params("arbitrary", "arbitrary"),
        name="matmul_residual",
    )(*ys, w, *rs)


def _final_norm_kernel(x_ref, g_ref, o_ref):
    o_ref[...] = _rms_normed(x_ref[...], g_ref[...])


def final_norm(x, g, *, row0, nrows, tm):
    d = x.shape[1]
    assert row0 % tm == 0 and nrows % tm == 0
    return pl.pallas_call(
        _final_norm_kernel,
        grid=(nrows // tm,),
        in_specs=[pl.BlockSpec((tm, d), lambda i: (row0 // tm + i, 0)), pl.BlockSpec((1, d), lambda i: (0, 0))],
        out_specs=pl.BlockSpec((tm, d), lambda i: (i, 0)),
        out_shape=jax.ShapeDtypeStruct((nrows, d), F32),
        compiler_params=_params("arbitrary"),
        name="final_norm",
    )(x, g.reshape(1, d))


def _dot_nt(a, b):
    return lax.dot_general(a, b, (((1,), (1,)), ((), ())), preferred_element_type=F32)


def _dot_tn(a, b):
    return lax.dot_general(a, b, (((0,), (0,)), ((), ())), preferred_element_type=F32)


def _log_sigmoid(x):
    return jnp.minimum(x, 0.0) - jnp.log1p(jnp.exp(-jnp.abs(x)))


def _silu(x):
    return 0.5 * x * (1.0 + jnp.tanh(0.5 * x))


def _cumsum_rows_exact(tril, x):
    hi = x.astype(BF16)
    r1 = x - hi.astype(F32)
    mid = r1.astype(BF16)
    lo = (r1 - mid.astype(F32)).astype(BF16)
    return (jnp.dot(tril, hi, preferred_element_type=F32) + jnp.dot(tril, mid, preferred_element_type=F32)
            + jnp.dot(tril, lo, preferred_element_type=F32))


def _mlstm_long_kernel(R, dqk, dv,
                       p_ref, gp_ref, bias_ref, hg_ref, c0_ref, n0_ref, m0_ref,
                       y_ref, c_ref, n_ref, m_ref, kt_scr):
    H = M_HEADS
    s1 = H * dqk
    scale = dqk ** -0.5

    @pl.when(pl.program_id(1) == 0)
    def _():
        c_ref[...] = c0_ref[...]
        n_ref[...] = n0_ref[...]
        m_ref[...] = m0_ref[...]

    r = lax.broadcasted_iota(jnp.int32, (R, R), 0)
    c = lax.broadcasted_iota(jnp.int32, (R, R), 1)
    causal = c <= r
    tril = causal.astype(BF16)

    capped = GATE_CAP * jnp.tanh((gp_ref[...] + bias_ref[...]) / GATE_CAP)
    cum = _cumsum_rows_exact(tril, _log_sigmoid(capped))
    ig_on_f = pltpu.roll(capped, H, axis=1)
    m_prev = m_ref[0]
    b_last = cum[R - 1:R, :]
    g_all = b_last - cum + ig_on_f
    m_new = jnp.maximum(b_last + m_prev, jnp.max(g_all, axis=0, keepdims=True))
    decay_all = jnp.exp(b_last + m_prev - m_new)
    wk_all = jnp.exp(g_all - m_new) * scale
    inter_all = cum + m_prev
    rows_t = (ig_on_f - cum).T

    q_cols = [slice(h * dqk, (h + 1) * dqk) for h in range(H)]
    k_cols = [slice(s1 + h * dqk, s1 + (h + 1) * dqk) for h in range(H)]
    v_cols = [slice(2 * s1 + h * dv, 2 * s1 + (h + 1) * dv) for h in range(H)]
    o_cols = [slice(2 * s1 + H * dv + h * dv, 2 * s1 + H * dv + (h + 1) * dv) for h in range(H)]

    for h in range(H):
        lane = H + h
        q = p_ref[:, q_cols[h]]
        c_prev = c_ref[0, h]
        c_bf = c_prev.astype(BF16)
        n_prev = n_ref[0, h:h + 1, :]
        kt_scr[h] = p_ref[:, k_cols[h]].T

        dlog = jnp.where(causal, cum[:, lane:lane + 1] + rows_t[lane:lane + 1, :], -jnp.inf)
        inter = inter_all[:, lane:lane + 1]
        m_t = jnp.maximum(inter, jnp.max(dlog, axis=1, keepdims=True))
        w_inter = jnp.exp(inter - m_t)
        sc = jnp.dot(q, kt_scr[h], preferred_element_type=F32) * scale * jnp.exp(dlog - m_t)
        num = (jnp.dot(sc.astype(BF16), p_ref[:, v_cols[h]], preferred_element_type=F32)
               + w_inter * jnp.dot(q, c_bf, preferred_element_type=F32))
        qn = jnp.sum(q.astype(F32) * n_prev, axis=1, keepdims=True)
        den = jnp.sum(sc, axis=1, keepdims=True) + w_inter * qn
        inv = 1.0 / jnp.maximum(jnp.abs(den), jnp.exp(-m_t))
        norm = inv * lax.rsqrt(inv * inv * jnp.mean(num * num, axis=1, keepdims=True) + EPS)
        o = p_ref[:, o_cols[h]].astype(F32)
        y_ref[:, h * dv:(h + 1) * dv] = (num * norm * hg_ref[:, h * dv:(h + 1) * dv]
                                         * jax.nn.sigmoid(o)).astype(y_ref.dtype)

        k = p_ref[:, k_cols[h]]
        w_k = wk_all[:, lane:lane + 1]
        decay = decay_all[:, lane:lane + 1]
        wv = (w_k * p_ref[:, v_cols[h]].astype(F32)).astype(BF16)
        c_ref[0, h] = decay * c_prev + _dot_tn(k, wv)
        n_ref[0, h:h + 1, :] = decay * n_prev + jnp.sum(w_k * k.astype(F32), axis=0, keepdims=True)

    m_ref[0] = m_new


def mlstm_long_mixer(proj, gate_pre, bias, head_g, c0, n0, m0, *, row0, batch, seq, rows):
    H = M_HEADS
    dqk, dv = c0.shape[-2], c0.shape[-1]
    R = rows
    lanes = gate_pre.shape[1]
    assert seq % R == 0 and row0 % R == 0 and lanes >= 2 * H
    nsteps = seq // R
    blk0 = row0 // R
    rows_of = lambda i, c: (blk0 + i * nsteps + c, 0)
    bias_l = jnp.pad(bias.reshape(1, 2 * H), ((0, 0), (0, lanes - 2 * H)))
    m0_l = jnp.pad(m0, ((0, 0), (H, lanes - 2 * H))).reshape(batch, 1, lanes)
    y, c_out, n_out, m_out = pl.pallas_call(
        functools.partial(_mlstm_long_kernel, R, dqk, dv),
        grid=(batch, nsteps),
        in_specs=[
            pl.BlockSpec((R, proj.shape[1]), rows_of),
            pl.BlockSpec((R, lanes), rows_of),
            pl.BlockSpec((1, lanes), lambda i, c: (0, 0)),
            pl.BlockSpec((1, H * dv), lambda i, c: (0, 0)),
            pl.BlockSpec((1, H, dqk, dv), lambda i, c: (i, 0, 0, 0)),
            pl.BlockSpec((1, H, dqk), lambda i, c: (i, 0, 0)),
            pl.BlockSpec((1, 1, lanes), lambda i, c: (i, 0, 0)),
        ],
        out_specs=[
            pl.BlockSpec((R, H * dv), lambda i, c: (i * nsteps + c, 0)),
            pl.BlockSpec((1, H, dqk, dv), lambda i, c: (i, 0, 0, 0)),
            pl.BlockSpec((1, H, dqk), lambda i, c: (i, 0, 0)),
            pl.BlockSpec((1, 1, lanes), lambda i, c: (i, 0, 0)),
        ],
        out_shape=[
            jax.ShapeDtypeStruct((batch * seq, H * dv), BF16),
            jax.ShapeDtypeStruct(c0.shape, F32),
            jax.ShapeDtypeStruct(n0.shape, F32),
            jax.ShapeDtypeStruct(m0_l.shape, F32),
        ],
        scratch_shapes=[pltpu.VMEM((H, dqk, R), BF16)],
        compiler_params=_params("arbitrary", "arbitrary"),
        name="mlstm_long_mixer",
    )(proj, gate_pre, bias_l, head_g.reshape(1, H * dv), c0, n0, m0_l)
    return y, c_out, n_out, m_out[:, 0, H:2 * H]


def _mlstm_short_kernel(nb, T, dqk, dv,
                        p_ref, gp_ref, bias_ref, hg_ref, c0_ref, n0_ref, m0_ref,
                        y_ref, c_ref, n_ref, m_ref):
    H = M_HEADS
    R = nb * T
    PAIR = 2 * T
    s1 = H * dqk
    scale = dqk ** -0.5
    assert T % 8 == 0 and nb % 2 == 0

    r = lax.broadcasted_iota(jnp.int32, (R, R), 0)
    c = lax.broadcasted_iota(jnp.int32, (R, R), 1)
    causal = ((r // T) == (c // T)) & (c <= r)
    tril = causal.astype(BF16)
    seq_cols = (lax.broadcasted_iota(jnp.int32, (nb, 1, R), 2) // T
                == lax.broadcasted_iota(jnp.int32, (nb, 1, R), 0)).astype(BF16)
    first_of_pair = lax.broadcasted_iota(jnp.int32, (PAIR, 1), 0) < T

    def per_seq(x):
        return x.reshape(nb, T, x.shape[-1])

    def seq_rows(x3):
        return jnp.broadcast_to(x3, (nb, T, x3.shape[-1])).reshape(R, x3.shape[-1])

    capped = GATE_CAP * jnp.tanh((gp_ref[...] + bias_ref[...]) / GATE_CAP)
    cum = _cumsum_rows_exact(tril, _log_sigmoid(capped))
    ig_on_f = pltpu.roll(capped, H, axis=1)
    m_prev = m0_ref[...]
    b_last = seq_rows(per_seq(cum)[:, T - 1:T, :])
    g_all = b_last - cum + ig_on_f
    m_new = jnp.maximum(b_last + m_prev, seq_rows(jnp.max(per_seq(g_all), axis=1, keepdims=True)))
    decay_all = jnp.exp(b_last + m_prev - m_new)
    wk_all = jnp.exp(g_all - m_new) * scale
    inter_all = cum + m_prev
    rows_t = (ig_on_f - cum).T

    for h in range(H):
        lane = H + h
        q = p_ref[:, h * dqk:(h + 1) * dqk]
        k = p_ref[:, s1 + h * dqk:s1 + (h + 1) * dqk]
        v = p_ref[:, 2 * s1 + h * dv:2 * s1 + (h + 1) * dv]
        o = p_ref[:, 2 * s1 + H * dv + h * dv:2 * s1 + H * dv + (h + 1) * dv]
        kf = k.astype(F32)

        dlog = jnp.where(causal, cum[:, lane:lane + 1] + rows_t[lane:lane + 1, :], -jnp.inf)
        inter = inter_all[:, lane:lane + 1]
        m_t = jnp.maximum(inter, jnp.max(dlog, axis=1, keepdims=True))
        w_intra = jnp.exp(dlog - m_t)
        w_inter = jnp.exp(inter - m_t)
        sc = _dot_nt(q, k) * scale * w_intra
        num_intra = jnp.dot(sc.astype(BF16), v, preferred_element_type=F32)
        n_prev = n0_ref[h]
        qn = jnp.sum(q.astype(F32) * n_prev, axis=1, keepdims=True)
        den = jnp.sum(sc, axis=1, keepdims=True) + w_inter * qn
        inv = 1.0 / jnp.maximum(jnp.abs(den), jnp.exp(-m_t))
        hg = hg_ref[:, h * dv:(h + 1) * dv]

        w_k = wk_all[:, lane:lane + 1]
        k_t = kf.T.astype(BF16)
        stacked = (k_t[None] * seq_cols).reshape(nb * dqk, R)
        upd = jnp.dot(stacked, (w_k * v.astype(F32)).astype(BF16), preferred_element_type=F32)
        n_ref[h] = (decay_all[:, lane:lane + 1] * n_prev
                    + seq_rows(jnp.sum(per_seq(w_k * kf), axis=1, keepdims=True)))

        for pr in range(nb // 2):
            rows = slice(pr * PAIR, (pr + 1) * PAIR)
            lhs = p_ref[rows, h * dqk:(h + 1) * dqk]
            qc = []
            for b in (2 * pr, 2 * pr + 1):
                c_prev = c0_ref[0, b, h]
                qc.append(jnp.dot(lhs, c_prev.astype(BF16), preferred_element_type=F32))
                c_ref[0, b, h] = decay_all[b * T:b * T + 1, lane:lane + 1] * c_prev + upd[b * dqk:(b + 1) * dqk, :]
            num = num_intra[rows, :] + w_inter[rows, :] * jnp.where(first_of_pair, qc[0], qc[1])
            hc = num * inv[rows, :]
            hn = hc * lax.rsqrt(jnp.mean(hc * hc, axis=1, keepdims=True) + EPS) * hg
            y_ref[rows, h * dv:(h + 1) * dv] = (hn * jax.nn.sigmoid(o[rows, :].astype(F32))).astype(y_ref.dtype)

    m_ref[...] = m_new


def mlstm_short_mixer(proj, gate_pre, bias, head_g, c0, n0, m0, *, row0, batch, seq, nb):
    H = M_HEADS
    dqk, dv = c0.shape[-2], c0.shape[-1]
    R = nb * seq
    lanes = gate_pre.shape[1]
    assert batch % nb == 0 and row0 % R == 0 and lanes >= 2 * H
    nblocks = batch // nb
    blk0 = row0 // R
    bias_l = jnp.pad(bias.reshape(1, 2 * H), ((0, 0), (0, lanes - 2 * H)))
    m0_l = jnp.repeat(jnp.pad(m0, ((0, 0), (H, lanes - 2 * H))), seq, axis=0)
    n0_t = jnp.repeat(jnp.swapaxes(n0, 0, 1), seq, axis=1)
    y, c_out, n_out, m_out = pl.pallas_call(
        functools.partial(_mlstm_short_kernel, nb, seq, dqk, dv),
        grid=(nblocks,),
        in_specs=[
            pl.BlockSpec((R, proj.shape[1]), lambda i: (blk0 + i, 0)),
            pl.BlockSpec((R, lanes), lambda i: (blk0 + i, 0)),
            pl.BlockSpec((1, lanes), lambda i: (0, 0)),
            pl.BlockSpec((1, H * dv), lambda i: (0, 0)),
            pl.BlockSpec((1, nb, H, dqk, dv), lambda i: (i, 0, 0, 0, 0)),
            pl.BlockSpec((H, R, dqk), lambda i: (0, i, 0)),
            pl.BlockSpec((R, lanes), lambda i: (i, 0)),
        ],
        out_specs=[
            pl.BlockSpec((R, H * dv), lambda i: (i, 0)),
            pl.BlockSpec((1, nb, H, dqk, dv), lambda i: (i, 0, 0, 0, 0)),
            pl.BlockSpec((H, R, dqk), lambda i: (0, i, 0)),
            pl.BlockSpec((R, lanes), lambda i: (i, 0)),
        ],
        out_shape=[
            jax.ShapeDtypeStruct((batch * seq, H * dv), BF16),
            jax.ShapeDtypeStruct((nblocks, nb, H, dqk, dv), F32),
            jax.ShapeDtypeStruct(n0_t.shape, F32),
            jax.ShapeDtypeStruct(m0_l.shape, F32),
        ],
        compiler_params=_params("arbitrary"),
        name="mlstm_short_mixer",
    )(proj, gate_pre, bias_l, head_g.reshape(1, H * dv), c0.reshape(nblocks, nb, H, dqk, dv), n0_t, m0_l)
    return (y, c_out.reshape(batch, H, dqk, dv), jnp.swapaxes(n_out[:, ::seq, :], 0, 1), m_out[::seq, H:2 * H])


def _hgrn_short_kernel(nb, T, heads, dk, dv,
                       p_ref, lb_ref, gn_ref, s0_ref, y_ref, s_ref,
                       qd_scr, kd_scr, qe_scr, ke_scr, dec_scr):
    R = nb * T
    fdim = heads * dk
    scale = dk ** -0.5
    W = 2 * dk
    PAIR = 2 * T
    assert T % 8 == 0 and T <= H_SUB and nb % 2 == 0

    r = lax.broadcasted_iota(jnp.int32, (R, R), 0)
    c = lax.broadcasted_iota(jnp.int32, (R, R), 1)
    causal = ((r // T) == (c // T)) & (c <= r)
    tril = causal.astype(BF16)
    seq_cols = (lax.broadcasted_iota(jnp.int32, (nb, 1, R), 2) // T
                == lax.broadcasted_iota(jnp.int32, (nb, 1, R), 0)).astype(BF16)
    first_of_pair = lax.broadcasted_iota(jnp.int32, (PAIR, 1), 0) < T

    def minus_ref_row(a, row):
        a3 = a.reshape(nb, T, a.shape[-1])
        return (a3 - a3[:, row:row + 1, :]).reshape(a.shape)

    for g in range(fdim // W):
        sl = slice(g * W, (g + 1) * W)
        q = p_ref[:, sl].astype(F32) * scale
        fpre = p_ref[:, fdim + g * W:fdim + (g + 1) * W].astype(F32)
        lb = lb_ref[:, sl]
        f = lb + (1.0 - lb) * jax.nn.sigmoid(fpre)
        k = 1.0 - f
        a = _cumsum_rows_exact(tril, jnp.log(f))
        d = minus_ref_row(a, T // 2)
        to_last = -minus_ref_row(a, T - 1)
        qd_scr[:, sl] = (q * jnp.exp(d)).astype(BF16)
        kd_scr[:, sl] = (k * jnp.exp(-d)).astype(BF16)
        qe_scr[:, sl] = (q * jnp.exp(a)).astype(BF16)
        ke_scr[:, sl] = k * jnp.exp(to_last)
        dec_scr[:, sl] = jnp.exp(a + to_last)

    for h in range(heads):
        sl = slice(h * dk, (h + 1) * dk)
        iv = p_ref[:, 2 * fdim + h * dv:2 * fdim + (h + 1) * dv]
        sc = jnp.where(causal, _dot_nt(qd_scr[:, sl], kd_scr[:, sl]), 0.0)
        oc = jnp.dot(sc.astype(BF16), iv, preferred_element_type=F32)
        ke_t = ke_scr[:, sl].T.astype(BF16)
        dec_t = dec_scr[:, sl].T
        stacked = (ke_t[None] * seq_cols).reshape(nb * dk, R)
        upd = jnp.dot(stacked, iv, preferred_element_type=F32)
        gn = gn_ref[:, h * dv:(h + 1) * dv]
        for pr in range(nb // 2):
            rows = slice(pr * PAIR, (pr + 1) * PAIR)
            lhs = qe_scr[rows, sl]
            inter = []
            for b in (2 * pr, 2 * pr + 1):
                s_prev = s0_ref[0, b, h]
                inter.append(jnp.dot(lhs, s_prev.astype(BF16), preferred_element_type=F32))
                s_ref[0, b, h] = dec_t[:, b * T:b * T + 1] * s_prev + upd[b * dk:(b + 1) * dk, :]
            o2 = oc[rows, :] + jnp.where(first_of_pair, inter[0], inter[1])
            gate = p_ref[rows, 2 * fdim + heads * dv + h * dv:2 * fdim + heads * dv + (h + 1) * dv].astype(F32)
            on = o2 * lax.rsqrt(jnp.mean(o2 * o2, axis=1, keepdims=True) + EPS) * gn
            y_ref[rows, h * dv:(h + 1) * dv] = (on * _silu(gate)).astype(y_ref.dtype)


def hgrn_short_mixer(proj, lb, gn_g, s0, *, row0, batch, seq, nb):
    heads, dk, dv = s0.shape[-3], s0.shape[-2], s0.shape[-1]
    R = nb * seq
    assert batch % nb == 0 and row0 % R == 0
    nblocks = batch // nb
    blk0 = row0 // R
    fdim = heads * dk
    s0 = s0.reshape(nblocks, nb, heads, dk, dv)
    y, s_out = pl.pallas_call(
        functools.partial(_hgrn_short_kernel, nb, seq, heads, dk, dv),
        grid=(nblocks,),
        in_specs=[
            pl.BlockSpec((R, proj.shape[1]), lambda i: (blk0 + i, 0)),
            pl.BlockSpec((1, fdim), lambda i: (0, 0)),
            pl.BlockSpec((1, heads * dv), lambda i: (0, 0)),
            pl.BlockSpec((1, nb, heads, dk, dv), lambda i: (i, 0, 0, 0, 0)),
        ],
        out_specs=[
            pl.BlockSpec((R, heads * dv), lambda i: (i, 0)),
            pl.BlockSpec((1, nb, heads, dk, dv), lambda i: (i, 0, 0, 0, 0)),
        ],
        out_shape=[
            jax.ShapeDtypeStruct((batch * seq, heads * dv), BF16),
            jax.ShapeDtypeStruct(s0.shape, F32),
        ],
        scratch_shapes=[
            pltpu.VMEM((R, fdim), BF16), pltpu.VMEM((R, fdim), BF16), pltpu.VMEM((R, fdim), BF16),
            pltpu.VMEM((R, fdim), F32), pltpu.VMEM((R, fdim), F32),
        ],
        compiler_params=_params("arbitrary"),
        name="hgrn_short_mixer",
    )(proj, lb.reshape(1, fdim), gn_g.reshape(1, heads * dv), s0)
    return y, s_out.reshape(batch, heads, dk, dv)


def _hgrn_long_kernel(R, heads, dk, dv, levels,
                      p_ref, lb_ref, gn_ref, s0_ref, y_ref, s_ref,
                      st_scr, dec_scr, qd_scr, kd_scr, qe_scr, ke_scr, ql_scr, kl_scr, k_scr, a_scr,
                      kt_scr, sb_scr, ivt_scr, sc_scr, oc_scr):
    fdim = heads * dk
    eps_unscaled = EPS * dk
    W = 2 * dk
    last_step = pl.num_programs(1) - 1

    @pl.when(pl.program_id(1) == 0)
    def _():
        for h in range(heads):
            st_scr[h] = s0_ref[0, h].T

    r = lax.broadcasted_iota(jnp.int32, (R, R), 0)
    c = lax.broadcasted_iota(jnp.int32, (R, R), 1)
    tril = (c <= r).astype(BF16)
    diag_mask = ((r // H_SUB) == (c // H_SUB)) & (c <= r)
    level_masks = [((r // G) == (c // G)) & ((r % G) >= G // 2) & ((c % G) < G // 2) for G in levels]

    NSB = R // H_SUB
    zeros_sub = jnp.zeros((H_SUB, dk), BF16)

    for g in range(fdim // W):
        fpre = p_ref[:, fdim + g * W:fdim + (g + 1) * W].astype(F32)
        lb = lb_ref[:, g * W:(g + 1) * W]
        f = lb + (1.0 - lb) * jax.nn.sigmoid(fpre)
        k = 1.0 - f
        a = _cumsum_rows_exact(tril, jnp.log(f))
        for j in range(W // dk):
            k_scr[g * (W // dk) + j] = k[:, j * dk:(j + 1) * dk]
            a_scr[g * (W // dk) + j] = a[:, j * dk:(j + 1) * dk]

    def rows_of(ref, h, start, count, stride):
        if count == 1:
            return ref[h, start:start + 1, :]
        return ref.at[h][pl.ds(start, count, stride=stride), :]

    for h in range(heads):
        sl = slice(h * dk, (h + 1) * dk)
        a_last = a_scr[h, R - 1:R, :]
        dec_scr[:, sl] = jnp.exp(a_last)
        mids = rows_of(a_scr, h, H_SUB // 2, NSB, H_SUB)
        c_qe = jnp.exp(mids)
        c_ke = jnp.exp(a_last - mids)
        c_lv = []
        for G in levels:
            per = G // H_SUB
            bnd = rows_of(a_scr, h, G // 2 - 1, R // G, G)
            consts = []
            for o in range(per):
                mid_o = rows_of(a_scr, h, o * H_SUB + H_SUB // 2, R // G, G)
                consts.append(jnp.exp(mid_o - bnd) if o >= per // 2 else jnp.exp(bnd - mid_o))
            c_lv.append(consts)
        for s in range(NSB):
            rows = slice(s * H_SUB, (s + 1) * H_SUB)
            d = a_scr[h, rows, :] - mids[s:s + 1, :]
            qd = p_ref[rows, sl].astype(F32) * jnp.exp(d)
            kd = k_scr[h, rows, :] * jnp.exp(-d)
            qd_scr[rows, sl] = qd.astype(BF16)
            kd_scr[rows, sl] = kd.astype(BF16)
            qe_scr[rows, sl] = (qd * c_qe[s:s + 1, :]).astype(BF16)
            ke_scr[rows, sl] = (kd * c_ke[s:s + 1, :]).astype(BF16)
            for li, G in enumerate(levels):
                per = G // H_SUB
                o, gi = s % per, s // per
                const = c_lv[li][o][gi:gi + 1, :]
                if o >= per // 2:
                    ql_scr[li, rows, sl] = (qd * const).astype(BF16)
                    kl_scr[li, rows, sl] = zeros_sub
                else:
                    kl_scr[li, rows, sl] = (kd * const).astype(BF16)
                    ql_scr[li, rows, sl] = zeros_sub

    def paired(lhs_a, lhs_b, rhs_ab):
        ca = rhs_ab.shape[1] // 2
        return (jnp.dot(lhs_a, rhs_ab[:, :ca], preferred_element_type=F32),
                jnp.dot(lhs_b, rhs_ab[:, ca:], preferred_element_type=F32))

    assert dk == dv
    npairs = heads // 2
    lanes_a = [slice(2 * p * dk, (2 * p + 1) * dk) for p in range(npairs)]
    lanes_b = [slice((2 * p + 1) * dk, (2 * p + 2) * dk) for p in range(npairs)]
    lanes_ab = [slice(2 * p * dk, (2 * p + 2) * dk) for p in range(npairs)]
    iv_of = lambda p: p_ref[:, 2 * fdim + 2 * p * dv:2 * fdim + (2 * p + 2) * dv]

    for p in range(npairs):
        kt_scr[p, 0, :, :R] = kd_scr[:, lanes_a[p]].T
        kt_scr[p, 0, :, R:] = kd_scr[:, lanes_b[p]].T
        for li in range(len(levels)):
            kt_scr[p, 1 + li, :, :R] = kl_scr[li, :, lanes_a[p]].T
            kt_scr[p, 1 + li, :, R:] = kl_scr[li, :, lanes_b[p]].T
        sb_scr[p, :, :dv] = st_scr[2 * p].astype(BF16).T
        sb_scr[p, :, dv:] = st_scr[2 * p + 1].astype(BF16).T
        ivt_scr[p] = iv_of(p).T
    for p in range(npairs):
        da, db = paired(qd_scr[:, lanes_a[p]], qd_scr[:, lanes_b[p]], kt_scr[p, 0])
        sc_a, sc_b = jnp.where(diag_mask, da, 0.0), jnp.where(diag_mask, db, 0.0)
        for li in range(len(levels)):
            la, lb_ = paired(ql_scr[li, :, lanes_a[p]], ql_scr[li, :, lanes_b[p]], kt_scr[p, 1 + li])
            sc_a, sc_b = jnp.where(level_masks[li], la, sc_a), jnp.where(level_masks[li], lb_, sc_b)
        sc_scr[2 * p] = sc_a.astype(BF16)
        sc_scr[2 * p + 1] = sc_b.astype(BF16)
    for p in range(npairs):
        intra_a, intra_b = paired(sc_scr[2 * p], sc_scr[2 * p + 1], iv_of(p))
        inter_a, inter_b = paired(qe_scr[:, lanes_a[p]], qe_scr[:, lanes_b[p]], sb_scr[p])
        oc_scr[:, lanes_a[p]] = intra_a + inter_a
        oc_scr[:, lanes_b[p]] = intra_b + inter_b
        upd_a = jnp.dot(ivt_scr[p, :dv, :], ke_scr[:, lanes_a[p]], preferred_element_type=F32)
        upd_b = jnp.dot(ivt_scr[p, dv:, :], ke_scr[:, lanes_b[p]], preferred_element_type=F32)
        st_scr[2 * p] = st_scr[2 * p] * dec_scr[:, lanes_a[p]] + upd_a
        st_scr[2 * p + 1] = st_scr[2 * p + 1] * dec_scr[:, lanes_b[p]] + upd_b
    for h in range(heads):
        sl = slice(h * dv, (h + 1) * dv)
        oc = oc_scr[:, sl]
        gate = p_ref[:, 2 * fdim + heads * dv + h * dv:2 * fdim + heads * dv + (h + 1) * dv].astype(F32)
        on = oc * lax.rsqrt(jnp.mean(oc * oc, axis=1, keepdims=True) + eps_unscaled) * gn_ref[:, sl]
        y_ref[:, sl] = (on * _silu(gate)).astype(y_ref.dtype)

    @pl.when(pl.program_id(1) == last_step)
    def _():
        for h in range(heads):
            s_ref[0, h] = st_scr[h].T


def hgrn_long_mixer(proj, lb, gn_g, s0, *, row0, batch, seq, rows):
    heads, dk, dv = s0.shape[-3], s0.shape[-2], s0.shape[-1]
    R = rows
    assert seq % R == 0 and R % (2 * H_SUB) == 0 and row0 % R == 0
    levels = []
    G = 2 * H_SUB
    while G <= R:
        levels.append(G)
        G *= 2
    assert levels[-1] == R
    nsteps = seq // R
    blk0 = row0 // R
    fdim = heads * dk
    wide = lambda: pltpu.VMEM((R, fdim), BF16)
    return pl.pallas_call(
        functools.partial(_hgrn_long_kernel, R, heads, dk, dv, tuple(levels)),
        grid=(batch, nsteps),
        in_specs=[
            pl.BlockSpec((R, proj.shape[1]), lambda i, c: (blk0 + i * nsteps + c, 0)),
            pl.BlockSpec((1, fdim), lambda i, c: (0, 0)),
            pl.BlockSpec((1, heads * dv), lambda i, c: (0, 0)),
            pl.BlockSpec((1, heads, dk, dv), lambda i, c: (i, 0, 0, 0)),
        ],
        out_specs=[
            pl.BlockSpec((R, heads * dv), lambda i, c: (i * nsteps + c, 0)),
            pl.BlockSpec((1, heads, dk, dv), lambda i, c: (i, 0, 0, 0)),
        ],
        out_shape=[
            jax.ShapeDtypeStruct((batch * seq, heads * dv), BF16),
            jax.ShapeDtypeStruct(s0.shape, F32),
        ],
        scratch_shapes=[
            pltpu.VMEM((heads, dv, dk), F32),
            pltpu.VMEM((1, fdim), F32),
            wide(), wide(), wide(), wide(),
            pltpu.VMEM((len(levels), R, fdim), BF16),
            pltpu.VMEM((len(levels), R, fdim), BF16),
            pltpu.VMEM((heads, R, dk), F32),
            pltpu.VMEM((heads, R, dk), F32),
            pltpu.VMEM((heads // 2, 1 + len(levels), dk, 2 * R), BF16),
            pltpu.VMEM((heads // 2, dk, 2 * dv), BF16),
            pltpu.VMEM((heads // 2, 2 * dv, R), BF16),
            pltpu.VMEM((heads, R, R), BF16),
            pltpu.VMEM((R, heads * dv), F32),
        ],
        compiler_params=_params("arbitrary", "arbitrary"),
        name="hgrn_long_mixer",
    )(proj, lb.reshape(1, fdim), gn_g.reshape(1, heads * dv), s0)


def kernel(x_prompt, x_sample, state_mlstm_C, state_mlstm_n, state_mlstm_m, state_hgrn_S,
           norm_mix_g, norm_ffn_g, norm_final_g, mlstm_w_in, mlstm_b_gates, mlstm_head_norm_g,
           mlstm_w_out, hgrn_w_in, hgrn_lower_bounds, hgrn_g_norm_g, hgrn_w_out, ffn_w_up, ffn_w_down):
    bp, tp, d = x_prompt.shape
    bs, ts, _ = x_sample.shape
    np_, ns_ = bp * tp, bs * ts
    depth = norm_mix_g.shape[0]
    H = M_HEADS
    dqk, dv = state_mlstm_C.shape[-2], state_mlstm_C.shape[-1]
    hh, hdk, hdv = state_hgrn_S.shape[-3], state_hgrn_S.shape[-2], state_hgrn_S.shape[-1]

    xs = (x_prompt.reshape(np_, d), x_sample.reshape(ns_, d))

    lb_all = jnp.cumsum(jax.nn.softmax(hgrn_lower_bounds.astype(F32), axis=0), axis=0)
    lb_all = lb_all - lb_all[0]

    Cp, Np, Mp, Sp, Cs, Ns, Ms, Ss = [], [], [], [], [], [], [], []
    for i in range(depth):
        j = i // 2
        if i % 2 == 0:
            n_main = 2 * H * dqk + 2 * H * dv
            w_in_t = jnp.swapaxes(mlstm_w_in, 1, 2)
            w_gates = jnp.pad(w_in_t[j, n_main:], ((0, LANES - 2 * H), (0, 0)))
            proj, gate_pre = norm_matmul(xs, norm_mix_g[i], w_in_t, j, n_main, tm=PROJ_ROWS, tn=PROJ_COLS,
                                         out_dtype=BF16, w_extra=w_gates, w_rows_are_outputs=True)
            args = (proj, gate_pre, mlstm_b_gates[j], mlstm_head_norm_g[j])
            yp, c_p, n_p, m_p = mlstm_long_mixer(
                *args, jnp.zeros((bp, H, dqk, dv), F32), jnp.zeros((bp, H, dqk), F32), jnp.zeros((bp, H), F32),
                row0=0, batch=bp, seq=tp, rows=MLSTM_STEP_ROWS)
            ysm, c_s, n_s, m_s = mlstm_short_mixer(
                *args, state_mlstm_C[j], state_mlstm_n[j], state_mlstm_m[j],
                row0=np_, batch=bs, seq=ts, nb=SAMPLE_SEQS)
            Cp.append(c_p); Np.append(n_p); Mp.append(m_p)
            Cs.append(c_s); Ns.append(n_s); Ms.append(m_s)
            w_out = mlstm_w_out
        else:
            proj = norm_matmul(xs, norm_mix_g[i], hgrn_w_in, j, hgrn_w_in.shape[-1],
                               tm=PROJ_ROWS, tn=PROJ_COLS, out_dtype=BF16)
            args = (proj, lb_all[i], hgrn_g_norm_g[j])
            yp, s_p = hgrn_long_mixer(*args, jnp.zeros((bp, hh, hdk, hdv), F32),
                                      row0=0, batch=bp, seq=tp, rows=HGRN_STEP_ROWS)
            ysm, s_s = hgrn_short_mixer(*args, state_hgrn_S[j], row0=np_, batch=bs, seq=ts, nb=SAMPLE_SEQS)
            Sp.append(s_p); Ss.append(s_s)
            w_out = hgrn_w_out
        x = matmul_residual((yp, ysm), w_out, j, xs, tm=OUT_ROWS, tn=d)
        act = norm_swiglu(x, norm_ffn_g[i], ffn_w_up, i, tm=PROJ_ROWS, tn=SWIGLU_COLS)
        x = matmul_residual((act,), ffn_w_down, i, (x,), tm=DOWN_ROWS, tn=DOWN_COLS)
        xs = (x,)

    y_prompt = final_norm(x, norm_final_g, row0=0, nrows=np_, tm=NORM_ROWS).reshape(bp, tp, d)
    y_sample = final_norm(x, norm_final_g, row0=np_, nrows=ns_, tm=NORM_ROWS).reshape(bs, ts, d)
    cat = lambda parts: jnp.stack(parts) if len(parts) > 1 else parts[0][None]
    return (y_prompt, y_sample, cat(Cp), cat(Np), cat(Mp), cat(Sp), cat(Cs), cat(Ns), cat(Ms), cat(Ss))
```

```python
import functools

import jax
import jax.numpy as jnp
from jax import lax
from jax.experimental import pallas as pl
from jax.experimental.pallas import tpu as pltpu

F32 = jnp.float32
BF16 = jnp.bfloat16

EPS = 1e-6
GATE_CAP = 15.0
M_HEADS = 8
H_SUB = 16

LANES = 128

PROJ_ROWS = 1024
PROJ_COLS = 1024
SWIGLU_COLS = 512
OUT_ROWS = 256
DOWN_ROWS = 512
DOWN_COLS = 512
NORM_ROWS = 512
MLSTM_STEP_ROWS = 256
HGRN_STEP_ROWS = 128
SAMPLE_SEQS = 8

VMEM_LIMIT_BYTES = 56 * 1024 * 1024


def _params(*sem):
    return pltpu.CompilerParams(dimension_semantics=sem, vmem_limit_bytes=VMEM_LIMIT_BYTES)


def _row_tiles(xs, tm):
    for x in xs:
        assert x.shape[0] % tm == 0, (x.shape, tm)
    return tuple(x.shape[0] // tm for x in xs)


def _clamped_row_tile(first, count, row_of, col_of, *ids):
    return (jnp.clip(row_of(*ids) - first, 0, count - 1), col_of(*ids))


def _row_specs(tiles, block, row_of, col_of, single_tile_unbuffered=False):
    specs, first = [], 0
    for count in tiles:
        mode = dict(pipeline_mode=pl.Buffered(1)) if single_tile_unbuffered and count == 1 else {}
        specs.append(pl.BlockSpec(block, functools.partial(_clamped_row_tile, first, count, row_of, col_of), **mode))
        first += count
    return specs


def _on_owner(refs, tiles, i, fn):
    if len(refs) == 1:
        fn(refs[0])
        return
    first = 0
    for ref, count in zip(refs, tiles):
        pl.when((i >= first) & (i < first + count))(functools.partial(fn, ref))
        first += count


def _rms_normed(x, g):
    ms = jnp.mean(x * x, axis=-1, keepdims=True)
    return x * lax.rsqrt(ms + EPS) * g


def _norm_matmul_kernel(tiles, with_extra, w_rows_are_outputs, *refs):
    nx = len(tiles)
    x_refs, g_ref, w_ref = refs[:nx], refs[nx], refs[nx + 1]
    if with_extra:
        we_ref, o_ref, oe_ref, h_scr = refs[nx + 2:]
    else:
        o_ref, h_scr = refs[nx + 2:]
    matmul = _dot_nt if w_rows_are_outputs else functools.partial(jnp.dot, preferred_element_type=F32)

    @pl.when(pl.program_id(1) == 0)
    def _():
        def build(x_ref):
            h_scr[...] = _rms_normed(x_ref[...], g_ref[...]).astype(BF16)
            if with_extra:
                oe_ref[...] = matmul(h_scr[...], we_ref[...].astype(BF16))
        _on_owner(x_refs, tiles, pl.program_id(0), build)

    o_ref[...] = matmul(h_scr[...], w_ref[...].astype(BF16)).astype(o_ref.dtype)


def norm_matmul(xs, g, w, layer, n_cols, *, tm, tn, out_dtype, w_extra=None, w_rows_are_outputs=False):
    d = xs[0].shape[1]
    tiles = _row_tiles(xs, tm)
    m = tm * sum(tiles)
    row_of, col0 = (lambda i, j: i), (lambda i, j: 0)
    w_spec = (pl.BlockSpec((None, tn, d), lambda i, j: (layer, j, 0)) if w_rows_are_outputs
              else pl.BlockSpec((None, d, tn), lambda i, j: (layer, 0, j)))
    in_specs = (_row_specs(tiles, (tm, d), row_of, col0, single_tile_unbuffered=len(tiles) > 1)
                + [pl.BlockSpec((1, d), lambda i, j: (0, 0)), w_spec])
    out_specs = [pl.BlockSpec((tm, tn), lambda i, j: (i, j))]
    out_shape = [jax.ShapeDtypeStruct((m, n_cols), out_dtype)]
    args = list(xs) + [g.reshape(1, d), w]
    if w_extra is not None:
        ne = w_extra.shape[0 if w_rows_are_outputs else 1]
        in_specs.append(pl.BlockSpec(w_extra.shape, lambda i, j: (0, 0)))
        out_specs.append(pl.BlockSpec((tm, ne), lambda i, j: (i, 0)))
        out_shape.append(jax.ShapeDtypeStruct((m, ne), F32))
        args.append(w_extra)
    outs = pl.pallas_call(
        functools.partial(_norm_matmul_kernel, tiles, w_extra is not None, w_rows_are_outputs),
        grid=(m // tm, n_cols // tn),
        in_specs=in_specs,
        out_specs=out_specs,
        out_shape=out_shape,
        scratch_shapes=[pltpu.VMEM((tm, d), BF16)],
        compiler_params=_params("arbitrary", "arbitrary"),
        name="norm_matmul",
    )(*args)
    return outs if w_extra is not None else outs[0]


def _norm_swiglu_kernel(x_ref, g_ref, wa_ref, wu_ref, o_ref, h_scr):
    @pl.when(pl.program_id(1) == 0)
    def _():
        h_scr[...] = _rms_normed(x_ref[...], g_ref[...]).astype(BF16)

    h = h_scr[...]
    a = jnp.dot(h, wa_ref[...].astype(BF16), preferred_element_type=F32)
    u = jnp.dot(h, wu_ref[...].astype(BF16), preferred_element_type=F32)
    o_ref[...] = (a * jax.nn.sigmoid(a) * u).astype(o_ref.dtype)


def norm_swiglu(x, g, w_up, layer, *, tm, tn):
    m, d = x.shape
    ff = w_up.shape[-1] // 2
    nj = ff // tn
    return pl.pallas_call(
        _norm_swiglu_kernel,
        grid=(m // tm, nj),
        in_specs=[
            pl.BlockSpec((tm, d), lambda i, j: (i, 0)),
            pl.BlockSpec((1, d), lambda i, j: (0, 0)),
            pl.BlockSpec((None, d, tn), lambda i, j: (layer, 0, j)),
            pl.BlockSpec((None, d, tn), lambda i, j: (layer, 0, j + nj)),
        ],
        out_specs=pl.BlockSpec((tm, tn), lambda i, j: (i, j)),
        out_shape=jax.ShapeDtypeStruct((m, ff), BF16),
        scratch_shapes=[pltpu.VMEM((tm, d), BF16)],
        compiler_params=_params("arbitrary", "arbitrary"),
        name="norm_swiglu",
    )(x, g.reshape(1, d), w_up, w_up)


def _matmul_residual_kernel(y_tiles, r_tiles, *refs):
    ny, nr = len(y_tiles), len(r_tiles)
    y_refs, w_ref, r_refs = refs[:ny], refs[ny], refs[ny + 1:ny + 1 + nr]
    o_ref, w_scr = refs[ny + 1 + nr:]
    i = pl.program_id(1)

    @pl.when(i == 0)
    def _():
        w_scr[...] = w_ref[...].astype(BF16)

    def product(y_ref):
        o_ref[...] = jnp.dot(y_ref[...], w_scr[...], preferred_element_type=F32)

    def add_residual(r_ref):
        o_ref[...] = o_ref[...] + r_ref[...]

    _on_owner(y_refs, y_tiles, i, product)
    _on_owner(r_refs, r_tiles, i, add_residual)


def matmul_residual(ys, w, layer, rs, *, tm, tn):
    k, n = w.shape[-2], w.shape[-1]
    y_tiles, r_tiles = _row_tiles(ys, tm), _row_tiles(rs, tm)
    assert sum(y_tiles) == sum(r_tiles)
    m = tm * sum(y_tiles)
    row_of = lambda j, i: i
    w_mode = dict(pipeline_mode=pl.Buffered(1)) if tn == n else {}
    in_specs = (_row_specs(y_tiles, (tm, k), row_of, lambda j, i: 0)
                + [pl.BlockSpec((None, k, tn), lambda j, i: (layer, 0, j), **w_mode)]
                + _row_specs(r_tiles, (tm, tn), row_of, lambda j, i: j))
    return pl.pallas_call(
        functools.partial(_matmul_residual_kernel, y_tiles, r_tiles),
        grid=(n // tn, m // tm),
        in_specs=in_specs,
        out_specs=pl.BlockSpec((tm, tn), lambda j, i: (i, j)),
        out_shape=jax.ShapeDtypeStruct((m, n), F32),
        scratch_shapes=[pltpu.VMEM((k, tn), BF16)],
        compiler_params=_params("arbitrary", "arbitrary"),
        name="matmul_residual",
    )(*ys, w, *rs)


def _final_norm_kernel(x_ref, g_ref, o_ref):
    o_ref[...] = _rms_normed(x_ref[...], g_ref[...])


def final_norm(x, g, *, row0, nrows, tm):
    d = x.shape[1]
    assert row0 % tm == 0 and nrows % tm == 0
    return pl.pallas_call(
        _final_norm_kernel,
        grid=(nrows // tm,),
        in_specs=[pl.BlockSpec((tm, d), lambda i: (row0 // tm + i, 0)), pl.BlockSpec((1, d), lambda i: (0, 0))],
        out_specs=pl.BlockSpec((tm, d), lambda i: (i, 0)),
        out_shape=jax.ShapeDtypeStruct((nrows, d), F32),
        compiler_params=_params("arbitrary"),
        name="final_norm",
    )(x, g.reshape(1, d))


def _dot_nt(a, b):
    return lax.dot_general(a, b, (((1,), (1,)), ((), ())), preferred_element_type=F32)


def _dot_tn(a, b):
    return lax.dot_general(a, b, (((0,), (0,)), ((), ())), preferred_element_type=F32)


def _log_sigmoid(x):
    return jnp.minimum(x, 0.0) - jnp.log1p(jnp.exp(-jnp.abs(x)))


def _silu(x):
    return 0.5 * x * (1.0 + jnp.tanh(0.5 * x))


def _cumsum_rows_exact(tril, x):
    hi = x.astype(BF16)
    r1 = x - hi.astype(F32)
    mid = r1.astype(BF16)
    lo = (r1 - mid.astype(F32)).astype(BF16)
    return (jnp.dot(tril, hi, preferred_element_type=F32) + jnp.dot(tril, mid, preferred_element_type=F32)
            + jnp.dot(tril, lo, preferred_element_type=F32))


def _mlstm_long_kernel(R, dqk, dv,
                       p_ref, gp_ref, bias_ref, hg_ref, c0_ref, n0_ref, m0_ref,
                       y_ref, c_ref, n_ref, m_ref, kt_scr):
    H = M_HEADS
    s1 = H * dqk
    scale = dqk ** -0.5

    @pl.when(pl.program_id(1) == 0)
    def _():
        c_ref[...] = c0_ref[...]
        n_ref[...] = n0_ref[...]
        m_ref[...] = m0_ref[...]

    r = lax.broadcasted_iota(jnp.int32, (R, R), 0)
    c = lax.broadcasted_iota(jnp.int32, (R, R), 1)
    causal = c <= r
    tril = causal.astype(BF16)

    capped = GATE_CAP * jnp.tanh((gp_ref[...] + bias_ref[...]) / GATE_CAP)
    cum = _cumsum_rows_exact(tril, _log_sigmoid(capped))
    ig_on_f = pltpu.roll(capped, H, axis=1)
    m_prev = m_ref[0]
    b_last = cum[R - 1:R, :]
    g_all = b_last - cum + ig_on_f
    m_new = jnp.maximum(b_last + m_prev, jnp.max(g_all, axis=0, keepdims=True))
    decay_all = jnp.exp(b_last + m_prev - m_new)
    wk_all = jnp.exp(g_all - m_new) * scale
    inter_all = cum + m_prev
    rows_t = (ig_on_f - cum).T

    q_cols = [slice(h * dqk, (h + 1) * dqk) for h in range(H)]
    k_cols = [slice(s1 + h * dqk, s1 + (h + 1) * dqk) for h in range(H)]
    v_cols = [slice(2 * s1 + h * dv, 2 * s1 + (h + 1) * dv) for h in range(H)]
    o_cols = [slice(2 * s1 + H * dv + h * dv, 2 * s1 + H * dv + (h + 1) * dv) for h in range(H)]

    for h in range(H):
        lane = H + h
        q = p_ref[:, q_cols[h]]
        c_prev = c_ref[0, h]
        c_bf = c_prev.astype(BF16)
        n_prev = n_ref[0, h:h + 1, :]
        kt_scr[h] = p_ref[:, k_cols[h]].T

        dlog = jnp.where(causal, cum[:, lane:lane + 1] + rows_t[lane:lane + 1, :], -jnp.inf)
        inter = inter_all[:, lane:lane + 1]
        m_t = jnp.maximum(inter, jnp.max(dlog, axis=1, keepdims=True))
        w_inter = jnp.exp(inter - m_t)
        sc = jnp.dot(q, kt_scr[h], preferred_element_type=F32) * scale * jnp.exp(dlog - m_t)
        num = (jnp.dot(sc.astype(BF16), p_ref[:, v_cols[h]], preferred_element_type=F32)
               + w_inter * jnp.dot(q, c_bf, preferred_element_type=F32))
        qn = jnp.sum(q.astype(F32) * n_prev, axis=1, keepdims=True)
        den = jnp.sum(sc, axis=1, keepdims=True) + w_inter * qn
        inv = 1.0 / jnp.maximum(jnp.abs(den), jnp.exp(-m_t))
        norm = inv * lax.rsqrt(inv * inv * jnp.mean(num * num, axis=1, keepdims=True) + EPS)
        o = p_ref[:, o_cols[h]].astype(F32)
        y_ref[:, h * dv:(h + 1) * dv] = (num * norm * hg_ref[:, h * dv:(h + 1) * dv]
                                         * jax.nn.sigmoid(o)).astype(y_ref.dtype)

        k = p_ref[:, k_cols[h]]
        w_k = wk_all[:, lane:lane + 1]
        decay = decay_all[:, lane:lane + 1]
        wv = (w_k * p_ref[:, v_cols[h]].astype(F32)).astype(BF16)
        c_ref[0, h] = decay * c_prev + _dot_tn(k, wv)
        n_ref[0, h:h + 1, :] = decay * n_prev + jnp.sum(w_k * k.astype(F32), axis=0, keepdims=True)

    m_ref[0] = m_new


def mlstm_long_mixer(proj, gate_pre, bias, head_g, c0, n0, m0, *, row0, batch, seq, rows):
    H = M_HEADS
    dqk, dv = c0.shape[-2], c0.shape[-1]
    R = rows
    lanes = gate_pre.shape[1]
    assert seq % R == 0 and row0 % R == 0 and lanes >= 2 * H
    nsteps = seq // R
    blk0 = row0 // R
    rows_of = lambda i, c: (blk0 + i * nsteps + c, 0)
    bias_l = jnp.pad(bias.reshape(1, 2 * H), ((0, 0), (0, lanes - 2 * H)))
    m0_l = jnp.pad(m0, ((0, 0), (H, lanes - 2 * H))).reshape(batch, 1, lanes)
    y, c_out, n_out, m_out = pl.pallas_call(
        functools.partial(_mlstm_long_kernel, R, dqk, dv),
        grid=(batch, nsteps),
        in_specs=[
            pl.BlockSpec((R, proj.shape[1]), rows_of),
            pl.BlockSpec((R, lanes), rows_of),
            pl.BlockSpec((1, lanes), lambda i, c: (0, 0)),
            pl.BlockSpec((1, H * dv), lambda i, c: (0, 0)),
            pl.BlockSpec((1, H, dqk, dv), lambda i, c: (i, 0, 0, 0)),
            pl.BlockSpec((1, H, dqk), lambda i, c: (i, 0, 0)),
            pl.BlockSpec((1, 1, lanes), lambda i, c: (i, 0, 0)),
        ],
        out_specs=[
            pl.BlockSpec((R, H * dv), lambda i, c: (i * nsteps + c, 0)),
            pl.BlockSpec((1, H, dqk, dv), lambda i, c: (i, 0, 0, 0)),
            pl.BlockSpec((1, H, dqk), lambda i, c: (i, 0, 0)),
            pl.BlockSpec((1, 1, lanes), lambda i, c: (i, 0, 0)),
        ],
        out_shape=[
            jax.ShapeDtypeStruct((batch * seq, H * dv), BF16),
            jax.ShapeDtypeStruct(c0.shape, F32),
            jax.ShapeDtypeStruct(n0.shape, F32),
            jax.ShapeDtypeStruct(m0_l.shape, F32),
        ],
        scratch_shapes=[pltpu.VMEM((H, dqk, R), BF16)],
        compiler_params=_params("arbitrary", "arbitrary"),
        name="mlstm_long_mixer",
    )(proj, gate_pre, bias_l, head_g.reshape(1, H * dv), c0, n0, m0_l)
    return y, c_out, n_out, m_out[:, 0, H:2 * H]


def _mlstm_short_kernel(nb, T, dqk, dv,
                        p_ref, gp_ref, bias_ref, hg_ref, c0_ref, n0_ref, m0_ref,
                        y_ref, c_ref, n_ref, m_ref):
    H = M_HEADS
    R = nb * T
    PAIR = 2 * T
    s1 = H * dqk
    scale = dqk ** -0.5
    assert T % 8 == 0 and nb % 2 == 0

    r = lax.broadcasted_iota(jnp.int32, (R, R), 0)
    c = lax.broadcasted_iota(jnp.int32, (R, R), 1)
    causal = ((r // T) == (c // T)) & (c <= r)
    tril = causal.astype(BF16)
    seq_cols = (lax.broadcasted_iota(jnp.int32, (nb, 1, R), 2) // T
                == lax.broadcasted_iota(jnp.int32, (nb, 1, R), 0)).astype(BF16)
    first_of_pair = lax.broadcasted_iota(jnp.int32, (PAIR, 1), 0) < T

    def per_seq(x):
        return x.reshape(nb, T, x.shape[-1])

    def seq_rows(x3):
        return jnp.broadcast_to(x3, (nb, T, x3.shape[-1])).reshape(R, x3.shape[-1])

    capped = GATE_CAP * jnp.tanh((gp_ref[...] + bias_ref[...]) / GATE_CAP)
    cum = _cumsum_rows_exact(tril, _log_sigmoid(capped))
    ig_on_f = pltpu.roll(capped, H, axis=1)
    m_prev = m0_ref[...]
    b_last = seq_rows(per_seq(cum)[:, T - 1:T, :])
    g_all = b_last - cum + ig_on_f
    m_new = jnp.maximum(b_last + m_prev, seq_rows(jnp.max(per_seq(g_all), axis=1, keepdims=True)))
    decay_all = jnp.exp(b_last + m_prev - m_new)
    wk_all = jnp.exp(g_all - m_new) * scale
    inter_all = cum + m_prev
    rows_t = (ig_on_f - cum).T

    for h in range(H):
        lane = H + h
        q = p_ref[:, h * dqk:(h + 1) * dqk]
        k = p_ref[:, s1 + h * dqk:s1 + (h + 1) * dqk]
        v = p_ref[:, 2 * s1 + h * dv:2 * s1 + (h + 1) * dv]
        o = p_ref[:, 2 * s1 + H * dv + h * dv:2 * s1 + H * dv + (h + 1) * dv]
        kf = k.astype(F32)

        dlog = jnp.where(causal, cum[:, lane:lane + 1] + rows_t[lane:lane + 1, :], -jnp.inf)
        inter = inter_all[:, lane:lane + 1]
        m_t = jnp.maximum(inter, jnp.max(dlog, axis=1, keepdims=True))
        w_intra = jnp.exp(dlog - m_t)
        w_inter = jnp.exp(inter - m_t)
        sc = _dot_nt(q, k) * scale * w_intra
        num_intra = jnp.dot(sc.astype(BF16), v, preferred_element_type=F32)
        n_prev = n0_ref[h]
        qn = jnp.sum(q.astype(F32) * n_prev, axis=1, keepdims=True)
        den = jnp.sum(sc, axis=1, keepdims=True) + w_inter * qn
        inv = 1.0 / jnp.maximum(jnp.abs(den), jnp.exp(-m_t))
        hg = hg_ref[:, h * dv:(h + 1) * dv]

        w_k = wk_all[:, lane:lane + 1]
        k_t = kf.T.astype(BF16)
        stacked = (k_t[None] * seq_cols).reshape(nb * dqk, R)
        upd = jnp.dot(stacked, (w_k * v.astype(F32)).astype(BF16), preferred_element_type=F32)
        n_ref[h] = (decay_all[:, lane:lane + 1] * n_prev
                    + seq_rows(jnp.sum(per_seq(w_k * kf), axis=1, keepdims=True)))

        for pr in range(nb // 2):
            rows = slice(pr * PAIR, (pr + 1) * PAIR)
            lhs = p_ref[rows, h * dqk:(h + 1) * dqk]
            qc = []
            for b in (2 * pr, 2 * pr + 1):
                c_prev = c0_ref[0, b, h]
                qc.append(jnp.dot(lhs, c_prev.astype(BF16), preferred_element_type=F32))
                c_ref[0, b, h] = decay_all[b * T:b * T + 1, lane:lane + 1] * c_prev + upd[b * dqk:(b + 1) * dqk, :]
            num = num_intra[rows, :] + w_inter[rows, :] * jnp.where(first_of_pair, qc[0], qc[1])
            hc = num * inv[rows, :]
            hn = hc * lax.rsqrt(jnp.mean(hc * hc, axis=1, keepdims=True) + EPS) * hg
            y_ref[rows, h * dv:(h + 1) * dv] = (hn * jax.nn.sigmoid(o[rows, :].astype(F32))).astype(y_ref.dtype)

    m_ref[...] = m_new


def mlstm_short_mixer(proj, gate_pre, bias, head_g, c0, n0, m0, *, row0, batch, seq, nb):
    H = M_HEADS
    dqk, dv = c0.shape[-2], c0.shape[-1]
    R = nb * seq
    lanes = gate_pre.shape[1]
    assert batch % nb == 0 and row0 % R == 0 and lanes >= 2 * H
    nblocks = batch // nb
    blk0 = row0 // R
    bias_l = jnp.pad(bias.reshape(1, 2 * H), ((0, 0), (0, lanes - 2 * H)))
    m0_l = jnp.repeat(jnp.pad(m0, ((0, 0), (H, lanes - 2 * H))), seq, axis=0)
    n0_t = jnp.repeat(jnp.swapaxes(n0, 0, 1), seq, axis=1)
    y, c_out, n_out, m_out = pl.pallas_call(
        functools.partial(_mlstm_short_kernel, nb, seq, dqk, dv),
        grid=(nblocks,),
        in_specs=[
            pl.BlockSpec((R, proj.shape[1]), lambda i: (blk0 + i, 0)),
            pl.BlockSpec((R, lanes), lambda i: (blk0 + i, 0)),
            pl.BlockSpec((1, lanes), lambda i: (0, 0)),
            pl.BlockSpec((1, H * dv), lambda i: (0, 0)),
            pl.BlockSpec((1, nb, H, dqk, dv), lambda i: (i, 0, 0, 0, 0)),
            pl.BlockSpec((H, R, dqk), lambda i: (0, i, 0)),
            pl.BlockSpec((R, lanes), lambda i: (i, 0)),
        ],
        out_specs=[
            pl.BlockSpec((R, H * dv), lambda i: (i, 0)),
            pl.BlockSpec((1, nb, H, dqk, dv), lambda i: (i, 0, 0, 0, 0)),
            pl.BlockSpec((H, R, dqk), lambda i: (0, i, 0)),
            pl.BlockSpec((R, lanes), lambda i: (i, 0)),
        ],
        out_shape=[
            jax.ShapeDtypeStruct((batch * seq, H * dv), BF16),
            jax.ShapeDtypeStruct((nblocks, nb, H, dqk, dv), F32),
            jax.ShapeDtypeStruct(n0_t.shape, F32),
            jax.ShapeDtypeStruct(m0_l.shape, F32),
        ],
        compiler_params=_params("arbitrary"),
        name="mlstm_short_mixer",
    )(proj, gate_pre, bias_l, head_g.reshape(1, H * dv), c0.reshape(nblocks, nb, H, dqk, dv), n0_t, m0_l)
    return (y, c_out.reshape(batch, H, dqk, dv), jnp.swapaxes(n_out[:, ::seq, :], 0, 1), m_out[::seq, H:2 * H])


def _hgrn_short_kernel(nb, T, heads, dk, dv,
                       p_ref, lb_ref, gn_ref, s0_ref, y_ref, s_ref,
                       qd_scr, kd_scr, qe_scr, ke_scr, dec_scr):
    R = nb * T
    fdim = heads * dk
    scale = dk ** -0.5
    W = 2 * dk
    PAIR = 2 * T
    assert T % 8 == 0 and T <= H_SUB and nb % 2 == 0

    r = lax.broadcasted_iota(jnp.int32, (R, R), 0)
    c = lax.broadcasted_iota(jnp.int32, (R, R), 1)
    causal = ((r // T) == (c // T)) & (c <= r)
    tril = causal.astype(BF16)
    seq_cols = (lax.broadcasted_iota(jnp.int32, (nb, 1, R), 2) // T
                == lax.broadcasted_iota(jnp.int32, (nb, 1, R), 0)).astype(BF16)
    first_of_pair = lax.broadcasted_iota(jnp.int32, (PAIR, 1), 0) < T

    def minus_ref_row(a, row):
        a3 = a.reshape(nb, T, a.shape[-1])
        return (a3 - a3[:, row:row + 1, :]).reshape(a.shape)

    for g in range(fdim // W):
        sl = slice(g * W, (g + 1) * W)
        q = p_ref[:, sl].astype(F32) * scale
        fpre = p_ref[:, fdim + g * W:fdim + (g + 1) * W].astype(F32)
        lb = lb_ref[:, sl]
        f = lb + (1.0 - lb) * jax.nn.sigmoid(fpre)
        k = 1.0 - f
        a = _cumsum_rows_exact(tril, jnp.log(f))
        d = minus_ref_row(a, T // 2)
        to_last = -minus_ref_row(a, T - 1)
        qd_scr[:, sl] = (q * jnp.exp(d)).astype(BF16)
        kd_scr[:, sl] = (k * jnp.exp(-d)).astype(BF16)
        qe_scr[:, sl] = (q * jnp.exp(a)).astype(BF16)
        ke_scr[:, sl] = k * jnp.exp(to_last)
        dec_scr[:, sl] = jnp.exp(a + to_last)

    for h in range(heads):
        sl = slice(h * dk, (h + 1) * dk)
        iv = p_ref[:, 2 * fdim + h * dv:2 * fdim + (h + 1) * dv]
        sc = jnp.where(causal, _dot_nt(qd_scr[:, sl], kd_scr[:, sl]), 0.0)
        oc = jnp.dot(sc.astype(BF16), iv, preferred_element_type=F32)
        ke_t = ke_scr[:, sl].T.astype(BF16)
        dec_t = dec_scr[:, sl].T
        stacked = (ke_t[None] * seq_cols).reshape(nb * dk, R)
        upd = jnp.dot(stacked, iv, preferred_element_type=F32)
        gn = gn_ref[:, h * dv:(h + 1) * dv]
        for pr in range(nb // 2):
            rows = slice(pr * PAIR, (pr + 1) * PAIR)
            lhs = qe_scr[rows, sl]
            inter = []
            for b in (2 * pr, 2 * pr + 1):
                s_prev = s0_ref[0, b, h]
                inter.append(jnp.dot(lhs, s_prev.astype(BF16), preferred_element_type=F32))
                s_ref[0, b, h] = dec_t[:, b * T:b * T + 1] * s_prev + upd[b * dk:(b + 1) * dk, :]
            o2 = oc[rows, :] + jnp.where(first_of_pair, inter[0], inter[1])
            gate = p_ref[rows, 2 * fdim + heads * dv + h * dv:2 * fdim + heads * dv + (h + 1) * dv].astype(F32)
            on = o2 * lax.rsqrt(jnp.mean(o2 * o2, axis=1, keepdims=True) + EPS) * gn
            y_ref[rows, h * dv:(h + 1) * dv] = (on * _silu(gate)).astype(y_ref.dtype)


def hgrn_short_mixer(proj, lb, gn_g, s0, *, row0, batch, seq, nb):
    heads, dk, dv = s0.shape[-3], s0.shape[-2], s0.shape[-1]
    R = nb * seq
    assert batch % nb == 0 and row0 % R == 0
    nblocks = batch // nb
    blk0 = row0 // R
    fdim = heads * dk
    s0 = s0.reshape(nblocks, nb, heads, dk, dv)
    y, s_out = pl.pallas_call(
        functools.partial(_hgrn_short_kernel, nb, seq, heads, dk, dv),
        grid=(nblocks,),
        in_specs=[
            pl.BlockSpec((R, proj.shape[1]), lambda i: (blk0 + i, 0)),
            pl.BlockSpec((1, fdim), lambda i: (0, 0)),
            pl.BlockSpec((1, heads * dv), lambda i: (0, 0)),
            pl.BlockSpec((1, nb, heads, dk, dv), lambda i: (i, 0, 0, 0, 0)),
        ],
        out_specs=[
            pl.BlockSpec((R, heads * dv), lambda i: (i, 0)),
            pl.BlockSpec((1, nb, heads, dk, dv), lambda i: (i, 0, 0, 0, 0)),
        ],
        out_shape=[
            jax.ShapeDtypeStruct((batch * seq, heads * dv), BF16),
            jax.ShapeDtypeStruct(s0.shape, F32),
        ],
        scratch_shapes=[
            pltpu.VMEM((R, fdim), BF16), pltpu.VMEM((R, fdim), BF16), pltpu.VMEM((R, fdim), BF16),
            pltpu.VMEM((R, fdim), F32), pltpu.VMEM((R, fdim), F32),
        ],
        compiler_params=_params("arbitrary"),
        name="hgrn_short_mixer",
    )(proj, lb.reshape(1, fdim), gn_g.reshape(1, heads * dv), s0)
    return y, s_out.reshape(batch, heads, dk, dv)


def _hgrn_long_kernel(R, heads, dk, dv, levels,
                      p_ref, lb_ref, gn_ref, s0_ref, y_ref, s_ref,
                      st_scr, dec_scr, qd_scr, kd_scr, qe_scr, ke_scr, ql_scr, kl_scr, k_scr, a_scr,
                      kt_scr, sb_scr, ivt_scr, sc_scr, oc_scr):
    fdim = heads * dk
    eps_unscaled = EPS * dk
    W = 2 * dk
    last_step = pl.num_programs(1) - 1

    @pl.when(pl.program_id(1) == 0)
    def _():
        for h in range(heads):
            st_scr[h] = s0_ref[0, h].T
        ql_scr[...] = jnp.zeros(ql_scr.shape, BF16)
        kl_scr[...] = jnp.zeros(kl_scr.shape, BF16)

    r = lax.broadcasted_iota(jnp.int32, (R, R), 0)
    c = lax.broadcasted_iota(jnp.int32, (R, R), 1)
    tril = (c <= r).astype(BF16)
    diag_mask = ((r // H_SUB) == (c // H_SUB)) & (c <= r)
    level_masks = [((r // G) == (c // G)) & ((r % G) >= G // 2) & ((c % G) < G // 2) for G in levels]

    NSB = R // H_SUB

    for g in range(fdim // W):
        fpre = p_ref[:, fdim + g * W:fdim + (g + 1) * W].astype(F32)
        lb = lb_ref[:, g * W:(g + 1) * W]
        f = lb + (1.0 - lb) * jax.nn.sigmoid(fpre)
        k = 1.0 - f
        a = _cumsum_rows_exact(tril, jnp.log(f))
        for j in range(W // dk):
            k_scr[g * (W // dk) + j] = k[:, j * dk:(j + 1) * dk]
            a_scr[g * (W // dk) + j] = a[:, j * dk:(j + 1) * dk]

    def rows_of(ref, h, start, count, stride):
        if count == 1:
            return ref[h, start:start + 1, :]
        return ref.at[h][pl.ds(start, count, stride=stride), :]

    for h in range(heads):
        sl = slice(h * dk, (h + 1) * dk)
        a_last = a_scr[h, R - 1:R, :]
        dec_scr[:, sl] = jnp.exp(a_last)
        mids = rows_of(a_scr, h, H_SUB // 2, NSB, H_SUB)
        c_qe = jnp.exp(mids)
        c_ke = jnp.exp(a_last - mids)
        c_lv = []
        for G in levels:
            per = G // H_SUB
            bnd = rows_of(a_scr, h, G // 2 - 1, R // G, G)
            consts = []
            for o in range(per):
                mid_o = rows_of(a_scr, h, o * H_SUB + H_SUB // 2, R // G, G)
                consts.append(jnp.exp(mid_o - bnd) if o >= per // 2 else jnp.exp(bnd - mid_o))
            c_lv.append(consts)
        for s in range(NSB):
            rows = slice(s * H_SUB, (s + 1) * H_SUB)
            d = a_scr[h, rows, :] - mids[s:s + 1, :]
            qd = p_ref[rows, sl].astype(F32) * jnp.exp(d)
            kd = k_scr[h, rows, :] * jnp.exp(-d)
            qd_scr[rows, sl] = qd.astype(BF16)
            kd_scr[rows, sl] = kd.astype(BF16)
            qe_scr[rows, sl] = (qd * c_qe[s:s + 1, :]).astype(BF16)
            ke_scr[rows, sl] = (kd * c_ke[s:s + 1, :]).astype(BF16)
            for li, G in enumerate(levels):
                per = G // H_SUB
                o, gi = s % per, s // per
                const = c_lv[li][o][gi:gi + 1, :]
                if o >= per // 2:
                    ql_scr[li, rows, sl] = (qd * const).astype(BF16)
                else:
                    kl_scr[li, rows, sl] = (kd * const).astype(BF16)

    def paired(lhs_a, lhs_b, rhs_ab):
        ca = rhs_ab.shape[1] // 2
        return (jnp.dot(lhs_a, rhs_ab[:, :ca], preferred_element_type=F32),
                jnp.dot(lhs_b, rhs_ab[:, ca:], preferred_element_type=F32))

    assert dk == dv
    npairs = heads // 2
    lanes_a = [slice(2 * p * dk, (2 * p + 1) * dk) for p in range(npairs)]
    lanes_b = [slice((2 * p + 1) * dk, (2 * p + 2) * dk) for p in range(npairs)]
    lanes_ab = [slice(2 * p * dk, (2 * p + 2) * dk) for p in range(npairs)]
    iv_of = lambda p: p_ref[:, 2 * fdim + 2 * p * dv:2 * fdim + (2 * p + 2) * dv]

    for p in range(npairs):
        kt_scr[p, 0, :, :R] = kd_scr[:, lanes_a[p]].T
        kt_scr[p, 0, :, R:] = kd_scr[:, lanes_b[p]].T
        for li in range(len(levels)):
            kt_scr[p, 1 + li, :, :R] = kl_scr[li, :, lanes_a[p]].T
            kt_scr[p, 1 + li, :, R:] = kl_scr[li, :, lanes_b[p]].T
        sb_scr[p, :, :dv] = st_scr[2 * p].astype(BF16).T
        sb_scr[p, :, dv:] = st_scr[2 * p + 1].astype(BF16).T
        ivt_scr[p] = iv_of(p).T
    for p in range(npairs):
        da, db = paired(qd_scr[:, lanes_a[p]], qd_scr[:, lanes_b[p]], kt_scr[p, 0])
        sc_a, sc_b = jnp.where(diag_mask, da, 0.0), jnp.where(diag_mask, db, 0.0)
        for li in range(len(levels)):
            la, lb_ = paired(ql_scr[li, :, lanes_a[p]], ql_scr[li, :, lanes_b[p]], kt_scr[p, 1 + li])
            sc_a, sc_b = jnp.where(level_masks[li], la, sc_a), jnp.where(level_masks[li], lb_, sc_b)
        sc_scr[2 * p] = sc_a.astype(BF16)
        sc_scr[2 * p + 1] = sc_b.astype(BF16)
    for p in range(npairs):
        intra_a, intra_b = paired(sc_scr[2 * p], sc_scr[2 * p + 1], iv_of(p))
        inter_a, inter_b = paired(qe_scr[:, lanes_a[p]], qe_scr[:, lanes_b[p]], sb_scr[p])
        oc_scr[:, lanes_a[p]] = intra_a + inter_a
        oc_scr[:, lanes_b[p]] = intra_b + inter_b
        upd_a = jnp.dot(ivt_scr[p, :dv, :], ke_scr[:, lanes_a[p]], preferred_element_type=F32)
        upd_b = jnp.dot(ivt_scr[p, dv:, :], ke_scr[:, lanes_b[p]], preferred_element_type=F32)
        st_scr[2 * p] = st_scr[2 * p] * dec_scr[:, lanes_a[p]] + upd_a
        st_scr[2 * p + 1] = st_scr[2 * p + 1] * dec_scr[:, lanes_b[p]] + upd_b
    for h in range(heads):
        sl = slice(h * dv, (h + 1) * dv)
        oc = oc_scr[:, sl]
        gate = p_ref[:, 2 * fdim + heads * dv + h * dv:2 * fdim + heads * dv + (h + 1) * dv].astype(F32)
        on = oc * lax.rsqrt(jnp.mean(oc * oc, axis=1, keepdims=True) + eps_unscaled) * gn_ref[:, sl]
        y_ref[:, sl] = (on * _silu(gate)).astype(y_ref.dtype)

    @pl.when(pl.program_id(1) == last_step)
    def _():
        for h in range(heads):
            s_ref[0, h] = st_scr[h].T


def hgrn_long_mixer(proj, lb, gn_g, s0, *, row0, batch, seq, rows):
    heads, dk, dv = s0.shape[-3], s0.shape[-2], s0.shape[-1]
    R = rows
    assert seq % R == 0 and R % (2 * H_SUB) == 0 and row0 % R == 0
    levels = []
    G = 2 * H_SUB
    while G <= R:
        levels.append(G)
        G *= 2
    assert levels[-1] == R
    nsteps = seq // R
    blk0 = row0 // R
    fdim = heads * dk
    wide = lambda: pltpu.VMEM((R, fdim), BF16)
    return pl.pallas_call(
        functools.partial(_hgrn_long_kernel, R, heads, dk, dv, tuple(levels)),
        grid=(batch, nsteps),
        in_specs=[
            pl.BlockSpec((R, proj.shape[1]), lambda i, c: (blk0 + i * nsteps + c, 0)),
            pl.BlockSpec((1, fdim), lambda i, c: (0, 0)),
            pl.BlockSpec((1, heads * dv), lambda i, c: (0, 0)),
            pl.BlockSpec((1, heads, dk, dv), lambda i, c: (i, 0, 0, 0)),
        ],
        out_specs=[
            pl.BlockSpec((R, heads * dv), lambda i, c: (i * nsteps + c, 0)),
            pl.BlockSpec((1, heads, dk, dv), lambda i, c: (i, 0, 0, 0)),
        ],
        out_shape=[
            jax.ShapeDtypeStruct((batch * seq, heads * dv), BF16),
            jax.ShapeDtypeStruct(s0.shape, F32),
        ],
        scratch_shapes=[
            pltpu.VMEM((heads, dv, dk), F32),
            pltpu.VMEM((1, fdim), F32),
            wide(), wide(), wide(), wide(),
            pltpu.VMEM((len(levels), R, fdim), BF16),
            pltpu.VMEM((len(levels), R, fdim), BF16),
            pltpu.VMEM((heads, R, dk), F32),
            pltpu.VMEM((heads, R, dk), F32),
            pltpu.VMEM((heads // 2, 1 + len(levels), dk, 2 * R), BF16),
            pltpu.VMEM((heads // 2, dk, 2 * dv), BF16),
            pltpu.VMEM((heads // 2, 2 * dv, R), BF16),
            pltpu.VMEM((heads, R, R), BF16),
            pltpu.VMEM((R, heads * dv), F32),
        ],
        compiler_params=_params("arbitrary", "arbitrary"),
        name="hgrn_long_mixer",
    )(proj, lb.reshape(1, fdim), gn_g.reshape(1, heads * dv), s0)


def kernel(x_prompt, x_sample, state_mlstm_C, state_mlstm_n, state_mlstm_m, state_hgrn_S,
           norm_mix_g, norm_ffn_g, norm_final_g, mlstm_w_in, mlstm_b_gates, mlstm_head_norm_g,
           mlstm_w_out, hgrn_w_in, hgrn_lower_bounds, hgrn_g_norm_g, hgrn_w_out, ffn_w_up, ffn_w_down):
    bp, tp, d = x_prompt.shape
    bs, ts, _ = x_sample.shape
    np_, ns_ = bp * tp, bs * ts
    depth = norm_mix_g.shape[0]
    H = M_HEADS
    dqk, dv = state_mlstm_C.shape[-2], state_mlstm_C.shape[-1]
    hh, hdk, hdv = state_hgrn_S.shape[-3], state_hgrn_S.shape[-2], state_hgrn_S.shape[-1]

    xs = (x_prompt.reshape(np_, d), x_sample.reshape(ns_, d))

    lb_all = jnp.cumsum(jax.nn.softmax(hgrn_lower_bounds.astype(F32), axis=0), axis=0)
    lb_all = lb_all - lb_all[0]

    Cp, Np, Mp, Sp, Cs, Ns, Ms, Ss = [], [], [], [], [], [], [], []
    for i in range(depth):
        j = i // 2
        if i % 2 == 0:
            n_main = 2 * H * dqk + 2 * H * dv
            w_in_t = jnp.swapaxes(mlstm_w_in, 1, 2)
            w_gates = jnp.pad(w_in_t[j, n_main:], ((0, LANES - 2 * H), (0, 0)))
            proj, gate_pre = norm_matmul(xs, norm_mix_g[i], w_in_t, j, n_main, tm=PROJ_ROWS, tn=PROJ_COLS,
                                         out_dtype=BF16, w_extra=w_gates, w_rows_are_outputs=True)
            args = (proj, gate_pre, mlstm_b_gates[j], mlstm_head_norm_g[j])
            yp, c_p, n_p, m_p = mlstm_long_mixer(
                *args, jnp.zeros((bp, H, dqk, dv), F32), jnp.zeros((bp, H, dqk), F32), jnp.zeros((bp, H), F32),
                row0=0, batch=bp, seq=tp, rows=MLSTM_STEP_ROWS)
            ysm, c_s, n_s, m_s = mlstm_short_mixer(
                *args, state_mlstm_C[j], state_mlstm_n[j], state_mlstm_m[j],
                row0=np_, batch=bs, seq=ts, nb=SAMPLE_SEQS)
            Cp.append(c_p); Np.append(n_p); Mp.append(m_p)
            Cs.append(c_s); Ns.append(n_s); Ms.append(m_s)
            w_out = mlstm_w_out
        else:
            proj = norm_matmul(xs, norm_mix_g[i], hgrn_w_in, j, hgrn_w_in.shape[-1],
                               tm=PROJ_ROWS, tn=PROJ_COLS, out_dtype=BF16)
            args = (proj, lb_all[i], hgrn_g_norm_g[j])
            yp, s_p = hgrn_long_mixer(*args, jnp.zeros((bp, hh, hdk, hdv), F32),
                                      row0=0, batch=bp, seq=tp, rows=HGRN_STEP_ROWS)
            ysm, s_s = hgrn_short_mixer(*args, state_hgrn_S[j], row0=np_, batch=bs, seq=ts, nb=SAMPLE_SEQS)
            Sp.append(s_p); Ss.append(s_s)
            w_out = hgrn_w_out
        x = matmul_residual((yp, ysm), w_out, j, xs, tm=OUT_ROWS, tn=d)
        act = norm_swiglu(x, norm_ffn_g[i], ffn_w_up, i, tm=PROJ_ROWS, tn=SWIGLU_COLS)
        x = matmul_residual((act,), ffn_w_down, i, (x,), tm=DOWN_ROWS, tn=DOWN_COLS)
        xs = (x,)

    y_prompt = final_norm(x, norm_final_g, row0=0, nrows=np_, tm=NORM_ROWS).reshape(bp, tp, d)
    y_sample = final_norm(x, norm_final_g, row0=np_, nrows=ns_, tm=NORM_ROWS).reshape(bs, ts, d)
    cat = lambda parts: jnp.stack(parts) if len(parts) > 1 else parts[0][None]
    return (y_prompt, y_sample, cat(Cp), cat(Np), cat(Mp), cat(Sp), cat(Cs), cat(Ns), cat(Ms), cat(Ss))
```

```python
import functools

import jax
import jax.numpy as jnp
from jax import lax
from jax.experimental import pallas as pl
from jax.experimental.pallas import tpu as pltpu

F32 = jnp.float32
BF16 = jnp.bfloat16

EPS = 1e-6
GATE_CAP = 15.0
M_HEADS = 8
H_SUB = 16

LANES = 128

PROJ_ROWS = 1024
PROJ_COLS = 1024
SWIGLU_COLS = 512
OUT_ROWS = 256
DOWN_ROWS = 512
DOWN_COLS = 512
NORM_ROWS = 512
MLSTM_STEP_ROWS = 256
HGRN_STEP_ROWS = 128
SAMPLE_SEQS = 8

VMEM_LIMIT_BYTES = 56 * 1024 * 1024


def _params(*sem):
    return pltpu.CompilerParams(dimension_semantics=sem, vmem_limit_bytes=VMEM_LIMIT_BYTES)


def _row_tiles(xs, tm):
    for x in xs:
        assert x.shape[0] % tm == 0, (x.shape, tm)
    return tuple(x.shape[0] // tm for x in xs)


def _clamped_row_tile(first, count, row_of, col_of, *ids):
    return (jnp.clip(row_of(*ids) - first, 0, count - 1), col_of(*ids))


def _row_specs(tiles, block, row_of, col_of, single_tile_unbuffered=False):
    specs, first = [], 0
    for count in tiles:
        mode = dict(pipeline_mode=pl.Buffered(1)) if single_tile_unbuffered and count == 1 else {}
        specs.append(pl.BlockSpec(block, functools.partial(_clamped_row_tile, first, count, row_of, col_of), **mode))
        first += count
    return specs


def _on_owner(refs, tiles, i, fn):
    if len(refs) == 1:
        fn(refs[0])
        return
    first = 0
    for ref, count in zip(refs, tiles):
        pl.when((i >= first) & (i < first + count))(functools.partial(fn, ref))
        first += count


def _rms_normed(x, g):
    ms = jnp.mean(x * x, axis=-1, keepdims=True)
    return x * lax.rsqrt(ms + EPS) * g


def _norm_matmul_kernel(tiles, with_extra, w_rows_are_outputs, *refs):
    nx = len(tiles)
    x_refs, g_ref, w_ref = refs[:nx], refs[nx], refs[nx + 1]
    if with_extra:
        we_ref, o_ref, oe_ref, h_scr = refs[nx + 2:]
    else:
        o_ref, h_scr = refs[nx + 2:]
    matmul = _dot_nt if w_rows_are_outputs else functools.partial(jnp.dot, preferred_element_type=F32)

    @pl.when(pl.program_id(1) == 0)
    def _():
        def build(x_ref):
            h_scr[...] = _rms_normed(x_ref[...], g_ref[...]).astype(BF16)
            if with_extra:
                oe_ref[...] = matmul(h_scr[...], we_ref[...].astype(BF16))
        _on_owner(x_refs, tiles, pl.program_id(0), build)

    o_ref[...] = matmul(h_scr[...], w_ref[...].astype(BF16)).astype(o_ref.dtype)


def norm_matmul(xs, g, w, layer, n_cols, *, tm, tn, out_dtype, w_extra=None, w_rows_are_outputs=False):
    d = xs[0].shape[1]
    tiles = _row_tiles(xs, tm)
    m = tm * sum(tiles)
    row_of, col0 = (lambda i, j: i), (lambda i, j: 0)
    w_spec = (pl.BlockSpec((None, tn, d), lambda i, j: (layer, j, 0)) if w_rows_are_outputs
              else pl.BlockSpec((None, d, tn), lambda i, j: (layer, 0, j)))
    in_specs = (_row_specs(tiles, (tm, d), row_of, col0, single_tile_unbuffered=len(tiles) > 1)
                + [pl.BlockSpec((1, d), lambda i, j: (0, 0)), w_spec])
    out_specs = [pl.BlockSpec((tm, tn), lambda i, j: (i, j))]
    out_shape = [jax.ShapeDtypeStruct((m, n_cols), out_dtype)]
    args = list(xs) + [g.reshape(1, d), w]
    if w_extra is not None:
        ne = w_extra.shape[0 if w_rows_are_outputs else 1]
        in_specs.append(pl.BlockSpec(w_extra.shape, lambda i, j: (0, 0)))
        out_specs.append(pl.BlockSpec((tm, ne), lambda i, j: (i, 0)))
        out_shape.append(jax.ShapeDtypeStruct((m, ne), F32))
        args.append(w_extra)
    outs = pl.pallas_call(
        functools.partial(_norm_matmul_kernel, tiles, w_extra is not None, w_rows_are_outputs),
        grid=(m // tm, n_cols // tn),
        in_specs=in_specs,
        out_specs=out_specs,
        out_shape=out_shape,
        scratch_shapes=[pltpu.VMEM((tm, d), BF16)],
        compiler_params=_params("arbitrary", "arbitrary"),
        name="norm_matmul",
    )(*args)
    return outs if w_extra is not None else outs[0]


def _norm_swiglu_kernel(x_ref, g_ref, wa_ref, wu_ref, o_ref, h_scr):
    @pl.when(pl.program_id(1) == 0)
    def _():
        h_scr[...] = _rms_normed(x_ref[...], g_ref[...]).astype(BF16)

    h = h_scr[...]
    a = jnp.dot(h, wa_ref[...].astype(BF16), preferred_element_type=F32)
    u = jnp.dot(h, wu_ref[...].astype(BF16), preferred_element_type=F32)
    o_ref[...] = (a * jax.nn.sigmoid(a) * u).astype(o_ref.dtype)


def norm_swiglu(x, g, w_up, layer, *, tm, tn):
    m, d = x.shape
    ff = w_up.shape[-1] // 2
    nj = ff // tn
    return pl.pallas_call(
        _norm_swiglu_kernel,
        grid=(m // tm, nj),
        in_specs=[
            pl.BlockSpec((tm, d), lambda i, j: (i, 0)),
            pl.BlockSpec((1, d), lambda i, j: (0, 0)),
            pl.BlockSpec((None, d, tn), lambda i, j: (layer, 0, j)),
            pl.BlockSpec((None, d, tn), lambda i, j: (layer, 0, j + nj)),
        ],
        out_specs=pl.BlockSpec((tm, tn), lambda i, j: (i, j)),
        out_shape=jax.ShapeDtypeStruct((m, ff), BF16),
        scratch_shapes=[pltpu.VMEM((tm, d), BF16)],
        compiler_params=_params("arbitrary", "arbitrary"),
        name="norm_swiglu",
    )(x, g.reshape(1, d), w_up, w_up)


def _matmul_residual_kernel(y_tiles, r_tiles, *refs):
    ny, nr = len(y_tiles), len(r_tiles)
    y_refs, w_ref, r_refs = refs[:ny], refs[ny], refs[ny + 1:ny + 1 + nr]
    o_ref, w_scr = refs[ny + 1 + nr:]
    i = pl.program_id(1)

    @pl.when(i == 0)
    def _():
        w_scr[...] = w_ref[...].astype(BF16)

    def product(y_ref):
        o_ref[...] = jnp.dot(y_ref[...], w_scr[...], preferred_element_type=F32)

    def add_residual(r_ref):
        o_ref[...] = o_ref[...] + r_ref[...]

    _on_owner(y_refs, y_tiles, i, product)
    _on_owner(r_refs, r_tiles, i, add_residual)


def matmul_residual(ys, w, layer, rs, *, tm, tn):
    k, n = w.shape[-2], w.shape[-1]
    y_tiles, r_tiles = _row_tiles(ys, tm), _row_tiles(rs, tm)
    assert sum(y_tiles) == sum(r_tiles)
    m = tm * sum(y_tiles)
    row_of = lambda j, i: i
    w_mode = dict(pipeline_mode=pl.Buffered(1)) if tn == n else {}
    in_specs = (_row_specs(y_tiles, (tm, k), row_of, lambda j, i: 0)
                + [pl.BlockSpec((None, k, tn), lambda j, i: (layer, 0, j), **w_mode)]
                + _row_specs(r_tiles, (tm, tn), row_of, lambda j, i: j))
    return pl.pallas_call(
        functools.partial(_matmul_residual_kernel, y_tiles, r_tiles),
        grid=(n // tn, m // tm),
        in_specs=in_specs,
        out_specs=pl.BlockSpec((tm, tn), lambda j, i: (i, j)),
        out_shape=jax.ShapeDtypeStruct((m, n), F32),
        scratch_shapes=[pltpu.VMEM((k, tn), BF16)],
        compiler_params=_params("arbitrary", "arbitrary"),
        name="matmul_residual",
    )(*ys, w, *rs)


def _final_norm_kernel(x_ref, g_ref, o_ref):
    o_ref[...] = _rms_normed(x_ref[...], g_ref[...])


def final_norm(x, g, *, row0, nrows, tm):
    d = x.shape[1]
    assert row0 % tm == 0 and nrows % tm == 0
    return pl.pallas_call(
        _final_norm_kernel,
        grid=(nrows // tm,),
        in_specs=[pl.BlockSpec((tm, d), lambda i: (row0 // tm + i, 0)), pl.BlockSpec((1, d), lambda i: (0, 0))],
        out_specs=pl.BlockSpec((tm, d), lambda i: (i, 0)),
        out_shape=jax.ShapeDtypeStruct((nrows, d), F32),
        compiler_params=_params("arbitrary"),
        name="final_norm",
    )(x, g.reshape(1, d))


def _dot_nt(a, b):
    return lax.dot_general(a, b, (((1,), (1,)), ((), ())), preferred_element_type=F32)


def _dot_tn(a, b):
    return lax.dot_general(a, b, (((0,), (0,)), ((), ())), preferred_element_type=F32)


def _log_sigmoid(x):
    return jnp.minimum(x, 0.0) - jnp.log1p(jnp.exp(-jnp.abs(x)))


def _silu(x):
    return 0.5 * x * (1.0 + jnp.tanh(0.5 * x))


def _cumsum_rows_exact(tril, x):
    hi = x.astype(BF16)
    r1 = x - hi.astype(F32)
    mid = r1.astype(BF16)
    lo = (r1 - mid.astype(F32)).astype(BF16)
    return (jnp.dot(tril, hi, preferred_element_type=F32) + jnp.dot(tril, mid, preferred_element_type=F32)
            + jnp.dot(tril, lo, preferred_element_type=F32))


def _mlstm_long_kernel(R, dqk, dv,
                       p_ref, gp_ref, bias_ref, hg_ref, c0_ref, n0_ref, m0_ref,
                       y_ref, c_ref, n_ref, m_ref, kt_scr):
    H = M_HEADS
    s1 = H * dqk
    scale = dqk ** -0.5

    @pl.when(pl.program_id(1) == 0)
    def _():
        c_ref[...] = c0_ref[...]
        n_ref[...] = n0_ref[...]
        m_ref[...] = m0_ref[...]

    r = lax.broadcasted_iota(jnp.int32, (R, R), 0)
    c = lax.broadcasted_iota(jnp.int32, (R, R), 1)
    causal = c <= r
    tril = causal.astype(BF16)

    capped = GATE_CAP * jnp.tanh((gp_ref[...] + bias_ref[...]) / GATE_CAP)
    cum = _cumsum_rows_exact(tril, _log_sigmoid(capped))
    ig_on_f = pltpu.roll(capped, H, axis=1)
    m_prev = m_ref[0]
    b_last = cum[R - 1:R, :]
    g_all = b_last - cum + ig_on_f
    m_new = jnp.maximum(b_last + m_prev, jnp.max(g_all, axis=0, keepdims=True))
    decay_all = jnp.exp(b_last + m_prev - m_new)
    wk_all = jnp.exp(g_all - m_new) * scale
    inter_all = cum + m_prev
    rows_t = (ig_on_f - cum).T

    q_cols = [slice(h * dqk, (h + 1) * dqk) for h in range(H)]
    k_cols = [slice(s1 + h * dqk, s1 + (h + 1) * dqk) for h in range(H)]
    v_cols = [slice(2 * s1 + h * dv, 2 * s1 + (h + 1) * dv) for h in range(H)]
    o_cols = [slice(2 * s1 + H * dv + h * dv, 2 * s1 + H * dv + (h + 1) * dv) for h in range(H)]

    for h in range(H):
        lane = H + h
        q = p_ref[:, q_cols[h]]
        c_prev = c_ref[0, h]
        c_bf = c_prev.astype(BF16)
        n_prev = n_ref[0, h:h + 1, :]
        kt_scr[h] = p_ref[:, k_cols[h]].T

        dlog = jnp.where(causal, cum[:, lane:lane + 1] + rows_t[lane:lane + 1, :], -jnp.inf)
        inter = inter_all[:, lane:lane + 1]
        m_t = jnp.maximum(inter, jnp.max(dlog, axis=1, keepdims=True))
        w_inter = jnp.exp(inter - m_t)
        sc = jnp.dot(q, kt_scr[h], preferred_element_type=F32) * scale * jnp.exp(dlog - m_t)
        num = (jnp.dot(sc.astype(BF16), p_ref[:, v_cols[h]], preferred_element_type=F32)
               + w_inter * jnp.dot(q, c_bf, preferred_element_type=F32))
        qn = jnp.sum(q.astype(F32) * n_prev, axis=1, keepdims=True)
        den = jnp.sum(sc, axis=1, keepdims=True) + w_inter * qn
        inv = 1.0 / jnp.maximum(jnp.abs(den), jnp.exp(-m_t))
        norm = inv * lax.rsqrt(inv * inv * jnp.mean(num * num, axis=1, keepdims=True) + EPS)
        o = p_ref[:, o_cols[h]].astype(F32)
        y_ref[:, h * dv:(h + 1) * dv] = (num * norm * hg_ref[:, h * dv:(h + 1) * dv]
                                         * jax.nn.sigmoid(o)).astype(y_ref.dtype)

        k = p_ref[:, k_cols[h]]
        w_k = wk_all[:, lane:lane + 1]
        decay = decay_all[:, lane:lane + 1]
        wv = (w_k * p_ref[:, v_cols[h]].astype(F32)).astype(BF16)
        c_ref[0, h] = decay * c_prev + _dot_tn(k, wv)
        n_ref[0, h:h + 1, :] = decay * n_prev + jnp.sum(w_k * k.astype(F32), axis=0, keepdims=True)

    m_ref[0] = m_new


def mlstm_long_mixer(proj, gate_pre, bias, head_g, c0, n0, m0, *, row0, batch, seq, rows):
    H = M_HEADS
    dqk, dv = c0.shape[-2], c0.shape[-1]
    R = rows
    lanes = gate_pre.shape[1]
    assert seq % R == 0 and row0 % R == 0 and lanes >= 2 * H
    nsteps = seq // R
    blk0 = row0 // R
    rows_of = lambda i, c: (blk0 + i * nsteps + c, 0)
    bias_l = jnp.pad(bias.reshape(1, 2 * H), ((0, 0), (0, lanes - 2 * H)))
    m0_l = jnp.pad(m0, ((0, 0), (H, lanes - 2 * H))).reshape(batch, 1, lanes)
    y, c_out, n_out, m_out = pl.pallas_call(
        functools.partial(_mlstm_long_kernel, R, dqk, dv),
        grid=(batch, nsteps),
        in_specs=[
            pl.BlockSpec((R, proj.shape[1]), rows_of),
            pl.BlockSpec((R, lanes), rows_of),
            pl.BlockSpec((1, lanes), lambda i, c: (0, 0)),
            pl.BlockSpec((1, H * dv), lambda i, c: (0, 0)),
            pl.BlockSpec((1, H, dqk, dv), lambda i, c: (i, 0, 0, 0)),
            pl.BlockSpec((1, H, dqk), lambda i, c: (i, 0, 0)),
            pl.BlockSpec((1, 1, lanes), lambda i, c: (i, 0, 0)),
        ],
        out_specs=[
            pl.BlockSpec((R, H * dv), lambda i, c: (i * nsteps + c, 0)),
            pl.BlockSpec((1, H, dqk, dv), lambda i, c: (i, 0, 0, 0)),
            pl.BlockSpec((1, H, dqk), lambda i, c: (i, 0, 0)),
            pl.BlockSpec((1, 1, lanes), lambda i, c: (i, 0, 0)),
        ],
        out_shape=[
            jax.ShapeDtypeStruct((batch * seq, H * dv), BF16),
            jax.ShapeDtypeStruct(c0.shape, F32),
            jax.ShapeDtypeStruct(n0.shape, F32),
            jax.ShapeDtypeStruct(m0_l.shape, F32),
        ],
        scratch_shapes=[pltpu.VMEM((H, dqk, R), BF16)],
        compiler_params=_params("arbitrary", "arbitrary"),
        name="mlstm_long_mixer",
    )(proj, gate_pre, bias_l, head_g.reshape(1, H * dv), c0, n0, m0_l)
    return y, c_out, n_out, m_out[:, 0, H:2 * H]


def _mlstm_short_kernel(nb, T, dqk, dv,
                        p_ref, gp_ref, bias_ref, hg_ref, c0_ref, n0_ref, m0_ref,
                        y_ref, c_ref, n_ref, m_ref):
    H = M_HEADS
    R = nb * T
    PAIR = 2 * T
    s1 = H * dqk
    scale = dqk ** -0.5
    assert T % 8 == 0 and nb % 2 == 0

    r = lax.broadcasted_iota(jnp.int32, (R, R), 0)
    c = lax.broadcasted_iota(jnp.int32, (R, R), 1)
    causal = ((r // T) == (c // T)) & (c <= r)
    tril = causal.astype(BF16)
    seq_cols = (lax.broadcasted_iota(jnp.int32, (nb, 1, R), 2) // T
                == lax.broadcasted_iota(jnp.int32, (nb, 1, R), 0)).astype(BF16)
    first_of_pair = lax.broadcasted_iota(jnp.int32, (PAIR, 1), 0) < T

    def per_seq(x):
        return x.reshape(nb, T, x.shape[-1])

    def seq_rows(x3):
        return jnp.broadcast_to(x3, (nb, T, x3.shape[-1])).reshape(R, x3.shape[-1])

    capped = GATE_CAP * jnp.tanh((gp_ref[...] + bias_ref[...]) / GATE_CAP)
    cum = _cumsum_rows_exact(tril, _log_sigmoid(capped))
    ig_on_f = pltpu.roll(capped, H, axis=1)
    m_prev = m0_ref[...]
    b_last = seq_rows(per_seq(cum)[:, T - 1:T, :])
    g_all = b_last - cum + ig_on_f
    m_new = jnp.maximum(b_last + m_prev, seq_rows(jnp.max(per_seq(g_all), axis=1, keepdims=True)))
    decay_all = jnp.exp(b_last + m_prev - m_new)
    wk_all = jnp.exp(g_all - m_new) * scale
    inter_all = cum + m_prev
    rows_t = (ig_on_f - cum).T

    for h in range(H):
        lane = H + h
        q = p_ref[:, h * dqk:(h + 1) * dqk]
        k = p_ref[:, s1 + h * dqk:s1 + (h + 1) * dqk]
        v = p_ref[:, 2 * s1 + h * dv:2 * s1 + (h + 1) * dv]
        o = p_ref[:, 2 * s1 + H * dv + h * dv:2 * s1 + H * dv + (h + 1) * dv]
        kf = k.astype(F32)

        dlog = jnp.where(causal, cum[:, lane:lane + 1] + rows_t[lane:lane + 1, :], -jnp.inf)
        inter = inter_all[:, lane:lane + 1]
        m_t = jnp.maximum(inter, jnp.max(dlog, axis=1, keepdims=True))
        w_intra = jnp.exp(dlog - m_t)
        w_inter = jnp.exp(inter - m_t)
        sc = _dot_nt(q, k) * scale * w_intra
        num_intra = jnp.dot(sc.astype(BF16), v, preferred_element_type=F32)
        n_prev = n0_ref[h]
        qn = jnp.sum(q.astype(F32) * n_prev, axis=1, keepdims=True)
        den = jnp.sum(sc, axis=1, keepdims=True) + w_inter * qn
        inv = 1.0 / jnp.maximum(jnp.abs(den), jnp.exp(-m_t))
        hg = hg_ref[:, h * dv:(h + 1) * dv]

        w_k = wk_all[:, lane:lane + 1]
        k_t = kf.T.astype(BF16)
        stacked = (k_t[None] * seq_cols).reshape(nb * dqk, R)
        upd = jnp.dot(stacked, (w_k * v.astype(F32)).astype(BF16), preferred_element_type=F32)
        n_ref[h] = (decay_all[:, lane:lane + 1] * n_prev
                    + seq_rows(jnp.sum(per_seq(w_k * kf), axis=1, keepdims=True)))

        for pr in range(nb // 2):
            rows = slice(pr * PAIR, (pr + 1) * PAIR)
            lhs = p_ref[rows, h * dqk:(h + 1) * dqk]
            qc = []
            for b in (2 * pr, 2 * pr + 1):
                c_prev = c0_ref[0, b, h]
                qc.append(jnp.dot(lhs, c_prev.astype(BF16), preferred_element_type=F32))
                c_ref[0, b, h] = decay_all[b * T:b * T + 1, lane:lane + 1] * c_prev + upd[b * dqk:(b + 1) * dqk, :]
            num = num_intra[rows, :] + w_inter[rows, :] * jnp.where(first_of_pair, qc[0], qc[1])
            hc = num * inv[rows, :]
            hn = hc * lax.rsqrt(jnp.mean(hc * hc, axis=1, keepdims=True) + EPS) * hg
            y_ref[rows, h * dv:(h + 1) * dv] = (hn * jax.nn.sigmoid(o[rows, :].astype(F32))).astype(y_ref.dtype)

    m_ref[...] = m_new


def mlstm_short_mixer(proj, gate_pre, bias, head_g, c0, n0, m0, *, row0, batch, seq, nb):
    H = M_HEADS
    dqk, dv = c0.shape[-2], c0.shape[-1]
    R = nb * seq
    lanes = gate_pre.shape[1]
    assert batch % nb == 0 and row0 % R == 0 and lanes >= 2 * H
    nblocks = batch // nb
    blk0 = row0 // R
    bias_l = jnp.pad(bias.reshape(1, 2 * H), ((0, 0), (0, lanes - 2 * H)))
    m0_l = jnp.repeat(jnp.pad(m0, ((0, 0), (H, lanes - 2 * H))), seq, axis=0)
    n0_t = jnp.repeat(jnp.swapaxes(n0, 0, 1), seq, axis=1)
    y, c_out, n_out, m_out = pl.pallas_call(
        functools.partial(_mlstm_short_kernel, nb, seq, dqk, dv),
        grid=(nblocks,),
        in_specs=[
            pl.BlockSpec((R, proj.shape[1]), lambda i: (blk0 + i, 0)),
            pl.BlockSpec((R, lanes), lambda i: (blk0 + i, 0)),
            pl.BlockSpec((1, lanes), lambda i: (0, 0)),
            pl.BlockSpec((1, H * dv), lambda i: (0, 0)),
            pl.BlockSpec((1, nb, H, dqk, dv), lambda i: (i, 0, 0, 0, 0)),
            pl.BlockSpec((H, R, dqk), lambda i: (0, i, 0)),
            pl.BlockSpec((R, lanes), lambda i: (i, 0)),
        ],
        out_specs=[
            pl.BlockSpec((R, H * dv), lambda i: (i, 0)),
            pl.BlockSpec((1, nb, H, dqk, dv), lambda i: (i, 0, 0, 0, 0)),
            pl.BlockSpec((H, R, dqk), lambda i: (0, i, 0)),
            pl.BlockSpec((R, lanes), lambda i: (i, 0)),
        ],
        out_shape=[
            jax.ShapeDtypeStruct((batch * seq, H * dv), BF16),
            jax.ShapeDtypeStruct((nblocks, nb, H, dqk, dv), F32),
            jax.ShapeDtypeStruct(n0_t.shape, F32),
            jax.ShapeDtypeStruct(m0_l.shape, F32),
        ],
        compiler_params=_params("arbitrary"),
        name="mlstm_short_mixer",
    )(proj, gate_pre, bias_l, head_g.reshape(1, H * dv), c0.reshape(nblocks, nb, H, dqk, dv), n0_t, m0_l)
    return (y, c_out.reshape(batch, H, dqk, dv), jnp.swapaxes(n_out[:, ::seq, :], 0, 1), m_out[::seq, H:2 * H])


def _hgrn_short_kernel(nb, T, heads, dk, dv,
                       p_ref, lb_ref, gn_ref, s0_ref, y_ref, s_ref,
                       qd_scr, kd_scr, qe_scr, ke_scr, dec_scr):
    R = nb * T
    fdim = heads * dk
    scale = dk ** -0.5
    W = 2 * dk
    PAIR = 2 * T
    assert T % 8 == 0 and T <= H_SUB and nb % 2 == 0

    r = lax.broadcasted_iota(jnp.int32, (R, R), 0)
    c = lax.broadcasted_iota(jnp.int32, (R, R), 1)
    causal = ((r // T) == (c // T)) & (c <= r)
    tril = causal.astype(BF16)
    seq_cols = (lax.broadcasted_iota(jnp.int32, (nb, 1, R), 2) // T
                == lax.broadcasted_iota(jnp.int32, (nb, 1, R), 0)).astype(BF16)
    first_of_pair = lax.broadcasted_iota(jnp.int32, (PAIR, 1), 0) < T

    def minus_ref_row(a, row):
        a3 = a.reshape(nb, T, a.shape[-1])
        return (a3 - a3[:, row:row + 1, :]).reshape(a.shape)

    for g in range(fdim // W):
        sl = slice(g * W, (g + 1) * W)
        q = p_ref[:, sl].astype(F32) * scale
        fpre = p_ref[:, fdim + g * W:fdim + (g + 1) * W].astype(F32)
        lb = lb_ref[:, sl]
        f = lb + (1.0 - lb) * jax.nn.sigmoid(fpre)
        k = 1.0 - f
        a = _cumsum_rows_exact(tril, jnp.log(f))
        d = minus_ref_row(a, T // 2)
        to_last = -minus_ref_row(a, T - 1)
        qd_scr[:, sl] = (q * jnp.exp(d)).astype(BF16)
        kd_scr[:, sl] = (k * jnp.exp(-d)).astype(BF16)
        qe_scr[:, sl] = (q * jnp.exp(a)).astype(BF16)
        ke_scr[:, sl] = k * jnp.exp(to_last)
        dec_scr[:, sl] = jnp.exp(a + to_last)

    for h in range(heads):
        sl = slice(h * dk, (h + 1) * dk)
        iv = p_ref[:, 2 * fdim + h * dv:2 * fdim + (h + 1) * dv]
        sc = jnp.where(causal, _dot_nt(qd_scr[:, sl], kd_scr[:, sl]), 0.0)
        oc = jnp.dot(sc.astype(BF16), iv, preferred_element_type=F32)
        ke_t = ke_scr[:, sl].T.astype(BF16)
        dec_t = dec_scr[:, sl].T
        stacked = (ke_t[None] * seq_cols).reshape(nb * dk, R)
        upd = jnp.dot(stacked, iv, preferred_element_type=F32)
        gn = gn_ref[:, h * dv:(h + 1) * dv]
        for pr in range(nb // 2):
            rows = slice(pr * PAIR, (pr + 1) * PAIR)
            lhs = qe_scr[rows, sl]
            inter = []
            for b in (2 * pr, 2 * pr + 1):
                s_prev = s0_ref[0, b, h]
                inter.append(jnp.dot(lhs, s_prev.astype(BF16), preferred_element_type=F32))
                s_ref[0, b, h] = dec_t[:, b * T:b * T + 1] * s_prev + upd[b * dk:(b + 1) * dk, :]
            o2 = oc[rows, :] + jnp.where(first_of_pair, inter[0], inter[1])
            gate = p_ref[rows, 2 * fdim + heads * dv + h * dv:2 * fdim + heads * dv + (h + 1) * dv].astype(F32)
            on = o2 * lax.rsqrt(jnp.mean(o2 * o2, axis=1, keepdims=True) + EPS) * gn
            y_ref[rows, h * dv:(h + 1) * dv] = (on * _silu(gate)).astype(y_ref.dtype)


def hgrn_short_mixer(proj, lb, gn_g, s0, *, row0, batch, seq, nb):
    heads, dk, dv = s0.shape[-3], s0.shape[-2], s0.shape[-1]
    R = nb * seq
    assert batch % nb == 0 and row0 % R == 0
    nblocks = batch // nb
    blk0 = row0 // R
    fdim = heads * dk
    s0 = s0.reshape(nblocks, nb, heads, dk, dv)
    y, s_out = pl.pallas_call(
        functools.partial(_hgrn_short_kernel, nb, seq, heads, dk, dv),
        grid=(nblocks,),
        in_specs=[
            pl.BlockSpec((R, proj.shape[1]), lambda i: (blk0 + i, 0)),
            pl.BlockSpec((1, fdim), lambda i: (0, 0)),
            pl.BlockSpec((1, heads * dv), lambda i: (0, 0)),
            pl.BlockSpec((1, nb, heads, dk, dv), lambda i: (i, 0, 0, 0, 0)),
        ],
        out_specs=[
            pl.BlockSpec((R, heads * dv), lambda i: (i, 0)),
            pl.BlockSpec((1, nb, heads, dk, dv), lambda i: (i, 0, 0, 0, 0)),
        ],
        out_shape=[
            jax.ShapeDtypeStruct((batch * seq, heads * dv), BF16),
            jax.ShapeDtypeStruct(s0.shape, F32),
        ],
        scratch_shapes=[
            pltpu.VMEM((R, fdim), BF16), pltpu.VMEM((R, fdim), BF16), pltpu.VMEM((R, fdim), BF16),
            pltpu.VMEM((R, fdim), F32), pltpu.VMEM((R, fdim), F32),
        ],
        compiler_params=_params("arbitrary"),
        name="hgrn_short_mixer",
    )(proj, lb.reshape(1, fdim), gn_g.reshape(1, heads * dv), s0)
    return y, s_out.reshape(batch, heads, dk, dv)


def _hgrn_long_kernel(R, heads, dk, dv, levels,
                      p_ref, lb_ref, gn_ref, s0_ref, y_ref, s_ref,
                      st_scr, dec_scr, qd_scr, kd_scr, qe_scr, ke_scr, ql_scr, kl_scr, k_scr, a_scr,
                      kt_scr, sb_scr, ivt_scr, sc_scr, oc_scr):
    fdim = heads * dk
    eps_unscaled = EPS * dk
    W = dk
    last_step = pl.num_programs(1) - 1

    @pl.when(pl.program_id(1) == 0)
    def _():
        for h in range(heads):
            st_scr[h] = s0_ref[0, h].T
        ql_scr[...] = jnp.zeros(ql_scr.shape, BF16)
        kl_scr[...] = jnp.zeros(kl_scr.shape, BF16)

    r = lax.broadcasted_iota(jnp.int32, (R, R), 0)
    c = lax.broadcasted_iota(jnp.int32, (R, R), 1)
    tril = (c <= r).astype(BF16)
    diag_mask = ((r // H_SUB) == (c // H_SUB)) & (c <= r)
    level_masks = [((r // G) == (c // G)) & ((r % G) >= G // 2) & ((c % G) < G // 2) for G in levels]

    NSB = R // H_SUB

    for g in range(fdim // W):
        fpre = p_ref[:, fdim + g * W:fdim + (g + 1) * W].astype(F32)
        lb = lb_ref[:, g * W:(g + 1) * W]
        f = lb + (1.0 - lb) * jax.nn.sigmoid(fpre)
        k = 1.0 - f
        a = _cumsum_rows_exact(tril, jnp.log(f))
        for j in range(W // dk):
            k_scr[g * (W // dk) + j] = k[:, j * dk:(j + 1) * dk]
            a_scr[g * (W // dk) + j] = a[:, j * dk:(j + 1) * dk]

    def rows_of(ref, h, start, count, stride):
        if count == 1:
            return ref[h, start:start + 1, :]
        return ref.at[h][pl.ds(start, count, stride=stride), :]

    for h in range(heads):
        sl = slice(h * dk, (h + 1) * dk)
        a_last = a_scr[h, R - 1:R, :]
        dec_scr[:, sl] = jnp.exp(a_last)
        mids = rows_of(a_scr, h, H_SUB // 2, NSB, H_SUB)
        c_qe = jnp.exp(mids)
        c_ke = jnp.exp(a_last - mids)
        c_lv = []
        for G in levels:
            per = G // H_SUB
            bnd = rows_of(a_scr, h, G // 2 - 1, R // G, G)
            consts = []
            for o in range(per):
                mid_o = rows_of(a_scr, h, o * H_SUB + H_SUB // 2, R // G, G)
                consts.append(jnp.exp(mid_o - bnd) if o >= per // 2 else jnp.exp(bnd - mid_o))
            c_lv.append(consts)
        for s in range(NSB):
            rows = slice(s * H_SUB, (s + 1) * H_SUB)
            d = a_scr[h, rows, :] - mids[s:s + 1, :]
            qd = p_ref[rows, sl].astype(F32) * jnp.exp(d)
            kd = k_scr[h, rows, :] * jnp.exp(-d)
            qd_scr[rows, sl] = qd.astype(BF16)
            kd_scr[rows, sl] = kd.astype(BF16)
            qe_scr[rows, sl] = (qd * c_qe[s:s + 1, :]).astype(BF16)
            ke_scr[rows, sl] = (kd * c_ke[s:s + 1, :]).astype(BF16)
            for li, G in enumerate(levels):
                per = G // H_SUB
                o, gi = s % per, s // per
                const = c_lv[li][o][gi:gi + 1, :]
                if o >= per // 2:
                    ql_scr[li, rows, sl] = (qd * const).astype(BF16)
                else:
                    kl_scr[li, rows, sl] = (kd * const).astype(BF16)

    def paired(lhs_a, lhs_b, rhs_ab):
        ca = rhs_ab.shape[1] // 2
        return (jnp.dot(lhs_a, rhs_ab[:, :ca], preferred_element_type=F32),
                jnp.dot(lhs_b, rhs_ab[:, ca:], preferred_element_type=F32))

    assert dk == dv
    npairs = heads // 2
    lanes_a = [slice(2 * p * dk, (2 * p + 1) * dk) for p in range(npairs)]
    lanes_b = [slice((2 * p + 1) * dk, (2 * p + 2) * dk) for p in range(npairs)]
    lanes_ab = [slice(2 * p * dk, (2 * p + 2) * dk) for p in range(npairs)]
    iv_of = lambda p: p_ref[:, 2 * fdim + 2 * p * dv:2 * fdim + (2 * p + 2) * dv]

    for p in range(npairs):
        kt_scr[p, 0, :, :R] = kd_scr[:, lanes_a[p]].T
        kt_scr[p, 0, :, R:] = kd_scr[:, lanes_b[p]].T
        for li in range(len(levels)):
            kt_scr[p, 1 + li, :, :R] = kl_scr[li, :, lanes_a[p]].T
            kt_scr[p, 1 + li, :, R:] = kl_scr[li, :, lanes_b[p]].T
        sb_scr[p, :, :dv] = st_scr[2 * p].astype(BF16).T
        sb_scr[p, :, dv:] = st_scr[2 * p + 1].astype(BF16).T
        ivt_scr[p] = iv_of(p).T
    for p in range(npairs):
        da, db = paired(qd_scr[:, lanes_a[p]], qd_scr[:, lanes_b[p]], kt_scr[p, 0])
        sc_a, sc_b = jnp.where(diag_mask, da, 0.0), jnp.where(diag_mask, db, 0.0)
        for li in range(len(levels)):
            la, lb_ = paired(ql_scr[li, :, lanes_a[p]], ql_scr[li, :, lanes_b[p]], kt_scr[p, 1 + li])
            sc_a, sc_b = jnp.where(level_masks[li], la, sc_a), jnp.where(level_masks[li], lb_, sc_b)
        sc_scr[2 * p] = sc_a.astype(BF16)
        sc_scr[2 * p + 1] = sc_b.astype(BF16)
    for p in range(npairs):
        intra_a, intra_b = paired(sc_scr[2 * p], sc_scr[2 * p + 1], iv_of(p))
        inter_a, inter_b = paired(qe_scr[:, lanes_a[p]], qe_scr[:, lanes_b[p]], sb_scr[p])
        oc_scr[:, lanes_a[p]] = intra_a + inter_a
        oc_scr[:, lanes_b[p]] = intra_b + inter_b
        upd_a = jnp.dot(ivt_scr[p, :dv, :], ke_scr[:, lanes_a[p]], preferred_element_type=F32)
        upd_b = jnp.dot(ivt_scr[p, dv:, :], ke_scr[:, lanes_b[p]], preferred_element_type=F32)
        st_scr[2 * p] = st_scr[2 * p] * dec_scr[:, lanes_a[p]] + upd_a
        st_scr[2 * p + 1] = st_scr[2 * p + 1] * dec_scr[:, lanes_b[p]] + upd_b
    for h in range(heads):
        sl = slice(h * dv, (h + 1) * dv)
        oc = oc_scr[:, sl]
        gate = p_ref[:, 2 * fdim + heads * dv + h * dv:2 * fdim + heads * dv + (h + 1) * dv].astype(F32)
        on = oc * lax.rsqrt(jnp.mean(oc * oc, axis=1, keepdims=True) + eps_unscaled) * gn_ref[:, sl]
        y_ref[:, sl] = (on * _silu(gate)).astype(y_ref.dtype)

    @pl.when(pl.program_id(1) == last_step)
    def _():
        for h in range(heads):
            s_ref[0, h] = st_scr[h].T


def hgrn_long_mixer(proj, lb, gn_g, s0, *, row0, batch, seq, rows):
    heads, dk, dv = s0.shape[-3], s0.shape[-2], s0.shape[-1]
    R = rows
    assert seq % R == 0 and R % (2 * H_SUB) == 0 and row0 % R == 0
    levels = []
    G = 2 * H_SUB
    while G <= R:
        levels.append(G)
        G *= 2
    assert levels[-1] == R
    nsteps = seq // R
    blk0 = row0 // R
    fdim = heads * dk
    wide = lambda: pltpu.VMEM((R, fdim), BF16)
    return pl.pallas_call(
        functools.partial(_hgrn_long_kernel, R, heads, dk, dv, tuple(levels)),
        grid=(batch, nsteps),
        in_specs=[
            pl.BlockSpec((R, proj.shape[1]), lambda i, c: (blk0 + i * nsteps + c, 0)),
            pl.BlockSpec((1, fdim), lambda i, c: (0, 0)),
            pl.BlockSpec((1, heads * dv), lambda i, c: (0, 0)),
            pl.BlockSpec((1, heads, dk, dv), lambda i, c: (i, 0, 0, 0)),
        ],
        out_specs=[
            pl.BlockSpec((R, heads * dv), lambda i, c: (i * nsteps + c, 0)),
            pl.BlockSpec((1, heads, dk, dv), lambda i, c: (i, 0, 0, 0)),
        ],
        out_shape=[
            jax.ShapeDtypeStruct((batch * seq, heads * dv), BF16),
            jax.ShapeDtypeStruct(s0.shape, F32),
        ],
        scratch_shapes=[
            pltpu.VMEM((heads, dv, dk), F32),
            pltpu.VMEM((1, fdim), F32),
            wide(), wide(), wide(), wide(),
            pltpu.VMEM((len(levels), R, fdim), BF16),
            pltpu.VMEM((len(levels), R, fdim), BF16),
            pltpu.VMEM((heads, R, dk), F32),
            pltpu.VMEM((heads, R, dk), F32),
            pltpu.VMEM((heads // 2, 1 + len(levels), dk, 2 * R), BF16),
            pltpu.VMEM((heads // 2, dk, 2 * dv), BF16),
            pltpu.VMEM((heads // 2, 2 * dv, R), BF16),
            pltpu.VMEM((heads, R, R), BF16),
            pltpu.VMEM((R, heads * dv), F32),
        ],
        compiler_params=_params("arbitrary", "arbitrary"),
        name="hgrn_long_mixer",
    )(proj, lb.reshape(1, fdim), gn_g.reshape(1, heads * dv), s0)


def kernel(x_prompt, x_sample, state_mlstm_C, state_mlstm_n, state_mlstm_m, state_hgrn_S,
           norm_mix_g, norm_ffn_g, norm_final_g, mlstm_w_in, mlstm_b_gates, mlstm_head_norm_g,
           mlstm_w_out, hgrn_w_in, hgrn_lower_bounds, hgrn_g_norm_g, hgrn_w_out, ffn_w_up, ffn_w_down):
    bp, tp, d = x_prompt.shape
    bs, ts, _ = x_sample.shape
    np_, ns_ = bp * tp, bs * ts
    depth = norm_mix_g.shape[0]
    H = M_HEADS
    dqk, dv = state_mlstm_C.shape[-2], state_mlstm_C.shape[-1]
    hh, hdk, hdv = state_hgrn_S.shape[-3], state_hgrn_S.shape[-2], state_hgrn_S.shape[-1]

    xs = (x_prompt.reshape(np_, d), x_sample.reshape(ns_, d))

    lb_all = jnp.cumsum(jax.nn.softmax(hgrn_lower_bounds.astype(F32), axis=0), axis=0)
    lb_all = lb_all - lb_all[0]

    Cp, Np, Mp, Sp, Cs, Ns, Ms, Ss = [], [], [], [], [], [], [], []
    for i in range(depth):
        j = i // 2
        if i % 2 == 0:
            n_main = 2 * H * dqk + 2 * H * dv
            w_in_t = jnp.swapaxes(mlstm_w_in, 1, 2)
            w_gates = jnp.pad(w_in_t[j, n_main:], ((0, LANES - 2 * H), (0, 0)))
            proj, gate_pre = norm_matmul(xs, norm_mix_g[i], w_in_t, j, n_main, tm=PROJ_ROWS, tn=PROJ_COLS,
                                         out_dtype=BF16, w_extra=w_gates, w_rows_are_outputs=True)
            args = (proj, gate_pre, mlstm_b_gates[j], mlstm_head_norm_g[j])
            yp, c_p, n_p, m_p = mlstm_long_mixer(
                *args, jnp.zeros((bp, H, dqk, dv), F32), jnp.zeros((bp, H, dqk), F32), jnp.zeros((bp, H), F32),
                row0=0, batch=bp, seq=tp, rows=MLSTM_STEP_ROWS)
            ysm, c_s, n_s, m_s = mlstm_short_mixer(
                *args, state_mlstm_C[j], state_mlstm_n[j], state_mlstm_m[j],
                row0=np_, batch=bs, seq=ts, nb=SAMPLE_SEQS)
            Cp.append(c_p); Np.append(n_p); Mp.append(m_p)
            Cs.append(c_s); Ns.append(n_s); Ms.append(m_s)
            w_out = mlstm_w_out
        else:
            proj = norm_matmul(xs, norm_mix_g[i], hgrn_w_in, j, hgrn_w_in.shape[-1],
                               tm=PROJ_ROWS, tn=PROJ_COLS, out_dtype=BF16)
            args = (proj, lb_all[i], hgrn_g_norm_g[j])
            yp, s_p = hgrn_long_mixer(*args, jnp.zeros((bp, hh, hdk, hdv), F32),
                                      row0=0, batch=bp, seq=tp, rows=HGRN_STEP_ROWS)
            ysm, s_s = hgrn_short_mixer(*args, state_hgrn_S[j], row0=np_, batch=bs, seq=ts, nb=SAMPLE_SEQS)
            Sp.append(s_p); Ss.append(s_s)
            w_out = hgrn_w_out
        x = matmul_residual((yp, ysm), w_out, j, xs, tm=OUT_ROWS, tn=d)
        act = norm_swiglu(x, norm_ffn_g[i], ffn_w_up, i, tm=PROJ_ROWS, tn=SWIGLU_COLS)
        x = matmul_residual((act,), ffn_w_down, i, (x,), tm=DOWN_ROWS, tn=DOWN_COLS)
        xs = (x,)

    y_prompt = final_norm(x, norm_final_g, row0=0, nrows=np_, tm=NORM_ROWS).reshape(bp, tp, d)
    y_sample = final_norm(x, norm_final_g, row0=np_, nrows=ns_, tm=NORM_ROWS).reshape(bs, ts, d)
    cat = lambda parts: jnp.stack(parts) if len(parts) > 1 else parts[0][None]
    return (y_prompt, y_sample, cat(Cp), cat(Np), cat(Mp), cat(Sp), cat(Cs), cat(Ns), cat(Ms), cat(Ss))
```
